```python
import math, functools
import jax, jax.numpy as jnp
from jax import lax
import numpy as np

D_MODEL = 1024
BATCH = 4
SEQ = 4096
DEPTH = 1
DEC_BATCH = 128
DEC_SEQ = 4
PAST_LEN = 8192
PAGE_SIZE = 128

DN_DK = 128
DN_DV = 128
DN_HEADS = D_MODEL // DN_DV
CONV_W = 4
DN_CHUNK = 64
SW_HEAD_DIM = 64
SW_HEADS = D_MODEL // SW_HEAD_DIM
SW_KV_HEADS = SW_HEADS // 4
SW_GROUP = SW_HEADS // SW_KV_HEADS
WINDOW = 128
ROT_DIM = SW_HEAD_DIM // 4
ROPE_THETA = 500000.0
D_FF = ((8 * D_MODEL // 3 + 63) // 64) * 64
N_SUB = 3
EPS = 1e-6
NEG_INF = -1e30

DN_QK = DN_HEADS * DN_DK
DN_V = DN_HEADS * DN_DV
CONV_CH = 2 * DN_QK + DN_V
SW_Q = SW_HEADS * SW_HEAD_DIM
SW_KV = SW_KV_HEADS * SW_HEAD_DIM
IN_SIZES = (CONV_CH, DN_V, DN_HEADS, DN_HEADS, SW_Q, SW_KV, SW_KV, D_MODEL, D_MODEL)
D_IN = sum(IN_SIZES)

kernel_name = 'hybrid_deltanet_swa_macaron_step'


def rmsnorm(x, gain):
    xf = x.astype(jnp.float32)
    y = xf * lax.rsqrt(jnp.mean(xf * xf, axis=-1, keepdims=True) + EPS)
    return (y * gain.astype(jnp.float32)).astype(x.dtype)


def l2norm(x):
    xf = x.astype(jnp.float32)
    return xf * lax.rsqrt(jnp.sum(xf * xf, axis=-1, keepdims=True) + EPS)


def swiglu(h, w_gate, w_up, w_down):
    return (jax.nn.silu(h @ w_gate) * (h @ w_up)) @ w_down


def short_conv(u, buf, w):
    T = u.shape[1]
    full = jnp.concatenate([buf.astype(u.dtype), u], axis=1)
    out = sum(full[:, i:i + T] * w[i] for i in range(CONV_W))
    return jax.nn.silu(out), full[:, T:]


def rope_partial(x, pos):
    half = ROT_DIM // 2
    inv_freq = ROPE_THETA ** (-jnp.arange(half, dtype=jnp.float32) * (2.0 / ROT_DIM))
    ang = pos.astype(jnp.float32)[:, None] * inv_freq[None, :]
    cos = jnp.cos(ang)[None, :, None, :]
    sin = jnp.sin(ang)[None, :, None, :]
    xr = x[..., :ROT_DIM].astype(jnp.float32)
    x1, x2 = xr[..., :half], xr[..., half:]
    rot = jnp.concatenate([x1 * cos - x2 * sin, x2 * cos + x1 * sin], axis=-1)
    return jnp.concatenate([rot.astype(x.dtype), x[..., ROT_DIM:]], axis=-1)


def gated_delta_rule(q, k, v, g, beta, S0):
    B, T, H, DK = q.shape
    DV = v.shape[-1]
    C = math.gcd(T, DN_CHUNK)
    N = T // C
    f32 = jnp.float32

    def blocks(a):
        return a.astype(f32).reshape(B, N, C, H, -1).transpose(0, 1, 3, 2, 4)

    qc, kc, vc = blocks(q), blocks(k), blocks(v)
    gc = jnp.cumsum(g.astype(f32).reshape(B, N, C, H).transpose(0, 1, 3, 2), axis=-1)
    bc = beta.astype(f32).reshape(B, N, C, H).transpose(0, 1, 3, 2)[..., None]
    incl = jnp.tril(jnp.ones((C, C), dtype=bool))
    strict = jnp.tril(jnp.ones((C, C), dtype=bool), -1)
    diff = gc[..., :, None] - gc[..., None, :]
    decay = jnp.where(incl, jnp.exp(jnp.where(incl, diff, 0.0)), 0.0)
    kb = kc * bc
    lower = jnp.where(strict, jnp.einsum('bnhid,bnhjd->bnhij', kb, kc) * decay, 0.0)
    rhs = jnp.concatenate([vc * bc, kb * jnp.exp(gc)[..., None]], axis=-1)
    sol = lax.linalg.triangular_solve(jnp.eye(C, dtype=f32) + lower, rhs,
                                      left_side=True, lower=True, unit_diagonal=True)
    u, w = sol[..., :DV], sol[..., DV:]
    a_intra = jnp.where(incl, jnp.einsum('bnhid,bnhjd->bnhij', qc, kc) * decay, 0.0)

    def step(S, xs):
        qn, kn, un, wn, gn, an = xs
        v_new = un - jnp.einsum('bhcd,bhde->bhce', wn, S)
        o = (jnp.einsum('bhcd,bhde->bhce', qn * jnp.exp(gn)[..., None], S)
             + jnp.einsum('bhij,bhje->bhie', an, v_new))
        g_last = gn[..., -1:]
        k_dec = kn * jnp.exp(g_last - gn)[..., None]
        S = S * jnp.exp(g_last)[..., None] + jnp.einsum('bhcd,bhce->bhde', k_dec, v_new)
        return S, o

    xs = tuple(jnp.moveaxis(a, 1, 0) for a in (qc, kc, u, w, gc, a_intra))
    S, o = lax.scan(step, S0.astype(f32), xs)
    o = o.transpose(1, 0, 3, 2, 4).reshape(B, T, H, DV)
    return o.astype(v.dtype), S.astype(S0.dtype)


def sink_softmax(scores, mask, sinks):
    s = jnp.where(mask, scores, NEG_INF)
    sink = jnp.broadcast_to(sinks.astype(jnp.float32)[:, :, None, None], s.shape[:-1] + (1,))
    p = jax.nn.softmax(jnp.concatenate([s, sink], axis=-1), axis=-1)
    return p[..., :-1]


def swa_banded(q, k, v, sinks):
    B, T = q.shape[:2]
    W = WINDOW
    NB = T // W
    qb = q.reshape(B, NB, W, SW_KV_HEADS, SW_GROUP, SW_HEAD_DIM)
    kb = k.reshape(B, NB, W, SW_KV_HEADS, SW_HEAD_DIM)
    vb = v.reshape(B, NB, W, SW_KV_HEADS, SW_HEAD_DIM)
    pad = ((0, 0), (1, 0), (0, 0), (0, 0), (0, 0))
    kk = jnp.concatenate([jnp.pad(kb, pad)[:, :NB], kb], axis=2)
    vv = jnp.concatenate([jnp.pad(vb, pad)[:, :NB], vb], axis=2)
    qi = jnp.arange(W)[:, None] + W
    sj = jnp.arange(2 * W)[None, :]
    d = qi - sj
    band = (d >= 0) & (d <= WINDOW)
    valid = (jnp.arange(NB) > 0)[:, None, None] | (sj >= W)[None]
    mask = (band[None] & valid)[None, :, None, None]
    scores = jnp.einsum('bnqkgd,bnskd->bnkgqs', qb, kk).astype(jnp.float32) * (SW_HEAD_DIM ** -0.5)
    p = sink_softmax(scores, mask, sinks.reshape(SW_KV_HEADS, SW_GROUP))
    o = jnp.einsum('bnkgqs,bnskd->bnqkgd', p.astype(vv.dtype), vv).reshape(B, T, SW_Q)
    nb = min(WINDOW, T)
    return o, k[:, -nb:], v[:, -nb:]


def swa_buffered(kbuf, vbuf, q, k, v, sinks):
    B, T = q.shape[:2]
    Wb = kbuf.shape[1]
    kk = jnp.concatenate([kbuf.astype(k.dtype), k], axis=1)
    vv = jnp.concatenate([vbuf.astype(v.dtype), v], axis=1)
    qg = q.reshape(B, T, SW_KV_HEADS, SW_GROUP, SW_HEAD_DIM)
    d = (jnp.arange(T)[:, None] + Wb) - jnp.arange(Wb + T)[None, :]
    mask = ((d >= 0) & (d <= WINDOW))[None, None, None]
    scores = jnp.einsum('bqkgd,bskd->bkgqs', qg, kk).astype(jnp.float32) * (SW_HEAD_DIM ** -0.5)
    p = sink_softmax(scores, mask, sinks.reshape(SW_KV_HEADS, SW_GROUP))
    o = jnp.einsum('bkgqs,bskd->bqkgd', p.astype(vv.dtype), vv).reshape(B, T, SW_Q)
    return o, kk[:, -Wb:], vv[:, -Wb:]


def token_mixer(h, pos, conv_buf, S0, attend, p):
    B, T = h.shape[:2]
    proj = h @ p['w_in']
    split_idx = np.cumsum(IN_SIZES)[:-1].tolist()
    u, z, b_raw, a_raw, q_sw, k_sw, v_sw, gate_a, gate_b = jnp.split(proj, split_idx, axis=-1)
    u, conv_new = short_conv(u, conv_buf, p['conv_w'])
    q_dn, k_dn, v_dn = jnp.split(u, [DN_QK, 2 * DN_QK], axis=-1)
    q_dn = l2norm(q_dn.reshape(B, T, DN_HEADS, DN_DK)) * (DN_DK ** -0.5)
    k_dn = l2norm(k_dn.reshape(B, T, DN_HEADS, DN_DK))
    v_dn = v_dn.reshape(B, T, DN_HEADS, DN_DV)
    beta = jax.nn.sigmoid(b_raw.astype(jnp.float32))
    g = -jnp.exp(p['a_log'].astype(jnp.float32)) * jax.nn.softplus(
        a_raw.astype(jnp.float32) + p['dt_bias'].astype(jnp.float32))
    o_dn, S_new = gated_delta_rule(q_dn, k_dn, v_dn, g, beta, S0)
    o_dn = (rmsnorm(o_dn, p['dn_norm']) * jax.nn.silu(z.reshape(B, T, DN_HEADS, DN_DV))).reshape(B, T, DN_V)
    q = rope_partial(q_sw.reshape(B, T, SW_HEADS, SW_HEAD_DIM), pos)
    k = rope_partial(k_sw.reshape(B, T, SW_KV_HEADS, SW_HEAD_DIM), pos)
    v = v_sw.reshape(B, T, SW_KV_HEADS, SW_HEAD_DIM)
    o_sw, kbuf_new, vbuf_new = attend(q, k, v, p['sinks'])
    y = jax.nn.sigmoid(gate_a) * o_dn + jax.nn.sigmoid(gate_b) * o_sw
    return y @ p['w_out'], (kbuf_new, vbuf_new, conv_new, S_new)


def decoder_layer(x, c, pos, conv_buf, S0, attend, p):
    mod = jax.nn.silu(c) @ p['w_ada'] + p['b_ada']
    sh1, sc1, g1, sh2, sc2, g2, sh3, sc3, g3 = [m[:, None, :] for m in jnp.split(mod, 3 * N_SUB, axis=-1)]
    h = rmsnorm(x, p['ffn1_norm_pre']) * (1 + sc1) + sh1
    x = x + 0.5 * g1 * rmsnorm(swiglu(h, p['ffn1_w_gate'], p['ffn1_w_up'], p['ffn1_w_down']), p['ffn1_norm_post'])
    h = rmsnorm(x, p['mix_norm_pre']) * (1 + sc2) + sh2
    y, state = token_mixer(h, pos, conv_buf, S0, attend, p)
    x = x + g2 * rmsnorm(y, p['mix_norm_post'])
    h = rmsnorm(x, p['ffn2_norm_pre']) * (1 + sc3) + sh3
    x = x + 0.5 * g3 * rmsnorm(swiglu(h, p['ffn2_w_gate'], p['ffn2_w_up'], p['ffn2_w_down']), p['ffn2_norm_post'])
    return x, state


def setup_inputs(seed: int = 0) -> dict:
    key = jax.random.key(seed)
    ks = jax.random.split(key, 40)
    f32 = jnp.float32

    def nrm(k, shape, s=1.0):
        return s * jax.random.normal(k, shape, f32)

    def gain(k, n=D_MODEL):
        return 1.0 + nrm(k, (DEPTH, n), 0.05)

    wb = min(WINDOW, PAST_LEN)
    dt = jnp.exp(jax.random.uniform(ks[30], (DEPTH, DN_HEADS), f32, math.log(1e-3), math.log(1e-1)))
    return {
        'x_prompt': nrm(ks[0], (BATCH, SEQ, D_MODEL)),
        'x_sample': nrm(ks[1], (DEC_BATCH, DEC_SEQ, D_MODEL)),
        'cache_swa_k': nrm(ks[2], (DEPTH, DEC_BATCH, wb, SW_KV_HEADS, SW_HEAD_DIM)),
        'cache_swa_v': nrm(ks[3], (DEPTH, DEC_BATCH, wb, SW_KV_HEADS, SW_HEAD_DIM)),
        'state_conv': nrm(ks[4], (DEPTH, DEC_BATCH, CONV_W - 1, CONV_CH)),
        'state_delta': nrm(ks[5], (DEPTH, DEC_BATCH, DN_HEADS, DN_DK, DN_DV), DN_DK ** -0.5),
        'c_prompt': nrm(ks[6], (BATCH, D_MODEL)),
        'c_sample': nrm(ks[7], (DEC_BATCH, D_MODEL)),
        'w_ada': nrm(ks[8], (DEPTH, D_MODEL, 3 * N_SUB * D_MODEL), 0.5 * D_MODEL ** -0.5),
        'b_ada': nrm(ks[9], (DEPTH, 3 * N_SUB * D_MODEL), 0.02),
        'ffn1_norm_pre': gain(ks[10]),
        'ffn1_norm_post': gain(ks[11]),
        'ffn1_w_gate': nrm(ks[12], (DEPTH, D_MODEL, D_FF), D_MODEL ** -0.5),
        'ffn1_w_up': nrm(ks[13], (DEPTH, D_MODEL, D_FF), D_MODEL ** -0.5),
        'ffn1_w_down': nrm(ks[14], (DEPTH, D_FF, D_MODEL), D_FF ** -0.5),
        'mix_norm_pre': gain(ks[15]),
        'mix_norm_post': gain(ks[16]),
        'w_in': nrm(ks[17], (DEPTH, D_MODEL, D_IN), D_MODEL ** -0.5),
        'conv_w': nrm(ks[18], (DEPTH, CONV_W, CONV_CH), CONV_W ** -0.5),
        'a_log': jnp.log(jax.random.uniform(ks[19], (DEPTH, DN_HEADS), f32, 1.0, 16.0)),
        'dt_bias': dt + jnp.log(-jnp.expm1(-dt)),
        'dn_norm': gain(ks[20], DN_DV),
        'sinks': nrm(ks[21], (DEPTH, SW_HEADS), 0.5),
        'w_out': nrm(ks[22], (DEPTH, D_MODEL, D_MODEL), D_MODEL ** -0.5),
        'ffn2_norm_pre': gain(ks[23]),
        'ffn2_norm_post': gain(ks[24]),
        'ffn2_w_gate': nrm(ks[25], (DEPTH, D_MODEL, D_FF), D_MODEL ** -0.5),
        'ffn2_w_up': nrm(ks[26], (DEPTH, D_MODEL, D_FF), D_MODEL ** -0.5),
        'ffn2_w_down': nrm(ks[27], (DEPTH, D_FF, D_MODEL), D_FF ** -0.5),
    }


def reference(x_prompt, x_sample, cache_swa_k, cache_swa_v, state_conv, state_delta,
              c_prompt, c_sample, w_ada, b_ada, ffn1_norm_pre, ffn1_norm_post,
              ffn1_w_gate, ffn1_w_up, ffn1_w_down, mix_norm_pre, mix_norm_post,
              w_in, conv_w, a_log, dt_bias, dn_norm, sinks, w_out,
              ffn2_norm_pre, ffn2_norm_post, ffn2_w_gate, ffn2_w_up, ffn2_w_down):
    B = x_prompt.shape[0]
    pos_p = jnp.arange(x_prompt.shape[1])
    pos_s = PAST_LEN + jnp.arange(x_sample.shape[1])
    y_p, y_s = x_prompt, x_sample
    kp, vp, cp, sp = [], [], [], []
    ksm, vsm, csm, ssm = [], [], [], []
    for l in range(DEPTH):
        p = dict(w_ada=w_ada[l], b_ada=b_ada[l],
                 ffn1_norm_pre=ffn1_norm_pre[l], ffn1_norm_post=ffn1_norm_post[l],
                 ffn1_w_gate=ffn1_w_gate[l], ffn1_w_up=ffn1_w_up[l], ffn1_w_down=ffn1_w_down[l],
                 mix_norm_pre=mix_norm_pre[l], mix_norm_post=mix_norm_post[l],
                 w_in=w_in[l], conv_w=conv_w[l], a_log=a_log[l], dt_bias=dt_bias[l],
                 dn_norm=dn_norm[l], sinks=sinks[l], w_out=w_out[l],
                 ffn2_norm_pre=ffn2_norm_pre[l], ffn2_norm_post=ffn2_norm_post[l],
                 ffn2_w_gate=ffn2_w_gate[l], ffn2_w_up=ffn2_w_up[l], ffn2_w_down=ffn2_w_down[l])
        conv0 = jnp.zeros((B, CONV_W - 1, CONV_CH), x_prompt.dtype)
        S0 = jnp.zeros((B, DN_HEADS, DN_DK, DN_DV), state_delta.dtype)
        y_p, (k1, v1, c1, s1) = decoder_layer(y_p, c_prompt, pos_p, conv0, S0, swa_banded, p)
        attend_s = functools.partial(swa_buffered, cache_swa_k[l], cache_swa_v[l])
        y_s, (k2, v2, c2, s2) = decoder_layer(y_s, c_sample, pos_s, state_conv[l], state_delta[l], attend_s, p)
        kp.append(k1); vp.append(v1); cp.append(c1); sp.append(s1)
        ksm.append(k2); vsm.append(v2); csm.append(c2); ssm.append(s2)
    swa_k_prompt, swa_v_prompt = jnp.stack(kp), jnp.stack(vp)
    conv_prompt, delta_prompt = jnp.stack(cp), jnp.stack(sp)
    swa_k_sample, swa_v_sample = jnp.stack(ksm), jnp.stack(vsm)
    conv_sample, delta_sample = jnp.stack(csm), jnp.stack(ssm)
    return (y_p, y_s, swa_k_prompt, swa_v_prompt, conv_prompt, delta_prompt,
            swa_k_sample, swa_v_sample, conv_sample, delta_sample)
```

```python
import functools
import math

import jax
import jax.numpy as jnp
from jax import lax
from jax.experimental import pallas as pl
from jax.experimental.pallas import tpu as pltpu

F32, BF16 = jnp.float32, jnp.bfloat16

D_MODEL = 1024
DN_HEADS, DN_DK, DN_DV = 8, 128, 128
CONV_W = 4
CONV_CH = 3 * D_MODEL
DN_CHUNK = 64
SW_HEAD_DIM, SW_HEADS, SW_KV_HEADS, SW_GROUP = 64, 16, 4, 4
SW_KV = SW_KV_HEADS * SW_HEAD_DIM
WINDOW = 128
ROT_DIM = 16
ROPE_THETA = 500000.0
PAST_LEN = 8192
EPS = 1e-6
NEG_INF = -1e30

LANE = 128
SUBLANE = 8
VMEM_LIMIT = 56 * 1024 * 1024

P_U, P_Z, P_QKV, P_G, P_BA = 0, 3072, 4096, 5632, 7680
P_TOTAL = 7808
HI = lax.Precision.HIGHEST


def _dot(a, b):
    return jnp.dot(a, b, preferred_element_type=F32)


def _dot_nt(a, b):
    return lax.dot_general(a, b, (((1,), (1,)), ((), ())), preferred_element_type=F32)


def _dot_tn(a, b):
    return lax.dot_general(a, b, (((0,), (0,)), ((), ())), preferred_element_type=F32)


def _dot_hi(a, b):
    return jnp.dot(a, b, preferred_element_type=F32, precision=HI)


def _silu(x):
    return x * jax.nn.sigmoid(x)


def _rms(x, gain):
    ms = jnp.mean(x * x, axis=-1, keepdims=True)
    return x * lax.rsqrt(ms + EPS) * gain


def _mod_rows(ref, per_token, tiles_per_seq):
    if per_token:
        return ref[...]
    b = pl.program_id(0) // tiles_per_seq
    return ref[pl.ds(b, 1), :]


def _const_spec(shape):
    return pl.BlockSpec(shape, lambda *_: (0,) * len(shape), pipeline_mode=pl.Buffered(1))


def _params(sem):
    return pltpu.CompilerParams(dimension_semantics=sem, vmem_limit_bytes=VMEM_LIMIT)


def _ada_kernel(c_ref, w_ref, b_ref, o_ref):
    h = _silu(c_ref[...]).astype(BF16)
    o_ref[...] = _dot(h, w_ref[...]) + b_ref[...]


def _ada(c, w_bf, b):
    rows, n = c.shape[0], w_bf.shape[1]
    return pl.pallas_call(
        _ada_kernel,
        out_shape=jax.ShapeDtypeStruct((rows, n), F32),
        grid=(n // D_MODEL,),
        in_specs=[
            pl.BlockSpec((rows, D_MODEL), lambda j: (0, 0)),
            pl.BlockSpec((D_MODEL, D_MODEL), lambda j: (0, j)),
            pl.BlockSpec((1, D_MODEL), lambda j: (0, j)),
        ],
        out_specs=pl.BlockSpec((rows, D_MODEL), lambda j: (0, j)),
        compiler_params=_params(("arbitrary",)),
        name="ada",
    )(c, w_bf, b)


def _mod_specs(per_token, tm, cols):
    if per_token:
        return [pl.BlockSpec((tm, D_MODEL), functools.partial(lambda i, c: (i, c), c=c)) for c in cols]
    return [pl.BlockSpec((SUBLANE, D_MODEL), functools.partial(lambda i, c: (0, c), c=c)) for c in cols]


def _ffn_kernel(x_ref, sh_ref, sc_ref, g_ref, pre_ref, post_ref, wg_ref, wu_ref, wd_ref, o_ref,
                *, per_token, tiles_per_seq):
    x = x_ref[...]
    sh = _mod_rows(sh_ref, per_token, tiles_per_seq)
    sc = _mod_rows(sc_ref, per_token, tiles_per_seq)
    g = _mod_rows(g_ref, per_token, tiles_per_seq)
    h = (_rms(x, pre_ref[...]) * (1.0 + sc) + sh).astype(BF16)
    act = (_silu(_dot(h, wg_ref[...])) * _dot(h, wu_ref[...])).astype(BF16)
    y = _dot(act, wd_ref[...])
    o_ref[...] = x + 0.5 * g * _rms(y, post_ref[...])


def _ffn(x, mod, col0, per_token, seq_len, pre, post, wg, wu, wd):
    n = x.shape[0]
    tm = min(512, n)
    dff = wg.shape[1]
    kern = functools.partial(_ffn_kernel, per_token=per_token, tiles_per_seq=max(seq_len // tm, 1))
    row = lambda i: (i, 0)
    return pl.pallas_call(
        kern,
        out_shape=jax.ShapeDtypeStruct((n, D_MODEL), F32),
        grid=(n // tm,),
        in_specs=[pl.BlockSpec((tm, D_MODEL), row)]
        + _mod_specs(per_token, tm, (col0, col0 + 1, col0 + 2))
        + [_const_spec((1, D_MODEL)), _const_spec((1, D_MODEL)),
           _const_spec((D_MODEL, dff)), _const_spec((D_MODEL, dff)), _const_spec((dff, D_MODEL))],
        out_specs=pl.BlockSpec((tm, D_MODEL), row),
        compiler_params=_params(("arbitrary",)),
        name="ffn",
    )(x, mod, mod, mod, pre, post, wg, wu, wd)


def _proj_kernel(x_ref, sh_ref, sc_ref, pre_ref, w_ref, u_ref, z_ref, qkv_ref, g_ref, ba_ref,
                 *, per_token, tiles_per_seq):
    sh = _mod_rows(sh_ref, per_token, tiles_per_seq)
    sc = _mod_rows(sc_ref, per_token, tiles_per_seq)
    h = (_rms(x_ref[...], pre_ref[...]) * (1.0 + sc) + sh).astype(BF16)
    u_ref[...] = _dot(h, w_ref[:, P_U:P_Z])
    z_ref[...] = _dot(h, w_ref[:, P_Z:P_QKV])
    qkv_ref[...] = _dot(h, w_ref[:, P_QKV:P_G])
    g_ref[...] = _dot(h, w_ref[:, P_G:P_BA])
    ba_ref[...] = _dot(h, w_ref[:, P_BA:P_TOTAL])


def _proj(x, mod, per_token, seq_len, pre, w):
    n = x.shape[0]
    tm = min(256, n)
    kern = functools.partial(_proj_kernel, per_token=per_token, tiles_per_seq=max(seq_len // tm, 1))
    row = lambda i: (i, 0)
    widths = (P_Z - P_U, P_QKV - P_Z, P_G - P_QKV, P_BA - P_G, P_TOTAL - P_BA)
    return pl.pallas_call(
        kern,
        out_shape=[jax.ShapeDtypeStruct((n, wd), F32) for wd in widths],
        grid=(n // tm,),
        in_specs=[pl.BlockSpec((tm, D_MODEL), row)]
        + _mod_specs(per_token, tm, (3, 4))
        + [_const_spec((1, D_MODEL)), _const_spec((D_MODEL, P_TOTAL))],
        out_specs=[pl.BlockSpec((tm, wd), row) for wd in widths],
        compiler_params=_params(("arbitrary",)),
        name="proj",
    )(x, mod, mod, pre, w)


def _tri_inv(low, c):
    ii = lax.broadcasted_iota(jnp.int32, (c, c), 0)
    jj = lax.broadcasted_iota(jnp.int32, (c, c), 1)
    x = jnp.where(ii == jj, 1.0, 0.0).astype(F32) - low
    p = low
    n = 2
    while n < c:
        p = _dot_hi(p, p)
        x = x + _dot_hi(x, p)
        n *= 2
    return x


def _dn_kernel(u_ref, z_ref, ba_ref, cw_ref, gp_ref, dnn_ref, conv0_ref, s0_ref,
               o_ref, convo_ref, so_ref, ubuf, q_s, k_s, v_s, bg_s, *, tm, chunk, valid):
    t = pl.program_id(1)

    @pl.when(t == 0)
    def _():
        ubuf[0:SUBLANE, :] = conv0_ref[...]
        so_ref[...] = s0_ref[...]

    ubuf[SUBLANE:SUBLANE + tm, :] = u_ref[...]
    for cb in range(CONV_CH // LANE):
        cols = slice(cb * LANE, (cb + 1) * LANE)
        acc = ubuf[pl.ds(SUBLANE - CONV_W + 1, tm), cols] * cw_ref[0:1, cols]
        for i in range(1, CONV_W):
            acc = acc + ubuf[pl.ds(SUBLANE - CONV_W + 1 + i, tm), cols] * cw_ref[i:i + 1, cols]
        a = _silu(acc)
        sec, off = divmod(cb * LANE, D_MODEL)
        dst = slice(off, off + LANE)
        if sec == 2:
            v_s[:, dst] = a
        else:
            an = a * lax.rsqrt(jnp.sum(a * a, axis=-1, keepdims=True) + EPS)
            if sec == 0:
                q_s[:, dst] = an * (DN_DK ** -0.5)
            else:
                k_s[:, dst] = an
    ubuf[0:SUBLANE, :] = ubuf[pl.ds(valid, SUBLANE), :]
    convo_ref[...] = ubuf[0:SUBLANE, :]

    ba = ba_ref[...]
    lane = lax.broadcasted_iota(jnp.int32, (tm, LANE), 1)
    rowi = lax.broadcasted_iota(jnp.int32, (tm, LANE), 0)
    xg = ba + gp_ref[0:1, :]
    softplus = jnp.maximum(xg, 0.0) + jnp.log1p(jnp.exp(-jnp.abs(xg)))
    gdec = -jnp.exp(gp_ref[1:2, :]) * softplus
    bg = jnp.where(lane < DN_HEADS, jax.nn.sigmoid(ba), gdec)
    bg_s[...] = jnp.where(rowi < valid, bg, 0.0)

    c = chunk
    ii = lax.broadcasted_iota(jnp.int32, (c, c), 0)
    jj = lax.broadcasted_iota(jnp.int32, (c, c), 1)
    incl = ii >= jj
    strict = ii > jj
    tril = jnp.where(incl, 1.0, 0.0).astype(F32)
    sel = jnp.where(lax.broadcasted_iota(jnp.int32, (SUBLANE, LANE), 1)
                    == lax.broadcasted_iota(jnp.int32, (SUBLANE, LANE), 0) + DN_HEADS, 1.0, 0.0).astype(F32)
    dnn = dnn_ref[...]

    def chunk_body(ci, carry):
        r0 = pl.multiple_of(ci * c, c)
        rows = pl.ds(r0, c)
        bgc = bg_s[rows, :]
        gcum = _dot_hi(tril, bgc)
        gcum_t = lax.dot_general(sel, gcum, (((1,), (1,)), ((), ())),
                                 preferred_element_type=F32, precision=HI)
        for h in range(DN_HEADS):
            hc = slice(h * DN_DK, (h + 1) * DN_DK)
            beta = bgc[:, h:h + 1]
            gc = gcum[:, DN_HEADS + h:DN_HEADS + h + 1]
            gr = gcum_t[h:h + 1, :]
            decay = jnp.where(incl, jnp.exp(jnp.where(incl, gc - gr, 0.0)), 0.0)
            qh, kh, vh = q_s[rows, hc], k_s[rows, hc], v_s[rows, hc]
            kb = kh * beta
            k_bf = kh.astype(BF16)
            low = jnp.where(strict, _dot_nt(kb.astype(BF16), k_bf) * decay, 0.0)
            tinv = _tri_inv(low, c)
            egc = jnp.exp(gc)
            s_old = so_ref[h]
            s_bf = s_old.astype(BF16)
            rhs = vh * beta - _dot((kb * egc).astype(BF16), s_bf)
            v_new = _dot_hi(tinv, rhs)
            vn_bf = v_new.astype(BF16)
            a_in = jnp.where(incl, _dot_nt(qh.astype(BF16), k_bf) * decay, 0.0)
            o = _dot((qh * egc).astype(BF16), s_bf) + _dot(a_in.astype(BF16), vn_bf)
            g_last = gc[c - 1:c, :]
            k_dec = kh * jnp.exp(g_last - gc)
            so_ref[h] = s_old * jnp.exp(g_last) + _dot_tn(k_dec.astype(BF16), vn_bf)
            o_ref[rows, hc] = _rms(o, dnn) * _silu(z_ref[rows, hc])
        return carry

    lax.fori_loop(0, tm // c, chunk_body, 0)


def _deltanet(u, z, ba, cw, gp, dnn, conv0, s0, *, batch, seq, tm, chunk, valid):
    nt = seq // tm
    row = lambda b, t: (b * nt + t, 0)
    per_b3 = lambda b, t: (b, 0, 0)
    per_b4 = lambda b, t: (b, 0, 0, 0)
    kern = functools.partial(_dn_kernel, tm=tm, chunk=chunk, valid=valid)
    return pl.pallas_call(
        kern,
        out_shape=[jax.ShapeDtypeStruct((batch * seq, D_MODEL), F32),
                   jax.ShapeDtypeStruct((batch, SUBLANE, CONV_CH), F32),
                   jax.ShapeDtypeStruct((batch, DN_HEADS, DN_DK, DN_DV), F32)],
        grid=(batch, nt),
        in_specs=[pl.BlockSpec((tm, CONV_CH), row),
                  pl.BlockSpec((tm, D_MODEL), row),
                  pl.BlockSpec((tm, LANE), row),
                  pl.BlockSpec((SUBLANE, CONV_CH), lambda b, t: (0, 0)),
                  pl.BlockSpec((SUBLANE, LANE), lambda b, t: (0, 0)),
                  pl.BlockSpec((1, DN_DV), lambda b, t: (0, 0)),
                  pl.BlockSpec((None, SUBLANE, CONV_CH), per_b3),
                  pl.BlockSpec((None, DN_HEADS, DN_DK, DN_DV), per_b4)],
        out_specs=[pl.BlockSpec((tm, D_MODEL), row),
                   pl.BlockSpec((None, SUBLANE, CONV_CH), per_b3),
                   pl.BlockSpec((None, DN_HEADS, DN_DK, DN_DV), per_b4)],
        scratch_shapes=[pltpu.VMEM((SUBLANE + tm, CONV_CH), F32),
                        pltpu.VMEM((tm, D_MODEL), F32),
                        pltpu.VMEM((tm, D_MODEL), F32),
                        pltpu.VMEM((tm, D_MODEL), F32),
                        pltpu.VMEM((tm, LANE), F32)],
        compiler_params=_params(("arbitrary", "arbitrary")),
        name="deltanet",
    )(u, z, ba, cw, gp, dnn, conv0, s0)


def _rope(x, cos, sa, sb):
    cols = []
    for cb in range(x.shape[1] // LANE):
        xc = x[:, cb * LANE:(cb + 1) * LANE]
        cols.append(xc * cos + pltpu.roll(xc, LANE - ROT_DIM // 2, 1) * sa + pltpu.roll(xc, ROT_DIM // 2, 1) * sb)
    return cols[0] if len(cols) == 1 else jnp.concatenate(cols, axis=1)


def _attend(q_bf, kk_bf, vv_bf, mask, sink):
    s = _dot_nt(q_bf, kk_bf) * (SW_HEAD_DIM ** -0.5)
    s = jnp.where(mask, s, NEG_INF)
    m = jnp.maximum(jnp.max(s, axis=-1, keepdims=True), sink)
    p = jnp.exp(s - m)
    den = jnp.sum(p, axis=-1, keepdims=True) + jnp.exp(sink - m)
    return _dot(p.astype(BF16), vv_bf) / den


def _swa_prompt_kernel(sinks_ref, qkv_ref, cos_ref, sa_ref, sb_ref, o_ref, kc_ref, vc_ref, kprev, vprev):
    n = pl.program_id(1)

    @pl.when(n == 0)
    def _():
        kprev[...] = jnp.zeros_like(kprev)
        vprev[...] = jnp.zeros_like(vprev)

    w = WINDOW
    cos, sa, sb = cos_ref[...], sa_ref[...], sb_ref[...]
    q = _rope(qkv_ref[:, 0:D_MODEL], cos, sa, sb).astype(BF16)
    k = _rope(qkv_ref[:, D_MODEL:D_MODEL + SW_KV], cos, sa, sb)
    v = qkv_ref[:, D_MODEL + SW_KV:D_MODEL + 2 * SW_KV]
    kc_ref[...] = k
    vc_ref[...] = v
    kk = jnp.concatenate([kprev[...], k], axis=0).astype(BF16)
    vv = jnp.concatenate([vprev[...], v], axis=0).astype(BF16)
    kprev[...] = k
    vprev[...] = v
    qi = lax.broadcasted_iota(jnp.int32, (w, 2 * w), 0) + w
    sj = lax.broadcasted_iota(jnp.int32, (w, 2 * w), 1)
    d = qi - sj
    mask = (d >= 0) & (d <= WINDOW) & ((sj >= w) | (n > 0))
    for kv in range(SW_KV_HEADS):
        kvc = slice(kv * SW_HEAD_DIM, (kv + 1) * SW_HEAD_DIM)
        for g in range(SW_GROUP):
            h = kv * SW_GROUP + g
            hc = slice(h * SW_HEAD_DIM, (h + 1) * SW_HEAD_DIM)
            o_ref[:, hc] = _attend(q[:, hc], kk[:, kvc], vv[:, kvc], mask, sinks_ref[h])


def _swa_prompt(sinks, qkv, cos, sa, sb, *, batch, seq):
    nb = seq // WINDOW
    row = lambda b, n: (b * nb + n, 0)
    tab = lambda b, n: (n, 0)
    per_b = lambda b, n: (b, 0, 0)
    return pl.pallas_call(
        _swa_prompt_kernel,
        out_shape=[jax.ShapeDtypeStruct((batch * seq, D_MODEL), F32),
                   jax.ShapeDtypeStruct((batch, WINDOW, SW_KV), F32),
                   jax.ShapeDtypeStruct((batch, WINDOW, SW_KV), F32)],
        grid=(batch, nb),
        in_specs=[pl.BlockSpec(memory_space=pltpu.SMEM),
                  pl.BlockSpec((WINDOW, D_MODEL + 2 * SW_KV), row),
                  pl.BlockSpec((WINDOW, LANE), tab),
                  pl.BlockSpec((WINDOW, LANE), tab),
                  pl.BlockSpec((WINDOW, LANE), tab)],
        out_specs=[pl.BlockSpec((WINDOW, D_MODEL), row),
                   pl.BlockSpec((None, WINDOW, SW_KV), per_b),
                   pl.BlockSpec((None, WINDOW, SW_KV), per_b)],
        scratch_shapes=[pltpu.VMEM((WINDOW, SW_KV), F32), pltpu.VMEM((WINDOW, SW_KV), F32)],
        compiler_params=_params(("arbitrary", "arbitrary")),
        name="swa_prompt",
    )(sinks, qkv, cos, sa, sb)


def _swa_sample_kernel(sinks_ref, qkv_ref, kbuf_ref, vbuf_ref, cos_ref, sa_ref, sb_ref,
                       o_ref, kc_ref, vc_ref, kk_s, vv_s, *, t_new):
    wb = WINDOW
    cos, sa, sb = cos_ref[...], sa_ref[...], sb_ref[...]
    q = _rope(qkv_ref[:, 0:D_MODEL], cos, sa, sb).astype(BF16)
    kk_s[0:wb, :] = kbuf_ref[...]
    vv_s[0:wb, :] = vbuf_ref[...]
    kk_s[wb:wb + SUBLANE, :] = _rope(qkv_ref[:, D_MODEL:D_MODEL + SW_KV], cos, sa, sb)
    vv_s[wb:wb + SUBLANE, :] = qkv_ref[:, D_MODEL + SW_KV:D_MODEL + 2 * SW_KV]
    kc_ref[...] = kk_s[pl.ds(t_new, wb), :]
    vc_ref[...] = vv_s[pl.ds(t_new, wb), :]
    kk = kk_s[...].astype(BF16)
    vv = vv_s[...].astype(BF16)
    rows = SW_GROUP * SUBLANE
    qt = lax.broadcasted_iota(jnp.int32, (rows, wb + SUBLANE), 0) % SUBLANE
    sj = lax.broadcasted_iota(jnp.int32, (rows, wb + SUBLANE), 1)
    d = qt + wb - sj
    mask = (d >= 0) & (d <= WINDOW)
    gi = lax.broadcasted_iota(jnp.int32, (rows, 1), 0) // SUBLANE
    for kv in range(SW_KV_HEADS):
        kvc = slice(kv * SW_HEAD_DIM, (kv + 1) * SW_HEAD_DIM)
        heads = [kv * SW_GROUP + g for g in range(SW_GROUP)]
        qs = jnp.concatenate([q[:, h * SW_HEAD_DIM:(h + 1) * SW_HEAD_DIM] for h in heads], axis=0)
        sink = jnp.zeros((rows, 1), F32)
        for g, h in enumerate(heads):
            sink = jnp.where(gi == g, sinks_ref[h], sink)
        o = _attend(qs, kk[:, kvc], vv[:, kvc], mask, sink)
        for g, h in enumerate(heads):
            o_ref[:, h * SW_HEAD_DIM:(h + 1) * SW_HEAD_DIM] = o[g * SUBLANE:(g + 1) * SUBLANE, :]


def _swa_sample(sinks, qkv, kbuf, vbuf, cos, sa, sb, *, batch, t_new):
    per_b = lambda b: (b, 0, 0)
    tab = lambda b: (0, 0)
    kern = functools.partial(_swa_sample_kernel, t_new=t_new)
    return pl.pallas_call(
        kern,
        out_shape=[jax.ShapeDtypeStruct((batch, SUBLANE, D_MODEL), F32),
                   jax.ShapeDtypeStruct((batch, WINDOW, SW_KV), F32),
                   jax.ShapeDtypeStruct((batch, WINDOW, SW_KV), F32)],
        grid=(batch,),
        in_specs=[pl.BlockSpec(memory_space=pltpu.SMEM),
                  pl.BlockSpec((None, SUBLANE, D_MODEL + 2 * SW_KV), per_b),
                  pl.BlockSpec((None, WINDOW, SW_KV), per_b),
                  pl.BlockSpec((None, WINDOW, SW_KV), per_b),
                  pl.BlockSpec((SUBLANE, LANE), tab),
                  pl.BlockSpec((SUBLANE, LANE), tab),
                  pl.BlockSpec((SUBLANE, LANE), tab)],
        out_specs=[pl.BlockSpec((None, SUBLANE, D_MODEL), per_b),
                   pl.BlockSpec((None, WINDOW, SW_KV), per_b),
                   pl.BlockSpec((None, WINDOW, SW_KV), per_b)],
        scratch_shapes=[pltpu.VMEM((WINDOW + SUBLANE, SW_KV), F32), pltpu.VMEM((WINDOW + SUBLANE, SW_KV), F32)],
        compiler_params=_params(("arbitrary",)),
        name="swa_sample",
    )(sinks, qkv, kbuf, vbuf, cos, sa, sb)


def _out_kernel(x_ref, g_ref, post_ref, ga_ref, gb_ref, odn_ref, osw_ref, w_ref, o_ref, *, per_token, tiles_per_seq):
    g = _mod_rows(g_ref, per_token, tiles_per_seq)
    y = jax.nn.sigmoid(ga_ref[...]) * odn_ref[...] + jax.nn.sigmoid(gb_ref[...]) * osw_ref[...]
    p = _dot(y.astype(BF16), w_ref[...])
    o_ref[...] = x_ref[...] + g * _rms(p, post_ref[...])


def _out(x, mod, per_token, seq_len, post, gates, o_dn, o_sw, w):
    n = x.shape[0]
    tm = min(512, n)
    kern = functools.partial(_out_kernel, per_token=per_token, tiles_per_seq=max(seq_len // tm, 1))
    row = lambda i: (i, 0)
    return pl.pallas_call(
        kern,
        out_shape=jax.ShapeDtypeStruct((n, D_MODEL), F32),
        grid=(n // tm,),
        in_specs=[pl.BlockSpec((tm, D_MODEL), row)]
        + _mod_specs(per_token, tm, (5,))
        + [_const_spec((1, D_MODEL)),
           pl.BlockSpec((tm, D_MODEL), lambda i: (i, 0)),
           pl.BlockSpec((tm, D_MODEL), lambda i: (i, 1)),
           pl.BlockSpec((tm, D_MODEL), row),
           pl.BlockSpec((tm, D_MODEL), row),
           _const_spec((D_MODEL, D_MODEL))],
        out_specs=pl.BlockSpec((tm, D_MODEL), row),
        compiler_params=_params(("arbitrary",)),
        name="out",
    )(x, mod, post, gates, gates, o_dn, o_sw, w)


def _rope_tables(pos):
    half = ROT_DIM // 2
    inv_freq = ROPE_THETA ** (-jnp.arange(half, dtype=F32) * (2.0 / ROT_DIM))
    ang = pos.astype(F32)[:, None] * inv_freq[None, :]
    cos, sin = jnp.cos(ang), jnp.sin(ang)
    n = pos.shape[0]
    rest = SW_HEAD_DIM - ROT_DIM
    c64 = jnp.concatenate([cos, cos, jnp.ones((n, rest), F32)], axis=1)
    a64 = jnp.concatenate([-sin, jnp.zeros((n, half + rest), F32)], axis=1)
    b64 = jnp.concatenate([jnp.zeros((n, half), F32), sin, jnp.zeros((n, rest), F32)], axis=1)
    rep = LANE // SW_HEAD_DIM
    return tuple(jnp.tile(t, (1, rep)) for t in (c64, a64, b64))


def _pad_rows(a, rows):
    return jnp.pad(a, ((0, rows - a.shape[0]),) + ((0, 0),) * (a.ndim - 1))


def _layer(x, mod, per_token, batch, seq, wts, conv0, s0, attend):
    x = _ffn(x, mod, 0, per_token, seq, wts["pre1"], wts["post1"], wts["wg1"], wts["wu1"], wts["wd1"])
    u, z, qkv, gates, ba = _proj(x, mod, per_token, seq, wts["pre2"], wts["w_in"])
    o_dn, conv_new, s_new = attend["dn"](u, z, ba, wts["conv_w"], wts["gparam"], wts["dn_norm"], conv0, s0)
    o_sw, k_new, v_new = attend["swa"](qkv)
    x = _out(x, mod, per_token, seq, wts["post2"], gates, o_dn, o_sw, wts["w_out"])
    x = _ffn(x, mod, 6, per_token, seq, wts["pre3"], wts["post3"], wts["wg2"], wts["wu2"], wts["wd2"])
    return x, (k_new, v_new, conv_new, s_new)


def kernel(x_prompt, x_sample, cache_swa_k, cache_swa_v, state_conv, state_delta, c_prompt, c_sample,
           w_ada, b_ada, ffn1_norm_pre, ffn1_norm_post, ffn1_w_gate, ffn1_w_up, ffn1_w_down,
           mix_norm_pre, mix_norm_post, w_in, conv_w, a_log, dt_bias, dn_norm, sinks, w_out,
           ffn2_norm_pre, ffn2_norm_post, ffn2_w_gate, ffn2_w_up, ffn2_w_down):
    depth = w_ada.shape[0]
    bp, tp, _ = x_prompt.shape
    bs, ts, _ = x_sample.shape
    assert tp % 512 == 0 and 1 <= ts <= SUBLANE
    dff = ffn1_w_gate.shape[-1]
    dff_pad = -(-dff // 256) * 256

    cos_p, sa_p, sb_p = _rope_tables(jnp.arange(tp))
    cos_s, sa_s, sb_s = (_pad_rows(t, SUBLANE) for t in _rope_tables(PAST_LEN + jnp.arange(ts)))

    xp = x_prompt.reshape(bp * tp, D_MODEL)
    xs = jnp.pad(x_sample, ((0, 0), (0, SUBLANE - ts), (0, 0))).reshape(bs * SUBLANE, D_MODEL)
    c_p = _pad_rows(c_prompt, SUBLANE)
    c_s = jnp.repeat(c_sample, SUBLANE, axis=0)

    outs_p, outs_s = [], []
    for l in range(depth):
        def wpad(w, axis):
            pad = [(0, 0), (0, 0)]
            pad[axis] = (0, dff_pad - dff)
            return jnp.pad(w, pad).astype(BF16)

        wi = w_in[l]
        w_in_p = jnp.concatenate(
            [wi[:, :4096], wi[:, 4112:7696], wi[:, 4096:4112], jnp.zeros((D_MODEL, P_TOTAL - 7696), F32)],
            axis=1).astype(BF16)
        gparam = jnp.zeros((SUBLANE, LANE), F32)
        gparam = gparam.at[0, DN_HEADS:2 * DN_HEADS].set(dt_bias[l]).at[1, DN_HEADS:2 * DN_HEADS].set(a_log[l])
        wts = dict(
            pre1=ffn1_norm_pre[l][None], post1=ffn1_norm_post[l][None],
            wg1=wpad(ffn1_w_gate[l], 1), wu1=wpad(ffn1_w_up[l], 1), wd1=wpad(ffn1_w_down[l], 0),
            pre2=mix_norm_pre[l][None], post2=mix_norm_post[l][None], w_in=w_in_p,
            conv_w=_pad_rows(conv_w[l], SUBLANE), gparam=gparam, dn_norm=dn_norm[l][None],
            w_out=w_out[l].astype(BF16),
            pre3=ffn2_norm_pre[l][None], post3=ffn2_norm_post[l][None],
            wg2=wpad(ffn2_w_gate[l], 1), wu2=wpad(ffn2_w_up[l], 1), wd2=wpad(ffn2_w_down[l], 0),
        )
        w_ada_bf = w_ada[l].astype(BF16)
        b_ada_l = b_ada[l][None]
        mod_p = _ada(c_p, w_ada_bf, b_ada_l)
        mod_s = _ada(c_s, w_ada_bf, b_ada_l)
        sink_l = sinks[l]

        conv0_p = jnp.zeros((bp, SUBLANE, CONV_CH), F32)
        s0_p = jnp.zeros((bp, DN_HEADS, DN_DK, DN_DV), F32)
        attend_p = dict(
            dn=functools.partial(_deltanet, batch=bp, seq=tp, tm=256, chunk=DN_CHUNK, valid=256),
            swa=lambda qkv: _swa_prompt(sink_l, qkv, cos_p, sa_p, sb_p, batch=bp, seq=tp),
        )
        xp, st_p = _layer(xp, mod_p, False, bp, tp, wts, conv0_p, s0_p, attend_p)

        conv0_s = jnp.pad(state_conv[l], ((0, 0), (SUBLANE - CONV_W + 1, 0), (0, 0)))
        kbuf = cache_swa_k[l].reshape(bs, WINDOW, SW_KV)
        vbuf = cache_swa_v[l].reshape(bs, WINDOW, SW_KV)
        attend_s = dict(
            dn=functools.partial(_deltanet, batch=bs, seq=SUBLANE, tm=SUBLANE, chunk=SUBLANE, valid=ts),
            swa=lambda qkv: _swa_sample(sink_l, qkv.reshape(bs, SUBLANE, -1), kbuf, vbuf, cos_s, sa_s, sb_s,
                                        batch=bs, t_new=ts),
        )

        def swa_s(qkv):
            o, k, v = attend_s["swa_raw"](qkv)
            return o.reshape(bs * SUBLANE, D_MODEL), k, v

        attend_s["swa_raw"] = attend_s["swa"]
        attend_s["swa"] = swa_s
        xs, st_s = _layer(xs, mod_s, True, bs, SUBLANE, wts, conv0_s, state_delta[l], attend_s)
        outs_p.append(st_p)
        outs_s.append(st_s)

    def stack(outs, batch, i):
        return jnp.stack([o[i] for o in outs])

    def kv5(a, batch):
        return a.reshape(depth, batch, WINDOW, SW_KV_HEADS, SW_HEAD_DIM)

    tail = slice(SUBLANE - CONV_W + 1, SUBLANE)
    y_p = xp.reshape(bp, tp, D_MODEL)
    y_s = xs.reshape(bs, SUBLANE, D_MODEL)[:, :ts]
    return (y_p, y_s,
            kv5(stack(outs_p, bp, 0), bp), kv5(stack(outs_p, bp, 1), bp),
            stack(outs_p, bp, 2)[:, :, tail], stack(outs_p, bp, 3),
            kv5(stack(outs_s, bs, 0), bs), kv5(stack(outs_s, bs, 1), bs),
            stack(outs_s, bs, 2)[:, :, tail], stack(outs_s, bs, 3))
```

```python
import functools
import math

import jax
import jax.numpy as jnp
from jax import lax
from jax.experimental import pallas as pl
from jax.experimental.pallas import tpu as pltpu

F32, BF16 = jnp.float32, jnp.bfloat16

D_MODEL = 1024
DN_HEADS, DN_DK, DN_DV = 8, 128, 128
CONV_W = 4
CONV_CH = 3 * D_MODEL
DN_CHUNK = 64
SW_HEAD_DIM, SW_HEADS, SW_KV_HEADS, SW_GROUP = 64, 16, 4, 4
SW_KV = SW_KV_HEADS * SW_HEAD_DIM
WINDOW = 128
ROT_DIM = 16
ROPE_THETA = 500000.0
PAST_LEN = 8192
EPS = 1e-6
NEG_INF = -1e30

LANE = 128
SUBLANE = 8
VMEM_LIMIT = 56 * 1024 * 1024

P_U, P_Z, P_QKV, P_G, P_BA = 0, 3072, 4096, 5632, 7680
P_TOTAL = 7808
HI = lax.Precision.HIGHEST


def _dot(a, b):
    return jnp.dot(a, b, preferred_element_type=F32)


def _dot_nt(a, b):
    return lax.dot_general(a, b, (((1,), (1,)), ((), ())), preferred_element_type=F32)


def _dot_tn(a, b):
    return lax.dot_general(a, b, (((0,), (0,)), ((), ())), preferred_element_type=F32)


def _dot_hi(a, b):
    return jnp.dot(a, b, preferred_element_type=F32, precision=HI)


def _silu(x):
    return x * jax.nn.sigmoid(x)


def _rms(x, gain):
    ms = jnp.mean(x * x, axis=-1, keepdims=True)
    return x * lax.rsqrt(ms + EPS) * gain


def _mod_rows(ref, per_token, tiles_per_seq):
    if per_token:
        return ref[...]
    b = pl.program_id(0) // tiles_per_seq
    return ref[pl.ds(b, 1), :]


def _const_spec(shape):
    return pl.BlockSpec(shape, lambda *_: (0,) * len(shape), pipeline_mode=pl.Buffered(1))


def _params(sem):
    return pltpu.CompilerParams(dimension_semantics=sem, vmem_limit_bytes=VMEM_LIMIT)


def _ada_kernel(c_ref, w_ref, b_ref, o_ref):
    h = _silu(c_ref[...]).astype(BF16)
    o_ref[...] = _dot(h, w_ref[...]) + b_ref[...]


def _ada(c, w_bf, b):
    rows, n = c.shape[0], w_bf.shape[1]
    return pl.pallas_call(
        _ada_kernel,
        out_shape=jax.ShapeDtypeStruct((rows, n), F32),
        grid=(n // D_MODEL,),
        in_specs=[
            pl.BlockSpec((rows, D_MODEL), lambda j: (0, 0)),
            pl.BlockSpec((D_MODEL, D_MODEL), lambda j: (0, j)),
            pl.BlockSpec((1, D_MODEL), lambda j: (0, j)),
        ],
        out_specs=pl.BlockSpec((rows, D_MODEL), lambda j: (0, j)),
        compiler_params=_params(("arbitrary",)),
        name="ada",
    )(c, w_bf, b)


def _mod_specs(per_token, tm, cols):
    if per_token:
        return [pl.BlockSpec((tm, D_MODEL), functools.partial(lambda i, c: (i, c), c=c)) for c in cols]
    return [pl.BlockSpec((SUBLANE, D_MODEL), functools.partial(lambda i, c: (0, c), c=c)) for c in cols]


def _ffn_kernel(x_ref, sh_ref, sc_ref, g_ref, pre_ref, post_ref, wg_ref, wu_ref, wd_ref, o_ref,
                *, per_token, tiles_per_seq):
    x = x_ref[...]
    sh = _mod_rows(sh_ref, per_token, tiles_per_seq)
    sc = _mod_rows(sc_ref, per_token, tiles_per_seq)
    g = _mod_rows(g_ref, per_token, tiles_per_seq)
    h = (_rms(x, pre_ref[...]) * (1.0 + sc) + sh).astype(BF16)
    act = (_silu(_dot(h, wg_ref[...])) * _dot(h, wu_ref[...])).astype(BF16)
    y = _dot(act, wd_ref[...])
    o_ref[...] = x + 0.5 * g * _rms(y, post_ref[...])


def _ffn(x, mod, col0, per_token, seq_len, pre, post, wg, wu, wd):
    n = x.shape[0]
    tm = min(512, n)
    dff = wg.shape[1]
    kern = functools.partial(_ffn_kernel, per_token=per_token, tiles_per_seq=max(seq_len // tm, 1))
    row = lambda i: (i, 0)
    return pl.pallas_call(
        kern,
        out_shape=jax.ShapeDtypeStruct((n, D_MODEL), F32),
        grid=(n // tm,),
        in_specs=[pl.BlockSpec((tm, D_MODEL), row)]
        + _mod_specs(per_token, tm, (col0, col0 + 1, col0 + 2))
        + [_const_spec((1, D_MODEL)), _const_spec((1, D_MODEL)),
           _const_spec((D_MODEL, dff)), _const_spec((D_MODEL, dff)), _const_spec((dff, D_MODEL))],
        out_specs=pl.BlockSpec((tm, D_MODEL), row),
        compiler_params=_params(("arbitrary",)),
        name="ffn",
    )(x, mod, mod, mod, pre, post, wg, wu, wd)


def _proj_kernel(x_ref, sh_ref, sc_ref, pre_ref, w_ref, u_ref, z_ref, qkv_ref, g_ref, ba_ref,
                 *, per_token, tiles_per_seq):
    sh = _mod_rows(sh_ref, per_token, tiles_per_seq)
    sc = _mod_rows(sc_ref, per_token, tiles_per_seq)
    h = (_rms(x_ref[...], pre_ref[...]) * (1.0 + sc) + sh).astype(BF16)
    u_ref[...] = _dot(h, w_ref[:, P_U:P_Z])
    z_ref[...] = _dot(h, w_ref[:, P_Z:P_QKV])
    qkv_ref[...] = _dot(h, w_ref[:, P_QKV:P_G])
    g_ref[...] = _dot(h, w_ref[:, P_G:P_BA])
    ba_ref[...] = _dot(h, w_ref[:, P_BA:P_TOTAL])


def _proj(x, mod, per_token, seq_len, pre, w):
    n = x.shape[0]
    tm = min(256, n)
    kern = functools.partial(_proj_kernel, per_token=per_token, tiles_per_seq=max(seq_len // tm, 1))
    row = lambda i: (i, 0)
    widths = (P_Z - P_U, P_QKV - P_Z, P_G - P_QKV, P_BA - P_G, P_TOTAL - P_BA)
    return pl.pallas_call(
        kern,
        out_shape=[jax.ShapeDtypeStruct((n, wd), F32) for wd in widths],
        grid=(n // tm,),
        in_specs=[pl.BlockSpec((tm, D_MODEL), row)]
        + _mod_specs(per_token, tm, (3, 4))
        + [_const_spec((1, D_MODEL)), _const_spec((D_MODEL, P_TOTAL))],
        out_specs=[pl.BlockSpec((tm, wd), row) for wd in widths],
        compiler_params=_params(("arbitrary",)),
        name="proj",
    )(x, mod, mod, pre, w)


def _mm_bf(a, b):
    return _dot(a.astype(BF16), b.astype(BF16))


def _tri_inv_all(lows, c):
    ii = lax.broadcasted_iota(jnp.int32, (c, c), 0)
    jj = lax.broadcasted_iota(jnp.int32, (c, c), 1)
    eye = jnp.where(ii == jj, 1.0, 0.0).astype(F32)
    xs = [eye - jnp.where(ii // 2 == jj // 2, low, 0.0) for low in lows]
    b = 2
    while b < c:
        join = (ii // (2 * b) == jj // (2 * b)) & (ii // b != jj // b)
        ys = [_mm_bf(x, jnp.where(join, low, 0.0)) for x, low in zip(xs, lows)]
        xs = [x - _mm_bf(y, x) for x, y in zip(xs, ys)]
        b *= 2
    return xs


def _dn_kernel(u_ref, z_ref, ba_ref, cw_ref, gp_ref, dnn_ref, conv0_ref, s0_ref,
               o_ref, convo_ref, so_ref, ubuf, q_s, k_s, v_s, bg_s, u_s, w_s, qe_s, kd_s, a_s, eg_s,
               *, tm, chunk, valid):
    t = pl.program_id(1)

    @pl.when(t == 0)
    def _():
        ubuf[0:SUBLANE, :] = conv0_ref[...]
        so_ref[...] = s0_ref[...]

    ubuf[SUBLANE:SUBLANE + tm, :] = u_ref[...]
    for cb in range(CONV_CH // LANE):
        cols = slice(cb * LANE, (cb + 1) * LANE)
        acc = ubuf[pl.ds(SUBLANE - CONV_W + 1, tm), cols] * cw_ref[0:1, cols]
        for i in range(1, CONV_W):
            acc = acc + ubuf[pl.ds(SUBLANE - CONV_W + 1 + i, tm), cols] * cw_ref[i:i + 1, cols]
        a = _silu(acc)
        sec, off = divmod(cb * LANE, D_MODEL)
        dst = slice(off, off + LANE)
        if sec == 2:
            v_s[:, dst] = a
        else:
            an = a * lax.rsqrt(jnp.sum(a * a, axis=-1, keepdims=True) + EPS)
            if sec == 0:
                q_s[:, dst] = an * (DN_DK ** -0.5)
            else:
                k_s[:, dst] = an
    ubuf[0:SUBLANE, :] = ubuf[pl.ds(valid, SUBLANE), :]
    convo_ref[...] = ubuf[0:SUBLANE, :]

    ba = ba_ref[...]
    lane = lax.broadcasted_iota(jnp.int32, (tm, LANE), 1)
    rowi = lax.broadcasted_iota(jnp.int32, (tm, LANE), 0)
    xg = ba + gp_ref[0:1, :]
    softplus = jnp.maximum(xg, 0.0) + jnp.log1p(jnp.exp(-jnp.abs(xg)))
    gdec = -jnp.exp(gp_ref[1:2, :]) * softplus
    bg = jnp.where(lane < DN_HEADS, jax.nn.sigmoid(ba), gdec)
    bg_s[...] = jnp.where(rowi < valid, bg, 0.0)

    c = chunk
    ii = lax.broadcasted_iota(jnp.int32, (c, c), 0)
    jj = lax.broadcasted_iota(jnp.int32, (c, c), 1)
    incl = ii >= jj
    strict = ii > jj
    tril = jnp.where(incl, 1.0, 0.0).astype(F32)
    sel = jnp.where(lax.broadcasted_iota(jnp.int32, (SUBLANE, LANE), 1)
                    == lax.broadcasted_iota(jnp.int32, (SUBLANE, LANE), 0) + DN_HEADS, 1.0, 0.0).astype(F32)
    dnn = dnn_ref[...]

    heads = range(DN_HEADS)
    hcols = [slice(h * DN_DK, (h + 1) * DN_DK) for h in heads]
    acols = [slice(h * c, (h + 1) * c) for h in heads]
    n_chunks = tm // c

    def chunk_rows(ci):
        return pl.ds(0, c) if n_chunks == 1 else pl.ds(pl.multiple_of(ci * c, c), c)

    def prep_body(ci, carry):
        rows = chunk_rows(ci)
        bgc = bg_s[rows, :]
        gcum = _dot_hi(tril, bgc)
        gcum_t = lax.dot_general(sel, gcum, (((1,), (1,)), ((), ())),
                                 preferred_element_type=F32, precision=HI)
        eg_s[ci] = jnp.broadcast_to(jnp.exp(gcum_t[:, c - 1:c]), (DN_HEADS, DN_DV))
        beta = [bgc[:, h:h + 1] for h in heads]
        gc = [gcum[:, DN_HEADS + h:DN_HEADS + h + 1] for h in heads]
        decay = [jnp.where(incl, jnp.exp(jnp.where(incl, gc[h] - gcum_t[h:h + 1, :], 0.0)), 0.0) for h in heads]
        q = [q_s[rows, hcols[h]] for h in heads]
        k = [k_s[rows, hcols[h]] for h in heads]
        kb = [k[h] * beta[h] for h in heads]
        k_bf = [k[h].astype(BF16) for h in heads]
        akk = [_dot_nt(kb[h].astype(BF16), k_bf[h]) for h in heads]
        aqk = [_dot_nt(q[h].astype(BF16), k_bf[h]) for h in heads]
        tinv = _tri_inv_all([jnp.where(strict, akk[h] * decay[h], 0.0) for h in heads], c)
        egc = [jnp.exp(gc[h]) for h in heads]
        rhs = [jnp.concatenate([v_s[rows, hcols[h]] * beta[h], kb[h] * egc[h]], axis=1) for h in heads]
        uw = [_mm_bf(tinv[h], rhs[h]) for h in heads]
        for h in heads:
            u_s[rows, hcols[h]] = uw[h][:, :DN_DV]
            w_s[rows, hcols[h]] = uw[h][:, DN_DV:].astype(BF16)
            qe_s[rows, hcols[h]] = (q[h] * egc[h]).astype(BF16)
            kd_s[rows, hcols[h]] = (k[h] * jnp.exp(gc[h][c - 1:c, :] - gc[h])).astype(BF16)
            a_s[rows, acols[h]] = jnp.where(incl, aqk[h] * decay[h], 0.0).astype(BF16)
        return carry

    def state_body(ci, carry):
        rows = chunk_rows(ci)
        egb = eg_s[ci]
        s_old = [so_ref[h] for h in heads]
        s_bf = [s.astype(BF16) for s in s_old]
        ws = [_dot(w_s[rows, hcols[h]], s_bf[h]) for h in heads]
        qs = [_dot(qe_s[rows, hcols[h]], s_bf[h]) for h in heads]
        vn_bf = [(u_s[rows, hcols[h]] - ws[h]).astype(BF16) for h in heads]
        av = [_dot(a_s[rows, acols[h]], vn_bf[h]) for h in heads]
        kv = [_dot_tn(kd_s[rows, hcols[h]], vn_bf[h]) for h in heads]
        for h in heads:
            so_ref[h] = s_old[h] * egb[h:h + 1, :] + kv[h]
            o_ref[rows, hcols[h]] = _rms(qs[h] + av[h], dnn) * _silu(z_ref[rows, hcols[h]])
        return carry

    if n_chunks == 1:
        prep_body(0, 0)
        state_body(0, 0)
    else:
        lax.fori_loop(0, n_chunks, prep_body, 0)
        lax.fori_loop(0, n_chunks, state_body, 0)


def _deltanet(u, z, ba, cw, gp, dnn, conv0, s0, *, batch, seq, tm, chunk, valid):
    nt = seq // tm
    row = lambda b, t: (b * nt + t, 0)
    per_b3 = lambda b, t: (b, 0, 0)
    per_b4 = lambda b, t: (b, 0, 0, 0)
    kern = functools.partial(_dn_kernel, tm=tm, chunk=chunk, valid=valid)
    return pl.pallas_call(
        kern,
        out_shape=[jax.ShapeDtypeStruct((batch * seq, D_MODEL), F32),
                   jax.ShapeDtypeStruct((batch, SUBLANE, CONV_CH), F32),
                   jax.ShapeDtypeStruct((batch, DN_HEADS, DN_DK, DN_DV), F32)],
        grid=(batch, nt),
        in_specs=[pl.BlockSpec((tm, CONV_CH), row),
                  pl.BlockSpec((tm, D_MODEL), row),
                  pl.BlockSpec((tm, LANE), row),
                  pl.BlockSpec((SUBLANE, CONV_CH), lambda b, t: (0, 0)),
                  pl.BlockSpec((SUBLANE, LANE), lambda b, t: (0, 0)),
                  pl.BlockSpec((1, DN_DV), lambda b, t: (0, 0)),
                  pl.BlockSpec((None, SUBLANE, CONV_CH), per_b3),
                  pl.BlockSpec((None, DN_HEADS, DN_DK, DN_DV), per_b4)],
        out_specs=[pl.BlockSpec((tm, D_MODEL), row),
                   pl.BlockSpec((None, SUBLANE, CONV_CH), per_b3),
                   pl.BlockSpec((None, DN_HEADS, DN_DK, DN_DV), per_b4)],
        scratch_shapes=[pltpu.VMEM((SUBLANE + tm, CONV_CH), F32),
                        pltpu.VMEM((tm, D_MODEL), F32),
                        pltpu.VMEM((tm, D_MODEL), F32),
                        pltpu.VMEM((tm, D_MODEL), F32),
                        pltpu.VMEM((tm, LANE), F32),
                        pltpu.VMEM((tm, D_MODEL), F32),
                        pltpu.VMEM((tm, D_MODEL), BF16),
                        pltpu.VMEM((tm, D_MODEL), BF16),
                        pltpu.VMEM((tm, D_MODEL), BF16),
                        pltpu.VMEM((tm, DN_HEADS * chunk), BF16),
                        pltpu.VMEM((tm // chunk, DN_HEADS, DN_DV), F32)],
        compiler_params=_params(("arbitrary", "arbitrary")),
        name="deltanet",
    )(u, z, ba, cw, gp, dnn, conv0, s0)


def _rope(x, cos, sa, sb):
    cols = []
    for cb in range(x.shape[1] // LANE):
        xc = x[:, cb * LANE:(cb + 1) * LANE]
        cols.append(xc * cos + pltpu.roll(xc, LANE - ROT_DIM // 2, 1) * sa + pltpu.roll(xc, ROT_DIM // 2, 1) * sb)
    return cols[0] if len(cols) == 1 else jnp.concatenate(cols, axis=1)


def _attend(q_bf, kk_bf, vv_bf, mask, sink):
    s = _dot_nt(q_bf, kk_bf) * (SW_HEAD_DIM ** -0.5)
    s = jnp.where(mask, s, NEG_INF)
    m = jnp.maximum(jnp.max(s, axis=-1, keepdims=True), sink)
    p = jnp.exp(s - m)
    den = jnp.sum(p, axis=-1, keepdims=True) + jnp.exp(sink - m)
    return _dot(p.astype(BF16), vv_bf) / den


def _swa_prompt_kernel(sinks_ref, qkv_ref, cos_ref, sa_ref, sb_ref, o_ref, kc_ref, vc_ref, kprev, vprev):
    n = pl.program_id(1)

    @pl.when(n == 0)
    def _():
        kprev[...] = jnp.zeros_like(kprev)
        vprev[...] = jnp.zeros_like(vprev)

    w = WINDOW
    cos, sa, sb = cos_ref[...], sa_ref[...], sb_ref[...]
    q = _rope(qkv_ref[:, 0:D_MODEL], cos, sa, sb).astype(BF16)
    k = _rope(qkv_ref[:, D_MODEL:D_MODEL + SW_KV], cos, sa, sb)
    v = qkv_ref[:, D_MODEL + SW_KV:D_MODEL + 2 * SW_KV]
    kc_ref[...] = k
    vc_ref[...] = v
    kk = jnp.concatenate([kprev[...], k], axis=0).astype(BF16)
    vv = jnp.concatenate([vprev[...], v], axis=0).astype(BF16)
    kprev[...] = k
    vprev[...] = v
    qi = lax.broadcasted_iota(jnp.int32, (w, 2 * w), 0) + w
    sj = lax.broadcasted_iota(jnp.int32, (w, 2 * w), 1)
    d = qi - sj
    mask = (d >= 0) & (d <= WINDOW) & ((sj >= w) | (n > 0))
    for kv in range(SW_KV_HEADS):
        kvc = slice(kv * SW_HEAD_DIM, (kv + 1) * SW_HEAD_DIM)
        for g in range(SW_GROUP):
            h = kv * SW_GROUP + g
            hc = slice(h * SW_HEAD_DIM, (h + 1) * SW_HEAD_DIM)
            o_ref[:, hc] = _attend(q[:, hc], kk[:, kvc], vv[:, kvc], mask, sinks_ref[h])


def _swa_prompt(sinks, qkv, cos, sa, sb, *, batch, seq):
    nb = seq // WINDOW
    row = lambda b, n: (b * nb + n, 0)
    tab = lambda b, n: (n, 0)
    per_b = lambda b, n: (b, 0, 0)
    return pl.pallas_call(
        _swa_prompt_kernel,
        out_shape=[jax.ShapeDtypeStruct((batch * seq, D_MODEL), F32),
                   jax.ShapeDtypeStruct((batch, WINDOW, SW_KV), F32),
                   jax.ShapeDtypeStruct((batch, WINDOW, SW_KV), F32)],
        grid=(batch, nb),
        in_specs=[pl.BlockSpec(memory_space=pltpu.SMEM),
                  pl.BlockSpec((WINDOW, D_MODEL + 2 * SW_KV), row),
                  pl.BlockSpec((WINDOW, LANE), tab),
                  pl.BlockSpec((WINDOW, LANE), tab),
                  pl.BlockSpec((WINDOW, LANE), tab)],
        out_specs=[pl.BlockSpec((WINDOW, D_MODEL), row),
                   pl.BlockSpec((None, WINDOW, SW_KV), per_b),
                   pl.BlockSpec((None, WINDOW, SW_KV), per_b)],
        scratch_shapes=[pltpu.VMEM((WINDOW, SW_KV), F32), pltpu.VMEM((WINDOW, SW_KV), F32)],
        compiler_params=_params(("arbitrary", "arbitrary")),
        name="swa_prompt",
    )(sinks, qkv, cos, sa, sb)


def _swa_sample_kernel(sinks_ref, qkv_ref, kbuf_ref, vbuf_ref, cos_ref, sa_ref, sb_ref,
                       o_ref, kc_ref, vc_ref, kk_s, vv_s, *, t_new):
    wb = WINDOW
    cos, sa, sb = cos_ref[...], sa_ref[...], sb_ref[...]
    q = _rope(qkv_ref[:, 0:D_MODEL], cos, sa, sb).astype(BF16)
    kk_s[0:wb, :] = kbuf_ref[...]
    vv_s[0:wb, :] = vbuf_ref[...]
    kk_s[wb:wb + SUBLANE, :] = _rope(qkv_ref[:, D_MODEL:D_MODEL + SW_KV], cos, sa, sb)
    vv_s[wb:wb + SUBLANE, :] = qkv_ref[:, D_MODEL + SW_KV:D_MODEL + 2 * SW_KV]
    kc_ref[...] = kk_s[pl.ds(t_new, wb), :]
    vc_ref[...] = vv_s[pl.ds(t_new, wb), :]
    kk = kk_s[...].astype(BF16)
    vv = vv_s[...].astype(BF16)
    rows = SW_GROUP * SUBLANE
    qt = lax.broadcasted_iota(jnp.int32, (rows, wb + SUBLANE), 0) % SUBLANE
    sj = lax.broadcasted_iota(jnp.int32, (rows, wb + SUBLANE), 1)
    d = qt + wb - sj
    mask = (d >= 0) & (d <= WINDOW)
    gi = lax.broadcasted_iota(jnp.int32, (rows, 1), 0) // SUBLANE
    for kv in range(SW_KV_HEADS):
        kvc = slice(kv * SW_HEAD_DIM, (kv + 1) * SW_HEAD_DIM)
        heads = [kv * SW_GROUP + g for g in range(SW_GROUP)]
        qs = jnp.concatenate([q[:, h * SW_HEAD_DIM:(h + 1) * SW_HEAD_DIM] for h in heads], axis=0)
        sink = jnp.zeros((rows, 1), F32)
        for g, h in enumerate(heads):
            sink = jnp.where(gi == g, sinks_ref[h], sink)
        o = _attend(qs, kk[:, kvc], vv[:, kvc], mask, sink)
        for g, h in enumerate(heads):
            o_ref[:, h * SW_HEAD_DIM:(h + 1) * SW_HEAD_DIM] = o[g * SUBLANE:(g + 1) * SUBLANE, :]


def _swa_sample(sinks, qkv, kbuf, vbuf, cos, sa, sb, *, batch, t_new):
    per_b = lambda b: (b, 0, 0)
    tab = lambda b: (0, 0)
    kern = functools.partial(_swa_sample_kernel, t_new=t_new)
    return pl.pallas_call(
        kern,
        out_shape=[jax.ShapeDtypeStruct((batch, SUBLANE, D_MODEL), F32),
                   jax.ShapeDtypeStruct((batch, WINDOW, SW_KV), F32),
                   jax.ShapeDtypeStruct((batch, WINDOW, SW_KV), F32)],
        grid=(batch,),
        in_specs=[pl.BlockSpec(memory_space=pltpu.SMEM),
                  pl.BlockSpec((None, SUBLANE, D_MODEL + 2 * SW_KV), per_b),
                  pl.BlockSpec((None, WINDOW, SW_KV), per_b),
                  pl.BlockSpec((None, WINDOW, SW_KV), per_b),
                  pl.BlockSpec((SUBLANE, LANE), tab),
                  pl.BlockSpec((SUBLANE, LANE), tab),
                  pl.BlockSpec((SUBLANE, LANE), tab)],
        out_specs=[pl.BlockSpec((None, SUBLANE, D_MODEL), per_b),
                   pl.BlockSpec((None, WINDOW, SW_KV), per_b),
                   pl.BlockSpec((None, WINDOW, SW_KV), per_b)],
        scratch_shapes=[pltpu.VMEM((WINDOW + SUBLANE, SW_KV), F32), pltpu.VMEM((WINDOW + SUBLANE, SW_KV), F32)],
        compiler_params=_params(("arbitrary",)),
        name="swa_sample",
    )(sinks, qkv, kbuf, vbuf, cos, sa, sb)


def _out_kernel(x_ref, g_ref, post_ref, ga_ref, gb_ref, odn_ref, osw_ref, w_ref, o_ref, *, per_token, tiles_per_seq):
    g = _mod_rows(g_ref, per_token, tiles_per_seq)
    y = jax.nn.sigmoid(ga_ref[...]) * odn_ref[...] + jax.nn.sigmoid(gb_ref[...]) * osw_ref[...]
    p = _dot(y.astype(BF16), w_ref[...])
    o_ref[...] = x_ref[...] + g * _rms(p, post_ref[...])


def _out(x, mod, per_token, seq_len, post, gates, o_dn, o_sw, w):
    n = x.shape[0]
    tm = min(512, n)
    kern = functools.partial(_out_kernel, per_token=per_token, tiles_per_seq=max(seq_len // tm, 1))
    row = lambda i: (i, 0)
    return pl.pallas_call(
        kern,
        out_shape=jax.ShapeDtypeStruct((n, D_MODEL), F32),
        grid=(n // tm,),
        in_specs=[pl.BlockSpec((tm, D_MODEL), row)]
        + _mod_specs(per_token, tm, (5,))
        + [_const_spec((1, D_MODEL)),
           pl.BlockSpec((tm, D_MODEL), lambda i: (i, 0)),
           pl.BlockSpec((tm, D_MODEL), lambda i: (i, 1)),
           pl.BlockSpec((tm, D_MODEL), row),
           pl.BlockSpec((tm, D_MODEL), row),
           _const_spec((D_MODEL, D_MODEL))],
        out_specs=pl.BlockSpec((tm, D_MODEL), row),
        compiler_params=_params(("arbitrary",)),
        name="out",
    )(x, mod, post, gates, gates, o_dn, o_sw, w)


def _rope_tables(pos):
    half = ROT_DIM // 2
    inv_freq = ROPE_THETA ** (-jnp.arange(half, dtype=F32) * (2.0 / ROT_DIM))
    ang = pos.astype(F32)[:, None] * inv_freq[None, :]
    cos, sin = jnp.cos(ang), jnp.sin(ang)
    n = pos.shape[0]
    rest = SW_HEAD_DIM - ROT_DIM
    c64 = jnp.concatenate([cos, cos, jnp.ones((n, rest), F32)], axis=1)
    a64 = jnp.concatenate([-sin, jnp.zeros((n, half + rest), F32)], axis=1)
    b64 = jnp.concatenate([jnp.zeros((n, half), F32), sin, jnp.zeros((n, rest), F32)], axis=1)
    rep = LANE // SW_HEAD_DIM
    return tuple(jnp.tile(t, (1, rep)) for t in (c64, a64, b64))


def _pad_rows(a, rows):
    return jnp.pad(a, ((0, rows - a.shape[0]),) + ((0, 0),) * (a.ndim - 1))


def _layer(x, mod, per_token, batch, seq, wts, conv0, s0, attend):
    x = _ffn(x, mod, 0, per_token, seq, wts["pre1"], wts["post1"], wts["wg1"], wts["wu1"], wts["wd1"])
    u, z, qkv, gates, ba = _proj(x, mod, per_token, seq, wts["pre2"], wts["w_in"])
    o_dn, conv_new, s_new = attend["dn"](u, z, ba, wts["conv_w"], wts["gparam"], wts["dn_norm"], conv0, s0)
    o_sw, k_new, v_new = attend["swa"](qkv)
    x = _out(x, mod, per_token, seq, wts["post2"], gates, o_dn, o_sw, wts["w_out"])
    x = _ffn(x, mod, 6, per_token, seq, wts["pre3"], wts["post3"], wts["wg2"], wts["wu2"], wts["wd2"])
    return x, (k_new, v_new, conv_new, s_new)


def kernel(x_prompt, x_sample, cache_swa_k, cache_swa_v, state_conv, state_delta, c_prompt, c_sample,
           w_ada, b_ada, ffn1_norm_pre, ffn1_norm_post, ffn1_w_gate, ffn1_w_up, ffn1_w_down,
           mix_norm_pre, mix_norm_post, w_in, conv_w, a_log, dt_bias, dn_norm, sinks, w_out,
           ffn2_norm_pre, ffn2_norm_post, ffn2_w_gate, ffn2_w_up, ffn2_w_down):
    depth = w_ada.shape[0]
    bp, tp, _ = x_prompt.shape
    bs, ts, _ = x_sample.shape
    assert tp % 512 == 0 and 1 <= ts <= SUBLANE
    dff = ffn1_w_gate.shape[-1]
    dff_pad = -(-dff // 256) * 256

    cos_p, sa_p, sb_p = _rope_tables(jnp.arange(tp))
    cos_s, sa_s, sb_s = (_pad_rows(t, SUBLANE) for t in _rope_tables(PAST_LEN + jnp.arange(ts)))

    xp = x_prompt.reshape(bp * tp, D_MODEL)
    xs = jnp.pad(x_sample, ((0, 0), (0, SUBLANE - ts), (0, 0))).reshape(bs * SUBLANE, D_MODEL)
    c_p = _pad_rows(c_prompt, SUBLANE)
    c_s = jnp.repeat(c_sample, SUBLANE, axis=0)

    outs_p, outs_s = [], []
    for l in range(depth):
        def wpad(w, axis):
            pad = [(0, 0), (0, 0)]
            pad[axis] = (0, dff_pad - dff)
            return jnp.pad(w, pad).astype(BF16)

        wi = w_in[l]
        w_in_p = jnp.concatenate(
            [wi[:, :4096], wi[:, 4112:7696], wi[:, 4096:4112], jnp.zeros((D_MODEL, P_TOTAL - 7696), F32)],
            axis=1).astype(BF16)
        gparam = jnp.zeros((SUBLANE, LANE), F32)
        gparam = gparam.at[0, DN_HEADS:2 * DN_HEADS].set(dt_bias[l]).at[1, DN_HEADS:2 * DN_HEADS].set(a_log[l])
        wts = dict(
            pre1=ffn1_norm_pre[l][None], post1=ffn1_norm_post[l][None],
            wg1=wpad(ffn1_w_gate[l], 1), wu1=wpad(ffn1_w_up[l], 1), wd1=wpad(ffn1_w_down[l], 0),
            pre2=mix_norm_pre[l][None], post2=mix_norm_post[l][None], w_in=w_in_p,
            conv_w=_pad_rows(conv_w[l], SUBLANE), gparam=gparam, dn_norm=dn_norm[l][None],
            w_out=w_out[l].astype(BF16),
            pre3=ffn2_norm_pre[l][None], post3=ffn2_norm_post[l][None],
            wg2=wpad(ffn2_w_gate[l], 1), wu2=wpad(ffn2_w_up[l], 1), wd2=wpad(ffn2_w_down[l], 0),
        )
        w_ada_bf = w_ada[l].astype(BF16)
        b_ada_l = b_ada[l][None]
        mod_p = _ada(c_p, w_ada_bf, b_ada_l)
        mod_s = _ada(c_s, w_ada_bf, b_ada_l)
        sink_l = sinks[l]

        conv0_p = jnp.zeros((bp, SUBLANE, CONV_CH), F32)
        s0_p = jnp.zeros((bp, DN_HEADS, DN_DK, DN_DV), F32)
        attend_p = dict(
            dn=functools.partial(_deltanet, batch=bp, seq=tp, tm=256, chunk=DN_CHUNK, valid=256),
            swa=lambda qkv: _swa_prompt(sink_l, qkv, cos_p, sa_p, sb_p, batch=bp, seq=tp),
        )
        xp, st_p = _layer(xp, mod_p, False, bp, tp, wts, conv0_p, s0_p, attend_p)

        conv0_s = jnp.pad(state_conv[l], ((0, 0), (SUBLANE - CONV_W + 1, 0), (0, 0)))
        kbuf = cache_swa_k[l].reshape(bs, WINDOW, SW_KV)
        vbuf = cache_swa_v[l].reshape(bs, WINDOW, SW_KV)
        attend_s = dict(
            dn=functools.partial(_deltanet, batch=bs, seq=SUBLANE, tm=SUBLANE, chunk=SUBLANE, valid=ts),
            swa=lambda qkv: _swa_sample(sink_l, qkv.reshape(bs, SUBLANE, -1), kbuf, vbuf, cos_s, sa_s, sb_s,
                                        batch=bs, t_new=ts),
        )

        def swa_s(qkv):
            o, k, v = attend_s["swa_raw"](qkv)
            return o.reshape(bs * SUBLANE, D_MODEL), k, v

        attend_s["swa_raw"] = attend_s["swa"]
        attend_s["swa"] = swa_s
        xs, st_s = _layer(xs, mod_s, True, bs, SUBLANE, wts, conv0_s, state_delta[l], attend_s)
        outs_p.append(st_p)
        outs_s.append(st_s)

    def stack(outs, batch, i):
        return jnp.stack([o[i] for o in outs])

    def kv5(a, batch):
        return a.reshape(depth, batch, WINDOW, SW_KV_HEADS, SW_HEAD_DIM)

    tail = slice(SUBLANE - CONV_W + 1, SUBLANE)
    y_p = xp.reshape(bp, tp, D_MODEL)
    y_s = xs.reshape(bs, SUBLANE, D_MODEL)[:, :ts]
    return (y_p, y_s,
            kv5(stack(outs_p, bp, 0), bp), kv5(stack(outs_p, bp, 1), bp),
            stack(outs_p, bp, 2)[:, :, tail], stack(outs_p, bp, 3),
            kv5(stack(outs_s, bs, 0), bs), kv5(stack(outs_s, bs, 1), bs),
            stack(outs_s, bs, 2)[:, :, tail], stack(outs_s, bs, 3))
```

```python
import functools

import jax
import jax.numpy as jnp
from jax import lax
from jax.experimental import pallas as pl
from jax.experimental.pallas import tpu as pltpu

F32, BF16 = jnp.float32, jnp.bfloat16

D_MODEL = 1024
DN_HEADS, DN_DK, DN_DV = 8, 128, 128
CONV_W = 4
CONV_CH = 3 * D_MODEL
DN_CHUNK = 64
SW_HEAD_DIM, SW_HEADS, SW_KV_HEADS, SW_GROUP = 64, 16, 4, 4
SW_KV = SW_KV_HEADS * SW_HEAD_DIM
WINDOW = 128
ROT_DIM = 16
ROPE_THETA = 500000.0
PAST_LEN = 8192
EPS = 1e-6
NEG_INF = -1e30

LANE = 128
SUBLANE = 8
VMEM_LIMIT = 56 * 1024 * 1024

P_U, P_Z, P_QKV, P_G, P_BA = 0, 3072, 4096, 5632, 7680
P_TOTAL = 7808
HI = lax.Precision.HIGHEST

DN_TILE = 256
DN_GROUP = 4
SAMPLE_SEQS = 4


def _dot(a, b):
    return jnp.dot(a, b, preferred_element_type=F32)


def _dot_nt(a, b):
    return lax.dot_general(a, b, (((1,), (1,)), ((), ())), preferred_element_type=F32)


def _dot_tn(a, b):
    return lax.dot_general(a, b, (((0,), (0,)), ((), ())), preferred_element_type=F32)


def _dot_hi(a, b):
    return jnp.dot(a, b, preferred_element_type=F32, precision=HI)


def _mm_bf(a, b):
    return _dot(a.astype(BF16), b.astype(BF16))


def _silu(x):
    return x * jax.nn.sigmoid(x)


def _rms(x, gain):
    ms = jnp.mean(x * x, axis=-1, keepdims=True)
    return x * lax.rsqrt(ms + EPS) * gain


def _mod_rows(ref, per_token, tiles_per_seq):
    if per_token:
        return ref[...]
    b = pl.program_id(0) // tiles_per_seq
    return ref[pl.ds(b, 1), :]


def _const_spec(shape):
    return pl.BlockSpec(shape, lambda *_: (0,) * len(shape), pipeline_mode=pl.Buffered(1))


def _params(sem):
    return pltpu.CompilerParams(dimension_semantics=sem, vmem_limit_bytes=VMEM_LIMIT)


def _ada_kernel(c_ref, w_ref, b_ref, o_ref):
    h = _silu(c_ref[...]).astype(BF16)
    o_ref[...] = _dot(h, w_ref[...]) + b_ref[...]


def _ada(c, w_bf, b):
    rows, n = c.shape[0], w_bf.shape[1]
    return pl.pallas_call(
        _ada_kernel,
        out_shape=jax.ShapeDtypeStruct((rows, n), F32),
        grid=(n // D_MODEL,),
        in_specs=[
            pl.BlockSpec((rows, D_MODEL), lambda j: (0, 0)),
            pl.BlockSpec((D_MODEL, D_MODEL), lambda j: (0, j)),
            pl.BlockSpec((1, D_MODEL), lambda j: (0, j)),
        ],
        out_specs=pl.BlockSpec((rows, D_MODEL), lambda j: (0, j)),
        compiler_params=_params(("arbitrary",)),
        name="ada",
    )(c, w_bf, b)


def _mod_specs(per_token, tm, cols):
    if per_token:
        return [pl.BlockSpec((tm, D_MODEL), functools.partial(lambda i, c: (i, c), c=c)) for c in cols]
    return [pl.BlockSpec((SUBLANE, D_MODEL), functools.partial(lambda i, c: (0, c), c=c)) for c in cols]


def _ffn_kernel(x_ref, sh_ref, sc_ref, g_ref, pre_ref, post_ref, wg_ref, wu_ref, wd_ref, o_ref,
                *, per_token, tiles_per_seq):
    x = x_ref[...]
    sh = _mod_rows(sh_ref, per_token, tiles_per_seq)
    sc = _mod_rows(sc_ref, per_token, tiles_per_seq)
    g = _mod_rows(g_ref, per_token, tiles_per_seq)
    h = (_rms(x, pre_ref[...]) * (1.0 + sc) + sh).astype(BF16)
    act = (_silu(_dot(h, wg_ref[...])) * _dot(h, wu_ref[...])).astype(BF16)
    y = _dot(act, wd_ref[...])
    o_ref[...] = x + 0.5 * g * _rms(y, post_ref[...])


def _ffn(x, mod, col0, per_token, seq_len, pre, post, wg, wu, wd):
    n = x.shape[0]
    tm = min(512, n)
    dff = wg.shape[1]
    kern = functools.partial(_ffn_kernel, per_token=per_token, tiles_per_seq=max(seq_len // tm, 1))
    row = lambda i: (i, 0)
    return pl.pallas_call(
        kern,
        out_shape=jax.ShapeDtypeStruct((n, D_MODEL), F32),
        grid=(n // tm,),
        in_specs=[pl.BlockSpec((tm, D_MODEL), row)]
        + _mod_specs(per_token, tm, (col0, col0 + 1, col0 + 2))
        + [_const_spec((1, D_MODEL)), _const_spec((1, D_MODEL)),
           _const_spec((D_MODEL, dff)), _const_spec((D_MODEL, dff)), _const_spec((dff, D_MODEL))],
        out_specs=pl.BlockSpec((tm, D_MODEL), row),
        compiler_params=_params(("arbitrary",)),
        name="ffn",
    )(x, mod, mod, mod, pre, post, wg, wu, wd)


def _proj_kernel(x_ref, sh_ref, sc_ref, pre_ref, w_ref, u_ref, z_ref, qkv_ref, g_ref, ba_ref,
                 *, per_token, tiles_per_seq):
    sh = _mod_rows(sh_ref, per_token, tiles_per_seq)
    sc = _mod_rows(sc_ref, per_token, tiles_per_seq)
    h = (_rms(x_ref[...], pre_ref[...]) * (1.0 + sc) + sh).astype(BF16)
    u_ref[...] = _dot(h, w_ref[:, P_U:P_Z])
    z_ref[...] = _dot(h, w_ref[:, P_Z:P_QKV])
    qkv_ref[...] = _dot(h, w_ref[:, P_QKV:P_G])
    g_ref[...] = _dot(h, w_ref[:, P_G:P_BA])
    ba_ref[...] = _dot(h, w_ref[:, P_BA:P_TOTAL])


def _proj(x, mod, per_token, seq_len, pre, w):
    n = x.shape[0]
    tm = min(256, n)
    kern = functools.partial(_proj_kernel, per_token=per_token, tiles_per_seq=max(seq_len // tm, 1))
    row = lambda i: (i, 0)
    widths = (P_Z - P_U, P_QKV - P_Z, P_G - P_QKV, P_BA - P_G, P_TOTAL - P_BA)
    return pl.pallas_call(
        kern,
        out_shape=[jax.ShapeDtypeStruct((n, wd), F32) for wd in widths],
        grid=(n // tm,),
        in_specs=[pl.BlockSpec((tm, D_MODEL), row)]
        + _mod_specs(per_token, tm, (3, 4))
        + [_const_spec((1, D_MODEL)), _const_spec((D_MODEL, P_TOTAL))],
        out_specs=[pl.BlockSpec((tm, wd), row) for wd in widths],
        compiler_params=_params(("arbitrary",)),
        name="proj",
    )(x, mod, mod, pre, w)


def _tri_inv_all(lows, c):
    ii = lax.broadcasted_iota(jnp.int32, (c, c), 0)
    jj = lax.broadcasted_iota(jnp.int32, (c, c), 1)
    eye = jnp.where(ii == jj, 1.0, 0.0).astype(F32)
    xs = [eye - jnp.where(ii // 2 == jj // 2, low, 0.0) for low in lows]
    b = 2
    while b < c:
        join = (ii // (2 * b) == jj // (2 * b)) & (ii // b != jj // b)
        ys = [_mm_bf(x, jnp.where(join, low, 0.0)) for x, low in zip(xs, lows)]
        xs = [x - _mm_bf(y, x) for x, y in zip(xs, ys)]
        b *= 2
    return xs


def _dn_kernel(u_ref, z_ref, ba_ref, cw_ref, gp_ref, dnn_ref, conv0_ref, s0_ref,
               o_ref, convo_ref, so_ref, ubuf, q_s, k_s, v_s, bg_s, u_s, w_s, qe_s, kd_s, a_s, eg_s,
               *, nseq, tm, chunk, valid):
    t = pl.program_id(1)
    rows_all = nseq * tm

    @pl.when(t == 0)
    def _():
        ubuf[:, 0:SUBLANE, :] = conv0_ref[...]
        so_ref[...] = s0_ref[...]

    for g in range(nseq):
        rws = slice(g * tm, (g + 1) * tm)
        ubuf[g, SUBLANE:SUBLANE + tm, :] = u_ref[rws, :]
        for cb in range(CONV_CH // LANE):
            cols = slice(cb * LANE, (cb + 1) * LANE)
            acc = ubuf[g, pl.ds(SUBLANE - CONV_W + 1, tm), cols] * cw_ref[0:1, cols]
            for i in range(1, CONV_W):
                acc = acc + ubuf[g, pl.ds(SUBLANE - CONV_W + 1 + i, tm), cols] * cw_ref[i:i + 1, cols]
            a = _silu(acc)
            sec, off = divmod(cb * LANE, D_MODEL)
            dst = slice(off, off + LANE)
            if sec == 2:
                v_s[rws, dst] = a
            else:
                an = a * lax.rsqrt(jnp.sum(a * a, axis=-1, keepdims=True) + EPS)
                if sec == 0:
                    q_s[rws, dst] = an * (DN_DK ** -0.5)
                else:
                    k_s[rws, dst] = an
        ubuf[g, 0:SUBLANE, :] = ubuf[g, pl.ds(valid, SUBLANE), :]
        convo_ref[g] = ubuf[g, 0:SUBLANE, :]

    ba = ba_ref[...]
    lane = lax.broadcasted_iota(jnp.int32, (rows_all, LANE), 1)
    rowi = lax.broadcasted_iota(jnp.int32, (rows_all, LANE), 0)
    xg = ba + gp_ref[0:1, :]
    softplus = jnp.maximum(xg, 0.0) + jnp.log1p(jnp.exp(-jnp.abs(xg)))
    gdec = -jnp.exp(gp_ref[1:2, :]) * softplus
    bg = jnp.where(lane < DN_HEADS, jax.nn.sigmoid(ba), gdec)
    bg_s[...] = jnp.where(rowi % tm < valid, bg, 0.0)

    c = chunk
    ii = lax.broadcasted_iota(jnp.int32, (c, c), 0)
    jj = lax.broadcasted_iota(jnp.int32, (c, c), 1)
    incl = ii >= jj
    strict = ii > jj
    tril = jnp.where(incl, 1.0, 0.0).astype(F32)
    sel = jnp.where(lax.broadcasted_iota(jnp.int32, (SUBLANE, LANE), 1)
                    == lax.broadcasted_iota(jnp.int32, (SUBLANE, LANE), 0) + DN_HEADS, 1.0, 0.0).astype(F32)
    dnn = dnn_ref[...]

    heads = range(DN_HEADS)
    hcols = [slice(h * DN_DK, (h + 1) * DN_DK) for h in heads]
    acols = [slice(h * c, (h + 1) * c) for h in heads]
    n_chunks = rows_all // c
    chunks_per_seq = tm // c
    gpc = DN_GROUP if n_chunks % DN_GROUP == 0 else 1

    def chunk_rows(ci):
        return pl.ds(ci * c, c) if isinstance(ci, int) else pl.ds(pl.multiple_of(ci * c, c), c)

    def prep(cis):
        n_c = len(cis)
        rows = [chunk_rows(ci) for ci in cis]
        pairs = [(j, h) for j in range(n_c) for h in heads]
        bgc = [bg_s[r, :] for r in rows]
        gcum = [_dot_hi(tril, b) for b in bgc]
        gcum_t = [lax.dot_general(sel, g, (((1,), (1,)), ((), ())), preferred_element_type=F32, precision=HI)
                  for g in gcum]
        for j in range(n_c):
            eg_s[cis[j]] = jnp.broadcast_to(jnp.exp(gcum_t[j][:, c - 1:c]), (DN_HEADS, DN_DV))
        beta = [bgc[j][:, h:h + 1] for j, h in pairs]
        gc = [gcum[j][:, DN_HEADS + h:DN_HEADS + h + 1] for j, h in pairs]
        decay = [jnp.where(incl, jnp.exp(jnp.where(incl, gc[p] - gcum_t[j][h:h + 1, :], 0.0)), 0.0)
                 for p, (j, h) in enumerate(pairs)]
        q = [q_s[rows[j], hcols[h]] for j, h in pairs]
        k = [k_s[rows[j], hcols[h]] for j, h in pairs]
        npair = range(len(pairs))
        kb = [k[p] * beta[p] for p in npair]
        k_bf = [k[p].astype(BF16) for p in npair]
        akk = [_dot_nt(kb[p].astype(BF16), k_bf[p]) for p in npair]
        aqk = [_dot_nt(q[p].astype(BF16), k_bf[p]) for p in npair]
        tinv = _tri_inv_all([jnp.where(strict, akk[p] * decay[p], 0.0) for p in npair], c)
        egc = [jnp.exp(gc[p]) for p in npair]
        rhs = [jnp.concatenate([v_s[rows[j], hcols[h]] * beta[p], kb[p] * egc[p]], axis=1)
               for p, (j, h) in enumerate(pairs)]
        uw = [_mm_bf(tinv[p], rhs[p]) for p in npair]
        sdt = w_s.dtype
        for p, (j, h) in enumerate(pairs):
            u_s[rows[j], hcols[h]] = uw[p][:, :DN_DV]
            w_s[rows[j], hcols[h]] = uw[p][:, DN_DV:].astype(sdt)
            qe_s[rows[j], hcols[h]] = (q[p] * egc[p]).astype(sdt)
            kd_s[rows[j], hcols[h]] = (k[p] * jnp.exp(gc[p][c - 1:c, :] - gc[p])).astype(sdt)
            a_s[rows[j], acols[h]] = jnp.where(incl, aqk[p] * decay[p], 0.0).astype(sdt)

    def state(items):
        trip = [(i, g, h) for i, (_, g) in enumerate(items) for h in heads]
        rows = [chunk_rows(ci) for ci, _ in items]
        egb = [eg_s[ci] for ci, _ in items]
        s_old = [so_ref[g, h] for _, g, h in trip]
        s_bf = [s.astype(BF16) for s in s_old]
        ws = [_dot(w_s[rows[i], hcols[h]].astype(BF16), s_bf[n]) for n, (i, g, h) in enumerate(trip)]
        qs = [_dot(qe_s[rows[i], hcols[h]].astype(BF16), s_bf[n]) for n, (i, g, h) in enumerate(trip)]
        vn_bf = [(u_s[rows[i], hcols[h]] - ws[n]).astype(BF16) for n, (i, g, h) in enumerate(trip)]
        av = [_dot(a_s[rows[i], acols[h]].astype(BF16), vn_bf[n]) for n, (i, g, h) in enumerate(trip)]
        kv = [_dot_tn(kd_s[rows[i], hcols[h]].astype(BF16), vn_bf[n]) for n, (i, g, h) in enumerate(trip)]
        for n, (i, g, h) in enumerate(trip):
            so_ref[g, h] = s_old[n] * egb[i][h:h + 1, :] + kv[n]
            o_ref[rows[i], hcols[h]] = _rms(qs[n] + av[n], dnn) * _silu(z_ref[rows[i], hcols[h]])

    if n_chunks == gpc:
        prep(list(range(n_chunks)))
    else:
        def prep_body(gi, carry):
            prep([gi * gpc + j for j in range(gpc)])
            return carry
        lax.fori_loop(0, n_chunks // gpc, prep_body, 0)
    if chunks_per_seq == 1:
        state([(g, g) for g in range(nseq)])
    else:
        assert nseq == 1

        def state_body(ci, carry):
            state([(ci, 0)])
            return carry
        lax.fori_loop(0, n_chunks, state_body, 0)


def _deltanet(u, z, ba, cw, gp, dnn, conv0, s0, *, batch, seq, nseq, tm, chunk, valid):
    nt = seq // tm
    assert nseq == 1 or nt == 1
    rows = nseq * tm
    row = lambda b, t: (b * nt + t, 0)
    per_b3 = lambda b, t: (b, 0, 0)
    per_b4 = lambda b, t: (b, 0, 0, 0)
    sdt = BF16 if chunk % 16 == 0 else F32
    kern = functools.partial(_dn_kernel, nseq=nseq, tm=tm, chunk=chunk, valid=valid)
    return pl.pallas_call(
        kern,
        out_shape=[jax.ShapeDtypeStruct((batch * seq, D_MODEL), F32),
                   jax.ShapeDtypeStruct((batch, SUBLANE, CONV_CH), F32),
                   jax.ShapeDtypeStruct((batch, DN_HEADS, DN_DK, DN_DV), F32)],
        grid=(batch // nseq, nt),
        in_specs=[pl.BlockSpec((rows, CONV_CH), row),
                  pl.BlockSpec((rows, D_MODEL), row),
                  pl.BlockSpec((rows, LANE), row),
                  pl.BlockSpec((SUBLANE, CONV_CH), lambda b, t: (0, 0)),
                  pl.BlockSpec((SUBLANE, LANE), lambda b, t: (0, 0)),
                  pl.BlockSpec((1, DN_DV), lambda b, t: (0, 0)),
                  pl.BlockSpec((nseq, SUBLANE, CONV_CH), per_b3),
                  pl.BlockSpec((nseq, DN_HEADS, DN_DK, DN_DV), per_b4)],
        out_specs=[pl.BlockSpec((rows, D_MODEL), row),
                   pl.BlockSpec((nseq, SUBLANE, CONV_CH), per_b3),
                   pl.BlockSpec((nseq, DN_HEADS, DN_DK, DN_DV), per_b4)],
        scratch_shapes=[pltpu.VMEM((nseq, SUBLANE + tm, CONV_CH), F32),
                        pltpu.VMEM((rows, D_MODEL), F32),
                        pltpu.VMEM((rows, D_MODEL), F32),
                        pltpu.VMEM((rows, D_MODEL), F32),
                        pltpu.VMEM((rows, LANE), F32),
                        pltpu.VMEM((rows, D_MODEL), F32),
                        pltpu.VMEM((rows, D_MODEL), sdt),
                        pltpu.VMEM((rows, D_MODEL), sdt),
                        pltpu.VMEM((rows, D_MODEL), sdt),
                        pltpu.VMEM((rows, DN_HEADS * chunk), sdt),
                        pltpu.VMEM((rows // chunk, DN_HEADS, DN_DV), F32)],
        compiler_params=_params(("arbitrary", "arbitrary")),
        name="deltanet",
    )(u, z, ba, cw, gp, dnn, conv0, s0)


def _rope(x, cos, sa, sb):
    cols = []
    for cb in range(x.shape[1] // LANE):
        xc = x[:, cb * LANE:(cb + 1) * LANE]
        cols.append(xc * cos + pltpu.roll(xc, LANE - ROT_DIM // 2, 1) * sa + pltpu.roll(xc, ROT_DIM // 2, 1) * sb)
    return cols[0] if len(cols) == 1 else jnp.concatenate(cols, axis=1)


def _head_pair_operands(x):
    lane = lax.broadcasted_iota(jnp.int32, (x.shape[0], LANE), 1)
    out = []
    for cb in range(x.shape[1] // LANE):
        c = x[:, cb * LANE:(cb + 1) * LANE]
        lo = jnp.where(lane < SW_HEAD_DIM, c, 0.0)
        hi = jnp.where(lane >= SW_HEAD_DIM, c, 0.0)
        out.append((lo, pltpu.roll(lo, SW_HEAD_DIM, 1)))
        out.append((pltpu.roll(hi, SW_HEAD_DIM, 1), hi))
    return out


def _ones_columns(w):
    rowi = lax.broadcasted_iota(jnp.int32, (4 * w, LANE), 0)
    lanei = lax.broadcasted_iota(jnp.int32, (4 * w, LANE), 1)
    return jnp.where((rowi < 2 * w) == (lanei < SW_HEAD_DIM), 1.0, 0.0).astype(BF16)


def _pair_softmax(scores, mask, sinks, lane, w):
    parts, es = [], []
    for half in range(2):
        s = jnp.where(mask, scores[:, half * 2 * w:(half + 1) * 2 * w], NEG_INF)
        m = jnp.maximum(jnp.max(s, axis=-1, keepdims=True), sinks[half])
        parts.append(jnp.exp(s - m).astype(BF16))
        es.append(jnp.exp(sinks[half] - m))
    return jnp.concatenate(parts, axis=1), jnp.where(lane < SW_HEAD_DIM, es[0], es[1])


def _swa_prompt_kernel(sinks_ref, qkv_ref, cos_ref, sa_ref, sb_ref, o_ref, kc_ref, vc_ref, kk2, vv2):
    n = pl.program_id(1)
    w = WINDOW

    @pl.when(n == 0)
    def _():
        kk2[...] = jnp.zeros_like(kk2)
        for j in range(SW_KV_HEADS):
            vv2[j, :, 0:LANE] = jnp.zeros((4 * w, LANE), BF16)
            vv2[j, :, LANE:2 * LANE] = _ones_columns(w)

    for j in range(SW_KV_HEADS):
        kk2[j, 0:w, :] = kk2[j, w:2 * w, :]
        kk2[j, 2 * w:3 * w, :] = kk2[j, 3 * w:4 * w, :]
        vv2[j, 0:w, 0:LANE] = vv2[j, w:2 * w, 0:LANE]
        vv2[j, 2 * w:3 * w, 0:LANE] = vv2[j, 3 * w:4 * w, 0:LANE]
    cos, sa, sb = cos_ref[...], sa_ref[...], sb_ref[...]
    q = (_rope(qkv_ref[:, 0:D_MODEL], cos, sa, sb) * (SW_HEAD_DIM ** -0.5)).astype(BF16)
    k = _rope(qkv_ref[:, D_MODEL:D_MODEL + SW_KV], cos, sa, sb)
    v = qkv_ref[:, D_MODEL + SW_KV:D_MODEL + 2 * SW_KV]
    kc_ref[...] = k
    vc_ref[...] = v
    for j, (left, right) in enumerate(_head_pair_operands(k)):
        kk2[j, w:2 * w, :] = left.astype(BF16)
        kk2[j, 3 * w:4 * w, :] = right.astype(BF16)
    for j, (left, right) in enumerate(_head_pair_operands(v)):
        vv2[j, w:2 * w, 0:LANE] = left.astype(BF16)
        vv2[j, 3 * w:4 * w, 0:LANE] = right.astype(BF16)
    qi = lax.broadcasted_iota(jnp.int32, (w, 2 * w), 0) + w
    sj = lax.broadcasted_iota(jnp.int32, (w, 2 * w), 1)
    d = qi - sj
    mask = (d >= 0) & (d <= WINDOW) & ((sj >= w) | (n > 0))
    lane = lax.broadcasted_iota(jnp.int32, (w, LANE), 1)
    n_pb = D_MODEL // LANE
    scores = [_dot_nt(q[:, pb * LANE:(pb + 1) * LANE], kk2[pb // 2]) for pb in range(n_pb)]
    soft = [_pair_softmax(scores[pb], mask, (sinks_ref[2 * pb], sinks_ref[2 * pb + 1]), lane, w)
            for pb in range(n_pb)]
    outs = [_dot(soft[pb][0], vv2[pb // 2]) for pb in range(n_pb)]
    for pb in range(n_pb):
        o_ref[:, pb * LANE:(pb + 1) * LANE] = outs[pb][:, 0:LANE] / (outs[pb][:, LANE:2 * LANE] + soft[pb][1])


def _swa_prompt(sinks, qkv, cos, sa, sb, *, batch, seq):
    nb = seq // WINDOW
    row = lambda b, n: (b * nb + n, 0)
    tab = lambda b, n: (n, 0)
    per_b = lambda b, n: (b, 0, 0)
    return pl.pallas_call(
        _swa_prompt_kernel,
        out_shape=[jax.ShapeDtypeStruct((batch * seq, D_MODEL), F32),
                   jax.ShapeDtypeStruct((batch, WINDOW, SW_KV), F32),
                   jax.ShapeDtypeStruct((batch, WINDOW, SW_KV), F32)],
        grid=(batch, nb),
        in_specs=[pl.BlockSpec(memory_space=pltpu.SMEM),
                  pl.BlockSpec((WINDOW, D_MODEL + 2 * SW_KV), row),
                  pl.BlockSpec((WINDOW, LANE), tab),
                  pl.BlockSpec((WINDOW, LANE), tab),
                  pl.BlockSpec((WINDOW, LANE), tab)],
        out_specs=[pl.BlockSpec((WINDOW, D_MODEL), row),
                   pl.BlockSpec((None, WINDOW, SW_KV), per_b),
                   pl.BlockSpec((None, WINDOW, SW_KV), per_b)],
        scratch_shapes=[pltpu.VMEM((SW_KV_HEADS, 4 * WINDOW, LANE), BF16),
                        pltpu.VMEM((SW_KV_HEADS, 4 * WINDOW, 2 * LANE), BF16)],
        compiler_params=_params(("arbitrary", "arbitrary")),
        name="swa_prompt",
    )(sinks, qkv, cos, sa, sb)


def _swa_sample_kernel(sinks_ref, qkv_ref, kbuf_ref, vbuf_ref, cos_ref, sa_ref, sb_ref,
                       o_ref, kc_ref, vc_ref, kk_s, vv_s, kk2, vv2, *, nseq, t_new):
    wb = WINDOW
    tp = SUBLANE

    @pl.when(pl.program_id(0) == 0)
    def _():
        kk2[...] = jnp.zeros_like(kk2)
        for g in range(nseq):
            for j in range(SW_KV_HEADS):
                vv2[g, j, :, 0:LANE] = jnp.zeros((4 * wb, LANE), BF16)
                vv2[g, j, :, LANE:2 * LANE] = _ones_columns(wb)

    cos, sa, sb = cos_ref[...], sa_ref[...], sb_ref[...]
    qs = []
    for g in range(nseq):
        q = (_rope(qkv_ref[g, :, 0:D_MODEL], cos, sa, sb) * (SW_HEAD_DIM ** -0.5)).astype(BF16)
        k_new = _rope(qkv_ref[g, :, D_MODEL:D_MODEL + SW_KV], cos, sa, sb)
        v_new = qkv_ref[g, :, D_MODEL + SW_KV:D_MODEL + 2 * SW_KV]
        k_old, v_old = kbuf_ref[g], vbuf_ref[g]
        kk_s[g, 0:wb, :] = k_old
        vv_s[g, 0:wb, :] = v_old
        kk_s[g, wb:wb + tp, :] = k_new
        vv_s[g, wb:wb + tp, :] = v_new
        kc_ref[g] = kk_s[g, pl.ds(t_new, wb), :]
        vc_ref[g] = vv_s[g, pl.ds(t_new, wb), :]
        for r0, n_r, kx, vx in ((0, wb, k_old, v_old), (wb, tp, k_new, v_new)):
            for j, (left, right) in enumerate(_head_pair_operands(kx)):
                kk2[g, j, r0:r0 + n_r, :] = left.astype(BF16)
                kk2[g, j, 2 * wb + r0:2 * wb + r0 + n_r, :] = right.astype(BF16)
            for j, (left, right) in enumerate(_head_pair_operands(vx)):
                vv2[g, j, r0:r0 + n_r, 0:LANE] = left.astype(BF16)
                vv2[g, j, 2 * wb + r0:2 * wb + r0 + n_r, 0:LANE] = right.astype(BF16)
        qs.append([jnp.concatenate([q[:, (2 * j) * LANE:(2 * j + 1) * LANE],
                                    q[:, (2 * j + 1) * LANE:(2 * j + 2) * LANE]], axis=0)
                   for j in range(SW_KV_HEADS)])
    rows = 2 * tp
    qt = lax.broadcasted_iota(jnp.int32, (rows, 2 * wb), 0) % tp
    sj = lax.broadcasted_iota(jnp.int32, (rows, 2 * wb), 1)
    d = qt + wb - sj
    mask = (d >= 0) & (d <= WINDOW)
    top = lax.broadcasted_iota(jnp.int32, (rows, 1), 0) < tp
    lane = lax.broadcasted_iota(jnp.int32, (rows, LANE), 1)
    items = [(g, j) for g in range(nseq) for j in range(SW_KV_HEADS)]
    scores = [_dot_nt(qs[g][j], kk2[g, j]) for g, j in items]
    soft = []
    for n, (g, j) in enumerate(items):
        sinks = [jnp.where(top, sinks_ref[4 * j + half], sinks_ref[4 * j + 2 + half]) for half in range(2)]
        soft.append(_pair_softmax(scores[n], mask, sinks, lane, wb))
    outs = [_dot(soft[n][0], vv2[g, j]) for n, (g, j) in enumerate(items)]
    for n, (g, j) in enumerate(items):
        o = outs[n][:, 0:LANE] / (outs[n][:, LANE:2 * LANE] + soft[n][1])
        o_ref[g, :, (2 * j) * LANE:(2 * j + 1) * LANE] = o[0:tp, :]
        o_ref[g, :, (2 * j + 1) * LANE:(2 * j + 2) * LANE] = o[tp:2 * tp, :]


def _swa_sample(sinks, qkv, kbuf, vbuf, cos, sa, sb, *, batch, nseq, t_new):
    per_b = lambda b: (b, 0, 0)
    tab = lambda b: (0, 0)
    kern = functools.partial(_swa_sample_kernel, nseq=nseq, t_new=t_new)
    return pl.pallas_call(
        kern,
        out_shape=[jax.ShapeDtypeStruct((batch, SUBLANE, D_MODEL), F32),
                   jax.ShapeDtypeStruct((batch, WINDOW, SW_KV), F32),
                   jax.ShapeDtypeStruct((batch, WINDOW, SW_KV), F32)],
        grid=(batch // nseq,),
        in_specs=[pl.BlockSpec(memory_space=pltpu.SMEM),
                  pl.BlockSpec((nseq, SUBLANE, D_MODEL + 2 * SW_KV), per_b),
                  pl.BlockSpec((nseq, WINDOW, SW_KV), per_b),
                  pl.BlockSpec((nseq, WINDOW, SW_KV), per_b),
                  pl.BlockSpec((SUBLANE, LANE), tab),
                  pl.BlockSpec((SUBLANE, LANE), tab),
                  pl.BlockSpec((SUBLANE, LANE), tab)],
        out_specs=[pl.BlockSpec((nseq, SUBLANE, D_MODEL), per_b),
                   pl.BlockSpec((nseq, WINDOW, SW_KV), per_b),
                   pl.BlockSpec((nseq, WINDOW, SW_KV), per_b)],
        scratch_shapes=[pltpu.VMEM((nseq, WINDOW + SUBLANE, SW_KV), F32),
                        pltpu.VMEM((nseq, WINDOW + SUBLANE, SW_KV), F32),
                        pltpu.VMEM((nseq, SW_KV_HEADS, 4 * WINDOW, LANE), BF16),
                        pltpu.VMEM((nseq, SW_KV_HEADS, 4 * WINDOW, 2 * LANE), BF16)],
        compiler_params=_params(("arbitrary",)),
        name="swa_sample",
    )(sinks, qkv, kbuf, vbuf, cos, sa, sb)


def _out_kernel(x_ref, g_ref, post_ref, ga_ref, gb_ref, odn_ref, osw_ref, w_ref, o_ref, *, per_token, tiles_per_seq):
    g = _mod_rows(g_ref, per_token, tiles_per_seq)
    y = jax.nn.sigmoid(ga_ref[...]) * odn_ref[...] + jax.nn.sigmoid(gb_ref[...]) * osw_ref[...]
    p = _dot(y.astype(BF16), w_ref[...])
    o_ref[...] = x_ref[...] + g * _rms(p, post_ref[...])


def _out(x, mod, per_token, seq_len, post, gates, o_dn, o_sw, w):
    n = x.shape[0]
    tm = min(512, n)
    kern = functools.partial(_out_kernel, per_token=per_token, tiles_per_seq=max(seq_len // tm, 1))
    row = lambda i: (i, 0)
    return pl.pallas_call(
        kern,
        out_shape=jax.ShapeDtypeStruct((n, D_MODEL), F32),
        grid=(n // tm,),
        in_specs=[pl.BlockSpec((tm, D_MODEL), row)]
        + _mod_specs(per_token, tm, (5,))
        + [_const_spec((1, D_MODEL)),
           pl.BlockSpec((tm, D_MODEL), lambda i: (i, 0)),
           pl.BlockSpec((tm, D_MODEL), lambda i: (i, 1)),
           pl.BlockSpec((tm, D_MODEL), row),
           pl.BlockSpec((tm, D_MODEL), row),
           _const_spec((D_MODEL, D_MODEL))],
        out_specs=pl.BlockSpec((tm, D_MODEL), row),
        compiler_params=_params(("arbitrary",)),
        name="out",
    )(x, mod, post, gates, gates, o_dn, o_sw, w)


def _rope_tables(pos):
    half = ROT_DIM // 2
    inv_freq = ROPE_THETA ** (-jnp.arange(half, dtype=F32) * (2.0 / ROT_DIM))
    ang = pos.astype(F32)[:, None] * inv_freq[None, :]
    cos, sin = jnp.cos(ang), jnp.sin(ang)
    n = pos.shape[0]
    rest = SW_HEAD_DIM - ROT_DIM
    c64 = jnp.concatenate([cos, cos, jnp.ones((n, rest), F32)], axis=1)
    a64 = jnp.concatenate([-sin, jnp.zeros((n, half + rest), F32)], axis=1)
    b64 = jnp.concatenate([jnp.zeros((n, half), F32), sin, jnp.zeros((n, rest), F32)], axis=1)
    rep = LANE // SW_HEAD_DIM
    return tuple(jnp.tile(t, (1, rep)) for t in (c64, a64, b64))


def _pad_rows(a, rows):
    return jnp.pad(a, ((0, rows - a.shape[0]),) + ((0, 0),) * (a.ndim - 1))


def _layer(x, mod, per_token, seq, wts, conv0, s0, dn_fn, swa_fn):
    x = _ffn(x, mod, 0, per_token, seq, wts["pre1"], wts["post1"], wts["wg1"], wts["wu1"], wts["wd1"])
    u, z, qkv, gates, ba = _proj(x, mod, per_token, seq, wts["pre2"], wts["w_in"])
    o_dn, conv_new, s_new = dn_fn(u, z, ba, wts["conv_w"], wts["gparam"], wts["dn_norm"], conv0, s0)
    o_sw, k_new, v_new = swa_fn(qkv)
    x = _out(x, mod, per_token, seq, wts["post2"], gates, o_dn, o_sw, wts["w_out"])
    x = _ffn(x, mod, 6, per_token, seq, wts["pre3"], wts["post3"], wts["wg2"], wts["wu2"], wts["wd2"])
    return x, (k_new, v_new, conv_new, s_new)


def kernel(x_prompt, x_sample, cache_swa_k, cache_swa_v, state_conv, state_delta, c_prompt, c_sample,
           w_ada, b_ada, ffn1_norm_pre, ffn1_norm_post, ffn1_w_gate, ffn1_w_up, ffn1_w_down,
           mix_norm_pre, mix_norm_post, w_in, conv_w, a_log, dt_bias, dn_norm, sinks, w_out,
           ffn2_norm_pre, ffn2_norm_post, ffn2_w_gate, ffn2_w_up, ffn2_w_down):
    depth = w_ada.shape[0]
    bp, tp, _ = x_prompt.shape
    bs, ts, _ = x_sample.shape
    assert tp % 512 == 0 and 1 <= ts <= SUBLANE and bs % SAMPLE_SEQS == 0
    dff = ffn1_w_gate.shape[-1]
    dff_pad = -(-dff // 256) * 256

    cos_p, sa_p, sb_p = _rope_tables(jnp.arange(tp))
    cos_s, sa_s, sb_s = (_pad_rows(t, SUBLANE) for t in _rope_tables(PAST_LEN + jnp.arange(ts)))

    xp = x_prompt.reshape(bp * tp, D_MODEL)
    xs = jnp.pad(x_sample, ((0, 0), (0, SUBLANE - ts), (0, 0))).reshape(bs * SUBLANE, D_MODEL)
    c_p = _pad_rows(c_prompt, SUBLANE)
    c_s = jnp.repeat(c_sample, SUBLANE, axis=0)

    outs_p, outs_s = [], []
    for l in range(depth):
        def wpad(w, axis):
            pad = [(0, 0), (0, 0)]
            pad[axis] = (0, dff_pad - dff)
            return jnp.pad(w, pad).astype(BF16)

        wi = w_in[l]
        w_in_p = jnp.concatenate(
            [wi[:, :4096], wi[:, 4112:7696], wi[:, 4096:4112], jnp.zeros((D_MODEL, P_TOTAL - 7696), F32)],
            axis=1).astype(BF16)
        gparam = jnp.zeros((SUBLANE, LANE), F32)
        gparam = gparam.at[0, DN_HEADS:2 * DN_HEADS].set(dt_bias[l]).at[1, DN_HEADS:2 * DN_HEADS].set(a_log[l])
        wts = dict(
            pre1=ffn1_norm_pre[l][None], post1=ffn1_norm_post[l][None],
            wg1=wpad(ffn1_w_gate[l], 1), wu1=wpad(ffn1_w_up[l], 1), wd1=wpad(ffn1_w_down[l], 0),
            pre2=mix_norm_pre[l][None], post2=mix_norm_post[l][None], w_in=w_in_p,
            conv_w=_pad_rows(conv_w[l], SUBLANE), gparam=gparam, dn_norm=dn_norm[l][None],
            w_out=w_out[l].astype(BF16),
            pre3=ffn2_norm_pre[l][None], post3=ffn2_norm_post[l][None],
            wg2=wpad(ffn2_w_gate[l], 1), wu2=wpad(ffn2_w_up[l], 1), wd2=wpad(ffn2_w_down[l], 0),
        )
        w_ada_bf = w_ada[l].astype(BF16)
        b_ada_l = b_ada[l][None]
        mod_p = _ada(c_p, w_ada_bf, b_ada_l)
        mod_s = _ada(c_s, w_ada_bf, b_ada_l)
        sink_l = sinks[l]

        conv0_p = jnp.zeros((bp, SUBLANE, CONV_CH), F32)
        s0_p = jnp.zeros((bp, DN_HEADS, DN_DK, DN_DV), F32)
        dn_p = functools.partial(_deltanet, batch=bp, seq=tp, nseq=1, tm=DN_TILE, chunk=DN_CHUNK, valid=DN_TILE)
        swa_p = lambda qkv: _swa_prompt(sink_l, qkv, cos_p, sa_p, sb_p, batch=bp, seq=tp)
        xp, st_p = _layer(xp, mod_p, False, tp, wts, conv0_p, s0_p, dn_p, swa_p)

        conv0_s = jnp.pad(state_conv[l], ((0, 0), (SUBLANE - CONV_W + 1, 0), (0, 0)))
        kbuf = cache_swa_k[l].reshape(bs, WINDOW, SW_KV)
        vbuf = cache_swa_v[l].reshape(bs, WINDOW, SW_KV)
        dn_s = functools.partial(_deltanet, batch=bs, seq=SUBLANE, nseq=SAMPLE_SEQS, tm=SUBLANE, chunk=SUBLANE,
                                 valid=ts)

        def swa_s(qkv):
            o, k, v = _swa_sample(sink_l, qkv.reshape(bs, SUBLANE, -1), kbuf, vbuf, cos_s, sa_s, sb_s,
                                  batch=bs, nseq=SAMPLE_SEQS, t_new=ts)
            return o.reshape(bs * SUBLANE, D_MODEL), k, v

        xs, st_s = _layer(xs, mod_s, True, SUBLANE, wts, conv0_s, state_delta[l], dn_s, swa_s)
        outs_p.append(st_p)
        outs_s.append(st_s)

    def stack(outs, i):
        return jnp.stack([o[i] for o in outs])

    def kv5(a, batch):
        return a.reshape(depth, batch, WINDOW, SW_KV_HEADS, SW_HEAD_DIM)

    tail = slice(SUBLANE - CONV_W + 1, SUBLANE)
    y_p = xp.reshape(bp, tp, D_MODEL)
    y_s = xs.reshape(bs, SUBLANE, D_MODEL)[:, :ts]
    return (y_p, y_s,
            kv5(stack(outs_p, 0), bp), kv5(stack(outs_p, 1), bp),
            stack(outs_p, 2)[:, :, tail], stack(outs_p, 3),
            kv5(stack(outs_s, 0), bs), kv5(stack(outs_s, 1), bs),
            stack(outs_s, 2)[:, :, tail], stack(outs_s, 3))
```

```python
import functools

import jax
import jax.numpy as jnp
from jax import lax
from jax.experimental import pallas as pl
from jax.experimental.pallas import tpu as pltpu

F32, BF16 = jnp.float32, jnp.bfloat16

D_MODEL = 1024
DN_HEADS, DN_DK, DN_DV = 8, 128, 128
CONV_W = 4
CONV_CH = 3 * D_MODEL
DN_CHUNK = 64
SW_HEAD_DIM, SW_HEADS, SW_KV_HEADS, SW_GROUP = 64, 16, 4, 4
SW_KV = SW_KV_HEADS * SW_HEAD_DIM
WINDOW = 128
ROT_DIM = 16
ROPE_THETA = 500000.0
PAST_LEN = 8192
EPS = 1e-6
NEG_INF = -1e30

LANE = 128
SUBLANE = 8
VMEM_LIMIT = 56 * 1024 * 1024

P_U, P_Z, P_QKV, P_G, P_BA = 0, 3072, 4096, 5632, 7680
P_TOTAL = 7808
HI = lax.Precision.HIGHEST

DN_TILE = 256
DN_GROUP = 4
SAMPLE_SEQS = 4


def _dot(a, b):
    return jnp.dot(a, b, preferred_element_type=F32)


def _dot_nt(a, b):
    return lax.dot_general(a, b, (((1,), (1,)), ((), ())), preferred_element_type=F32)


def _dot_tn(a, b):
    return lax.dot_general(a, b, (((0,), (0,)), ((), ())), preferred_element_type=F32)


def _dot_hi(a, b):
    return jnp.dot(a, b, preferred_element_type=F32, precision=HI)


def _mm_bf(a, b):
    return _dot(a.astype(BF16), b.astype(BF16))


def _silu(x):
    return x * jax.nn.sigmoid(x)


def _rms(x, gain):
    ms = jnp.mean(x * x, axis=-1, keepdims=True)
    return x * lax.rsqrt(ms + EPS) * gain


def _mod_rows(ref, per_token, tiles_per_seq):
    if per_token:
        return ref[...]
    b = pl.program_id(0) // tiles_per_seq
    return ref[pl.ds(b, 1), :]


def _const_spec(shape):
    return pl.BlockSpec(shape, lambda *_: (0,) * len(shape), pipeline_mode=pl.Buffered(1))


def _params(sem):
    return pltpu.CompilerParams(dimension_semantics=sem, vmem_limit_bytes=VMEM_LIMIT)


def _ada_kernel(c_ref, w_ref, b_ref, o_ref):
    h = _silu(c_ref[...]).astype(BF16)
    o_ref[...] = _dot(h, w_ref[...]) + b_ref[...]


def _ada(c, w_bf, b):
    rows, n = c.shape[0], w_bf.shape[1]
    return pl.pallas_call(
        _ada_kernel,
        out_shape=jax.ShapeDtypeStruct((rows, n), F32),
        grid=(n // D_MODEL,),
        in_specs=[
            pl.BlockSpec((rows, D_MODEL), lambda j: (0, 0)),
            pl.BlockSpec((D_MODEL, D_MODEL), lambda j: (0, j)),
            pl.BlockSpec((1, D_MODEL), lambda j: (0, j)),
        ],
        out_specs=pl.BlockSpec((rows, D_MODEL), lambda j: (0, j)),
        compiler_params=_params(("arbitrary",)),
        name="ada",
    )(c, w_bf, b)


def _mod_specs(per_token, tm, cols):
    if per_token:
        return [pl.BlockSpec((tm, D_MODEL), functools.partial(lambda i, c: (i, c), c=c)) for c in cols]
    return [pl.BlockSpec((SUBLANE, D_MODEL), functools.partial(lambda i, c: (0, c), c=c)) for c in cols]


def _ffn_kernel(x_ref, sh_ref, sc_ref, g_ref, pre_ref, post_ref, wg_ref, wu_ref, wd_ref, o_ref,
                *, per_token, tiles_per_seq):
    x = x_ref[...]
    sh = _mod_rows(sh_ref, per_token, tiles_per_seq)
    sc = _mod_rows(sc_ref, per_token, tiles_per_seq)
    g = _mod_rows(g_ref, per_token, tiles_per_seq)
    h = (_rms(x, pre_ref[...]) * (1.0 + sc) + sh).astype(BF16)
    act = (_silu(_dot(h, wg_ref[...])) * _dot(h, wu_ref[...])).astype(BF16)
    y = _dot(act, wd_ref[...])
    o_ref[...] = x + 0.5 * g * _rms(y, post_ref[...])


def _ffn(x, mod, col0, per_token, seq_len, pre, post, wg, wu, wd):
    n = x.shape[0]
    tm = min(512, n)
    dff = wg.shape[1]
    kern = functools.partial(_ffn_kernel, per_token=per_token, tiles_per_seq=max(seq_len // tm, 1))
    row = lambda i: (i, 0)
    return pl.pallas_call(
        kern,
        out_shape=jax.ShapeDtypeStruct((n, D_MODEL), F32),
        grid=(n // tm,),
        in_specs=[pl.BlockSpec((tm, D_MODEL), row)]
        + _mod_specs(per_token, tm, (col0, col0 + 1, col0 + 2))
        + [_const_spec((1, D_MODEL)), _const_spec((1, D_MODEL)),
           _const_spec((D_MODEL, dff)), _const_spec((D_MODEL, dff)), _const_spec((dff, D_MODEL))],
        out_specs=pl.BlockSpec((tm, D_MODEL), row),
        compiler_params=_params(("arbitrary",)),
        name="ffn",
    )(x, mod, mod, mod, pre, post, wg, wu, wd)


def _proj_kernel(x_ref, sh_ref, sc_ref, pre_ref, wa_ref, wb_ref, wc_ref, u_ref, z_ref, qkv_ref, g_ref, ba_ref,
                 *, per_token, tiles_per_seq):
    sh = _mod_rows(sh_ref, per_token, tiles_per_seq)
    sc = _mod_rows(sc_ref, per_token, tiles_per_seq)
    h = (_rms(x_ref[...], pre_ref[...]) * (1.0 + sc) + sh).astype(BF16)
    u_ref[...] = _dot(h, wa_ref[:, 0:P_Z])
    z_ref[...] = _dot(h, wa_ref[:, P_Z:P_QKV])
    qkv_ref[...] = _dot(h, wb_ref[:, 0:P_G - P_QKV])
    g_ref[...] = _dot(h, wb_ref[:, P_G - P_QKV:P_BA - P_QKV])
    ba_ref[...] = _dot(h, wc_ref[...])


def _proj(x, mod, per_token, seq_len, pre, wa, wb, wc):
    n = x.shape[0]
    tm = min(256, n)
    kern = functools.partial(_proj_kernel, per_token=per_token, tiles_per_seq=max(seq_len // tm, 1))
    row = lambda i: (i, 0)
    widths = (P_Z - P_U, P_QKV - P_Z, P_G - P_QKV, P_BA - P_G, P_TOTAL - P_BA)
    return pl.pallas_call(
        kern,
        out_shape=[jax.ShapeDtypeStruct((n, wd), F32) for wd in widths],
        grid=(n // tm,),
        in_specs=[pl.BlockSpec((tm, D_MODEL), row)]
        + _mod_specs(per_token, tm, (3, 4))
        + [_const_spec((1, D_MODEL)), _const_spec(wa.shape), _const_spec(wb.shape), _const_spec(wc.shape)],
        out_specs=[pl.BlockSpec((tm, wd), row) for wd in widths],
        compiler_params=_params(("arbitrary",)),
        name="proj",
    )(x, mod, mod, pre, wa, wb, wc)


def _tri_inv_all(lows, c):
    ii = lax.broadcasted_iota(jnp.int32, (c, c), 0)
    jj = lax.broadcasted_iota(jnp.int32, (c, c), 1)
    eye = jnp.where(ii == jj, 1.0, 0.0).astype(F32)
    xs = [eye - jnp.where(ii // 2 == jj // 2, low, 0.0) for low in lows]
    b = 2
    while b < c:
        join = (ii // (2 * b) == jj // (2 * b)) & (ii // b != jj // b)
        ys = [_mm_bf(x, jnp.where(join, low, 0.0)) for x, low in zip(xs, lows)]
        xs = [x - _mm_bf(y, x) for x, y in zip(xs, ys)]
        b *= 2
    return xs


def _dn_kernel(u_ref, z_ref, ba_ref, cw_ref, gp_ref, dnn_ref, conv0_ref, s0_ref,
               o_ref, convo_ref, so_ref, ubuf, q_s, k_s, v_s, bg_s, u_s, w_s, qe_s, kd_s, a_s, eg_s,
               *, nseq, tm, chunk, valid):
    t = pl.program_id(1)
    rows_all = nseq * tm

    n_cb = CONV_CH // LANE

    @pl.when(t == 0)
    def _():
        for g in range(nseq):
            for cb in range(n_cb):
                ubuf[g, cb, 0:SUBLANE, :] = conv0_ref[g, :, cb * LANE:(cb + 1) * LANE]
        so_ref[...] = s0_ref[...]

    for g in range(nseq):
        rws = slice(g * tm, (g + 1) * tm)
        for cb in range(n_cb):
            cols = slice(cb * LANE, (cb + 1) * LANE)
            ubuf[g, cb, SUBLANE:SUBLANE + tm, :] = u_ref[rws, cols]
            acc = ubuf[g, cb, pl.ds(SUBLANE - CONV_W + 1, tm), :] * cw_ref[0:1, cols]
            for i in range(1, CONV_W):
                acc = acc + ubuf[g, cb, pl.ds(SUBLANE - CONV_W + 1 + i, tm), :] * cw_ref[i:i + 1, cols]
            a = _silu(acc)
            sec, off = divmod(cb * LANE, D_MODEL)
            dst = slice(off, off + LANE)
            if sec == 2:
                v_s[rws, dst] = a
            else:
                an = a * lax.rsqrt(jnp.sum(a * a, axis=-1, keepdims=True) + EPS)
                if sec == 0:
                    q_s[rws, dst] = an * (DN_DK ** -0.5)
                else:
                    k_s[rws, dst] = an
            tail = ubuf[g, cb, pl.ds(valid, SUBLANE), :]
            ubuf[g, cb, 0:SUBLANE, :] = tail
            convo_ref[g, :, cols] = tail

    ba = ba_ref[...]
    lane = lax.broadcasted_iota(jnp.int32, (rows_all, LANE), 1)
    rowi = lax.broadcasted_iota(jnp.int32, (rows_all, LANE), 0)
    xg = ba + gp_ref[0:1, :]
    softplus = jnp.maximum(xg, 0.0) + jnp.log1p(jnp.exp(-jnp.abs(xg)))
    gdec = -jnp.exp(gp_ref[1:2, :]) * softplus
    bg = jnp.where(lane < DN_HEADS, jax.nn.sigmoid(ba), gdec)
    bg_s[...] = jnp.where(rowi % tm < valid, bg, 0.0)

    c = chunk
    ii = lax.broadcasted_iota(jnp.int32, (c, c), 0)
    jj = lax.broadcasted_iota(jnp.int32, (c, c), 1)
    incl = ii >= jj
    strict = ii > jj
    tril = jnp.where(incl, 1.0, 0.0).astype(F32)
    sel = jnp.where(lax.broadcasted_iota(jnp.int32, (SUBLANE, LANE), 1)
                    == lax.broadcasted_iota(jnp.int32, (SUBLANE, LANE), 0) + DN_HEADS, 1.0, 0.0).astype(F32)
    dnn = dnn_ref[...]

    heads = range(DN_HEADS)
    hcols = [slice(h * DN_DK, (h + 1) * DN_DK) for h in heads]
    acols = [slice(h * c, (h + 1) * c) for h in heads]
    n_chunks = rows_all // c
    chunks_per_seq = tm // c
    gpc = DN_GROUP if n_chunks % DN_GROUP == 0 else 1

    def chunk_rows(ci):
        return pl.ds(ci * c, c) if isinstance(ci, int) else pl.ds(pl.multiple_of(ci * c, c), c)

    def prep(cis):
        n_c = len(cis)
        rows = [chunk_rows(ci) for ci in cis]
        pairs = [(j, h) for j in range(n_c) for h in heads]
        bgc = [bg_s[r, :] for r in rows]
        gcum = [_dot_hi(tril, b) for b in bgc]
        gcum_t = [lax.dot_general(sel, g, (((1,), (1,)), ((), ())), preferred_element_type=F32, precision=HI)
                  for g in gcum]
        for j in range(n_c):
            eg_s[cis[j]] = jnp.broadcast_to(jnp.exp(gcum_t[j][:, c - 1:c]), (DN_HEADS, DN_DV))
        beta = [bgc[j][:, h:h + 1] for j, h in pairs]
        gc = [gcum[j][:, DN_HEADS + h:DN_HEADS + h + 1] for j, h in pairs]
        decay = [jnp.where(incl, jnp.exp(jnp.where(incl, gc[p] - gcum_t[j][h:h + 1, :], 0.0)), 0.0)
                 for p, (j, h) in enumerate(pairs)]
        q = [q_s[rows[j], hcols[h]] for j, h in pairs]
        k = [k_s[rows[j], hcols[h]] for j, h in pairs]
        npair = range(len(pairs))
        kb = [k[p] * beta[p] for p in npair]
        k_bf = [k[p].astype(BF16) for p in npair]
        akk = [_dot_nt(kb[p].astype(BF16), k_bf[p]) for p in npair]
        aqk = [_dot_nt(q[p].astype(BF16), k_bf[p]) for p in npair]
        tinv = _tri_inv_all([jnp.where(strict, akk[p] * decay[p], 0.0) for p in npair], c)
        egc = [jnp.exp(gc[p]) for p in npair]
        rhs = [jnp.concatenate([v_s[rows[j], hcols[h]] * beta[p], kb[p] * egc[p]], axis=1)
               for p, (j, h) in enumerate(pairs)]
        uw = [_mm_bf(tinv[p], rhs[p]) for p in npair]
        sdt = w_s.dtype
        for p, (j, h) in enumerate(pairs):
            u_s[rows[j], hcols[h]] = uw[p][:, :DN_DV]
            w_s[rows[j], hcols[h]] = uw[p][:, DN_DV:].astype(sdt)
            qe_s[rows[j], hcols[h]] = (q[p] * egc[p]).astype(sdt)
            kd_s[rows[j], hcols[h]] = (k[p] * jnp.exp(gc[p][c - 1:c, :] - gc[p])).astype(sdt)
            a_s[rows[j], acols[h]] = jnp.where(incl, aqk[p] * decay[p], 0.0).astype(sdt)

    def state(items):
        trip = [(i, g, h) for i, (_, g) in enumerate(items) for h in heads]
        rows = [chunk_rows(ci) for ci, _ in items]
        egb = [eg_s[ci] for ci, _ in items]
        s_old = [so_ref[g, h] for _, g, h in trip]
        s_bf = [s.astype(BF16) for s in s_old]
        ws = [_dot(w_s[rows[i], hcols[h]].astype(BF16), s_bf[n]) for n, (i, g, h) in enumerate(trip)]
        qs = [_dot(qe_s[rows[i], hcols[h]].astype(BF16), s_bf[n]) for n, (i, g, h) in enumerate(trip)]
        vn_bf = [(u_s[rows[i], hcols[h]] - ws[n]).astype(BF16) for n, (i, g, h) in enumerate(trip)]
        av = [_dot(a_s[rows[i], acols[h]].astype(BF16), vn_bf[n]) for n, (i, g, h) in enumerate(trip)]
        kv = [_dot_tn(kd_s[rows[i], hcols[h]].astype(BF16), vn_bf[n]) for n, (i, g, h) in enumerate(trip)]
        for n, (i, g, h) in enumerate(trip):
            so_ref[g, h] = s_old[n] * egb[i][h:h + 1, :] + kv[n]
            o_ref[rows[i], hcols[h]] = _rms(qs[n] + av[n], dnn) * _silu(z_ref[rows[i], hcols[h]])

    if n_chunks == gpc:
        prep(list(range(n_chunks)))
    else:
        def prep_body(gi, carry):
            prep([gi * gpc + j for j in range(gpc)])
            return carry
        lax.fori_loop(0, n_chunks // gpc, prep_body, 0)
    if chunks_per_seq == 1:
        state([(g, g) for g in range(nseq)])
    else:
        assert nseq == 1

        def state_body(ci, carry):
            state([(ci, 0)])
            return carry
        lax.fori_loop(0, n_chunks, state_body, 0)


def _deltanet(u, z, ba, cw, gp, dnn, conv0, s0, *, batch, seq, nseq, tm, chunk, valid):
    nt = seq // tm
    assert nseq == 1 or nt == 1
    rows = nseq * tm
    row = lambda b, t: (b * nt + t, 0)
    per_b3 = lambda b, t: (b, 0, 0)
    per_b4 = lambda b, t: (b, 0, 0, 0)
    sdt = BF16 if chunk % 16 == 0 else F32
    kern = functools.partial(_dn_kernel, nseq=nseq, tm=tm, chunk=chunk, valid=valid)
    return pl.pallas_call(
        kern,
        out_shape=[jax.ShapeDtypeStruct((batch * seq, D_MODEL), F32),
                   jax.ShapeDtypeStruct((batch, SUBLANE, CONV_CH), F32),
                   jax.ShapeDtypeStruct((batch, DN_HEADS, DN_DK, DN_DV), F32)],
        grid=(batch // nseq, nt),
        in_specs=[pl.BlockSpec((rows, CONV_CH), row),
                  pl.BlockSpec((rows, D_MODEL), row),
                  pl.BlockSpec((rows, LANE), row),
                  pl.BlockSpec((SUBLANE, CONV_CH), lambda b, t: (0, 0)),
                  pl.BlockSpec((SUBLANE, LANE), lambda b, t: (0, 0)),
                  pl.BlockSpec((1, DN_DV), lambda b, t: (0, 0)),
                  pl.BlockSpec((nseq, SUBLANE, CONV_CH), per_b3),
                  pl.BlockSpec((nseq, DN_HEADS, DN_DK, DN_DV), per_b4)],
        out_specs=[pl.BlockSpec((rows, D_MODEL), row),
                   pl.BlockSpec((nseq, SUBLANE, CONV_CH), per_b3),
                   pl.BlockSpec((nseq, DN_HEADS, DN_DK, DN_DV), per_b4)],
        scratch_shapes=[pltpu.VMEM((nseq, CONV_CH // LANE, SUBLANE + tm, LANE), F32),
                        pltpu.VMEM((rows, D_MODEL), F32),
                        pltpu.VMEM((rows, D_MODEL), F32),
                        pltpu.VMEM((rows, D_MODEL), F32),
                        pltpu.VMEM((rows, LANE), F32),
                        pltpu.VMEM((rows, D_MODEL), F32),
                        pltpu.VMEM((rows, D_MODEL), sdt),
                        pltpu.VMEM((rows, D_MODEL), sdt),
                        pltpu.VMEM((rows, D_MODEL), sdt),
                        pltpu.VMEM((rows, DN_HEADS * chunk), sdt),
                        pltpu.VMEM((rows // chunk, DN_HEADS, DN_DV), F32)],
        compiler_params=_params(("arbitrary", "arbitrary")),
        name="deltanet",
    )(u, z, ba, cw, gp, dnn, conv0, s0)


def _rope(x, cos, sa, sb):
    cols = []
    for cb in range(x.shape[1] // LANE):
        xc = x[:, cb * LANE:(cb + 1) * LANE]
        cols.append(xc * cos + pltpu.roll(xc, LANE - ROT_DIM // 2, 1) * sa + pltpu.roll(xc, ROT_DIM // 2, 1) * sb)
    return cols[0] if len(cols) == 1 else jnp.concatenate(cols, axis=1)


def _head_pair_operands(x):
    lane = lax.broadcasted_iota(jnp.int32, (x.shape[0], LANE), 1)
    out = []
    for cb in range(x.shape[1] // LANE):
        c = x[:, cb * LANE:(cb + 1) * LANE]
        lo = jnp.where(lane < SW_HEAD_DIM, c, 0.0)
        hi = jnp.where(lane >= SW_HEAD_DIM, c, 0.0)
        out.append((lo, pltpu.roll(lo, SW_HEAD_DIM, 1)))
        out.append((pltpu.roll(hi, SW_HEAD_DIM, 1), hi))
    return out


def _ones_columns(w):
    rowi = lax.broadcasted_iota(jnp.int32, (4 * w, LANE), 0)
    lanei = lax.broadcasted_iota(jnp.int32, (4 * w, LANE), 1)
    return jnp.where((rowi < 2 * w) == (lanei < SW_HEAD_DIM), 1.0, 0.0).astype(BF16)


def _pair_softmax(scores, mask, sinks, lane, w):
    parts, es = [], []
    for half in range(2):
        s = jnp.where(mask, scores[:, half * 2 * w:(half + 1) * 2 * w], NEG_INF)
        m = jnp.maximum(jnp.max(s, axis=-1, keepdims=True), sinks[half])
        parts.append(jnp.exp(s - m).astype(BF16))
        es.append(jnp.exp(sinks[half] - m))
    return jnp.concatenate(parts, axis=1), jnp.where(lane < SW_HEAD_DIM, es[0], es[1])


def _swa_prompt_kernel(sinks_ref, qkv_ref, cos_ref, sa_ref, sb_ref, o_ref, kc_ref, vc_ref, kk2, vv2):
    n = pl.program_id(1)
    w = WINDOW

    @pl.when(n == 0)
    def _():
        kk2[...] = jnp.zeros_like(kk2)
        for j in range(SW_KV_HEADS):
            vv2[j, :, 0:LANE] = jnp.zeros((4 * w, LANE), BF16)
            vv2[j, :, LANE:2 * LANE] = _ones_columns(w)

    for j in range(SW_KV_HEADS):
        kk2[j, 0:w, :] = kk2[j, w:2 * w, :]
        kk2[j, 2 * w:3 * w, :] = kk2[j, 3 * w:4 * w, :]
        vv2[j, 0:w, 0:LANE] = vv2[j, w:2 * w, 0:LANE]
        vv2[j, 2 * w:3 * w, 0:LANE] = vv2[j, 3 * w:4 * w, 0:LANE]
    cos, sa, sb = cos_ref[...], sa_ref[...], sb_ref[...]
    q = (_rope(qkv_ref[:, 0:D_MODEL], cos, sa, sb) * (SW_HEAD_DIM ** -0.5)).astype(BF16)
    k = _rope(qkv_ref[:, D_MODEL:D_MODEL + SW_KV], cos, sa, sb)
    v = qkv_ref[:, D_MODEL + SW_KV:D_MODEL + 2 * SW_KV]
    kc_ref[...] = k
    vc_ref[...] = v
    for j, (left, right) in enumerate(_head_pair_operands(k)):
        kk2[j, w:2 * w, :] = left.astype(BF16)
        kk2[j, 3 * w:4 * w, :] = right.astype(BF16)
    for j, (left, right) in enumerate(_head_pair_operands(v)):
        vv2[j, w:2 * w, 0:LANE] = left.astype(BF16)
        vv2[j, 3 * w:4 * w, 0:LANE] = right.astype(BF16)
    qi = lax.broadcasted_iota(jnp.int32, (w, 2 * w), 0) + w
    sj = lax.broadcasted_iota(jnp.int32, (w, 2 * w), 1)
    d = qi - sj
    mask = (d >= 0) & (d <= WINDOW) & ((sj >= w) | (n > 0))
    lane = lax.broadcasted_iota(jnp.int32, (w, LANE), 1)
    n_pb = D_MODEL // LANE
    scores = [_dot_nt(q[:, pb * LANE:(pb + 1) * LANE], kk2[pb // 2]) for pb in range(n_pb)]
    soft = [_pair_softmax(scores[pb], mask, (sinks_ref[2 * pb], sinks_ref[2 * pb + 1]), lane, w)
            for pb in range(n_pb)]
    outs = [_dot(soft[pb][0], vv2[pb // 2]) for pb in range(n_pb)]
    for pb in range(n_pb):
        o_ref[:, pb * LANE:(pb + 1) * LANE] = outs[pb][:, 0:LANE] / (outs[pb][:, LANE:2 * LANE] + soft[pb][1])


def _swa_prompt(sinks, qkv, cos, sa, sb, *, batch, seq):
    nb = seq // WINDOW
    row = lambda b, n: (b * nb + n, 0)
    tab = lambda b, n: (n, 0)
    per_b = lambda b, n: (b, 0, 0)
    return pl.pallas_call(
        _swa_prompt_kernel,
        out_shape=[jax.ShapeDtypeStruct((batch * seq, D_MODEL), F32),
                   jax.ShapeDtypeStruct((batch, WINDOW, SW_KV), F32),
                   jax.ShapeDtypeStruct((batch, WINDOW, SW_KV), F32)],
        grid=(batch, nb),
        in_specs=[pl.BlockSpec(memory_space=pltpu.SMEM),
                  pl.BlockSpec((WINDOW, D_MODEL + 2 * SW_KV), row),
                  pl.BlockSpec((WINDOW, LANE), tab),
                  pl.BlockSpec((WINDOW, LANE), tab),
                  pl.BlockSpec((WINDOW, LANE), tab)],
        out_specs=[pl.BlockSpec((WINDOW, D_MODEL), row),
                   pl.BlockSpec((None, WINDOW, SW_KV), per_b),
                   pl.BlockSpec((None, WINDOW, SW_KV), per_b)],
        scratch_shapes=[pltpu.VMEM((SW_KV_HEADS, 4 * WINDOW, LANE), BF16),
                        pltpu.VMEM((SW_KV_HEADS, 4 * WINDOW, 2 * LANE), BF16)],
        compiler_params=_params(("arbitrary", "arbitrary")),
        name="swa_prompt",
    )(sinks, qkv, cos, sa, sb)


def _swa_sample_kernel(sinks_ref, qkv_ref, kbuf_ref, vbuf_ref, cos_ref, sa_ref, sb_ref,
                       o_ref, kc_ref, vc_ref, kk_s, vv_s, kk2, vv2, *, nseq, t_new):
    wb = WINDOW
    tp = SUBLANE

    @pl.when(pl.program_id(0) == 0)
    def _():
        kk2[...] = jnp.zeros_like(kk2)
        for g in range(nseq):
            for j in range(SW_KV_HEADS):
                vv2[g, j, :, 0:LANE] = jnp.zeros((4 * wb, LANE), BF16)
                vv2[g, j, :, LANE:2 * LANE] = _ones_columns(wb)

    cos, sa, sb = cos_ref[...], sa_ref[...], sb_ref[...]
    qs = []
    for g in range(nseq):
        q = (_rope(qkv_ref[g, :, 0:D_MODEL], cos, sa, sb) * (SW_HEAD_DIM ** -0.5)).astype(BF16)
        k_new = _rope(qkv_ref[g, :, D_MODEL:D_MODEL + SW_KV], cos, sa, sb)
        v_new = qkv_ref[g, :, D_MODEL + SW_KV:D_MODEL + 2 * SW_KV]
        k_old, v_old = kbuf_ref[g], vbuf_ref[g]
        kk_s[g, 0:wb, :] = k_old
        vv_s[g, 0:wb, :] = v_old
        kk_s[g, wb:wb + tp, :] = k_new
        vv_s[g, wb:wb + tp, :] = v_new
        kc_ref[g] = kk_s[g, pl.ds(t_new, wb), :]
        vc_ref[g] = vv_s[g, pl.ds(t_new, wb), :]
        for r0, n_r, kx, vx in ((0, wb, k_old, v_old), (wb, tp, k_new, v_new)):
            for j, (left, right) in enumerate(_head_pair_operands(kx)):
                kk2[g, j, r0:r0 + n_r, :] = left.astype(BF16)
                kk2[g, j, 2 * wb + r0:2 * wb + r0 + n_r, :] = right.astype(BF16)
            for j, (left, right) in enumerate(_head_pair_operands(vx)):
                vv2[g, j, r0:r0 + n_r, 0:LANE] = left.astype(BF16)
                vv2[g, j, 2 * wb + r0:2 * wb + r0 + n_r, 0:LANE] = right.astype(BF16)
        qs.append([jnp.concatenate([q[:, (2 * j) * LANE:(2 * j + 1) * LANE],
                                    q[:, (2 * j + 1) * LANE:(2 * j + 2) * LANE]], axis=0)
                   for j in range(SW_KV_HEADS)])
    rows = 2 * tp
    qt = lax.broadcasted_iota(jnp.int32, (rows, 2 * wb), 0) % tp
    sj = lax.broadcasted_iota(jnp.int32, (rows, 2 * wb), 1)
    d = qt + wb - sj
    mask = (d >= 0) & (d <= WINDOW)
    top = lax.broadcasted_iota(jnp.int32, (rows, 1), 0) < tp
    lane = lax.broadcasted_iota(jnp.int32, (rows, LANE), 1)
    items = [(g, j) for g in range(nseq) for j in range(SW_KV_HEADS)]
    scores = [_dot_nt(qs[g][j], kk2[g, j]) for g, j in items]
    soft = []
    for n, (g, j) in enumerate(items):
        sinks = [jnp.where(top, sinks_ref[4 * j + half], sinks_ref[4 * j + 2 + half]) for half in range(2)]
        soft.append(_pair_softmax(scores[n], mask, sinks, lane, wb))
    outs = [_dot(soft[n][0], vv2[g, j]) for n, (g, j) in enumerate(items)]
    for n, (g, j) in enumerate(items):
        o = outs[n][:, 0:LANE] / (outs[n][:, LANE:2 * LANE] + soft[n][1])
        o_ref[g, :, (2 * j) * LANE:(2 * j + 1) * LANE] = o[0:tp, :]
        o_ref[g, :, (2 * j + 1) * LANE:(2 * j + 2) * LANE] = o[tp:2 * tp, :]


def _swa_sample(sinks, qkv, kbuf, vbuf, cos, sa, sb, *, batch, nseq, t_new):
    per_b = lambda b: (b, 0, 0)
    tab = lambda b: (0, 0)
    kern = functools.partial(_swa_sample_kernel, nseq=nseq, t_new=t_new)
    return pl.pallas_call(
        kern,
        out_shape=[jax.ShapeDtypeStruct((batch, SUBLANE, D_MODEL), F32),
                   jax.ShapeDtypeStruct((batch, WINDOW, SW_KV), F32),
                   jax.ShapeDtypeStruct((batch, WINDOW, SW_KV), F32)],
        grid=(batch // nseq,),
        in_specs=[pl.BlockSpec(memory_space=pltpu.SMEM),
                  pl.BlockSpec((nseq, SUBLANE, D_MODEL + 2 * SW_KV), per_b),
                  pl.BlockSpec((nseq, WINDOW, SW_KV), per_b),
                  pl.BlockSpec((nseq, WINDOW, SW_KV), per_b),
                  pl.BlockSpec((SUBLANE, LANE), tab),
                  pl.BlockSpec((SUBLANE, LANE), tab),
                  pl.BlockSpec((SUBLANE, LANE), tab)],
        out_specs=[pl.BlockSpec((nseq, SUBLANE, D_MODEL), per_b),
                   pl.BlockSpec((nseq, WINDOW, SW_KV), per_b),
                   pl.BlockSpec((nseq, WINDOW, SW_KV), per_b)],
        scratch_shapes=[pltpu.VMEM((nseq, WINDOW + SUBLANE, SW_KV), F32),
                        pltpu.VMEM((nseq, WINDOW + SUBLANE, SW_KV), F32),
                        pltpu.VMEM((nseq, SW_KV_HEADS, 4 * WINDOW, LANE), BF16),
                        pltpu.VMEM((nseq, SW_KV_HEADS, 4 * WINDOW, 2 * LANE), BF16)],
        compiler_params=_params(("arbitrary",)),
        name="swa_sample",
    )(sinks, qkv, kbuf, vbuf, cos, sa, sb)


def _out_kernel(x_ref, g_ref, post_ref, ga_ref, gb_ref, odn_ref, osw_ref, w_ref, o_ref, *, per_token, tiles_per_seq):
    g = _mod_rows(g_ref, per_token, tiles_per_seq)
    y = jax.nn.sigmoid(ga_ref[...]) * odn_ref[...] + jax.nn.sigmoid(gb_ref[...]) * osw_ref[...]
    p = _dot(y.astype(BF16), w_ref[...])
    o_ref[...] = x_ref[...] + g * _rms(p, post_ref[...])


def _out(x, mod, per_token, seq_len, post, gates, o_dn, o_sw, w):
    n = x.shape[0]
    tm = min(512, n)
    kern = functools.partial(_out_kernel, per_token=per_token, tiles_per_seq=max(seq_len // tm, 1))
    row = lambda i: (i, 0)
    return pl.pallas_call(
        kern,
        out_shape=jax.ShapeDtypeStruct((n, D_MODEL), F32),
        grid=(n // tm,),
        in_specs=[pl.BlockSpec((tm, D_MODEL), row)]
        + _mod_specs(per_token, tm, (5,))
        + [_const_spec((1, D_MODEL)),
           pl.BlockSpec((tm, D_MODEL), lambda i: (i, 0)),
           pl.BlockSpec((tm, D_MODEL), lambda i: (i, 1)),
           pl.BlockSpec((tm, D_MODEL), row),
           pl.BlockSpec((tm, D_MODEL), row),
           _const_spec((D_MODEL, D_MODEL))],
        out_specs=pl.BlockSpec((tm, D_MODEL), row),
        compiler_params=_params(("arbitrary",)),
        name="out",
    )(x, mod, post, gates, gates, o_dn, o_sw, w)


def _rope_tables(pos):
    half = ROT_DIM // 2
    inv_freq = ROPE_THETA ** (-jnp.arange(half, dtype=F32) * (2.0 / ROT_DIM))
    ang = pos.astype(F32)[:, None] * inv_freq[None, :]
    cos, sin = jnp.cos(ang), jnp.sin(ang)
    n = pos.shape[0]
    rest = SW_HEAD_DIM - ROT_DIM
    c64 = jnp.concatenate([cos, cos, jnp.ones((n, rest), F32)], axis=1)
    a64 = jnp.concatenate([-sin, jnp.zeros((n, half + rest), F32)], axis=1)
    b64 = jnp.concatenate([jnp.zeros((n, half), F32), sin, jnp.zeros((n, rest), F32)], axis=1)
    rep = LANE // SW_HEAD_DIM
    return tuple(jnp.tile(t, (1, rep)) for t in (c64, a64, b64))


def _pad_rows(a, rows):
    return jnp.pad(a, ((0, rows - a.shape[0]),) + ((0, 0),) * (a.ndim - 1))


def _layer(x, mod, per_token, seq, wts, conv0, s0, dn_fn, swa_fn):
    x = _ffn(x, mod, 0, per_token, seq, wts["pre1"], wts["post1"], wts["wg1"], wts["wu1"], wts["wd1"])
    u, z, qkv, gates, ba = _proj(x, mod, per_token, seq, wts["pre2"], *wts["w_in"])
    o_dn, conv_new, s_new = dn_fn(u, z, ba, wts["conv_w"], wts["gparam"], wts["dn_norm"], conv0, s0)
    o_sw, k_new, v_new = swa_fn(qkv)
    x = _out(x, mod, per_token, seq, wts["post2"], gates, o_dn, o_sw, wts["w_out"])
    x = _ffn(x, mod, 6, per_token, seq, wts["pre3"], wts["post3"], wts["wg2"], wts["wu2"], wts["wd2"])
    return x, (k_new, v_new, conv_new, s_new)


def kernel(x_prompt, x_sample, cache_swa_k, cache_swa_v, state_conv, state_delta, c_prompt, c_sample,
           w_ada, b_ada, ffn1_norm_pre, ffn1_norm_post, ffn1_w_gate, ffn1_w_up, ffn1_w_down,
           mix_norm_pre, mix_norm_post, w_in, conv_w, a_log, dt_bias, dn_norm, sinks, w_out,
           ffn2_norm_pre, ffn2_norm_post, ffn2_w_gate, ffn2_w_up, ffn2_w_down):
    depth = w_ada.shape[0]
    bp, tp, _ = x_prompt.shape
    bs, ts, _ = x_sample.shape
    assert tp % 512 == 0 and 1 <= ts <= SUBLANE and bs % SAMPLE_SEQS == 0

    cos_p, sa_p, sb_p = _rope_tables(jnp.arange(tp))
    cos_s, sa_s, sb_s = (_pad_rows(t, SUBLANE) for t in _rope_tables(PAST_LEN + jnp.arange(ts)))

    xp = x_prompt.reshape(bp * tp, D_MODEL)
    xs = jnp.pad(x_sample, ((0, 0), (0, SUBLANE - ts), (0, 0))).reshape(bs * SUBLANE, D_MODEL)
    c_p = _pad_rows(c_prompt, SUBLANE)
    c_s = jnp.repeat(c_sample, SUBLANE, axis=0)

    outs_p, outs_s = [], []
    for l in range(depth):
        wi = w_in[l]
        n_ba = 2 * DN_HEADS
        w_in_a = wi[:, :P_QKV].astype(BF16)
        w_in_b = wi[:, P_QKV + n_ba:].astype(BF16)
        w_in_c = jnp.pad(wi[:, P_QKV:P_QKV + n_ba], ((0, 0), (0, LANE - n_ba))).astype(BF16)
        gparam = jnp.zeros((SUBLANE, LANE), F32)
        gparam = gparam.at[0, DN_HEADS:2 * DN_HEADS].set(dt_bias[l]).at[1, DN_HEADS:2 * DN_HEADS].set(a_log[l])
        wts = dict(
            pre1=ffn1_norm_pre[l][None], post1=ffn1_norm_post[l][None],
            wg1=ffn1_w_gate[l].astype(BF16), wu1=ffn1_w_up[l].astype(BF16), wd1=ffn1_w_down[l].astype(BF16),
            pre2=mix_norm_pre[l][None], post2=mix_norm_post[l][None], w_in=(w_in_a, w_in_b, w_in_c),
            conv_w=_pad_rows(conv_w[l], SUBLANE), gparam=gparam, dn_norm=dn_norm[l][None],
            w_out=w_out[l].astype(BF16),
            pre3=ffn2_norm_pre[l][None], post3=ffn2_norm_post[l][None],
            wg2=ffn2_w_gate[l].astype(BF16), wu2=ffn2_w_up[l].astype(BF16), wd2=ffn2_w_down[l].astype(BF16),
        )
        w_ada_bf = w_ada[l].astype(BF16)
        b_ada_l = b_ada[l][None]
        mod_p = _ada(c_p, w_ada_bf, b_ada_l)
        mod_s = _ada(c_s, w_ada_bf, b_ada_l)
        sink_l = sinks[l]

        conv0_p = jnp.zeros((bp, SUBLANE, CONV_CH), F32)
        s0_p = jnp.zeros((bp, DN_HEADS, DN_DK, DN_DV), F32)
        dn_p = functools.partial(_deltanet, batch=bp, seq=tp, nseq=1, tm=DN_TILE, chunk=DN_CHUNK, valid=DN_TILE)
        swa_p = lambda qkv: _swa_prompt(sink_l, qkv, cos_p, sa_p, sb_p, batch=bp, seq=tp)
        xp, st_p = _layer(xp, mod_p, False, tp, wts, conv0_p, s0_p, dn_p, swa_p)

        conv0_s = jnp.pad(state_conv[l], ((0, 0), (SUBLANE - CONV_W + 1, 0), (0, 0)))
        kbuf = cache_swa_k[l].reshape(bs, WINDOW, SW_KV)
        vbuf = cache_swa_v[l].reshape(bs, WINDOW, SW_KV)
        dn_s = functools.partial(_deltanet, batch=bs, seq=SUBLANE, nseq=SAMPLE_SEQS, tm=SUBLANE, chunk=SUBLANE,
                                 valid=ts)

        def swa_s(qkv):
            o, k, v = _swa_sample(sink_l, qkv.reshape(bs, SUBLANE, -1), kbuf, vbuf, cos_s, sa_s, sb_s,
                                  batch=bs, nseq=SAMPLE_SEQS, t_new=ts)
            return o.reshape(bs * SUBLANE, D_MODEL), k, v

        xs, st_s = _layer(xs, mod_s, True, SUBLANE, wts, conv0_s, state_delta[l], dn_s, swa_s)
        outs_p.append(st_p)
        outs_s.append(st_s)

    def stack(outs, i):
        return jnp.stack([o[i] for o in outs])

    def kv5(a, batch):
        return a.reshape(depth, batch, WINDOW, SW_KV_HEADS, SW_HEAD_DIM)

    tail = slice(SUBLANE - CONV_W + 1, SUBLANE)
    y_p = xp.reshape(bp, tp, D_MODEL)
    y_s = xs.reshape(bs, SUBLANE, D_MODEL)[:, :ts]
    return (y_p, y_s,
            kv5(stack(outs_p, 0), bp), kv5(stack(outs_p, 1), bp),
            stack(outs_p, 2)[:, :, tail], stack(outs_p, 3),
            kv5(stack(outs_s, 0), bs), kv5(stack(outs_s, 1), bs),
            stack(outs_s, 2)[:, :, tail], stack(outs_s, 3))
```

```python
import functools

import jax
import jax.numpy as jnp
from jax import lax
from jax.experimental import pallas as pl
from jax.experimental.pallas import tpu as pltpu

F32, BF16 = jnp.float32, jnp.bfloat16

D_MODEL = 1024
DN_HEADS, DN_DK, DN_DV = 8, 128, 128
CONV_W = 4
CONV_CH = 3 * D_MODEL
DN_CHUNK = 64
SW_HEAD_DIM, SW_HEADS, SW_KV_HEADS, SW_GROUP = 64, 16, 4, 4
SW_KV = SW_KV_HEADS * SW_HEAD_DIM
WINDOW = 128
ROT_DIM = 16
ROPE_THETA = 500000.0
PAST_LEN = 8192
EPS = 1e-6
NEG_INF = -1e30

LANE = 128
SUBLANE = 8
VMEM_LIMIT = 56 * 1024 * 1024

P_U, P_Z, P_QKV, P_G, P_BA = 0, 3072, 4096, 5632, 7680
P_TOTAL = 7808
HI = lax.Precision.HIGHEST

DN_TILE = 128
DN_GROUP = 4
SAMPLE_SEQS = 4


def _dot(a, b):
    return jnp.dot(a, b, preferred_element_type=F32)


def _dot_nt(a, b):
    return lax.dot_general(a, b, (((1,), (1,)), ((), ())), preferred_element_type=F32)


def _dot_tn(a, b):
    return lax.dot_general(a, b, (((0,), (0,)), ((), ())), preferred_element_type=F32)


def _dot_hi(a, b):
    return jnp.dot(a, b, preferred_element_type=F32, precision=HI)


def _mm_bf(a, b):
    return _dot(a.astype(BF16), b.astype(BF16))


def _silu(x):
    return x * jax.nn.sigmoid(x)


def _rms(x, gain):
    ms = jnp.mean(x * x, axis=-1, keepdims=True)
    return x * lax.rsqrt(ms + EPS) * gain


def _mod_rows(ref, per_token, tiles_per_seq):
    if per_token:
        return ref[...]
    b = pl.program_id(0) // tiles_per_seq
    return ref[pl.ds(b, 1), :]


def _const_spec(shape):
    return pl.BlockSpec(shape, lambda *_: (0,) * len(shape), pipeline_mode=pl.Buffered(1))


def _params(sem):
    return pltpu.CompilerParams(dimension_semantics=sem, vmem_limit_bytes=VMEM_LIMIT)


def _ada_kernel(c_ref, w_ref, b_ref, o_ref):
    h = _silu(c_ref[...]).astype(BF16)
    o_ref[...] = _dot(h, w_ref[...]) + b_ref[...]


def _ada(c, w_bf, b):
    rows, n = c.shape[0], w_bf.shape[1]
    return pl.pallas_call(
        _ada_kernel,
        out_shape=jax.ShapeDtypeStruct((rows, n), F32),
        grid=(n // D_MODEL,),
        in_specs=[
            pl.BlockSpec((rows, D_MODEL), lambda j: (0, 0)),
            pl.BlockSpec((D_MODEL, D_MODEL), lambda j: (0, j)),
            pl.BlockSpec((1, D_MODEL), lambda j: (0, j)),
        ],
        out_specs=pl.BlockSpec((rows, D_MODEL), lambda j: (0, j)),
        compiler_params=_params(("arbitrary",)),
        name="ada",
    )(c, w_bf, b)


def _mod_specs(mod_block, tm, cols):
    if mod_block is None:
        return [pl.BlockSpec((tm, D_MODEL), functools.partial(lambda i, c: (i, c), c=c)) for c in cols]
    return [pl.BlockSpec((SUBLANE, D_MODEL), functools.partial(lambda i, c: (mod_block, c), c=c)) for c in cols]


def _ffn_kernel(x_ref, sh_ref, sc_ref, g_ref, pre_ref, post_ref, wg_ref, wu_ref, wd_ref, o_ref,
                *, per_token, tiles_per_seq):
    x = x_ref[...]
    sh = _mod_rows(sh_ref, per_token, tiles_per_seq)
    sc = _mod_rows(sc_ref, per_token, tiles_per_seq)
    g = _mod_rows(g_ref, per_token, tiles_per_seq)
    h = (_rms(x, pre_ref[...]) * (1.0 + sc) + sh).astype(BF16)
    act = (_silu(_dot(h, wg_ref[...])) * _dot(h, wu_ref[...])).astype(BF16)
    y = _dot(act, wd_ref[...])
    o_ref[...] = x + 0.5 * g * _rms(y, post_ref[...])


def _ffn(x, mod, col0, mod_block, seq_len, pre, post, wg, wu, wd):
    n = x.shape[0]
    tm = min(512, n)
    dff = wg.shape[1]
    kern = functools.partial(_ffn_kernel, per_token=mod_block is None, tiles_per_seq=max(seq_len // tm, 1))
    row = lambda i: (i, 0)
    return pl.pallas_call(
        kern,
        out_shape=jax.ShapeDtypeStruct((n, D_MODEL), F32),
        grid=(n // tm,),
        in_specs=[pl.BlockSpec((tm, D_MODEL), row)]
        + _mod_specs(mod_block, tm, (col0, col0 + 1, col0 + 2))
        + [_const_spec((1, D_MODEL)), _const_spec((1, D_MODEL)),
           _const_spec((D_MODEL, dff)), _const_spec((D_MODEL, dff)), _const_spec((dff, D_MODEL))],
        out_specs=pl.BlockSpec((tm, D_MODEL), row),
        compiler_params=_params(("arbitrary",)),
        name="ffn",
    )(x, mod, mod, mod, pre, post, wg, wu, wd)


def _proj_kernel(x_ref, sh_ref, sc_ref, pre_ref, wa_ref, wb_ref, wc_ref, u_ref, z_ref, qkv_ref, g_ref, ba_ref,
                 *, per_token, tiles_per_seq):
    sh = _mod_rows(sh_ref, per_token, tiles_per_seq)
    sc = _mod_rows(sc_ref, per_token, tiles_per_seq)
    h = (_rms(x_ref[...], pre_ref[...]) * (1.0 + sc) + sh).astype(BF16)
    u_ref[...] = _dot(h, wa_ref[:, 0:P_Z])
    z_ref[...] = _dot(h, wa_ref[:, P_Z:P_QKV])
    qkv_ref[...] = _dot(h, wb_ref[:, 0:P_G - P_QKV])
    g_ref[...] = _dot(h, wb_ref[:, P_G - P_QKV:P_BA - P_QKV])
    ba_ref[...] = _dot(h, wc_ref[...])


def _proj(x, mod, mod_block, seq_len, pre, wa, wb, wc):
    n = x.shape[0]
    tm = min(256, n)
    kern = functools.partial(_proj_kernel, per_token=mod_block is None, tiles_per_seq=max(seq_len // tm, 1))
    row = lambda i: (i, 0)
    widths = (P_Z - P_U, P_QKV - P_Z, P_G - P_QKV, P_BA - P_G, P_TOTAL - P_BA)
    return pl.pallas_call(
        kern,
        out_shape=[jax.ShapeDtypeStruct((n, wd), F32) for wd in widths],
        grid=(n // tm,),
        in_specs=[pl.BlockSpec((tm, D_MODEL), row)]
        + _mod_specs(mod_block, tm, (3, 4))
        + [_const_spec((1, D_MODEL)), _const_spec(wa.shape), _const_spec(wb.shape), _const_spec(wc.shape)],
        out_specs=[pl.BlockSpec((tm, wd), row) for wd in widths],
        compiler_params=_params(("arbitrary",)),
        name="proj",
    )(x, mod, mod, pre, wa, wb, wc)


def _tri_inv_all(lows, c):
    ii = lax.broadcasted_iota(jnp.int32, (c, c), 0)
    jj = lax.broadcasted_iota(jnp.int32, (c, c), 1)
    eye = jnp.where(ii == jj, 1.0, 0.0).astype(F32)
    xs = [eye - jnp.where(ii // 2 == jj // 2, low, 0.0) for low in lows]
    b = 2
    while b < c:
        join = (ii // (2 * b) == jj // (2 * b)) & (ii // b != jj // b)
        ys = [_mm_bf(x, jnp.where(join, low, 0.0)) for x, low in zip(xs, lows)]
        xs = [x - _mm_bf(y, x) for x, y in zip(xs, ys)]
        b *= 2
    return xs


def _dn_kernel(u_ref, z_ref, ba_ref, cw_ref, gp_ref, dnn_ref, conv0_ref, s0_ref,
               o_ref, convo_ref, so_ref, ubuf, q_s, k_s, v_s, bg_s, u_s, w_s, qe_s, kd_s, a_s, eg_s,
               *, nseq, tm, chunk, valid):
    t = pl.program_id(1)
    rows_all = nseq * tm

    n_cb = CONV_CH // LANE

    @pl.when(t == 0)
    def _():
        for g in range(nseq):
            for cb in range(n_cb):
                ubuf[g, cb, 0:SUBLANE, :] = conv0_ref[g, :, cb * LANE:(cb + 1) * LANE]
        so_ref[...] = s0_ref[...]

    for g in range(nseq):
        rws = slice(g * tm, (g + 1) * tm)
        for cb in range(n_cb):
            cols = slice(cb * LANE, (cb + 1) * LANE)
            ubuf[g, cb, SUBLANE:SUBLANE + tm, :] = u_ref[g, :, cols]
            acc = ubuf[g, cb, pl.ds(SUBLANE - CONV_W + 1, tm), :] * cw_ref[0:1, cols]
            for i in range(1, CONV_W):
                acc = acc + ubuf[g, cb, pl.ds(SUBLANE - CONV_W + 1 + i, tm), :] * cw_ref[i:i + 1, cols]
            a = _silu(acc)
            sec, off = divmod(cb * LANE, D_MODEL)
            dst = slice(off, off + LANE)
            if sec == 2:
                v_s[rws, dst] = a
            else:
                an = a * lax.rsqrt(jnp.sum(a * a, axis=-1, keepdims=True) + EPS)
                if sec == 0:
                    q_s[rws, dst] = an * (DN_DK ** -0.5)
                else:
                    k_s[rws, dst] = an
            tail = ubuf[g, cb, pl.ds(valid, SUBLANE), :]
            ubuf[g, cb, 0:SUBLANE, :] = tail
            convo_ref[g, :, cols] = tail

    lane = lax.broadcasted_iota(jnp.int32, (tm, LANE), 1)
    rowi = lax.broadcasted_iota(jnp.int32, (tm, LANE), 0)
    for g in range(nseq):
        ba = ba_ref[g]
        xg = ba + gp_ref[0:1, :]
        softplus = jnp.maximum(xg, 0.0) + jnp.log1p(jnp.exp(-jnp.abs(xg)))
        gdec = -jnp.exp(gp_ref[1:2, :]) * softplus
        bg = jnp.where(lane < DN_HEADS, jax.nn.sigmoid(ba), gdec)
        bg_s[g * tm:(g + 1) * tm, :] = jnp.where(rowi < valid, bg, 0.0)

    c = chunk
    ii = lax.broadcasted_iota(jnp.int32, (c, c), 0)
    jj = lax.broadcasted_iota(jnp.int32, (c, c), 1)
    incl = ii >= jj
    strict = ii > jj
    tril = jnp.where(incl, 1.0, 0.0).astype(F32)
    sel = jnp.where(lax.broadcasted_iota(jnp.int32, (SUBLANE, LANE), 1)
                    == lax.broadcasted_iota(jnp.int32, (SUBLANE, LANE), 0) + DN_HEADS, 1.0, 0.0).astype(F32)
    dnn = dnn_ref[...]

    heads = range(DN_HEADS)
    hcols = [slice(h * DN_DK, (h + 1) * DN_DK) for h in heads]
    acols = [slice(h * c, (h + 1) * c) for h in heads]
    n_chunks = rows_all // c
    chunks_per_seq = tm // c
    gpc = DN_GROUP if n_chunks % DN_GROUP == 0 else 1

    def chunk_rows(ci):
        return pl.ds(ci * c, c) if isinstance(ci, int) else pl.ds(pl.multiple_of(ci * c, c), c)

    def prep(cis):
        n_c = len(cis)
        rows = [chunk_rows(ci) for ci in cis]
        pairs = [(j, h) for j in range(n_c) for h in heads]
        bgc = [bg_s[r, :] for r in rows]
        gcum = [_dot_hi(tril, b) for b in bgc]
        gcum_t = [lax.dot_general(sel, g, (((1,), (1,)), ((), ())), preferred_element_type=F32, precision=HI)
                  for g in gcum]
        for j in range(n_c):
            eg_s[cis[j]] = jnp.broadcast_to(jnp.exp(gcum_t[j][:, c - 1:c]), (DN_HEADS, DN_DV))
        beta = [bgc[j][:, h:h + 1] for j, h in pairs]
        gc = [gcum[j][:, DN_HEADS + h:DN_HEADS + h + 1] for j, h in pairs]
        decay = [jnp.where(incl, jnp.exp(jnp.where(incl, gc[p] - gcum_t[j][h:h + 1, :], 0.0)), 0.0)
                 for p, (j, h) in enumerate(pairs)]
        q = [q_s[rows[j], hcols[h]] for j, h in pairs]
        k = [k_s[rows[j], hcols[h]] for j, h in pairs]
        npair = range(len(pairs))
        kb = [k[p] * beta[p] for p in npair]
        k_bf = [k[p].astype(BF16) for p in npair]
        akk = [_dot_nt(kb[p].astype(BF16), k_bf[p]) for p in npair]
        aqk = [_dot_nt(q[p].astype(BF16), k_bf[p]) for p in npair]
        tinv = _tri_inv_all([jnp.where(strict, akk[p] * decay[p], 0.0) for p in npair], c)
        egc = [jnp.exp(gc[p]) for p in npair]
        rhs = [jnp.concatenate([v_s[rows[j], hcols[h]] * beta[p], kb[p] * egc[p]], axis=1)
               for p, (j, h) in enumerate(pairs)]
        uw = [_mm_bf(tinv[p], rhs[p]) for p in npair]
        sdt = w_s.dtype
        for p, (j, h) in enumerate(pairs):
            u_s[rows[j], hcols[h]] = uw[p][:, :DN_DV]
            w_s[rows[j], hcols[h]] = uw[p][:, DN_DV:].astype(sdt)
            qe_s[rows[j], hcols[h]] = (q[p] * egc[p]).astype(sdt)
            kd_s[rows[j], hcols[h]] = (k[p] * jnp.exp(gc[p][c - 1:c, :] - gc[p])).astype(sdt)
            a_s[rows[j], acols[h]] = jnp.where(incl, aqk[p] * decay[p], 0.0).astype(sdt)

    def state(cpos):
        trip = [(g, h) for g in range(nseq) for h in heads]
        rows = [chunk_rows(g * chunks_per_seq + cpos) for g in range(nseq)]
        local = chunk_rows(cpos)
        egb = [eg_s[g * chunks_per_seq + cpos] for g in range(nseq)]
        s_old = [so_ref[g, h] for g, h in trip]
        s_bf = [s.astype(BF16) for s in s_old]
        ws = [_dot(w_s[rows[g], hcols[h]].astype(BF16), s_bf[n]) for n, (g, h) in enumerate(trip)]
        qs = [_dot(qe_s[rows[g], hcols[h]].astype(BF16), s_bf[n]) for n, (g, h) in enumerate(trip)]
        vn_bf = [(u_s[rows[g], hcols[h]] - ws[n]).astype(BF16) for n, (g, h) in enumerate(trip)]
        av = [_dot(a_s[rows[g], acols[h]].astype(BF16), vn_bf[n]) for n, (g, h) in enumerate(trip)]
        kv = [_dot_tn(kd_s[rows[g], hcols[h]].astype(BF16), vn_bf[n]) for n, (g, h) in enumerate(trip)]
        for n, (g, h) in enumerate(trip):
            so_ref[g, h] = s_old[n] * egb[g][h:h + 1, :] + kv[n]
            o_ref[g, local, hcols[h]] = _rms(qs[n] + av[n], dnn) * _silu(z_ref[g, local, hcols[h]])

    for gi in range(n_chunks // gpc):
        prep([gi * gpc + j for j in range(gpc)])
    if chunks_per_seq == 1:
        state(0)
    else:
        def state_body(cpos, carry):
            state(cpos)
            return carry
        lax.fori_loop(0, chunks_per_seq, state_body, 0)


def _deltanet(u, z, ba, cw, gp, dnn, conv0, s0, *, nseq, tm, chunk, valid):
    batch, seq, _ = u.shape
    nt = seq // tm
    rows = nseq * tm
    row = lambda b, t: (b, t, 0)
    per_b3 = lambda b, t: (b, 0, 0)
    per_b4 = lambda b, t: (b, 0, 0, 0)
    sdt = BF16 if chunk % 16 == 0 else F32
    kern = functools.partial(_dn_kernel, nseq=nseq, tm=tm, chunk=chunk, valid=valid)
    return pl.pallas_call(
        kern,
        out_shape=[jax.ShapeDtypeStruct((batch, seq, D_MODEL), F32),
                   jax.ShapeDtypeStruct((batch, SUBLANE, CONV_CH), F32),
                   jax.ShapeDtypeStruct((batch, DN_HEADS, DN_DK, DN_DV), F32)],
        grid=(batch // nseq, nt),
        in_specs=[pl.BlockSpec((nseq, tm, CONV_CH), row),
                  pl.BlockSpec((nseq, tm, D_MODEL), row),
                  pl.BlockSpec((nseq, tm, LANE), row),
                  pl.BlockSpec((SUBLANE, CONV_CH), lambda b, t: (0, 0)),
                  pl.BlockSpec((SUBLANE, LANE), lambda b, t: (0, 0)),
                  pl.BlockSpec((1, DN_DV), lambda b, t: (0, 0)),
                  pl.BlockSpec((nseq, SUBLANE, CONV_CH), per_b3),
                  pl.BlockSpec((nseq, DN_HEADS, DN_DK, DN_DV), per_b4)],
        out_specs=[pl.BlockSpec((nseq, tm, D_MODEL), row),
                   pl.BlockSpec((nseq, SUBLANE, CONV_CH), per_b3),
                   pl.BlockSpec((nseq, DN_HEADS, DN_DK, DN_DV), per_b4)],
        scratch_shapes=[pltpu.VMEM((nseq, CONV_CH // LANE, SUBLANE + tm, LANE), F32),
                        pltpu.VMEM((rows, D_MODEL), F32),
                        pltpu.VMEM((rows, D_MODEL), F32),
                        pltpu.VMEM((rows, D_MODEL), F32),
                        pltpu.VMEM((rows, LANE), F32),
                        pltpu.VMEM((rows, D_MODEL), F32),
                        pltpu.VMEM((rows, D_MODEL), sdt),
                        pltpu.VMEM((rows, D_MODEL), sdt),
                        pltpu.VMEM((rows, D_MODEL), sdt),
                        pltpu.VMEM((rows, DN_HEADS * chunk), sdt),
                        pltpu.VMEM((rows // chunk, DN_HEADS, DN_DV), F32)],
        compiler_params=_params(("arbitrary", "arbitrary")),
        name="deltanet",
    )(u, z, ba, cw, gp, dnn, conv0, s0)


def _rope(x, cos, sa, sb):
    cols = []
    for cb in range(x.shape[1] // LANE):
        xc = x[:, cb * LANE:(cb + 1) * LANE]
        cols.append(xc * cos + pltpu.roll(xc, LANE - ROT_DIM // 2, 1) * sa + pltpu.roll(xc, ROT_DIM // 2, 1) * sb)
    return cols[0] if len(cols) == 1 else jnp.concatenate(cols, axis=1)


def _head_pair_operands(x):
    lane = lax.broadcasted_iota(jnp.int32, (x.shape[0], LANE), 1)
    out = []
    for cb in range(x.shape[1] // LANE):
        c = x[:, cb * LANE:(cb + 1) * LANE]
        lo = jnp.where(lane < SW_HEAD_DIM, c, 0.0)
        hi = jnp.where(lane >= SW_HEAD_DIM, c, 0.0)
        out.append((lo, pltpu.roll(lo, SW_HEAD_DIM, 1)))
        out.append((pltpu.roll(hi, SW_HEAD_DIM, 1), hi))
    return out


def _ones_columns(w):
    rowi = lax.broadcasted_iota(jnp.int32, (4 * w, LANE), 0)
    lanei = lax.broadcasted_iota(jnp.int32, (4 * w, LANE), 1)
    return jnp.where((rowi < 2 * w) == (lanei < SW_HEAD_DIM), 1.0, 0.0).astype(BF16)


def _pair_softmax(scores, mask, sinks, lane, w):
    parts, es = [], []
    for half in range(2):
        s = jnp.where(mask, scores[:, half * 2 * w:(half + 1) * 2 * w], NEG_INF)
        m = jnp.maximum(jnp.max(s, axis=-1, keepdims=True), sinks[half])
        parts.append(jnp.exp(s - m).astype(BF16))
        es.append(jnp.exp(sinks[half] - m))
    return jnp.concatenate(parts, axis=1), jnp.where(lane < SW_HEAD_DIM, es[0], es[1])


def _swa_prompt_kernel(sinks_ref, qkv_ref, cos_ref, sa_ref, sb_ref, o_ref, kc_ref, vc_ref, kk2, vv2):
    n = pl.program_id(1)
    w = WINDOW

    @pl.when(n == 0)
    def _():
        kk2[...] = jnp.zeros_like(kk2)
        for j in range(SW_KV_HEADS):
            vv2[j, :, 0:LANE] = jnp.zeros((4 * w, LANE), BF16)
            vv2[j, :, LANE:2 * LANE] = _ones_columns(w)

    for j in range(SW_KV_HEADS):
        kk2[j, 0:w, :] = kk2[j, w:2 * w, :]
        kk2[j, 2 * w:3 * w, :] = kk2[j, 3 * w:4 * w, :]
        vv2[j, 0:w, 0:LANE] = vv2[j, w:2 * w, 0:LANE]
        vv2[j, 2 * w:3 * w, 0:LANE] = vv2[j, 3 * w:4 * w, 0:LANE]
    cos, sa, sb = cos_ref[...], sa_ref[...], sb_ref[...]
    q = (_rope(qkv_ref[:, 0:D_MODEL], cos, sa, sb) * (SW_HEAD_DIM ** -0.5)).astype(BF16)
    k = _rope(qkv_ref[:, D_MODEL:D_MODEL + SW_KV], cos, sa, sb)
    v = qkv_ref[:, D_MODEL + SW_KV:D_MODEL + 2 * SW_KV]
    kc_ref[...] = k
    vc_ref[...] = v
    for j, (left, right) in enumerate(_head_pair_operands(k)):
        kk2[j, w:2 * w, :] = left.astype(BF16)
        kk2[j, 3 * w:4 * w, :] = right.astype(BF16)
    for j, (left, right) in enumerate(_head_pair_operands(v)):
        vv2[j, w:2 * w, 0:LANE] = left.astype(BF16)
        vv2[j, 3 * w:4 * w, 0:LANE] = right.astype(BF16)
    qi = lax.broadcasted_iota(jnp.int32, (w, 2 * w), 0) + w
    sj = lax.broadcasted_iota(jnp.int32, (w, 2 * w), 1)
    d = qi - sj
    mask = (d >= 0) & (d <= WINDOW) & ((sj >= w) | (n > 0))
    lane = lax.broadcasted_iota(jnp.int32, (w, LANE), 1)
    n_pb = D_MODEL // LANE
    scores = [_dot_nt(q[:, pb * LANE:(pb + 1) * LANE], kk2[pb // 2]) for pb in range(n_pb)]
    soft = [_pair_softmax(scores[pb], mask, (sinks_ref[2 * pb], sinks_ref[2 * pb + 1]), lane, w)
            for pb in range(n_pb)]
    outs = [_dot(soft[pb][0], vv2[pb // 2]) for pb in range(n_pb)]
    for pb in range(n_pb):
        o_ref[:, pb * LANE:(pb + 1) * LANE] = outs[pb][:, 0:LANE] / (outs[pb][:, LANE:2 * LANE] + soft[pb][1])


def _swa_prompt(sinks, qkv, cos, sa, sb, *, batch, seq):
    nb = seq // WINDOW
    row = lambda b, n: (b * nb + n, 0)
    tab = lambda b, n: (n, 0)
    per_b = lambda b, n: (b, 0, 0)
    return pl.pallas_call(
        _swa_prompt_kernel,
        out_shape=[jax.ShapeDtypeStruct((batch * seq, D_MODEL), F32),
                   jax.ShapeDtypeStruct((batch, WINDOW, SW_KV), F32),
                   jax.ShapeDtypeStruct((batch, WINDOW, SW_KV), F32)],
        grid=(batch, nb),
        in_specs=[pl.BlockSpec(memory_space=pltpu.SMEM),
                  pl.BlockSpec((WINDOW, D_MODEL + 2 * SW_KV), row),
                  pl.BlockSpec((WINDOW, LANE), tab),
                  pl.BlockSpec((WINDOW, LANE), tab),
                  pl.BlockSpec((WINDOW, LANE), tab)],
        out_specs=[pl.BlockSpec((WINDOW, D_MODEL), row),
                   pl.BlockSpec((None, WINDOW, SW_KV), per_b),
                   pl.BlockSpec((None, WINDOW, SW_KV), per_b)],
        scratch_shapes=[pltpu.VMEM((SW_KV_HEADS, 4 * WINDOW, LANE), BF16),
                        pltpu.VMEM((SW_KV_HEADS, 4 * WINDOW, 2 * LANE), BF16)],
        compiler_params=_params(("arbitrary", "arbitrary")),
        name="swa_prompt",
    )(sinks, qkv, cos, sa, sb)


def _swa_sample_kernel(sinks_ref, qkv_ref, kbuf_ref, vbuf_ref, cos_ref, sa_ref, sb_ref,
                       o_ref, kc_ref, vc_ref, kk_s, vv_s, kk2, vv2, *, nseq, t_new):
    wb = WINDOW
    tp = SUBLANE

    @pl.when(pl.program_id(0) == 0)
    def _():
        kk2[...] = jnp.zeros_like(kk2)
        for g in range(nseq):
            for j in range(SW_KV_HEADS):
                vv2[g, j, :, 0:LANE] = jnp.zeros((4 * wb, LANE), BF16)
                vv2[g, j, :, LANE:2 * LANE] = _ones_columns(wb)

    cos, sa, sb = cos_ref[...], sa_ref[...], sb_ref[...]
    qs = []
    for g in range(nseq):
        q = (_rope(qkv_ref[g, :, 0:D_MODEL], cos, sa, sb) * (SW_HEAD_DIM ** -0.5)).astype(BF16)
        k_new = _rope(qkv_ref[g, :, D_MODEL:D_MODEL + SW_KV], cos, sa, sb)
        v_new = qkv_ref[g, :, D_MODEL + SW_KV:D_MODEL + 2 * SW_KV]
        k_old, v_old = kbuf_ref[g], vbuf_ref[g]
        kk_s[g, 0:wb, :] = k_old
        vv_s[g, 0:wb, :] = v_old
        kk_s[g, wb:wb + tp, :] = k_new
        vv_s[g, wb:wb + tp, :] = v_new
        kc_ref[g] = kk_s[g, pl.ds(t_new, wb), :]
        vc_ref[g] = vv_s[g, pl.ds(t_new, wb), :]
        for r0, n_r, kx, vx in ((0, wb, k_old, v_old), (wb, tp, k_new, v_new)):
            for j, (left, right) in enumerate(_head_pair_operands(kx)):
                kk2[g, j, r0:r0 + n_r, :] = left.astype(BF16)
                kk2[g, j, 2 * wb + r0:2 * wb + r0 + n_r, :] = right.astype(BF16)
            for j, (left, right) in enumerate(_head_pair_operands(vx)):
                vv2[g, j, r0:r0 + n_r, 0:LANE] = left.astype(BF16)
                vv2[g, j, 2 * wb + r0:2 * wb + r0 + n_r, 0:LANE] = right.astype(BF16)
        qs.append([jnp.concatenate([q[:, (2 * j) * LANE:(2 * j + 1) * LANE],
                                    q[:, (2 * j + 1) * LANE:(2 * j + 2) * LANE]], axis=0)
                   for j in range(SW_KV_HEADS)])
    rows = 2 * tp
    qt = lax.broadcasted_iota(jnp.int32, (rows, 2 * wb), 0) % tp
    sj = lax.broadcasted_iota(jnp.int32, (rows, 2 * wb), 1)
    d = qt + wb - sj
    mask = (d >= 0) & (d <= WINDOW)
    top = lax.broadcasted_iota(jnp.int32, (rows, 1), 0) < tp
    lane = lax.broadcasted_iota(jnp.int32, (rows, LANE), 1)
    items = [(g, j) for g in range(nseq) for j in range(SW_KV_HEADS)]
    scores = [_dot_nt(qs[g][j], kk2[g, j]) for g, j in items]
    soft = []
    for n, (g, j) in enumerate(items):
        sinks = [jnp.where(top, sinks_ref[4 * j + half], sinks_ref[4 * j + 2 + half]) for half in range(2)]
        soft.append(_pair_softmax(scores[n], mask, sinks, lane, wb))
    outs = [_dot(soft[n][0], vv2[g, j]) for n, (g, j) in enumerate(items)]
    for n, (g, j) in enumerate(items):
        o = outs[n][:, 0:LANE] / (outs[n][:, LANE:2 * LANE] + soft[n][1])
        o_ref[g, :, (2 * j) * LANE:(2 * j + 1) * LANE] = o[0:tp, :]
        o_ref[g, :, (2 * j + 1) * LANE:(2 * j + 2) * LANE] = o[tp:2 * tp, :]


def _swa_sample(sinks, qkv, kbuf, vbuf, cos, sa, sb, *, batch, nseq, t_new):
    per_b = lambda b: (b, 0, 0)
    tab = lambda b: (0, 0)
    kern = functools.partial(_swa_sample_kernel, nseq=nseq, t_new=t_new)
    return pl.pallas_call(
        kern,
        out_shape=[jax.ShapeDtypeStruct((batch, SUBLANE, D_MODEL), F32),
                   jax.ShapeDtypeStruct((batch, WINDOW, SW_KV), F32),
                   jax.ShapeDtypeStruct((batch, WINDOW, SW_KV), F32)],
        grid=(batch // nseq,),
        in_specs=[pl.BlockSpec(memory_space=pltpu.SMEM),
                  pl.BlockSpec((nseq, SUBLANE, D_MODEL + 2 * SW_KV), per_b),
                  pl.BlockSpec((nseq, WINDOW, SW_KV), per_b),
                  pl.BlockSpec((nseq, WINDOW, SW_KV), per_b),
                  pl.BlockSpec((SUBLANE, LANE), tab),
                  pl.BlockSpec((SUBLANE, LANE), tab),
                  pl.BlockSpec((SUBLANE, LANE), tab)],
        out_specs=[pl.BlockSpec((nseq, SUBLANE, D_MODEL), per_b),
                   pl.BlockSpec((nseq, WINDOW, SW_KV), per_b),
                   pl.BlockSpec((nseq, WINDOW, SW_KV), per_b)],
        scratch_shapes=[pltpu.VMEM((nseq, WINDOW + SUBLANE, SW_KV), F32),
                        pltpu.VMEM((nseq, WINDOW + SUBLANE, SW_KV), F32),
                        pltpu.VMEM((nseq, SW_KV_HEADS, 4 * WINDOW, LANE), BF16),
                        pltpu.VMEM((nseq, SW_KV_HEADS, 4 * WINDOW, 2 * LANE), BF16)],
        compiler_params=_params(("arbitrary",)),
        name="swa_sample",
    )(sinks, qkv, kbuf, vbuf, cos, sa, sb)


def _out_kernel(x_ref, g_ref, post_ref, ga_ref, gb_ref, odn_ref, osw_ref, w_ref, o_ref, *, per_token, tiles_per_seq):
    g = _mod_rows(g_ref, per_token, tiles_per_seq)
    y = jax.nn.sigmoid(ga_ref[...]) * odn_ref[...] + jax.nn.sigmoid(gb_ref[...]) * osw_ref[...]
    p = _dot(y.astype(BF16), w_ref[...])
    o_ref[...] = x_ref[...] + g * _rms(p, post_ref[...])


def _out(x, mod, mod_block, seq_len, post, gates, o_dn, o_sw, w):
    n = x.shape[0]
    tm = min(512, n)
    kern = functools.partial(_out_kernel, per_token=mod_block is None, tiles_per_seq=max(seq_len // tm, 1))
    row = lambda i: (i, 0)
    return pl.pallas_call(
        kern,
        out_shape=jax.ShapeDtypeStruct((n, D_MODEL), F32),
        grid=(n // tm,),
        in_specs=[pl.BlockSpec((tm, D_MODEL), row)]
        + _mod_specs(mod_block, tm, (5,))
        + [_const_spec((1, D_MODEL)),
           pl.BlockSpec((tm, D_MODEL), lambda i: (i, 0)),
           pl.BlockSpec((tm, D_MODEL), lambda i: (i, 1)),
           pl.BlockSpec((tm, D_MODEL), row),
           pl.BlockSpec((tm, D_MODEL), row),
           _const_spec((D_MODEL, D_MODEL))],
        out_specs=pl.BlockSpec((tm, D_MODEL), row),
        compiler_params=_params(("arbitrary",)),
        name="out",
    )(x, mod, post, gates, gates, o_dn, o_sw, w)


def _rope_tables(pos):
    half = ROT_DIM // 2
    inv_freq = ROPE_THETA ** (-jnp.arange(half, dtype=F32) * (2.0 / ROT_DIM))
    ang = pos.astype(F32)[:, None] * inv_freq[None, :]
    cos, sin = jnp.cos(ang), jnp.sin(ang)
    n = pos.shape[0]
    rest = SW_HEAD_DIM - ROT_DIM
    c64 = jnp.concatenate([cos, cos, jnp.ones((n, rest), F32)], axis=1)
    a64 = jnp.concatenate([-sin, jnp.zeros((n, half + rest), F32)], axis=1)
    b64 = jnp.concatenate([jnp.zeros((n, half), F32), sin, jnp.zeros((n, rest), F32)], axis=1)
    rep = LANE // SW_HEAD_DIM
    return tuple(jnp.tile(t, (1, rep)) for t in (c64, a64, b64))


def _pad_rows(a, rows):
    return jnp.pad(a, ((0, rows - a.shape[0]),) + ((0, 0),) * (a.ndim - 1))


def _layer(x, mod, mod_block, seq, wts, conv0, s0, dn_fn, swa_fn):
    x = _ffn(x, mod, 0, mod_block, seq, wts["pre1"], wts["post1"], wts["wg1"], wts["wu1"], wts["wd1"])
    u, z, qkv, gates, ba = _proj(x, mod, mod_block, seq, wts["pre2"], *wts["w_in"])
    per_seq = lambda a: a.reshape(-1, seq, a.shape[-1])
    o_dn, conv_new, s_new = dn_fn(per_seq(u), per_seq(z), per_seq(ba), wts["conv_w"], wts["gparam"],
                                  wts["dn_norm"], conv0, s0)
    o_dn = o_dn.reshape(-1, D_MODEL)
    o_sw, k_new, v_new = swa_fn(qkv)
    x = _out(x, mod, mod_block, seq, wts["post2"], gates, o_dn, o_sw, wts["w_out"])
    x = _ffn(x, mod, 6, mod_block, seq, wts["pre3"], wts["post3"], wts["wg2"], wts["wu2"], wts["wd2"])
    return x, (k_new, v_new, conv_new, s_new)


def kernel(x_prompt, x_sample, cache_swa_k, cache_swa_v, state_conv, state_delta, c_prompt, c_sample,
           w_ada, b_ada, ffn1_norm_pre, ffn1_norm_post, ffn1_w_gate, ffn1_w_up, ffn1_w_down,
           mix_norm_pre, mix_norm_post, w_in, conv_w, a_log, dt_bias, dn_norm, sinks, w_out,
           ffn2_norm_pre, ffn2_norm_post, ffn2_w_gate, ffn2_w_up, ffn2_w_down):
    depth = w_ada.shape[0]
    bp, tp, _ = x_prompt.shape
    bs, ts, _ = x_sample.shape
    assert tp % 512 == 0 and 1 <= ts <= SUBLANE and bs % SAMPLE_SEQS == 0

    cos_p, sa_p, sb_p = _rope_tables(jnp.arange(tp))
    cos_s, sa_s, sb_s = (_pad_rows(t, SUBLANE) for t in _rope_tables(PAST_LEN + jnp.arange(ts)))

    xp = x_prompt.reshape(bp * tp, D_MODEL)
    xs = jnp.pad(x_sample, ((0, 0), (0, SUBLANE - ts), (0, 0))).reshape(bs * SUBLANE, D_MODEL)
    c_all = jnp.concatenate([jnp.repeat(c_sample, SUBLANE, axis=0), _pad_rows(c_prompt, SUBLANE)], axis=0)
    prompt_mod_block = bs

    outs_p, outs_s = [], []
    for l in range(depth):
        wi = w_in[l]
        n_ba = 2 * DN_HEADS
        w_in_a = wi[:, :P_QKV].astype(BF16)
        w_in_b = wi[:, P_QKV + n_ba:].astype(BF16)
        w_in_c = jnp.pad(wi[:, P_QKV:P_QKV + n_ba], ((0, 0), (0, LANE - n_ba))).astype(BF16)
        gparam = jnp.zeros((SUBLANE, LANE), F32)
        gparam = gparam.at[0, DN_HEADS:2 * DN_HEADS].set(dt_bias[l]).at[1, DN_HEADS:2 * DN_HEADS].set(a_log[l])
        wts = dict(
            pre1=ffn1_norm_pre[l][None], post1=ffn1_norm_post[l][None],
            wg1=ffn1_w_gate[l].astype(BF16), wu1=ffn1_w_up[l].astype(BF16), wd1=ffn1_w_down[l].astype(BF16),
            pre2=mix_norm_pre[l][None], post2=mix_norm_post[l][None], w_in=(w_in_a, w_in_b, w_in_c),
            conv_w=_pad_rows(conv_w[l], SUBLANE), gparam=gparam, dn_norm=dn_norm[l][None],
            w_out=w_out[l].astype(BF16),
            pre3=ffn2_norm_pre[l][None], post3=ffn2_norm_post[l][None],
            wg2=ffn2_w_gate[l].astype(BF16), wu2=ffn2_w_up[l].astype(BF16), wd2=ffn2_w_down[l].astype(BF16),
        )
        w_ada_bf = w_ada[l].astype(BF16)
        b_ada_l = b_ada[l][None]
        mod = _ada(c_all, w_ada_bf, b_ada_l)
        sink_l = sinks[l]

        conv0_p = jnp.zeros((bp, SUBLANE, CONV_CH), F32)
        s0_p = jnp.zeros((bp, DN_HEADS, DN_DK, DN_DV), F32)
        dn_p = functools.partial(_deltanet, nseq=bp, tm=DN_TILE, chunk=DN_CHUNK, valid=DN_TILE)
        swa_p = lambda qkv: _swa_prompt(sink_l, qkv, cos_p, sa_p, sb_p, batch=bp, seq=tp)
        xp, st_p = _layer(xp, mod, prompt_mod_block, tp, wts, conv0_p, s0_p, dn_p, swa_p)

        conv0_s = jnp.pad(state_conv[l], ((0, 0), (SUBLANE - CONV_W + 1, 0), (0, 0)))
        kbuf = cache_swa_k[l].reshape(bs, WINDOW, SW_KV)
        vbuf = cache_swa_v[l].reshape(bs, WINDOW, SW_KV)
        dn_s = functools.partial(_deltanet, nseq=SAMPLE_SEQS, tm=SUBLANE, chunk=SUBLANE, valid=ts)

        def swa_s(qkv):
            o, k, v = _swa_sample(sink_l, qkv.reshape(bs, SUBLANE, -1), kbuf, vbuf, cos_s, sa_s, sb_s,
                                  batch=bs, nseq=SAMPLE_SEQS, t_new=ts)
            return o.reshape(bs * SUBLANE, D_MODEL), k, v

        xs, st_s = _layer(xs, mod, None, SUBLANE, wts, conv0_s, state_delta[l], dn_s, swa_s)
        outs_p.append(st_p)
        outs_s.append(st_s)

    def stack(outs, i):
        return jnp.stack([o[i] for o in outs])

    def kv5(a, batch):
        return a.reshape(depth, batch, WINDOW, SW_KV_HEADS, SW_HEAD_DIM)

    tail = slice(SUBLANE - CONV_W + 1, SUBLANE)
    y_p = xp.reshape(bp, tp, D_MODEL)
    y_s = xs.reshape(bs, SUBLANE, D_MODEL)[:, :ts]
    return (y_p, y_s,
            kv5(stack(outs_p, 0), bp), kv5(stack(outs_p, 1), bp),
            stack(outs_p, 2)[:, :, tail], stack(outs_p, 3),
            kv5(stack(outs_s, 0), bs), kv5(stack(outs_s, 1), bs),
            stack(outs_s, 2)[:, :, tail], stack(outs_s, 3))
```

```python
import functools

import jax
import jax.numpy as jnp
from jax import lax
from jax.experimental import pallas as pl
from jax.experimental.pallas import tpu as pltpu

F32, BF16 = jnp.float32, jnp.bfloat16

D_MODEL = 1024
DN_HEADS, DN_DK, DN_DV = 8, 128, 128
CONV_W = 4
CONV_CH = 3 * D_MODEL
DN_CHUNK = 64
SW_HEAD_DIM, SW_HEADS, SW_KV_HEADS, SW_GROUP = 64, 16, 4, 4
SW_KV = SW_KV_HEADS * SW_HEAD_DIM
WINDOW = 128
ROT_DIM = 16
ROPE_THETA = 500000.0
PAST_LEN = 8192
EPS = 1e-6
NEG_INF = -1e30

LANE = 128
SUBLANE = 8
VMEM_LIMIT = 56 * 1024 * 1024

P_U, P_Z, P_QKV, P_G, P_BA = 0, 3072, 4096, 5632, 7680
P_TOTAL = 7808
HI = lax.Precision.HIGHEST

DN_TILE = 128
DN_GROUP = 4
SAMPLE_SEQS = 4
FFN_SUBTILES = 2


def _dot(a, b):
    return jnp.dot(a, b, preferred_element_type=F32)


def _dot_nt(a, b):
    return lax.dot_general(a, b, (((1,), (1,)), ((), ())), preferred_element_type=F32)


def _dot_tn(a, b):
    return lax.dot_general(a, b, (((0,), (0,)), ((), ())), preferred_element_type=F32)


def _dot_hi(a, b):
    return jnp.dot(a, b, preferred_element_type=F32, precision=HI)


def _mm_bf(a, b):
    return _dot(a.astype(BF16), b.astype(BF16))


def _silu(x):
    return x * jax.nn.sigmoid(x)


def _rms(x, gain):
    ms = jnp.mean(x * x, axis=-1, keepdims=True)
    return x * lax.rsqrt(ms + EPS) * gain


def _mod_rows(ref, per_token, tiles_per_seq):
    if per_token:
        return ref[...]
    b = pl.program_id(0) // tiles_per_seq
    return ref[pl.ds(b, 1), :]


def _const_spec(shape):
    return pl.BlockSpec(shape, lambda *_: (0,) * len(shape), pipeline_mode=pl.Buffered(1))


def _params(sem):
    return pltpu.CompilerParams(dimension_semantics=sem, vmem_limit_bytes=VMEM_LIMIT)


def _ada_kernel(c_ref, w_ref, b_ref, o_ref):
    h = _silu(c_ref[...]).astype(BF16)
    o_ref[...] = _dot(h, w_ref[...]) + b_ref[...]


def _ada(c, w_bf, b):
    rows, n = c.shape[0], w_bf.shape[1]
    return pl.pallas_call(
        _ada_kernel,
        out_shape=jax.ShapeDtypeStruct((rows, n), F32),
        grid=(n // D_MODEL,),
        in_specs=[
            pl.BlockSpec((rows, D_MODEL), lambda j: (0, 0)),
            pl.BlockSpec((D_MODEL, D_MODEL), lambda j: (0, j)),
            pl.BlockSpec((1, D_MODEL), lambda j: (0, j)),
        ],
        out_specs=pl.BlockSpec((rows, D_MODEL), lambda j: (0, j)),
        compiler_params=_params(("arbitrary",)),
        name="ada",
    )(c, w_bf, b)


def _mod_specs(mod_block, tm, cols):
    if mod_block is None:
        return [pl.BlockSpec((tm, D_MODEL), functools.partial(lambda i, c: (i, c), c=c)) for c in cols]
    return [pl.BlockSpec((SUBLANE, D_MODEL), functools.partial(lambda i, c: (mod_block, c), c=c)) for c in cols]


def _ffn_kernel(x_ref, sh_ref, sc_ref, g_ref, pre_ref, post_ref, wg_ref, wu_ref, wd_ref, o_ref,
                *, per_token, tiles_per_seq):
    tm = x_ref.shape[0]
    rows = [slice(i * tm // FFN_SUBTILES, (i + 1) * tm // FFN_SUBTILES) for i in range(FFN_SUBTILES)]
    sh = _mod_rows(sh_ref, per_token, tiles_per_seq)
    sc = _mod_rows(sc_ref, per_token, tiles_per_seq)
    g = _mod_rows(g_ref, per_token, tiles_per_seq)
    pick = (lambda m, r: m[r]) if per_token else (lambda m, r: m)
    xs = [x_ref[r, :] for r in rows]
    hs = [(_rms(x, pre_ref[...]) * (1.0 + pick(sc, r)) + pick(sh, r)).astype(BF16) for x, r in zip(xs, rows)]
    gates = [_dot(h, wg_ref[...]) for h in hs]
    ups = [_dot(h, wu_ref[...]) for h in hs]
    acts = [(_silu(a) * b).astype(BF16) for a, b in zip(gates, ups)]
    ys = [_dot(a, wd_ref[...]) for a in acts]
    for x, y, r in zip(xs, ys, rows):
        o_ref[r, :] = x + 0.5 * pick(g, r) * _rms(y, post_ref[...])


def _ffn(x, mod, col0, mod_block, seq_len, pre, post, wg, wu, wd):
    n = x.shape[0]
    tm = min(512, n)
    dff = wg.shape[1]
    kern = functools.partial(_ffn_kernel, per_token=mod_block is None, tiles_per_seq=max(seq_len // tm, 1))
    row = lambda i: (i, 0)
    return pl.pallas_call(
        kern,
        out_shape=jax.ShapeDtypeStruct((n, D_MODEL), F32),
        grid=(n // tm,),
        in_specs=[pl.BlockSpec((tm, D_MODEL), row)]
        + _mod_specs(mod_block, tm, (col0, col0 + 1, col0 + 2))
        + [_const_spec((1, D_MODEL)), _const_spec((1, D_MODEL)),
           _const_spec((D_MODEL, dff)), _const_spec((D_MODEL, dff)), _const_spec((dff, D_MODEL))],
        out_specs=pl.BlockSpec((tm, D_MODEL), row),
        compiler_params=_params(("arbitrary",)),
        name="ffn",
    )(x, mod, mod, mod, pre, post, wg, wu, wd)


def _proj_kernel(x_ref, sh_ref, sc_ref, pre_ref, wa_ref, wb_ref, wc_ref, u_ref, z_ref, qkv_ref, g_ref, ba_ref,
                 *, per_token, tiles_per_seq):
    sh = _mod_rows(sh_ref, per_token, tiles_per_seq)
    sc = _mod_rows(sc_ref, per_token, tiles_per_seq)
    h = (_rms(x_ref[...], pre_ref[...]) * (1.0 + sc) + sh).astype(BF16)
    u_ref[...] = _dot(h, wa_ref[:, 0:P_Z])
    z_ref[...] = _dot(h, wa_ref[:, P_Z:P_QKV])
    qkv_ref[...] = _dot(h, wb_ref[:, 0:P_G - P_QKV])
    g_ref[...] = _dot(h, wb_ref[:, P_G - P_QKV:P_BA - P_QKV]).astype(g_ref.dtype)
    ba_ref[...] = _dot(h, wc_ref[...])


def _proj(x, mod, mod_block, seq_len, pre, wa, wb, wc):
    n = x.shape[0]
    tm = min(256, n)
    kern = functools.partial(_proj_kernel, per_token=mod_block is None, tiles_per_seq=max(seq_len // tm, 1))
    row = lambda i: (i, 0)
    widths = (P_Z - P_U, P_QKV - P_Z, P_G - P_QKV, P_BA - P_G, P_TOTAL - P_BA)
    return pl.pallas_call(
        kern,
        out_shape=[jax.ShapeDtypeStruct((n, wd), BF16 if wd == P_BA - P_G else F32) for wd in widths],
        grid=(n // tm,),
        in_specs=[pl.BlockSpec((tm, D_MODEL), row)]
        + _mod_specs(mod_block, tm, (3, 4))
        + [_const_spec((1, D_MODEL)), _const_spec(wa.shape), _const_spec(wb.shape), _const_spec(wc.shape)],
        out_specs=[pl.BlockSpec((tm, wd), row) for wd in widths],
        compiler_params=_params(("arbitrary",)),
        name="proj",
    )(x, mod, mod, pre, wa, wb, wc)


def _tri_inv_all(lows, c):
    ii = lax.broadcasted_iota(jnp.int32, (c, c), 0)
    jj = lax.broadcasted_iota(jnp.int32, (c, c), 1)
    eye = jnp.where(ii == jj, 1.0, 0.0).astype(F32)
    xs = [eye - jnp.where(ii // 2 == jj // 2, low, 0.0) for low in lows]
    b = 2
    while b < c:
        join = (ii // (2 * b) == jj // (2 * b)) & (ii // b != jj // b)
        ys = [_mm_bf(x, jnp.where(join, low, 0.0)) for x, low in zip(xs, lows)]
        xs = [x - _mm_bf(y, x) for x, y in zip(xs, ys)]
        b *= 2
    return xs


def _dn_kernel(u_ref, z_ref, ba_ref, cw_ref, gp_ref, dnn_ref, conv0_ref, s0_ref,
               o_ref, convo_ref, so_ref, ubuf, q_s, k_s, v_s, bg_s, u_s, w_s, qe_s, kd_s, a_s, eg_s,
               *, nseq, tm, chunk, valid):
    t = pl.program_id(1)
    rows_all = nseq * tm

    n_cb = CONV_CH // LANE

    @pl.when(t == 0)
    def _():
        for g in range(nseq):
            for cb in range(n_cb):
                ubuf[g, cb, 0:SUBLANE, :] = conv0_ref[g, :, cb * LANE:(cb + 1) * LANE]
        so_ref[...] = s0_ref[...]

    for g in range(nseq):
        rws = slice(g * tm, (g + 1) * tm)
        for cb in range(n_cb):
            cols = slice(cb * LANE, (cb + 1) * LANE)
            ubuf[g, cb, SUBLANE:SUBLANE + tm, :] = u_ref[g, :, cols]
            acc = ubuf[g, cb, pl.ds(SUBLANE - CONV_W + 1, tm), :] * cw_ref[0:1, cols]
            for i in range(1, CONV_W):
                acc = acc + ubuf[g, cb, pl.ds(SUBLANE - CONV_W + 1 + i, tm), :] * cw_ref[i:i + 1, cols]
            a = _silu(acc)
            sec, off = divmod(cb * LANE, D_MODEL)
            dst = slice(off, off + LANE)
            if sec == 2:
                v_s[rws, dst] = a
            else:
                an = a * lax.rsqrt(jnp.sum(a * a, axis=-1, keepdims=True) + EPS)
                if sec == 0:
                    q_s[rws, dst] = an * (DN_DK ** -0.5)
                else:
                    k_s[rws, dst] = an
            tail = ubuf[g, cb, pl.ds(valid, SUBLANE), :]
            ubuf[g, cb, 0:SUBLANE, :] = tail
            convo_ref[g, :, cols] = tail

    lane = lax.broadcasted_iota(jnp.int32, (tm, LANE), 1)
    rowi = lax.broadcasted_iota(jnp.int32, (tm, LANE), 0)
    for g in range(nseq):
        ba = ba_ref[g]
        xg = ba + gp_ref[0:1, :]
        softplus = jnp.maximum(xg, 0.0) + jnp.log1p(jnp.exp(-jnp.abs(xg)))
        gdec = -jnp.exp(gp_ref[1:2, :]) * softplus
        bg = jnp.where(lane < DN_HEADS, jax.nn.sigmoid(ba), gdec)
        bg_s[g * tm:(g + 1) * tm, :] = jnp.where(rowi < valid, bg, 0.0)

    c = chunk
    ii = lax.broadcasted_iota(jnp.int32, (c, c), 0)
    jj = lax.broadcasted_iota(jnp.int32, (c, c), 1)
    incl = ii >= jj
    strict = ii > jj
    tril = jnp.where(incl, 1.0, 0.0).astype(F32)
    sel = jnp.where(lax.broadcasted_iota(jnp.int32, (SUBLANE, LANE), 1)
                    == lax.broadcasted_iota(jnp.int32, (SUBLANE, LANE), 0) + DN_HEADS, 1.0, 0.0).astype(F32)
    dnn = dnn_ref[...]

    heads = range(DN_HEADS)
    hcols = [slice(h * DN_DK, (h + 1) * DN_DK) for h in heads]
    acols = [slice(h * c, (h + 1) * c) for h in heads]
    n_chunks = rows_all // c
    chunks_per_seq = tm // c
    gpc = DN_GROUP if n_chunks % DN_GROUP == 0 else 1

    def chunk_rows(ci):
        return pl.ds(ci * c, c) if isinstance(ci, int) else pl.ds(pl.multiple_of(ci * c, c), c)

    def prep(cis):
        n_c = len(cis)
        rows = [chunk_rows(ci) for ci in cis]
        pairs = [(j, h) for j in range(n_c) for h in heads]
        bgc = [bg_s[r, :] for r in rows]
        gcum = [_dot_hi(tril, b) for b in bgc]
        gcum_t = [lax.dot_general(sel, g, (((1,), (1,)), ((), ())), preferred_element_type=F32, precision=HI)
                  for g in gcum]
        for j in range(n_c):
            eg_s[cis[j]] = jnp.broadcast_to(jnp.exp(gcum_t[j][:, c - 1:c]), (DN_HEADS, DN_DV))
        beta = [bgc[j][:, h:h + 1] for j, h in pairs]
        gc = [gcum[j][:, DN_HEADS + h:DN_HEADS + h + 1] for j, h in pairs]
        decay = [jnp.where(incl, jnp.exp(jnp.where(incl, gc[p] - gcum_t[j][h:h + 1, :], 0.0)), 0.0)
                 for p, (j, h) in enumerate(pairs)]
        q = [q_s[rows[j], hcols[h]] for j, h in pairs]
        k = [k_s[rows[j], hcols[h]] for j, h in pairs]
        npair = range(len(pairs))
        kb = [k[p] * beta[p] for p in npair]
        k_bf = [k[p].astype(BF16) for p in npair]
        akk = [_dot_nt(kb[p].astype(BF16), k_bf[p]) for p in npair]
        aqk = [_dot_nt(q[p].astype(BF16), k_bf[p]) for p in npair]
        tinv = _tri_inv_all([jnp.where(strict, akk[p] * decay[p], 0.0) for p in npair], c)
        egc = [jnp.exp(gc[p]) for p in npair]
        rhs = [jnp.concatenate([v_s[rows[j], hcols[h]] * beta[p], kb[p] * egc[p]], axis=1)
               for p, (j, h) in enumerate(pairs)]
        uw = [_mm_bf(tinv[p], rhs[p]) for p in npair]
        sdt = w_s.dtype
        for p, (j, h) in enumerate(pairs):
            u_s[rows[j], hcols[h]] = uw[p][:, :DN_DV]
            w_s[rows[j], hcols[h]] = uw[p][:, DN_DV:].astype(sdt)
            qe_s[rows[j], hcols[h]] = (q[p] * egc[p]).astype(sdt)
            kd_s[rows[j], hcols[h]] = (k[p] * jnp.exp(gc[p][c - 1:c, :] - gc[p])).astype(sdt)
            a_s[rows[j], acols[h]] = jnp.where(incl, aqk[p] * decay[p], 0.0).astype(sdt)

    def state(cpos):
        trip = [(g, h) for g in range(nseq) for h in heads]
        rows = [chunk_rows(g * chunks_per_seq + cpos) for g in range(nseq)]
        local = chunk_rows(cpos)
        egb = [eg_s[g * chunks_per_seq + cpos] for g in range(nseq)]
        s_old = [so_ref[g, h] for g, h in trip]
        s_bf = [s.astype(BF16) for s in s_old]
        ws = [_dot(w_s[rows[g], hcols[h]].astype(BF16), s_bf[n]) for n, (g, h) in enumerate(trip)]
        qs = [_dot(qe_s[rows[g], hcols[h]].astype(BF16), s_bf[n]) for n, (g, h) in enumerate(trip)]
        vn_bf = [(u_s[rows[g], hcols[h]] - ws[n]).astype(BF16) for n, (g, h) in enumerate(trip)]
        av = [_dot(a_s[rows[g], acols[h]].astype(BF16), vn_bf[n]) for n, (g, h) in enumerate(trip)]
        kv = [_dot_tn(kd_s[rows[g], hcols[h]].astype(BF16), vn_bf[n]) for n, (g, h) in enumerate(trip)]
        for n, (g, h) in enumerate(trip):
            so_ref[g, h] = s_old[n] * egb[g][h:h + 1, :] + kv[n]
            o = _rms(qs[n] + av[n], dnn) * _silu(z_ref[g, local, hcols[h]])
            o_ref[g, local, hcols[h]] = o.astype(o_ref.dtype)

    for gi in range(n_chunks // gpc):
        prep([gi * gpc + j for j in range(gpc)])
    if chunks_per_seq == 1:
        state(0)
    else:
        def state_body(cpos, carry):
            state(cpos)
            return carry
        lax.fori_loop(0, chunks_per_seq, state_body, 0)


def _deltanet(u, z, ba, cw, gp, dnn, conv0, s0, *, nseq, tm, chunk, valid, o_dtype):
    batch, seq, _ = u.shape
    nt = seq // tm
    rows = nseq * tm
    row = lambda b, t: (b, t, 0)
    per_b3 = lambda b, t: (b, 0, 0)
    per_b4 = lambda b, t: (b, 0, 0, 0)
    sdt = BF16 if chunk % 16 == 0 else F32
    kern = functools.partial(_dn_kernel, nseq=nseq, tm=tm, chunk=chunk, valid=valid)
    return pl.pallas_call(
        kern,
        out_shape=[jax.ShapeDtypeStruct((batch, seq, D_MODEL), o_dtype),
                   jax.ShapeDtypeStruct((batch, SUBLANE, CONV_CH), F32),
                   jax.ShapeDtypeStruct((batch, DN_HEADS, DN_DK, DN_DV), F32)],
        grid=(batch // nseq, nt),
        in_specs=[pl.BlockSpec((nseq, tm, CONV_CH), row),
                  pl.BlockSpec((nseq, tm, D_MODEL), row),
                  pl.BlockSpec((nseq, tm, LANE), row),
                  pl.BlockSpec((SUBLANE, CONV_CH), lambda b, t: (0, 0)),
                  pl.BlockSpec((SUBLANE, LANE), lambda b, t: (0, 0)),
                  pl.BlockSpec((1, DN_DV), lambda b, t: (0, 0)),
                  pl.BlockSpec((nseq, SUBLANE, CONV_CH), per_b3),
                  pl.BlockSpec((nseq, DN_HEADS, DN_DK, DN_DV), per_b4)],
        out_specs=[pl.BlockSpec((nseq, tm, D_MODEL), row),
                   pl.BlockSpec((nseq, SUBLANE, CONV_CH), per_b3),
                   pl.BlockSpec((nseq, DN_HEADS, DN_DK, DN_DV), per_b4)],
        scratch_shapes=[pltpu.VMEM((nseq, CONV_CH // LANE, SUBLANE + tm, LANE), F32),
                        pltpu.VMEM((rows, D_MODEL), F32),
                        pltpu.VMEM((rows, D_MODEL), F32),
                        pltpu.VMEM((rows, D_MODEL), F32),
                        pltpu.VMEM((rows, LANE), F32),
                        pltpu.VMEM((rows, D_MODEL), F32),
                        pltpu.VMEM((rows, D_MODEL), sdt),
                        pltpu.VMEM((rows, D_MODEL), sdt),
                        pltpu.VMEM((rows, D_MODEL), sdt),
                        pltpu.VMEM((rows, DN_HEADS * chunk), sdt),
                        pltpu.VMEM((rows // chunk, DN_HEADS, DN_DV), F32)],
        compiler_params=_params(("arbitrary", "arbitrary")),
        name="deltanet",
    )(u, z, ba, cw, gp, dnn, conv0, s0)


def _rope(x, cos, sa, sb):
    cols = []
    for cb in range(x.shape[1] // LANE):
        xc = x[:, cb * LANE:(cb + 1) * LANE]
        cols.append(xc * cos + pltpu.roll(xc, LANE - ROT_DIM // 2, 1) * sa + pltpu.roll(xc, ROT_DIM // 2, 1) * sb)
    return cols[0] if len(cols) == 1 else jnp.concatenate(cols, axis=1)


def _head_pair_operands(x):
    lane = lax.broadcasted_iota(jnp.int32, (x.shape[0], LANE), 1)
    out = []
    for cb in range(x.shape[1] // LANE):
        c = x[:, cb * LANE:(cb + 1) * LANE]
        lo = jnp.where(lane < SW_HEAD_DIM, c, 0.0)
        hi = jnp.where(lane >= SW_HEAD_DIM, c, 0.0)
        out.append((lo, pltpu.roll(lo, SW_HEAD_DIM, 1)))
        out.append((pltpu.roll(hi, SW_HEAD_DIM, 1), hi))
    return out


def _ones_columns(w):
    rowi = lax.broadcasted_iota(jnp.int32, (4 * w, LANE), 0)
    lanei = lax.broadcasted_iota(jnp.int32, (4 * w, LANE), 1)
    return jnp.where((rowi < 2 * w) == (lanei < SW_HEAD_DIM), 1.0, 0.0).astype(BF16)


def _pair_softmax(scores, mask, sinks, lane, w):
    parts, es = [], []
    for half in range(2):
        s = jnp.where(mask, scores[:, half * 2 * w:(half + 1) * 2 * w], NEG_INF)
        m = jnp.maximum(jnp.max(s, axis=-1, keepdims=True), sinks[half])
        parts.append(jnp.exp(s - m).astype(BF16))
        es.append(jnp.exp(sinks[half] - m))
    return jnp.concatenate(parts, axis=1), jnp.where(lane < SW_HEAD_DIM, es[0], es[1])


def _swa_prompt_kernel(sinks_ref, qkv_ref, cos_ref, sa_ref, sb_ref, o_ref, kc_ref, vc_ref, kk2, vv2):
    n = pl.program_id(1)
    w = WINDOW

    @pl.when(n == 0)
    def _():
        kk2[...] = jnp.zeros_like(kk2)
        for j in range(SW_KV_HEADS):
            vv2[j, :, 0:LANE] = jnp.zeros((4 * w, LANE), BF16)
            vv2[j, :, LANE:2 * LANE] = _ones_columns(w)

    for j in range(SW_KV_HEADS):
        kk2[j, 0:w, :] = kk2[j, w:2 * w, :]
        kk2[j, 2 * w:3 * w, :] = kk2[j, 3 * w:4 * w, :]
        vv2[j, 0:w, 0:LANE] = vv2[j, w:2 * w, 0:LANE]
        vv2[j, 2 * w:3 * w, 0:LANE] = vv2[j, 3 * w:4 * w, 0:LANE]
    cos, sa, sb = cos_ref[...], sa_ref[...], sb_ref[...]
    q = (_rope(qkv_ref[:, 0:D_MODEL], cos, sa, sb) * (SW_HEAD_DIM ** -0.5)).astype(BF16)
    k = _rope(qkv_ref[:, D_MODEL:D_MODEL + SW_KV], cos, sa, sb)
    v = qkv_ref[:, D_MODEL + SW_KV:D_MODEL + 2 * SW_KV]
    kc_ref[...] = k
    vc_ref[...] = v
    for j, (left, right) in enumerate(_head_pair_operands(k)):
        kk2[j, w:2 * w, :] = left.astype(BF16)
        kk2[j, 3 * w:4 * w, :] = right.astype(BF16)
    for j, (left, right) in enumerate(_head_pair_operands(v)):
        vv2[j, w:2 * w, 0:LANE] = left.astype(BF16)
        vv2[j, 3 * w:4 * w, 0:LANE] = right.astype(BF16)
    qi = lax.broadcasted_iota(jnp.int32, (w, 2 * w), 0) + w
    sj = lax.broadcasted_iota(jnp.int32, (w, 2 * w), 1)
    d = qi - sj
    mask = (d >= 0) & (d <= WINDOW) & ((sj >= w) | (n > 0))
    lane = lax.broadcasted_iota(jnp.int32, (w, LANE), 1)
    n_pb = D_MODEL // LANE
    scores = [_dot_nt(q[:, pb * LANE:(pb + 1) * LANE], kk2[pb // 2]) for pb in range(n_pb)]
    soft = [_pair_softmax(scores[pb], mask, (sinks_ref[2 * pb], sinks_ref[2 * pb + 1]), lane, w)
            for pb in range(n_pb)]
    outs = [_dot(soft[pb][0], vv2[pb // 2]) for pb in range(n_pb)]
    for pb in range(n_pb):
        o = outs[pb][:, 0:LANE] / (outs[pb][:, LANE:2 * LANE] + soft[pb][1])
        o_ref[:, pb * LANE:(pb + 1) * LANE] = o.astype(o_ref.dtype)


def _swa_prompt(sinks, qkv, cos, sa, sb, *, batch, seq):
    nb = seq // WINDOW
    row = lambda b, n: (b * nb + n, 0)
    tab = lambda b, n: (n, 0)
    per_b = lambda b, n: (b, 0, 0)
    return pl.pallas_call(
        _swa_prompt_kernel,
        out_shape=[jax.ShapeDtypeStruct((batch * seq, D_MODEL), BF16),
                   jax.ShapeDtypeStruct((batch, WINDOW, SW_KV), F32),
                   jax.ShapeDtypeStruct((batch, WINDOW, SW_KV), F32)],
        grid=(batch, nb),
        in_specs=[pl.BlockSpec(memory_space=pltpu.SMEM),
                  pl.BlockSpec((WINDOW, D_MODEL + 2 * SW_KV), row),
                  pl.BlockSpec((WINDOW, LANE), tab),
                  pl.BlockSpec((WINDOW, LANE), tab),
                  pl.BlockSpec((WINDOW, LANE), tab)],
        out_specs=[pl.BlockSpec((WINDOW, D_MODEL), row),
                   pl.BlockSpec((None, WINDOW, SW_KV), per_b),
                   pl.BlockSpec((None, WINDOW, SW_KV), per_b)],
        scratch_shapes=[pltpu.VMEM((SW_KV_HEADS, 4 * WINDOW, LANE), BF16),
                        pltpu.VMEM((SW_KV_HEADS, 4 * WINDOW, 2 * LANE), BF16)],
        compiler_params=_params(("arbitrary", "arbitrary")),
        name="swa_prompt",
    )(sinks, qkv, cos, sa, sb)


def _swa_sample_kernel(sinks_ref, qkv_ref, kbuf_ref, vbuf_ref, cos_ref, sa_ref, sb_ref,
                       o_ref, kc_ref, vc_ref, kk_s, vv_s, kk2, vv2, *, nseq, t_new):
    wb = WINDOW
    tp = SUBLANE

    @pl.when(pl.program_id(0) == 0)
    def _():
        kk2[...] = jnp.zeros_like(kk2)
        for g in range(nseq):
            for j in range(SW_KV_HEADS):
                vv2[g, j, :, 0:LANE] = jnp.zeros((4 * wb, LANE), BF16)
                vv2[g, j, :, LANE:2 * LANE] = _ones_columns(wb)

    cos, sa, sb = cos_ref[...], sa_ref[...], sb_ref[...]
    qs = []
    for g in range(nseq):
        q = (_rope(qkv_ref[g, :, 0:D_MODEL], cos, sa, sb) * (SW_HEAD_DIM ** -0.5)).astype(BF16)
        k_new = _rope(qkv_ref[g, :, D_MODEL:D_MODEL + SW_KV], cos, sa, sb)
        v_new = qkv_ref[g, :, D_MODEL + SW_KV:D_MODEL + 2 * SW_KV]
        k_old, v_old = kbuf_ref[g], vbuf_ref[g]
        kk_s[g, 0:wb, :] = k_old
        vv_s[g, 0:wb, :] = v_old
        kk_s[g, wb:wb + tp, :] = k_new
        vv_s[g, wb:wb + tp, :] = v_new
        kc_ref[g] = kk_s[g, pl.ds(t_new, wb), :]
        vc_ref[g] = vv_s[g, pl.ds(t_new, wb), :]
        for r0, n_r, kx, vx in ((0, wb, k_old, v_old), (wb, tp, k_new, v_new)):
            for j, (left, right) in enumerate(_head_pair_operands(kx)):
                kk2[g, j, r0:r0 + n_r, :] = left.astype(BF16)
                kk2[g, j, 2 * wb + r0:2 * wb + r0 + n_r, :] = right.astype(BF16)
            for j, (left, right) in enumerate(_head_pair_operands(vx)):
                vv2[g, j, r0:r0 + n_r, 0:LANE] = left.astype(BF16)
                vv2[g, j, 2 * wb + r0:2 * wb + r0 + n_r, 0:LANE] = right.astype(BF16)
        qs.append([jnp.concatenate([q[:, (2 * j) * LANE:(2 * j + 1) * LANE],
                                    q[:, (2 * j + 1) * LANE:(2 * j + 2) * LANE]], axis=0)
                   for j in range(SW_KV_HEADS)])
    rows = 2 * tp
    qt = lax.broadcasted_iota(jnp.int32, (rows, 2 * wb), 0) % tp
    sj = lax.broadcasted_iota(jnp.int32, (rows, 2 * wb), 1)
    d = qt + wb - sj
    mask = (d >= 0) & (d <= WINDOW)
    top = lax.broadcasted_iota(jnp.int32, (rows, 1), 0) < tp
    lane = lax.broadcasted_iota(jnp.int32, (rows, LANE), 1)
    items = [(g, j) for g in range(nseq) for j in range(SW_KV_HEADS)]
    scores = [_dot_nt(qs[g][j], kk2[g, j]) for g, j in items]
    soft = []
    for n, (g, j) in enumerate(items):
        sinks = [jnp.where(top, sinks_ref[4 * j + half], sinks_ref[4 * j + 2 + half]) for half in range(2)]
        soft.append(_pair_softmax(scores[n], mask, sinks, lane, wb))
    outs = [_dot(soft[n][0], vv2[g, j]) for n, (g, j) in enumerate(items)]
    for n, (g, j) in enumerate(items):
        o = outs[n][:, 0:LANE] / (outs[n][:, LANE:2 * LANE] + soft[n][1])
        o_ref[g, :, (2 * j) * LANE:(2 * j + 1) * LANE] = o[0:tp, :]
        o_ref[g, :, (2 * j + 1) * LANE:(2 * j + 2) * LANE] = o[tp:2 * tp, :]


def _swa_sample(sinks, qkv, kbuf, vbuf, cos, sa, sb, *, batch, nseq, t_new):
    per_b = lambda b: (b, 0, 0)
    tab = lambda b: (0, 0)
    kern = functools.partial(_swa_sample_kernel, nseq=nseq, t_new=t_new)
    return pl.pallas_call(
        kern,
        out_shape=[jax.ShapeDtypeStruct((batch, SUBLANE, D_MODEL), F32),
                   jax.ShapeDtypeStruct((batch, WINDOW, SW_KV), F32),
                   jax.ShapeDtypeStruct((batch, WINDOW, SW_KV), F32)],
        grid=(batch // nseq,),
        in_specs=[pl.BlockSpec(memory_space=pltpu.SMEM),
                  pl.BlockSpec((nseq, SUBLANE, D_MODEL + 2 * SW_KV), per_b),
                  pl.BlockSpec((nseq, WINDOW, SW_KV), per_b),
                  pl.BlockSpec((nseq, WINDOW, SW_KV), per_b),
                  pl.BlockSpec((SUBLANE, LANE), tab),
                  pl.BlockSpec((SUBLANE, LANE), tab),
                  pl.BlockSpec((SUBLANE, LANE), tab)],
        out_specs=[pl.BlockSpec((nseq, SUBLANE, D_MODEL), per_b),
                   pl.BlockSpec((nseq, WINDOW, SW_KV), per_b),
                   pl.BlockSpec((nseq, WINDOW, SW_KV), per_b)],
        scratch_shapes=[pltpu.VMEM((nseq, WINDOW + SUBLANE, SW_KV), F32),
                        pltpu.VMEM((nseq, WINDOW + SUBLANE, SW_KV), F32),
                        pltpu.VMEM((nseq, SW_KV_HEADS, 4 * WINDOW, LANE), BF16),
                        pltpu.VMEM((nseq, SW_KV_HEADS, 4 * WINDOW, 2 * LANE), BF16)],
        compiler_params=_params(("arbitrary",)),
        name="swa_sample",
    )(sinks, qkv, kbuf, vbuf, cos, sa, sb)


def _out_kernel(x_ref, g_ref, post_ref, ga_ref, gb_ref, odn_ref, osw_ref, w_ref, o_ref, *, per_token, tiles_per_seq):
    g = _mod_rows(g_ref, per_token, tiles_per_seq)
    f32 = lambda ref: ref[...].astype(F32)
    y = jax.nn.sigmoid(f32(ga_ref)) * f32(odn_ref) + jax.nn.sigmoid(f32(gb_ref)) * f32(osw_ref)
    p = _dot(y.astype(BF16), w_ref[...])
    o_ref[...] = x_ref[...] + g * _rms(p, post_ref[...])


def _out(x, mod, mod_block, seq_len, post, gates, o_dn, o_sw, w):
    n = x.shape[0]
    tm = min(512, n)
    kern = functools.partial(_out_kernel, per_token=mod_block is None, tiles_per_seq=max(seq_len // tm, 1))
    row = lambda i: (i, 0)
    return pl.pallas_call(
        kern,
        out_shape=jax.ShapeDtypeStruct((n, D_MODEL), F32),
        grid=(n // tm,),
        in_specs=[pl.BlockSpec((tm, D_MODEL), row)]
        + _mod_specs(mod_block, tm, (5,))
        + [_const_spec((1, D_MODEL)),
           pl.BlockSpec((tm, D_MODEL), lambda i: (i, 0)),
           pl.BlockSpec((tm, D_MODEL), lambda i: (i, 1)),
           pl.BlockSpec((tm, D_MODEL), row),
           pl.BlockSpec((tm, D_MODEL), row),
           _const_spec((D_MODEL, D_MODEL))],
        out_specs=pl.BlockSpec((tm, D_MODEL), row),
        compiler_params=_params(("arbitrary",)),
        name="out",
    )(x, mod, post, gates, gates, o_dn, o_sw, w)


def _rope_tables(pos):
    half = ROT_DIM // 2
    inv_freq = ROPE_THETA ** (-jnp.arange(half, dtype=F32) * (2.0 / ROT_DIM))
    ang = pos.astype(F32)[:, None] * inv_freq[None, :]
    cos, sin = jnp.cos(ang), jnp.sin(ang)
    n = pos.shape[0]
    rest = SW_HEAD_DIM - ROT_DIM
    c64 = jnp.concatenate([cos, cos, jnp.ones((n, rest), F32)], axis=1)
    a64 = jnp.concatenate([-sin, jnp.zeros((n, half + rest), F32)], axis=1)
    b64 = jnp.concatenate([jnp.zeros((n, half), F32), sin, jnp.zeros((n, rest), F32)], axis=1)
    rep = LANE // SW_HEAD_DIM
    return tuple(jnp.tile(t, (1, rep)) for t in (c64, a64, b64))


def _pad_rows(a, rows):
    return jnp.pad(a, ((0, rows - a.shape[0]),) + ((0, 0),) * (a.ndim - 1))


def _layer(x, mod, mod_block, seq, wts, conv0, s0, dn_fn, swa_fn):
    x = _ffn(x, mod, 0, mod_block, seq, wts["pre1"], wts["post1"], wts["wg1"], wts["wu1"], wts["wd1"])
    u, z, qkv, gates, ba = _proj(x, mod, mod_block, seq, wts["pre2"], *wts["w_in"])
    per_seq = lambda a: a.reshape(-1, seq, a.shape[-1])
    o_dn, conv_new, s_new = dn_fn(per_seq(u), per_seq(z), per_seq(ba), wts["conv_w"], wts["gparam"],
                                  wts["dn_norm"], conv0, s0)
    o_dn = o_dn.reshape(-1, D_MODEL)
    o_sw, k_new, v_new = swa_fn(qkv)
    x = _out(x, mod, mod_block, seq, wts["post2"], gates, o_dn, o_sw, wts["w_out"])
    x = _ffn(x, mod, 6, mod_block, seq, wts["pre3"], wts["post3"], wts["wg2"], wts["wu2"], wts["wd2"])
    return x, (k_new, v_new, conv_new, s_new)


def kernel(x_prompt, x_sample, cache_swa_k, cache_swa_v, state_conv, state_delta, c_prompt, c_sample,
           w_ada, b_ada, ffn1_norm_pre, ffn1_norm_post, ffn1_w_gate, ffn1_w_up, ffn1_w_down,
           mix_norm_pre, mix_norm_post, w_in, conv_w, a_log, dt_bias, dn_norm, sinks, w_out,
           ffn2_norm_pre, ffn2_norm_post, ffn2_w_gate, ffn2_w_up, ffn2_w_down):
    depth = w_ada.shape[0]
    bp, tp, _ = x_prompt.shape
    bs, ts, _ = x_sample.shape
    assert tp % 512 == 0 and 1 <= ts <= SUBLANE and bs % SAMPLE_SEQS == 0

    cos_p, sa_p, sb_p = _rope_tables(jnp.arange(tp))
    cos_s, sa_s, sb_s = (_pad_rows(t, SUBLANE) for t in _rope_tables(PAST_LEN + jnp.arange(ts)))

    xp = x_prompt.reshape(bp * tp, D_MODEL)
    xs = jnp.pad(x_sample, ((0, 0), (0, SUBLANE - ts), (0, 0))).reshape(bs * SUBLANE, D_MODEL)
    c_all = jnp.concatenate([jnp.repeat(c_sample, SUBLANE, axis=0), _pad_rows(c_prompt, SUBLANE)], axis=0)
    prompt_mod_block = bs

    outs_p, outs_s = [], []
    for l in range(depth):
        wi = w_in[l]
        n_ba = 2 * DN_HEADS
        w_in_a = wi[:, :P_QKV].astype(BF16)
        w_in_b = wi[:, P_QKV + n_ba:].astype(BF16)
        w_in_c = jnp.pad(wi[:, P_QKV:P_QKV + n_ba], ((0, 0), (0, LANE - n_ba))).astype(BF16)
        gparam = jnp.zeros((SUBLANE, LANE), F32)
        gparam = gparam.at[0, DN_HEADS:2 * DN_HEADS].set(dt_bias[l]).at[1, DN_HEADS:2 * DN_HEADS].set(a_log[l])
        wts = dict(
            pre1=ffn1_norm_pre[l][None], post1=ffn1_norm_post[l][None],
            wg1=ffn1_w_gate[l].astype(BF16), wu1=ffn1_w_up[l].astype(BF16), wd1=ffn1_w_down[l].astype(BF16),
            pre2=mix_norm_pre[l][None], post2=mix_norm_post[l][None], w_in=(w_in_a, w_in_b, w_in_c),
            conv_w=_pad_rows(conv_w[l], SUBLANE), gparam=gparam, dn_norm=dn_norm[l][None],
            w_out=w_out[l].astype(BF16),
            pre3=ffn2_norm_pre[l][None], post3=ffn2_norm_post[l][None],
            wg2=ffn2_w_gate[l].astype(BF16), wu2=ffn2_w_up[l].astype(BF16), wd2=ffn2_w_down[l].astype(BF16),
        )
        w_ada_bf = w_ada[l].astype(BF16)
        b_ada_l = b_ada[l][None]
        mod = _ada(c_all, w_ada_bf, b_ada_l)
        sink_l = sinks[l]

        conv0_p = jnp.zeros((bp, SUBLANE, CONV_CH), F32)
        s0_p = jnp.zeros((bp, DN_HEADS, DN_DK, DN_DV), F32)
        dn_p = functools.partial(_deltanet, nseq=bp, tm=DN_TILE, chunk=DN_CHUNK, valid=DN_TILE, o_dtype=BF16)
        swa_p = lambda qkv: _swa_prompt(sink_l, qkv, cos_p, sa_p, sb_p, batch=bp, seq=tp)
        xp, st_p = _layer(xp, mod, prompt_mod_block, tp, wts, conv0_p, s0_p, dn_p, swa_p)

        conv0_s = jnp.pad(state_conv[l], ((0, 0), (SUBLANE - CONV_W + 1, 0), (0, 0)))
        kbuf = cache_swa_k[l].reshape(bs, WINDOW, SW_KV)
        vbuf = cache_swa_v[l].reshape(bs, WINDOW, SW_KV)
        dn_s = functools.partial(_deltanet, nseq=SAMPLE_SEQS, tm=SUBLANE, chunk=SUBLANE, valid=ts, o_dtype=F32)

        def swa_s(qkv):
            o, k, v = _swa_sample(sink_l, qkv.reshape(bs, SUBLANE, -1), kbuf, vbuf, cos_s, sa_s, sb_s,
                                  batch=bs, nseq=SAMPLE_SEQS, t_new=ts)
            return o.reshape(bs * SUBLANE, D_MODEL), k, v

        xs, st_s = _layer(xs, mod, None, SUBLANE, wts, conv0_s, state_delta[l], dn_s, swa_s)
        outs_p.append(st_p)
        outs_s.append(st_s)

    def stack(outs, i):
        return jnp.stack([o[i] for o in outs])

    def kv5(a, batch):
        return a.reshape(depth, batch, WINDOW, SW_KV_HEADS, SW_HEAD_DIM)

    tail = slice(SUBLANE - CONV_W + 1, SUBLANE)
    y_p = xp.reshape(bp, tp, D_MODEL)
    y_s = xs.reshape(bs, SUBLANE, D_MODEL)[:, :ts]
    return (y_p, y_s,
            kv5(stack(outs_p, 0), bp), kv5(stack(outs_p, 1), bp),
            stack(outs_p, 2)[:, :, tail], stack(outs_p, 3),
            kv5(stack(outs_s, 0), bs), kv5(stack(outs_s, 1), bs),
            stack(outs_s, 2)[:, :, tail], stack(outs_s, 3))
```

```python
import functools

import jax
import jax.numpy as jnp
from jax import lax
from jax.experimental import pallas as pl
from jax.experimental.pallas import tpu as pltpu

F32, BF16 = jnp.float32, jnp.bfloat16

D_MODEL = 1024
DN_HEADS, DN_DK, DN_DV = 8, 128, 128
CONV_W = 4
CONV_CH = 3 * D_MODEL
DN_CHUNK = 64
SW_HEAD_DIM, SW_HEADS, SW_KV_HEADS, SW_GROUP = 64, 16, 4, 4
SW_KV = SW_KV_HEADS * SW_HEAD_DIM
WINDOW = 128
ROT_DIM = 16
ROPE_THETA = 500000.0
PAST_LEN = 8192
EPS = 1e-6
NEG_INF = -1e30

LANE = 128
SUBLANE = 8
VMEM_LIMIT = 56 * 1024 * 1024

P_U, P_Z, P_QKV, P_G, P_BA = 0, 3072, 4096, 5632, 7680
P_TOTAL = 7808
HI = lax.Precision.HIGHEST

DN_TILE = 128
DN_GROUP = 4
SAMPLE_SEQS = 4
FFN_SUBTILES = 2


def _dot(a, b):
    return jnp.dot(a, b, preferred_element_type=F32)


def _dot_nt(a, b):
    return lax.dot_general(a, b, (((1,), (1,)), ((), ())), preferred_element_type=F32)


def _dot_tn(a, b):
    return lax.dot_general(a, b, (((0,), (0,)), ((), ())), preferred_element_type=F32)


def _dot_hi(a, b):
    return jnp.dot(a, b, preferred_element_type=F32, precision=HI)


def _mm_bf(a, b):
    return _dot(a.astype(BF16), b.astype(BF16))


def _silu(x):
    return x * jax.nn.sigmoid(x)


def _rms(x, gain):
    ms = jnp.mean(x * x, axis=-1, keepdims=True)
    return x * lax.rsqrt(ms + EPS) * gain


def _mod_rows(ref, per_token, tiles_per_seq):
    if per_token:
        return ref[...]
    b = pl.program_id(0) // tiles_per_seq
    return ref[pl.ds(b, 1), :]


def _const_spec(shape):
    return pl.BlockSpec(shape, lambda *_: (0,) * len(shape), pipeline_mode=pl.Buffered(1))


def _params(sem):
    return pltpu.CompilerParams(dimension_semantics=sem, vmem_limit_bytes=VMEM_LIMIT)


def _ada_kernel(c_ref, w_ref, b_ref, o_ref):
    h = _silu(c_ref[...]).astype(BF16)
    o_ref[...] = _dot(h, w_ref[...].astype(BF16)) + b_ref[...]


def _ada(c, w, b):
    rows, n = c.shape[0], w.shape[1]
    return pl.pallas_call(
        _ada_kernel,
        out_shape=jax.ShapeDtypeStruct((rows, n), F32),
        grid=(n // D_MODEL,),
        in_specs=[
            pl.BlockSpec((rows, D_MODEL), lambda j: (0, 0)),
            pl.BlockSpec((D_MODEL, D_MODEL), lambda j: (0, j)),
            pl.BlockSpec((1, D_MODEL), lambda j: (0, j)),
        ],
        out_specs=pl.BlockSpec((rows, D_MODEL), lambda j: (0, j)),
        compiler_params=_params(("arbitrary",)),
        name="ada",
    )(c, w, b)


def _mod_specs(mod_block, tm, cols):
    if mod_block is None:
        return [pl.BlockSpec((tm, D_MODEL), functools.partial(lambda i, c: (i, c), c=c)) for c in cols]
    return [pl.BlockSpec((SUBLANE, D_MODEL), functools.partial(lambda i, c: (mod_block, c), c=c)) for c in cols]


def _ffn_stages(xs, rows, pick, sh, sc, g, pre, post, wg_ref, wu_ref, wd_ref):
    hs = [(_rms(x, pre) * (1.0 + pick(sc, r)) + pick(sh, r)).astype(BF16) for x, r in zip(xs, rows)]
    gates = [_dot(h, wg_ref[...]) for h in hs]
    ups = [_dot(h, wu_ref[...]) for h in hs]
    acts = [(_silu(a) * b).astype(BF16) for a, b in zip(gates, ups)]
    ys = [_dot(a, wd_ref[...]) for a in acts]
    return [x + 0.5 * pick(g, r) * _rms(y, post) for x, y, r in zip(xs, ys, rows)]


def _row_groups(tm):
    return [slice(i * tm // FFN_SUBTILES, (i + 1) * tm // FFN_SUBTILES) for i in range(FFN_SUBTILES)]


def _ffn_kernel(x_ref, sh_ref, sc_ref, g_ref, pre_ref, post_ref, wg_ref, wu_ref, wd_ref, o_ref,
                *, per_token, tiles_per_seq):
    rows = _row_groups(x_ref.shape[0])
    sh = _mod_rows(sh_ref, per_token, tiles_per_seq)
    sc = _mod_rows(sc_ref, per_token, tiles_per_seq)
    g = _mod_rows(g_ref, per_token, tiles_per_seq)
    pick = (lambda m, r: m[r]) if per_token else (lambda m, r: m)
    xs = [x_ref[r, :] for r in rows]
    outs = _ffn_stages(xs, rows, pick, sh, sc, g, pre_ref[...], post_ref[...], wg_ref, wu_ref, wd_ref)
    for o, r in zip(outs, rows):
        o_ref[r, :] = o


def _ffn(x, mod, col0, mod_block, seq_len, pre, post, wg, wu, wd):
    n = x.shape[0]
    tm = min(512, n)
    dff = wg.shape[1]
    kern = functools.partial(_ffn_kernel, per_token=mod_block is None, tiles_per_seq=max(seq_len // tm, 1))
    row = lambda i: (i, 0)
    return pl.pallas_call(
        kern,
        out_shape=jax.ShapeDtypeStruct((n, D_MODEL), F32),
        grid=(n // tm,),
        in_specs=[pl.BlockSpec((tm, D_MODEL), row)]
        + _mod_specs(mod_block, tm, (col0, col0 + 1, col0 + 2))
        + [_const_spec((1, D_MODEL)), _const_spec((1, D_MODEL)),
           _const_spec((D_MODEL, dff)), _const_spec((D_MODEL, dff)), _const_spec((dff, D_MODEL))],
        out_specs=pl.BlockSpec((tm, D_MODEL), row),
        compiler_params=_params(("arbitrary",)),
        name="ffn",
    )(x, mod, mod, mod, pre, post, wg, wu, wd)


def _proj_kernel(x_ref, sh_ref, sc_ref, pre_ref, wa_ref, wb_ref, wc_ref, u_ref, z_ref, qkv_ref, g_ref, ba_ref,
                 *, per_token, tiles_per_seq):
    sh = _mod_rows(sh_ref, per_token, tiles_per_seq)
    sc = _mod_rows(sc_ref, per_token, tiles_per_seq)
    h = (_rms(x_ref[...], pre_ref[...]) * (1.0 + sc) + sh).astype(BF16)
    u_ref[...] = _dot(h, wa_ref[:, 0:P_Z])
    z_ref[...] = _dot(h, wa_ref[:, P_Z:P_QKV])
    qkv_ref[...] = _dot(h, wb_ref[:, 0:P_G - P_QKV])
    g_ref[...] = _dot(h, wb_ref[:, P_G - P_QKV:P_BA - P_QKV]).astype(g_ref.dtype)
    ba_ref[...] = _dot(h, wc_ref[...])


def _proj(x, mod, mod_block, seq_len, pre, wa, wb, wc):
    n = x.shape[0]
    tm = min(256, n)
    kern = functools.partial(_proj_kernel, per_token=mod_block is None, tiles_per_seq=max(seq_len // tm, 1))
    row = lambda i: (i, 0)
    widths = (P_Z - P_U, P_QKV - P_Z, P_G - P_QKV, P_BA - P_G, P_TOTAL - P_BA)
    return pl.pallas_call(
        kern,
        out_shape=[jax.ShapeDtypeStruct((n, wd), BF16 if wd == P_BA - P_G else F32) for wd in widths],
        grid=(n // tm,),
        in_specs=[pl.BlockSpec((tm, D_MODEL), row)]
        + _mod_specs(mod_block, tm, (3, 4))
        + [_const_spec((1, D_MODEL)), _const_spec(wa.shape), _const_spec(wb.shape), _const_spec(wc.shape)],
        out_specs=[pl.BlockSpec((tm, wd), row) for wd in widths],
        compiler_params=_params(("arbitrary",)),
        name="proj",
    )(x, mod, mod, pre, wa, wb, wc)


def _tri_inv_all(lows, c):
    ii = lax.broadcasted_iota(jnp.int32, (c, c), 0)
    jj = lax.broadcasted_iota(jnp.int32, (c, c), 1)
    eye = jnp.where(ii == jj, 1.0, 0.0).astype(F32)
    xs = [eye - jnp.where(ii // 2 == jj // 2, low, 0.0) for low in lows]
    b = 2
    while b < c:
        join = (ii // (2 * b) == jj // (2 * b)) & (ii // b != jj // b)
        ys = [_mm_bf(x, jnp.where(join, low, 0.0)) for x, low in zip(xs, lows)]
        xs = [x - _mm_bf(y, x) for x, y in zip(xs, ys)]
        b *= 2
    return xs


def _dn_kernel(u_ref, z_ref, ba_ref, cw_ref, gp_ref, dnn_ref, conv0_ref, s0_ref,
               o_ref, convo_ref, so_ref, ubuf, q_s, k_s, v_s, bg_s, u_s, w_s, qe_s, kd_s, a_s, eg_s,
               *, nseq, tm, chunk, valid):
    t = pl.program_id(1)
    rows_all = nseq * tm

    n_cb = CONV_CH // LANE

    @pl.when(t == 0)
    def _():
        for g in range(nseq):
            for cb in range(n_cb):
                ubuf[g, cb, 0:SUBLANE, :] = conv0_ref[g, :, cb * LANE:(cb + 1) * LANE]
        so_ref[...] = s0_ref[...]

    for g in range(nseq):
        rws = slice(g * tm, (g + 1) * tm)
        for cb in range(n_cb):
            cols = slice(cb * LANE, (cb + 1) * LANE)
            ubuf[g, cb, SUBLANE:SUBLANE + tm, :] = u_ref[g, :, cols]
            acc = ubuf[g, cb, pl.ds(SUBLANE - CONV_W + 1, tm), :] * cw_ref[0:1, cols]
            for i in range(1, CONV_W):
                acc = acc + ubuf[g, cb, pl.ds(SUBLANE - CONV_W + 1 + i, tm), :] * cw_ref[i:i + 1, cols]
            a = _silu(acc)
            sec, off = divmod(cb * LANE, D_MODEL)
            dst = slice(off, off + LANE)
            if sec == 2:
                v_s[rws, dst] = a
            else:
                an = a * lax.rsqrt(jnp.sum(a * a, axis=-1, keepdims=True) + EPS)
                if sec == 0:
                    q_s[rws, dst] = an * (DN_DK ** -0.5)
                else:
                    k_s[rws, dst] = an
            tail = ubuf[g, cb, pl.ds(valid, SUBLANE), :]
            ubuf[g, cb, 0:SUBLANE, :] = tail
            convo_ref[g, :, cols] = tail

    lane = lax.broadcasted_iota(jnp.int32, (tm, LANE), 1)
    rowi = lax.broadcasted_iota(jnp.int32, (tm, LANE), 0)
    for g in range(nseq):
        ba = ba_ref[g]
        xg = ba + gp_ref[0:1, :]
        softplus = jnp.maximum(xg, 0.0) + jnp.log1p(jnp.exp(-jnp.abs(xg)))
        gdec = -jnp.exp(gp_ref[1:2, :]) * softplus
        bg = jnp.where(lane < DN_HEADS, jax.nn.sigmoid(ba), gdec)
        bg_s[g * tm:(g + 1) * tm, :] = jnp.where(rowi < valid, bg, 0.0)

    c = chunk
    ii = lax.broadcasted_iota(jnp.int32, (c, c), 0)
    jj = lax.broadcasted_iota(jnp.int32, (c, c), 1)
    incl = ii >= jj
    strict = ii > jj
    tril = jnp.where(incl, 1.0, 0.0).astype(F32)
    sel = jnp.where(lax.broadcasted_iota(jnp.int32, (SUBLANE, LANE), 1)
                    == lax.broadcasted_iota(jnp.int32, (SUBLANE, LANE), 0) + DN_HEADS, 1.0, 0.0).astype(F32)
    dnn = dnn_ref[...]

    heads = range(DN_HEADS)
    hcols = [slice(h * DN_DK, (h + 1) * DN_DK) for h in heads]
    acols = [slice(h * c, (h + 1) * c) for h in heads]
    n_chunks = rows_all // c
    chunks_per_seq = tm // c
    gpc = DN_GROUP if n_chunks % DN_GROUP == 0 else 1

    def chunk_rows(ci):
        return pl.ds(ci * c, c) if isinstance(ci, int) else pl.ds(pl.multiple_of(ci * c, c), c)

    def prep(cis):
        n_c = len(cis)
        rows = [chunk_rows(ci) for ci in cis]
        pairs = [(j, h) for j in range(n_c) for h in heads]
        bgc = [bg_s[r, :] for r in rows]
        gcum = [_dot_hi(tril, b) for b in bgc]
        gcum_t = [lax.dot_general(sel, g, (((1,), (1,)), ((), ())), preferred_element_type=F32, precision=HI)
                  for g in gcum]
        for j in range(n_c):
            eg_s[cis[j]] = jnp.broadcast_to(jnp.exp(gcum_t[j][:, c - 1:c]), (DN_HEADS, DN_DV))
        beta = [bgc[j][:, h:h + 1] for j, h in pairs]
        gc = [gcum[j][:, DN_HEADS + h:DN_HEADS + h + 1] for j, h in pairs]
        decay = [jnp.where(incl, jnp.exp(jnp.where(incl, gc[p] - gcum_t[j][h:h + 1, :], 0.0)), 0.0)
                 for p, (j, h) in enumerate(pairs)]
        q = [q_s[rows[j], hcols[h]] for j, h in pairs]
        k = [k_s[rows[j], hcols[h]] for j, h in pairs]
        npair = range(len(pairs))
        kb = [k[p] * beta[p] for p in npair]
        k_bf = [k[p].astype(BF16) for p in npair]
        akk = [_dot_nt(kb[p].astype(BF16), k_bf[p]) for p in npair]
        aqk = [_dot_nt(q[p].astype(BF16), k_bf[p]) for p in npair]
        tinv = _tri_inv_all([jnp.where(strict, akk[p] * decay[p], 0.0) for p in npair], c)
        egc = [jnp.exp(gc[p]) for p in npair]
        rhs = [jnp.concatenate([v_s[rows[j], hcols[h]] * beta[p], kb[p] * egc[p]], axis=1)
               for p, (j, h) in enumerate(pairs)]
        uw = [_mm_bf(tinv[p], rhs[p]) for p in npair]
        sdt = w_s.dtype
        for p, (j, h) in enumerate(pairs):
            u_s[rows[j], hcols[h]] = uw[p][:, :DN_DV]
            w_s[rows[j], hcols[h]] = uw[p][:, DN_DV:].astype(sdt)
            qe_s[rows[j], hcols[h]] = (q[p] * egc[p]).astype(sdt)
            kd_s[rows[j], hcols[h]] = (k[p] * jnp.exp(gc[p][c - 1:c, :] - gc[p])).astype(sdt)
            a_s[rows[j], acols[h]] = jnp.where(incl, aqk[p] * decay[p], 0.0).astype(sdt)

    def state(cpos):
        trip = [(g, h) for g in range(nseq) for h in heads]
        rows = [chunk_rows(g * chunks_per_seq + cpos) for g in range(nseq)]
        local = chunk_rows(cpos)
        egb = [eg_s[g * chunks_per_seq + cpos] for g in range(nseq)]
        s_old = [so_ref[g, h] for g, h in trip]
        s_bf = [s.astype(BF16) for s in s_old]
        ws = [_dot(w_s[rows[g], hcols[h]].astype(BF16), s_bf[n]) for n, (g, h) in enumerate(trip)]
        qs = [_dot(qe_s[rows[g], hcols[h]].astype(BF16), s_bf[n]) for n, (g, h) in enumerate(trip)]
        vn_bf = [(u_s[rows[g], hcols[h]] - ws[n]).astype(BF16) for n, (g, h) in enumerate(trip)]
        av = [_dot(a_s[rows[g], acols[h]].astype(BF16), vn_bf[n]) for n, (g, h) in enumerate(trip)]
        kv = [_dot_tn(kd_s[rows[g], hcols[h]].astype(BF16), vn_bf[n]) for n, (g, h) in enumerate(trip)]
        for n, (g, h) in enumerate(trip):
            so_ref[g, h] = s_old[n] * egb[g][h:h + 1, :] + kv[n]
            o = _rms(qs[n] + av[n], dnn) * _silu(z_ref[g, local, hcols[h]])
            o_ref[g, local, hcols[h]] = o.astype(o_ref.dtype)

    for gi in range(n_chunks // gpc):
        prep([gi * gpc + j for j in range(gpc)])
    if chunks_per_seq == 1:
        state(0)
    else:
        def state_body(cpos, carry):
            state(cpos)
            return carry
        lax.fori_loop(0, chunks_per_seq, state_body, 0)


def _deltanet(u, z, ba, cw, gp, dnn, conv0, s0, *, nseq, tm, chunk, valid, o_dtype):
    batch, seq, _ = u.shape
    nt = seq // tm
    rows = nseq * tm
    row = lambda b, t: (b, t, 0)
    per_b3 = lambda b, t: (b, 0, 0)
    per_b4 = lambda b, t: (b, 0, 0, 0)
    sdt = BF16 if chunk % 16 == 0 else F32
    kern = functools.partial(_dn_kernel, nseq=nseq, tm=tm, chunk=chunk, valid=valid)
    return pl.pallas_call(
        kern,
        out_shape=[jax.ShapeDtypeStruct((batch, seq, D_MODEL), o_dtype),
                   jax.ShapeDtypeStruct((batch, SUBLANE, CONV_CH), F32),
                   jax.ShapeDtypeStruct((batch, DN_HEADS, DN_DK, DN_DV), F32)],
        grid=(batch // nseq, nt),
        in_specs=[pl.BlockSpec((nseq, tm, CONV_CH), row),
                  pl.BlockSpec((nseq, tm, D_MODEL), row),
                  pl.BlockSpec((nseq, tm, LANE), row),
                  pl.BlockSpec((SUBLANE, CONV_CH), lambda b, t: (0, 0)),
                  pl.BlockSpec((SUBLANE, LANE), lambda b, t: (0, 0)),
                  pl.BlockSpec((1, DN_DV), lambda b, t: (0, 0)),
                  pl.BlockSpec((nseq, SUBLANE, CONV_CH), per_b3),
                  pl.BlockSpec((nseq, DN_HEADS, DN_DK, DN_DV), per_b4)],
        out_specs=[pl.BlockSpec((nseq, tm, D_MODEL), row),
                   pl.BlockSpec((nseq, SUBLANE, CONV_CH), per_b3),
                   pl.BlockSpec((nseq, DN_HEADS, DN_DK, DN_DV), per_b4)],
        scratch_shapes=[pltpu.VMEM((nseq, CONV_CH // LANE, SUBLANE + tm, LANE), F32),
                        pltpu.VMEM((rows, D_MODEL), F32),
                        pltpu.VMEM((rows, D_MODEL), F32),
                        pltpu.VMEM((rows, D_MODEL), F32),
                        pltpu.VMEM((rows, LANE), F32),
                        pltpu.VMEM((rows, D_MODEL), F32),
                        pltpu.VMEM((rows, D_MODEL), sdt),
                        pltpu.VMEM((rows, D_MODEL), sdt),
                        pltpu.VMEM((rows, D_MODEL), sdt),
                        pltpu.VMEM((rows, DN_HEADS * chunk), sdt),
                        pltpu.VMEM((rows // chunk, DN_HEADS, DN_DV), F32)],
        compiler_params=_params(("arbitrary", "arbitrary")),
        name="deltanet",
    )(u, z, ba, cw, gp, dnn, conv0, s0)


def _rope(x, cos, sa, sb):
    cols = []
    for cb in range(x.shape[1] // LANE):
        xc = x[:, cb * LANE:(cb + 1) * LANE]
        cols.append(xc * cos + pltpu.roll(xc, LANE - ROT_DIM // 2, 1) * sa + pltpu.roll(xc, ROT_DIM // 2, 1) * sb)
    return cols[0] if len(cols) == 1 else jnp.concatenate(cols, axis=1)


def _head_pair_operands(x):
    lane = lax.broadcasted_iota(jnp.int32, (x.shape[0], LANE), 1)
    out = []
    for cb in range(x.shape[1] // LANE):
        c = x[:, cb * LANE:(cb + 1) * LANE]
        lo = jnp.where(lane < SW_HEAD_DIM, c, 0.0)
        hi = jnp.where(lane >= SW_HEAD_DIM, c, 0.0)
        out.append((lo, pltpu.roll(lo, SW_HEAD_DIM, 1)))
        out.append((pltpu.roll(hi, SW_HEAD_DIM, 1), hi))
    return out


def _ones_columns(w):
    rowi = lax.broadcasted_iota(jnp.int32, (4 * w, LANE), 0)
    lanei = lax.broadcasted_iota(jnp.int32, (4 * w, LANE), 1)
    return jnp.where((rowi < 2 * w) == (lanei < SW_HEAD_DIM), 1.0, 0.0).astype(BF16)


def _pair_softmax(scores, mask, sinks, lane, w):
    parts, es = [], []
    for half in range(2):
        s = jnp.where(mask, scores[:, half * 2 * w:(half + 1) * 2 * w], NEG_INF)
        m = jnp.maximum(jnp.max(s, axis=-1, keepdims=True), sinks[half])
        parts.append(jnp.exp(s - m).astype(BF16))
        es.append(jnp.exp(sinks[half] - m))
    return jnp.concatenate(parts, axis=1), jnp.where(lane < SW_HEAD_DIM, es[0], es[1])


def _swa_prompt_kernel(sinks_ref, qkv_ref, cos_ref, sa_ref, sb_ref, o_ref, kc_ref, vc_ref, kk2, vv2):
    n = pl.program_id(1)
    w = WINDOW

    @pl.when(n == 0)
    def _():
        kk2[...] = jnp.zeros_like(kk2)
        for j in range(SW_KV_HEADS):
            vv2[j, :, 0:LANE] = jnp.zeros((4 * w, LANE), BF16)
            vv2[j, :, LANE:2 * LANE] = _ones_columns(w)

    for j in range(SW_KV_HEADS):
        kk2[j, 0:w, :] = kk2[j, w:2 * w, :]
        kk2[j, 2 * w:3 * w, :] = kk2[j, 3 * w:4 * w, :]
        vv2[j, 0:w, 0:LANE] = vv2[j, w:2 * w, 0:LANE]
        vv2[j, 2 * w:3 * w, 0:LANE] = vv2[j, 3 * w:4 * w, 0:LANE]
    cos, sa, sb = cos_ref[...], sa_ref[...], sb_ref[...]
    q = (_rope(qkv_ref[:, 0:D_MODEL], cos, sa, sb) * (SW_HEAD_DIM ** -0.5)).astype(BF16)
    k = _rope(qkv_ref[:, D_MODEL:D_MODEL + SW_KV], cos, sa, sb)
    v = qkv_ref[:, D_MODEL + SW_KV:D_MODEL + 2 * SW_KV]
    kc_ref[...] = k
    vc_ref[...] = v
    for j, (left, right) in enumerate(_head_pair_operands(k)):
        kk2[j, w:2 * w, :] = left.astype(BF16)
        kk2[j, 3 * w:4 * w, :] = right.astype(BF16)
    for j, (left, right) in enumerate(_head_pair_operands(v)):
        vv2[j, w:2 * w, 0:LANE] = left.astype(BF16)
        vv2[j, 3 * w:4 * w, 0:LANE] = right.astype(BF16)
    qi = lax.broadcasted_iota(jnp.int32, (w, 2 * w), 0) + w
    sj = lax.broadcasted_iota(jnp.int32, (w, 2 * w), 1)
    d = qi - sj
    mask = (d >= 0) & (d <= WINDOW) & ((sj >= w) | (n > 0))
    lane = lax.broadcasted_iota(jnp.int32, (w, LANE), 1)
    n_pb = D_MODEL // LANE
    scores = [_dot_nt(q[:, pb * LANE:(pb + 1) * LANE], kk2[pb // 2]) for pb in range(n_pb)]
    soft = [_pair_softmax(scores[pb], mask, (sinks_ref[2 * pb], sinks_ref[2 * pb + 1]), lane, w)
            for pb in range(n_pb)]
    outs = [_dot(soft[pb][0], vv2[pb // 2]) for pb in range(n_pb)]
    for pb in range(n_pb):
        o = outs[pb][:, 0:LANE] / (outs[pb][:, LANE:2 * LANE] + soft[pb][1])
        o_ref[:, pb * LANE:(pb + 1) * LANE] = o.astype(o_ref.dtype)


def _swa_prompt(sinks, qkv, cos, sa, sb, *, batch, seq):
    nb = seq // WINDOW
    row = lambda b, n: (b * nb + n, 0)
    tab = lambda b, n: (n, 0)
    per_b = lambda b, n: (b, 0, 0)
    return pl.pallas_call(
        _swa_prompt_kernel,
        out_shape=[jax.ShapeDtypeStruct((batch * seq, D_MODEL), BF16),
                   jax.ShapeDtypeStruct((batch, WINDOW, SW_KV), F32),
                   jax.ShapeDtypeStruct((batch, WINDOW, SW_KV), F32)],
        grid=(batch, nb),
        in_specs=[pl.BlockSpec(memory_space=pltpu.SMEM),
                  pl.BlockSpec((WINDOW, D_MODEL + 2 * SW_KV), row),
                  pl.BlockSpec((WINDOW, LANE), tab),
                  pl.BlockSpec((WINDOW, LANE), tab),
                  pl.BlockSpec((WINDOW, LANE), tab)],
        out_specs=[pl.BlockSpec((WINDOW, D_MODEL), row),
                   pl.BlockSpec((None, WINDOW, SW_KV), per_b),
                   pl.BlockSpec((None, WINDOW, SW_KV), per_b)],
        scratch_shapes=[pltpu.VMEM((SW_KV_HEADS, 4 * WINDOW, LANE), BF16),
                        pltpu.VMEM((SW_KV_HEADS, 4 * WINDOW, 2 * LANE), BF16)],
        compiler_params=_params(("arbitrary", "arbitrary")),
        name="swa_prompt",
    )(sinks, qkv, cos, sa, sb)


def _swa_sample_kernel(sinks_ref, qkv_ref, kbuf_ref, vbuf_ref, cos_ref, sa_ref, sb_ref,
                       o_ref, kc_ref, vc_ref, kk_s, vv_s, kk2, vv2, *, nseq, t_new):
    wb = WINDOW
    tp = SUBLANE

    @pl.when(pl.program_id(0) == 0)
    def _():
        kk2[...] = jnp.zeros_like(kk2)
        for g in range(nseq):
            for j in range(SW_KV_HEADS):
                vv2[g, j, :, 0:LANE] = jnp.zeros((4 * wb, LANE), BF16)
                vv2[g, j, :, LANE:2 * LANE] = _ones_columns(wb)

    cos, sa, sb = cos_ref[...], sa_ref[...], sb_ref[...]
    qs = []
    for g in range(nseq):
        q = (_rope(qkv_ref[g, :, 0:D_MODEL], cos, sa, sb) * (SW_HEAD_DIM ** -0.5)).astype(BF16)
        k_new = _rope(qkv_ref[g, :, D_MODEL:D_MODEL + SW_KV], cos, sa, sb)
        v_new = qkv_ref[g, :, D_MODEL + SW_KV:D_MODEL + 2 * SW_KV]
        k_old, v_old = kbuf_ref[g], vbuf_ref[g]
        kk_s[g, 0:wb, :] = k_old
        vv_s[g, 0:wb, :] = v_old
        kk_s[g, wb:wb + tp, :] = k_new
        vv_s[g, wb:wb + tp, :] = v_new
        kc_ref[g] = kk_s[g, pl.ds(t_new, wb), :]
        vc_ref[g] = vv_s[g, pl.ds(t_new, wb), :]
        for r0, n_r, kx, vx in ((0, wb, k_old, v_old), (wb, tp, k_new, v_new)):
            for j, (left, right) in enumerate(_head_pair_operands(kx)):
                kk2[g, j, r0:r0 + n_r, :] = left.astype(BF16)
                kk2[g, j, 2 * wb + r0:2 * wb + r0 + n_r, :] = right.astype(BF16)
            for j, (left, right) in enumerate(_head_pair_operands(vx)):
                vv2[g, j, r0:r0 + n_r, 0:LANE] = left.astype(BF16)
                vv2[g, j, 2 * wb + r0:2 * wb + r0 + n_r, 0:LANE] = right.astype(BF16)
        qs.append([jnp.concatenate([q[:, (2 * j) * LANE:(2 * j + 1) * LANE],
                                    q[:, (2 * j + 1) * LANE:(2 * j + 2) * LANE]], axis=0)
                   for j in range(SW_KV_HEADS)])
    rows = 2 * tp
    qt = lax.broadcasted_iota(jnp.int32, (rows, 2 * wb), 0) % tp
    sj = lax.broadcasted_iota(jnp.int32, (rows, 2 * wb), 1)
    d = qt + wb - sj
    mask = (d >= 0) & (d <= WINDOW)
    top = lax.broadcasted_iota(jnp.int32, (rows, 1), 0) < tp
    lane = lax.broadcasted_iota(jnp.int32, (rows, LANE), 1)
    items = [(g, j) for g in range(nseq) for j in range(SW_KV_HEADS)]
    scores = [_dot_nt(qs[g][j], kk2[g, j]) for g, j in items]
    soft = []
    for n, (g, j) in enumerate(items):
        sinks = [jnp.where(top, sinks_ref[4 * j + half], sinks_ref[4 * j + 2 + half]) for half in range(2)]
        soft.append(_pair_softmax(scores[n], mask, sinks, lane, wb))
    outs = [_dot(soft[n][0], vv2[g, j]) for n, (g, j) in enumerate(items)]
    for n, (g, j) in enumerate(items):
        o = outs[n][:, 0:LANE] / (outs[n][:, LANE:2 * LANE] + soft[n][1])
        o_ref[g, :, (2 * j) * LANE:(2 * j + 1) * LANE] = o[0:tp, :]
        o_ref[g, :, (2 * j + 1) * LANE:(2 * j + 2) * LANE] = o[tp:2 * tp, :]


def _swa_sample(sinks, qkv, kbuf, vbuf, cos, sa, sb, *, batch, nseq, t_new):
    per_b = lambda b: (b, 0, 0)
    tab = lambda b: (0, 0)
    kern = functools.partial(_swa_sample_kernel, nseq=nseq, t_new=t_new)
    return pl.pallas_call(
        kern,
        out_shape=[jax.ShapeDtypeStruct((batch, SUBLANE, D_MODEL), F32),
                   jax.ShapeDtypeStruct((batch, WINDOW, SW_KV), F32),
                   jax.ShapeDtypeStruct((batch, WINDOW, SW_KV), F32)],
        grid=(batch // nseq,),
        in_specs=[pl.BlockSpec(memory_space=pltpu.SMEM),
                  pl.BlockSpec((nseq, SUBLANE, D_MODEL + 2 * SW_KV), per_b),
                  pl.BlockSpec((nseq, WINDOW, SW_KV), per_b),
                  pl.BlockSpec((nseq, WINDOW, SW_KV), per_b),
                  pl.BlockSpec((SUBLANE, LANE), tab),
                  pl.BlockSpec((SUBLANE, LANE), tab),
                  pl.BlockSpec((SUBLANE, LANE), tab)],
        out_specs=[pl.BlockSpec((nseq, SUBLANE, D_MODEL), per_b),
                   pl.BlockSpec((nseq, WINDOW, SW_KV), per_b),
                   pl.BlockSpec((nseq, WINDOW, SW_KV), per_b)],
        scratch_shapes=[pltpu.VMEM((nseq, WINDOW + SUBLANE, SW_KV), F32),
                        pltpu.VMEM((nseq, WINDOW + SUBLANE, SW_KV), F32),
                        pltpu.VMEM((nseq, SW_KV_HEADS, 4 * WINDOW, LANE), BF16),
                        pltpu.VMEM((nseq, SW_KV_HEADS, 4 * WINDOW, 2 * LANE), BF16)],
        compiler_params=_params(("arbitrary",)),
        name="swa_sample",
    )(sinks, qkv, kbuf, vbuf, cos, sa, sb)


def _out_ffn_kernel(x_ref, g2_ref, post2_ref, ga_ref, gb_ref, odn_ref, osw_ref, wo_ref,
                    sh_ref, sc_ref, g_ref, pre_ref, post_ref, wg_ref, wu_ref, wd_ref, o_ref,
                    *, per_token, tiles_per_seq):
    rows = _row_groups(x_ref.shape[0])
    g2 = _mod_rows(g2_ref, per_token, tiles_per_seq)
    sh = _mod_rows(sh_ref, per_token, tiles_per_seq)
    sc = _mod_rows(sc_ref, per_token, tiles_per_seq)
    g = _mod_rows(g_ref, per_token, tiles_per_seq)
    pick = (lambda m, r: m[r]) if per_token else (lambda m, r: m)
    f32 = lambda ref, r: ref[r, :].astype(F32)
    ys = [(jax.nn.sigmoid(f32(ga_ref, r)) * f32(odn_ref, r)
           + jax.nn.sigmoid(f32(gb_ref, r)) * f32(osw_ref, r)).astype(BF16) for r in rows]
    ps = [_dot(y, wo_ref[...]) for y in ys]
    xs = [x_ref[r, :] + pick(g2, r) * _rms(p, post2_ref[...]) for p, r in zip(ps, rows)]
    outs = _ffn_stages(xs, rows, pick, sh, sc, g, pre_ref[...], post_ref[...], wg_ref, wu_ref, wd_ref)
    for o, r in zip(outs, rows):
        o_ref[r, :] = o


def _out_ffn(x, mod, mod_block, seq_len, post2, gates, o_dn, o_sw, w_out, pre, post, wg, wu, wd):
    n = x.shape[0]
    tm = min(512, n)
    dff = wg.shape[1]
    kern = functools.partial(_out_ffn_kernel, per_token=mod_block is None, tiles_per_seq=max(seq_len // tm, 1))
    row = lambda i: (i, 0)
    return pl.pallas_call(
        kern,
        out_shape=jax.ShapeDtypeStruct((n, D_MODEL), F32),
        grid=(n // tm,),
        in_specs=[pl.BlockSpec((tm, D_MODEL), row)]
        + _mod_specs(mod_block, tm, (5,))
        + [_const_spec((1, D_MODEL)),
           pl.BlockSpec((tm, D_MODEL), lambda i: (i, 0)),
           pl.BlockSpec((tm, D_MODEL), lambda i: (i, 1)),
           pl.BlockSpec((tm, D_MODEL), row),
           pl.BlockSpec((tm, D_MODEL), row),
           _const_spec((D_MODEL, D_MODEL))]
        + _mod_specs(mod_block, tm, (6, 7, 8))
        + [_const_spec((1, D_MODEL)), _const_spec((1, D_MODEL)),
           _const_spec((D_MODEL, dff)), _const_spec((D_MODEL, dff)), _const_spec((dff, D_MODEL))],
        out_specs=pl.BlockSpec((tm, D_MODEL), row),
        compiler_params=_params(("arbitrary",)),
        name="out_ffn",
    )(x, mod, post2, gates, gates, o_dn, o_sw, w_out, mod, mod, mod, pre, post, wg, wu, wd)


def _rope_tables(pos):
    half = ROT_DIM // 2
    inv_freq = ROPE_THETA ** (-jnp.arange(half, dtype=F32) * (2.0 / ROT_DIM))
    ang = pos.astype(F32)[:, None] * inv_freq[None, :]
    cos, sin = jnp.cos(ang), jnp.sin(ang)
    n = pos.shape[0]
    rest = SW_HEAD_DIM - ROT_DIM
    c64 = jnp.concatenate([cos, cos, jnp.ones((n, rest), F32)], axis=1)
    a64 = jnp.concatenate([-sin, jnp.zeros((n, half + rest), F32)], axis=1)
    b64 = jnp.concatenate([jnp.zeros((n, half), F32), sin, jnp.zeros((n, rest), F32)], axis=1)
    rep = LANE // SW_HEAD_DIM
    return tuple(jnp.tile(t, (1, rep)) for t in (c64, a64, b64))


def _pad_rows(a, rows):
    return jnp.pad(a, ((0, rows - a.shape[0]),) + ((0, 0),) * (a.ndim - 1))


def _layer(x, mod, mod_block, seq, rows, wts, conv0, s0, dn_fn, swa_fn):
    def per_seq(a):
        a = a.reshape(-1, seq, a.shape[-1])
        return a if rows == seq else jnp.pad(a, ((0, 0), (0, rows - seq), (0, 0)))

    def flat(a):
        return a[:, :seq].reshape(-1, a.shape[-1])

    x = _ffn(x, mod, 0, mod_block, seq, wts["pre1"], wts["post1"], wts["wg1"], wts["wu1"], wts["wd1"])
    u, z, qkv, gates, ba = _proj(x, mod, mod_block, seq, wts["pre2"], *wts["w_in"])
    o_dn, conv_new, s_new = dn_fn(per_seq(u), per_seq(z), per_seq(ba), wts["conv_w"], wts["gparam"],
                                  wts["dn_norm"], conv0, s0)
    o_sw, k_new, v_new = swa_fn(per_seq(qkv))
    x = _out_ffn(x, mod, mod_block, seq, wts["post2"], gates, flat(o_dn), flat(o_sw), wts["w_out"],
                 wts["pre3"], wts["post3"], wts["wg2"], wts["wu2"], wts["wd2"])
    return x, (k_new, v_new, conv_new, s_new)


def kernel(x_prompt, x_sample, cache_swa_k, cache_swa_v, state_conv, state_delta, c_prompt, c_sample,
           w_ada, b_ada, ffn1_norm_pre, ffn1_norm_post, ffn1_w_gate, ffn1_w_up, ffn1_w_down,
           mix_norm_pre, mix_norm_post, w_in, conv_w, a_log, dt_bias, dn_norm, sinks, w_out,
           ffn2_norm_pre, ffn2_norm_post, ffn2_w_gate, ffn2_w_up, ffn2_w_down):
    depth = w_ada.shape[0]
    bp, tp, _ = x_prompt.shape
    bs, ts, _ = x_sample.shape
    assert tp % 512 == 0 and 1 <= ts <= SUBLANE and bs % SAMPLE_SEQS == 0 and (bs * ts) % SUBLANE == 0

    cos_p, sa_p, sb_p = _rope_tables(jnp.arange(tp))
    cos_s, sa_s, sb_s = (_pad_rows(t, SUBLANE) for t in _rope_tables(PAST_LEN + jnp.arange(ts)))

    xp = x_prompt.reshape(bp * tp, D_MODEL)
    xs = x_sample.reshape(bs * ts, D_MODEL)
    c_all = jnp.concatenate([jnp.repeat(c_sample, ts, axis=0), _pad_rows(c_prompt, SUBLANE)], axis=0)
    prompt_mod_block = bs * ts // SUBLANE

    outs_p, outs_s = [], []
    for l in range(depth):
        wi = w_in[l]
        n_ba = 2 * DN_HEADS
        w_in_a = wi[:, :P_QKV].astype(BF16)
        w_in_b = wi[:, P_QKV + n_ba:].astype(BF16)
        w_in_c = jnp.pad(wi[:, P_QKV:P_QKV + n_ba], ((0, 0), (0, LANE - n_ba))).astype(BF16)
        gparam = jnp.zeros((SUBLANE, LANE), F32)
        gparam = gparam.at[0, DN_HEADS:2 * DN_HEADS].set(dt_bias[l]).at[1, DN_HEADS:2 * DN_HEADS].set(a_log[l])
        wts = dict(
            pre1=ffn1_norm_pre[l][None], post1=ffn1_norm_post[l][None],
            wg1=ffn1_w_gate[l].astype(BF16), wu1=ffn1_w_up[l].astype(BF16), wd1=ffn1_w_down[l].astype(BF16),
            pre2=mix_norm_pre[l][None], post2=mix_norm_post[l][None], w_in=(w_in_a, w_in_b, w_in_c),
            conv_w=_pad_rows(conv_w[l], SUBLANE), gparam=gparam, dn_norm=dn_norm[l][None],
            w_out=w_out[l].astype(BF16),
            pre3=ffn2_norm_pre[l][None], post3=ffn2_norm_post[l][None],
            wg2=ffn2_w_gate[l].astype(BF16), wu2=ffn2_w_up[l].astype(BF16), wd2=ffn2_w_down[l].astype(BF16),
        )
        b_ada_l = b_ada[l][None]
        mod = _ada(c_all, w_ada[l], b_ada_l)
        sink_l = sinks[l]

        conv0_p = jnp.zeros((bp, SUBLANE, CONV_CH), F32)
        s0_p = jnp.zeros((bp, DN_HEADS, DN_DK, DN_DV), F32)
        dn_p = functools.partial(_deltanet, nseq=bp, tm=DN_TILE, chunk=DN_CHUNK, valid=DN_TILE, o_dtype=BF16)

        def swa_p(qkv):
            o, k, v = _swa_prompt(sink_l, qkv.reshape(bp * tp, -1), cos_p, sa_p, sb_p, batch=bp, seq=tp)
            return o.reshape(bp, tp, D_MODEL), k, v

        xp, st_p = _layer(xp, mod, prompt_mod_block, tp, tp, wts, conv0_p, s0_p, dn_p, swa_p)

        conv0_s = jnp.pad(state_conv[l], ((0, 0), (SUBLANE - CONV_W + 1, 0), (0, 0)))
        kbuf = cache_swa_k[l].reshape(bs, WINDOW, SW_KV)
        vbuf = cache_swa_v[l].reshape(bs, WINDOW, SW_KV)
        dn_s = functools.partial(_deltanet, nseq=SAMPLE_SEQS, tm=SUBLANE, chunk=SUBLANE, valid=ts, o_dtype=F32)

        swa_s = lambda qkv: _swa_sample(sink_l, qkv, kbuf, vbuf, cos_s, sa_s, sb_s,
                                        batch=bs, nseq=SAMPLE_SEQS, t_new=ts)
        xs, st_s = _layer(xs, mod, None, ts, SUBLANE, wts, conv0_s, state_delta[l], dn_s, swa_s)
        outs_p.append(st_p)
        outs_s.append(st_s)

    def stack(outs, i):
        return jnp.stack([o[i] for o in outs])

    def kv5(a, batch):
        return a.reshape(depth, batch, WINDOW, SW_KV_HEADS, SW_HEAD_DIM)

    tail = slice(SUBLANE - CONV_W + 1, SUBLANE)
    y_p = xp.reshape(bp, tp, D_MODEL)
    y_s = xs.reshape(bs, ts, D_MODEL)
    return (y_p, y_s,
            kv5(stack(outs_p, 0), bp), kv5(stack(outs_p, 1), bp),
            stack(outs_p, 2)[:, :, tail], stack(outs_p, 3),
            kv5(stack(outs_s, 0), bs), kv5(stack(outs_s, 1), bs),
            stack(outs_s, 2)[:, :, tail], stack(outs_s, 3))
```

```python
import functools

import jax
import jax.numpy as jnp
from jax import lax
from jax.experimental import pallas as pl
from jax.experimental.pallas import tpu as pltpu

F32, BF16 = jnp.float32, jnp.bfloat16

D_MODEL = 1024
DN_HEADS, DN_DK, DN_DV = 8, 128, 128
CONV_W = 4
CONV_CH = 3 * D_MODEL
DN_CHUNK = 64
SW_HEAD_DIM, SW_HEADS, SW_KV_HEADS, SW_GROUP = 64, 16, 4, 4
SW_KV = SW_KV_HEADS * SW_HEAD_DIM
WINDOW = 128
ROT_DIM = 16
ROPE_THETA = 500000.0
PAST_LEN = 8192
EPS = 1e-6
NEG_INF = -1e30

LANE = 128
SUBLANE = 8
VMEM_LIMIT = 56 * 1024 * 1024

P_U, P_Z, P_QKV, P_G, P_BA = 0, 3072, 4096, 5632, 7680
P_TOTAL = 7808
HI = lax.Precision.HIGHEST

DN_TILE = 128
DN_GROUP = 4
SAMPLE_SEQS = 4
FFN_SUBTILES = 2


def _dot(a, b):
    return jnp.dot(a, b, preferred_element_type=F32)


def _dot_nt(a, b):
    return lax.dot_general(a, b, (((1,), (1,)), ((), ())), preferred_element_type=F32)


def _dot_tn(a, b):
    return lax.dot_general(a, b, (((0,), (0,)), ((), ())), preferred_element_type=F32)


def _dot_hi(a, b):
    return jnp.dot(a, b, preferred_element_type=F32, precision=HI)


def _mm_bf(a, b):
    return _dot(a.astype(BF16), b.astype(BF16))


def _silu(x):
    return x * jax.nn.sigmoid(x)


def _rms(x, gain):
    ms = jnp.mean(x * x, axis=-1, keepdims=True)
    return x * lax.rsqrt(ms + EPS) * gain


def _mod_rows(ref, per_token, tiles_per_seq):
    if per_token:
        return ref[...]
    b = pl.program_id(0) // tiles_per_seq
    return ref[pl.ds(b, 1), :]


def _const_spec(shape):
    return pl.BlockSpec(shape, lambda *_: (0,) * len(shape), pipeline_mode=pl.Buffered(1))


def _params(sem):
    return pltpu.CompilerParams(dimension_semantics=sem, vmem_limit_bytes=VMEM_LIMIT)


def _cast_kernel(x_ref, o_ref):
    o_ref[...] = x_ref[...].astype(o_ref.dtype)


def _row_tile(rows, cap=768):
    return max(d for d in range(16, cap + 1, 16) if rows % d == 0 and rows // d >= 4)


def _to_bf16(w):
    r, c = w.shape
    tr = _row_tile(r)
    spec = pl.BlockSpec((tr, c), lambda i: (i, 0))
    return pl.pallas_call(
        _cast_kernel, out_shape=jax.ShapeDtypeStruct((r, c), BF16), grid=(r // tr,),
        in_specs=[spec], out_specs=spec, compiler_params=_params(("arbitrary",)), name="cast",
    )(w)


def _split_w_in_kernel(w_ref, a_ref, b_ref, c_ref):
    w = w_ref[...]
    n_ba = 2 * DN_HEADS
    a_ref[...] = w[:, :P_QKV].astype(BF16)
    b_ref[...] = w[:, P_QKV + n_ba:].astype(BF16)
    ba = jnp.concatenate([w[:, P_QKV:P_QKV + n_ba], jnp.zeros((w.shape[0], LANE - n_ba), F32)], axis=1)
    c_ref[...] = ba.astype(BF16)


def _split_w_in(w):
    r, c = w.shape
    tr = _row_tile(r, cap=128)
    widths = (P_QKV, c - P_QKV - 2 * DN_HEADS, LANE)
    return pl.pallas_call(
        _split_w_in_kernel,
        out_shape=[jax.ShapeDtypeStruct((r, wd), BF16) for wd in widths], grid=(r // tr,),
        in_specs=[pl.BlockSpec((tr, c), lambda i: (i, 0))],
        out_specs=[pl.BlockSpec((tr, wd), lambda i: (i, 0)) for wd in widths],
        compiler_params=_params(("arbitrary",)), name="split_w_in",
    )(w)


def _ada_kernel(c_ref, w_ref, b_ref, o_ref):
    h = _silu(c_ref[...]).astype(BF16)
    o_ref[...] = _dot(h, w_ref[...].astype(BF16)) + b_ref[...]


def _ada(c, w, b):
    rows, n = c.shape[0], w.shape[1]
    return pl.pallas_call(
        _ada_kernel,
        out_shape=jax.ShapeDtypeStruct((rows, n), F32),
        grid=(n // D_MODEL,),
        in_specs=[
            pl.BlockSpec((rows, D_MODEL), lambda j: (0, 0)),
            pl.BlockSpec((D_MODEL, D_MODEL), lambda j: (0, j)),
            pl.BlockSpec((1, D_MODEL), lambda j: (0, j)),
        ],
        out_specs=pl.BlockSpec((rows, D_MODEL), lambda j: (0, j)),
        compiler_params=_params(("arbitrary",)),
        name="ada",
    )(c, w, b)


def _mod_specs(mod_block, tm, cols):
    if mod_block is None:
        return [pl.BlockSpec((tm, D_MODEL), functools.partial(lambda i, c: (i, c), c=c)) for c in cols]
    return [pl.BlockSpec((SUBLANE, D_MODEL), functools.partial(lambda i, c: (mod_block, c), c=c)) for c in cols]


def _ffn_stages(xs, rows, pick, sh, sc, g, pre, post, wg_ref, wu_ref, wd_ref):
    hs = [(_rms(x, pre) * (1.0 + pick(sc, r)) + pick(sh, r)).astype(BF16) for x, r in zip(xs, rows)]
    gates = [_dot(h, wg_ref[...]) for h in hs]
    ups = [_dot(h, wu_ref[...]) for h in hs]
    acts = [(_silu(a) * b).astype(BF16) for a, b in zip(gates, ups)]
    ys = [_dot(a, wd_ref[...]) for a in acts]
    return [x + 0.5 * pick(g, r) * _rms(y, post) for x, y, r in zip(xs, ys, rows)]


def _row_groups(tm):
    return [slice(i * tm // FFN_SUBTILES, (i + 1) * tm // FFN_SUBTILES) for i in range(FFN_SUBTILES)]


def _ffn_kernel(x_ref, sh_ref, sc_ref, g_ref, pre_ref, post_ref, wg_ref, wu_ref, wd_ref, o_ref,
                *, per_token, tiles_per_seq):
    rows = _row_groups(x_ref.shape[0])
    sh = _mod_rows(sh_ref, per_token, tiles_per_seq)
    sc = _mod_rows(sc_ref, per_token, tiles_per_seq)
    g = _mod_rows(g_ref, per_token, tiles_per_seq)
    pick = (lambda m, r: m[r]) if per_token else (lambda m, r: m)
    xs = [x_ref[r, :] for r in rows]
    outs = _ffn_stages(xs, rows, pick, sh, sc, g, pre_ref[...], post_ref[...], wg_ref, wu_ref, wd_ref)
    for o, r in zip(outs, rows):
        o_ref[r, :] = o


def _ffn(x, mod, col0, mod_block, seq_len, pre, post, wg, wu, wd):
    n = x.shape[0]
    tm = min(512, n)
    dff = wg.shape[1]
    kern = functools.partial(_ffn_kernel, per_token=mod_block is None, tiles_per_seq=max(seq_len // tm, 1))
    row = lambda i: (i, 0)
    return pl.pallas_call(
        kern,
        out_shape=jax.ShapeDtypeStruct((n, D_MODEL), F32),
        grid=(n // tm,),
        in_specs=[pl.BlockSpec((tm, D_MODEL), row)]
        + _mod_specs(mod_block, tm, (col0, col0 + 1, col0 + 2))
        + [_const_spec((1, D_MODEL)), _const_spec((1, D_MODEL)),
           _const_spec((D_MODEL, dff)), _const_spec((D_MODEL, dff)), _const_spec((dff, D_MODEL))],
        out_specs=pl.BlockSpec((tm, D_MODEL), row),
        compiler_params=_params(("arbitrary",)),
        name="ffn",
    )(x, mod, mod, mod, pre, post, wg, wu, wd)


def _proj_kernel(x_ref, sh_ref, sc_ref, pre_ref, wa_ref, wb_ref, wc_ref, u_ref, z_ref, qkv_ref, g_ref, ba_ref,
                 *, per_token, tiles_per_seq):
    sh = _mod_rows(sh_ref, per_token, tiles_per_seq)
    sc = _mod_rows(sc_ref, per_token, tiles_per_seq)
    h = (_rms(x_ref[...], pre_ref[...]) * (1.0 + sc) + sh).astype(BF16)
    u_ref[...] = _dot(h, wa_ref[:, 0:P_Z])
    z_ref[...] = _dot(h, wa_ref[:, P_Z:P_QKV])
    qkv_ref[...] = _dot(h, wb_ref[:, 0:P_G - P_QKV])
    g_ref[...] = _dot(h, wb_ref[:, P_G - P_QKV:P_BA - P_QKV]).astype(g_ref.dtype)
    ba_ref[...] = _dot(h, wc_ref[...])


def _proj(x, mod, mod_block, seq_len, pre, wa, wb, wc):
    n = x.shape[0]
    tm = min(256, n)
    kern = functools.partial(_proj_kernel, per_token=mod_block is None, tiles_per_seq=max(seq_len // tm, 1))
    row = lambda i: (i, 0)
    widths = (P_Z - P_U, P_QKV - P_Z, P_G - P_QKV, P_BA - P_G, P_TOTAL - P_BA)
    return pl.pallas_call(
        kern,
        out_shape=[jax.ShapeDtypeStruct((n, wd), BF16 if wd == P_BA - P_G else F32) for wd in widths],
        grid=(n // tm,),
        in_specs=[pl.BlockSpec((tm, D_MODEL), row)]
        + _mod_specs(mod_block, tm, (3, 4))
        + [_const_spec((1, D_MODEL)), _const_spec(wa.shape), _const_spec(wb.shape), _const_spec(wc.shape)],
        out_specs=[pl.BlockSpec((tm, wd), row) for wd in widths],
        compiler_params=_params(("arbitrary",)),
        name="proj",
    )(x, mod, mod, pre, wa, wb, wc)


def _tri_inv_all(lows, c):
    ii = lax.broadcasted_iota(jnp.int32, (c, c), 0)
    jj = lax.broadcasted_iota(jnp.int32, (c, c), 1)
    eye = jnp.where(ii == jj, 1.0, 0.0).astype(F32)
    xs = [eye - jnp.where(ii // 2 == jj // 2, low, 0.0) for low in lows]
    b = 2
    while b < c:
        join = (ii // (2 * b) == jj // (2 * b)) & (ii // b != jj // b)
        ys = [_mm_bf(x, jnp.where(join, low, 0.0)) for x, low in zip(xs, lows)]
        xs = [x - _mm_bf(y, x) for x, y in zip(xs, ys)]
        b *= 2
    return xs


def _dn_kernel(u_ref, z_ref, ba_ref, cw_ref, gp_ref, dnn_ref, conv0_ref, s0_ref,
               o_ref, convo_ref, so_ref, ubuf, q_s, k_s, v_s, bg_s, u_s, w_s, qe_s, kd_s, a_s, eg_s,
               *, nseq, tm, chunk, valid):
    t = pl.program_id(1)
    rows_all = nseq * tm

    n_cb = CONV_CH // LANE

    @pl.when(t == 0)
    def _():
        for g in range(nseq):
            for cb in range(n_cb):
                ubuf[g, cb, 0:SUBLANE, :] = conv0_ref[g, :, cb * LANE:(cb + 1) * LANE]
        so_ref[...] = s0_ref[...]

    for g in range(nseq):
        rws = slice(g * tm, (g + 1) * tm)
        for cb in range(n_cb):
            cols = slice(cb * LANE, (cb + 1) * LANE)
            ubuf[g, cb, SUBLANE:SUBLANE + tm, :] = u_ref[g, :, cols]
            acc = ubuf[g, cb, pl.ds(SUBLANE - CONV_W + 1, tm), :] * cw_ref[0:1, cols]
            for i in range(1, CONV_W):
                acc = acc + ubuf[g, cb, pl.ds(SUBLANE - CONV_W + 1 + i, tm), :] * cw_ref[i:i + 1, cols]
            a = _silu(acc)
            sec, off = divmod(cb * LANE, D_MODEL)
            dst = slice(off, off + LANE)
            if sec == 2:
                v_s[rws, dst] = a
            else:
                an = a * lax.rsqrt(jnp.sum(a * a, axis=-1, keepdims=True) + EPS)
                if sec == 0:
                    q_s[rws, dst] = an * (DN_DK ** -0.5)
                else:
                    k_s[rws, dst] = an
            tail = ubuf[g, cb, pl.ds(valid, SUBLANE), :]
            ubuf[g, cb, 0:SUBLANE, :] = tail
            convo_ref[g, :, cols] = tail

    lane = lax.broadcasted_iota(jnp.int32, (tm, LANE), 1)
    rowi = lax.broadcasted_iota(jnp.int32, (tm, LANE), 0)
    for g in range(nseq):
        ba = ba_ref[g]
        xg = ba + gp_ref[0:1, :]
        softplus = jnp.maximum(xg, 0.0) + jnp.log1p(jnp.exp(-jnp.abs(xg)))
        gdec = -jnp.exp(gp_ref[1:2, :]) * softplus
        bg = jnp.where(lane < DN_HEADS, jax.nn.sigmoid(ba), gdec)
        bg_s[g * tm:(g + 1) * tm, :] = jnp.where(rowi < valid, bg, 0.0)

    c = chunk
    ii = lax.broadcasted_iota(jnp.int32, (c, c), 0)
    jj = lax.broadcasted_iota(jnp.int32, (c, c), 1)
    incl = ii >= jj
    strict = ii > jj
    tril = jnp.where(incl, 1.0, 0.0).astype(F32)
    sel = jnp.where(lax.broadcasted_iota(jnp.int32, (SUBLANE, LANE), 1)
                    == lax.broadcasted_iota(jnp.int32, (SUBLANE, LANE), 0) + DN_HEADS, 1.0, 0.0).astype(F32)
    dnn = dnn_ref[...]

    heads = range(DN_HEADS)
    hcols = [slice(h * DN_DK, (h + 1) * DN_DK) for h in heads]
    acols = [slice(h * c, (h + 1) * c) for h in heads]
    n_chunks = rows_all // c
    chunks_per_seq = tm // c
    gpc = DN_GROUP if n_chunks % DN_GROUP == 0 else 1

    def chunk_rows(ci):
        return pl.ds(ci * c, c) if isinstance(ci, int) else pl.ds(pl.multiple_of(ci * c, c), c)

    def prep(cis):
        n_c = len(cis)
        rows = [chunk_rows(ci) for ci in cis]
        pairs = [(j, h) for j in range(n_c) for h in heads]
        bgc = [bg_s[r, :] for r in rows]
        gcum = [_dot_hi(tril, b) for b in bgc]
        gcum_t = [lax.dot_general(sel, g, (((1,), (1,)), ((), ())), preferred_element_type=F32, precision=HI)
                  for g in gcum]
        for j in range(n_c):
            eg_s[cis[j]] = jnp.broadcast_to(jnp.exp(gcum_t[j][:, c - 1:c]), (DN_HEADS, DN_DV))
        beta = [bgc[j][:, h:h + 1] for j, h in pairs]
        gc = [gcum[j][:, DN_HEADS + h:DN_HEADS + h + 1] for j, h in pairs]
        decay = [jnp.where(incl, jnp.exp(jnp.where(incl, gc[p] - gcum_t[j][h:h + 1, :], 0.0)), 0.0)
                 for p, (j, h) in enumerate(pairs)]
        q = [q_s[rows[j], hcols[h]] for j, h in pairs]
        k = [k_s[rows[j], hcols[h]] for j, h in pairs]
        npair = range(len(pairs))
        kb = [k[p] * beta[p] for p in npair]
        k_bf = [k[p].astype(BF16) for p in npair]
        akk = [_dot_nt(kb[p].astype(BF16), k_bf[p]) for p in npair]
        aqk = [_dot_nt(q[p].astype(BF16), k_bf[p]) for p in npair]
        tinv = _tri_inv_all([jnp.where(strict, akk[p] * decay[p], 0.0) for p in npair], c)
        egc = [jnp.exp(gc[p]) for p in npair]
        rhs = [jnp.concatenate([v_s[rows[j], hcols[h]] * beta[p], kb[p] * egc[p]], axis=1)
               for p, (j, h) in enumerate(pairs)]
        uw = [_mm_bf(tinv[p], rhs[p]) for p in npair]
        sdt = w_s.dtype
        for p, (j, h) in enumerate(pairs):
            u_s[rows[j], hcols[h]] = uw[p][:, :DN_DV]
            w_s[rows[j], hcols[h]] = uw[p][:, DN_DV:].astype(sdt)
            qe_s[rows[j], hcols[h]] = (q[p] * egc[p]).astype(sdt)
            kd_s[rows[j], hcols[h]] = (k[p] * jnp.exp(gc[p][c - 1:c, :] - gc[p])).astype(sdt)
            a_s[rows[j], acols[h]] = jnp.where(incl, aqk[p] * decay[p], 0.0).astype(sdt)

    def state(cpos):
        trip = [(g, h) for g in range(nseq) for h in heads]
        rows = [chunk_rows(g * chunks_per_seq + cpos) for g in range(nseq)]
        local = chunk_rows(cpos)
        egb = [eg_s[g * chunks_per_seq + cpos] for g in range(nseq)]
        s_old = [so_ref[g, h] for g, h in trip]
        s_bf = [s.astype(BF16) for s in s_old]
        ws = [_dot(w_s[rows[g], hcols[h]].astype(BF16), s_bf[n]) for n, (g, h) in enumerate(trip)]
        qs = [_dot(qe_s[rows[g], hcols[h]].astype(BF16), s_bf[n]) for n, (g, h) in enumerate(trip)]
        vn_bf = [(u_s[rows[g], hcols[h]] - ws[n]).astype(BF16) for n, (g, h) in enumerate(trip)]
        av = [_dot(a_s[rows[g], acols[h]].astype(BF16), vn_bf[n]) for n, (g, h) in enumerate(trip)]
        kv = [_dot_tn(kd_s[rows[g], hcols[h]].astype(BF16), vn_bf[n]) for n, (g, h) in enumerate(trip)]
        for n, (g, h) in enumerate(trip):
            so_ref[g, h] = s_old[n] * egb[g][h:h + 1, :] + kv[n]
            o = _rms(qs[n] + av[n], dnn) * _silu(z_ref[g, local, hcols[h]])
            o_ref[g, local, hcols[h]] = o.astype(o_ref.dtype)

    for gi in range(n_chunks // gpc):
        prep([gi * gpc + j for j in range(gpc)])
    if chunks_per_seq == 1:
        state(0)
    else:
        def state_body(cpos, carry):
            state(cpos)
            return carry
        lax.fori_loop(0, chunks_per_seq, state_body, 0)


def _deltanet(u, z, ba, cw, gp, dnn, conv0, s0, *, nseq, tm, chunk, valid, o_dtype):
    batch, seq, _ = u.shape
    nt = seq // tm
    rows = nseq * tm
    row = lambda b, t: (b, t, 0)
    per_b3 = lambda b, t: (b, 0, 0)
    per_b4 = lambda b, t: (b, 0, 0, 0)
    sdt = BF16 if chunk % 16 == 0 else F32
    kern = functools.partial(_dn_kernel, nseq=nseq, tm=tm, chunk=chunk, valid=valid)
    return pl.pallas_call(
        kern,
        out_shape=[jax.ShapeDtypeStruct((batch, seq, D_MODEL), o_dtype),
                   jax.ShapeDtypeStruct((batch, SUBLANE, CONV_CH), F32),
                   jax.ShapeDtypeStruct((batch, DN_HEADS, DN_DK, DN_DV), F32)],
        grid=(batch // nseq, nt),
        in_specs=[pl.BlockSpec((nseq, tm, CONV_CH), row),
                  pl.BlockSpec((nseq, tm, D_MODEL), row),
                  pl.BlockSpec((nseq, tm, LANE), row),
                  pl.BlockSpec((SUBLANE, CONV_CH), lambda b, t: (0, 0)),
                  pl.BlockSpec((SUBLANE, LANE), lambda b, t: (0, 0)),
                  pl.BlockSpec((1, DN_DV), lambda b, t: (0, 0)),
                  pl.BlockSpec((nseq, SUBLANE, CONV_CH), per_b3),
                  pl.BlockSpec((nseq, DN_HEADS, DN_DK, DN_DV), per_b4)],
        out_specs=[pl.BlockSpec((nseq, tm, D_MODEL), row),
                   pl.BlockSpec((nseq, SUBLANE, CONV_CH), per_b3),
                   pl.BlockSpec((nseq, DN_HEADS, DN_DK, DN_DV), per_b4)],
        scratch_shapes=[pltpu.VMEM((nseq, CONV_CH // LANE, SUBLANE + tm, LANE), F32),
                        pltpu.VMEM((rows, D_MODEL), F32),
                        pltpu.VMEM((rows, D_MODEL), F32),
                        pltpu.VMEM((rows, D_MODEL), F32),
                        pltpu.VMEM((rows, LANE), F32),
                        pltpu.VMEM((rows, D_MODEL), F32),
                        pltpu.VMEM((rows, D_MODEL), sdt),
                        pltpu.VMEM((rows, D_MODEL), sdt),
                        pltpu.VMEM((rows, D_MODEL), sdt),
                        pltpu.VMEM((rows, DN_HEADS * chunk), sdt),
                        pltpu.VMEM((rows // chunk, DN_HEADS, DN_DV), F32)],
        compiler_params=_params(("arbitrary", "arbitrary")),
        name="deltanet",
    )(u, z, ba, cw, gp, dnn, conv0, s0)


def _rope(x, cos, sa, sb):
    cols = []
    for cb in range(x.shape[1] // LANE):
        xc = x[:, cb * LANE:(cb + 1) * LANE]
        cols.append(xc * cos + pltpu.roll(xc, LANE - ROT_DIM // 2, 1) * sa + pltpu.roll(xc, ROT_DIM // 2, 1) * sb)
    return cols[0] if len(cols) == 1 else jnp.concatenate(cols, axis=1)


def _head_pair_operands(x):
    lane = lax.broadcasted_iota(jnp.int32, (x.shape[0], LANE), 1)
    out = []
    for cb in range(x.shape[1] // LANE):
        c = x[:, cb * LANE:(cb + 1) * LANE]
        lo = jnp.where(lane < SW_HEAD_DIM, c, 0.0)
        hi = jnp.where(lane >= SW_HEAD_DIM, c, 0.0)
        out.append((lo, pltpu.roll(lo, SW_HEAD_DIM, 1)))
        out.append((pltpu.roll(hi, SW_HEAD_DIM, 1), hi))
    return out


def _ones_columns(w):
    rowi = lax.broadcasted_iota(jnp.int32, (4 * w, LANE), 0)
    lanei = lax.broadcasted_iota(jnp.int32, (4 * w, LANE), 1)
    return jnp.where((rowi < 2 * w) == (lanei < SW_HEAD_DIM), 1.0, 0.0).astype(BF16)


def _pair_softmax(scores, mask, sinks, lane, w):
    parts, es = [], []
    for half in range(2):
        s = jnp.where(mask, scores[:, half * 2 * w:(half + 1) * 2 * w], NEG_INF)
        m = jnp.maximum(jnp.max(s, axis=-1, keepdims=True), sinks[half])
        parts.append(jnp.exp(s - m).astype(BF16))
        es.append(jnp.exp(sinks[half] - m))
    return jnp.concatenate(parts, axis=1), jnp.where(lane < SW_HEAD_DIM, es[0], es[1])


def _swa_prompt_kernel(sinks_ref, qkv_ref, cos_ref, sa_ref, sb_ref, o_ref, kc_ref, vc_ref, kk2, vv2):
    n = pl.program_id(1)
    w = WINDOW

    @pl.when(n == 0)
    def _():
        kk2[...] = jnp.zeros_like(kk2)
        for j in range(SW_KV_HEADS):
            vv2[j, :, 0:LANE] = jnp.zeros((4 * w, LANE), BF16)
            vv2[j, :, LANE:2 * LANE] = _ones_columns(w)

    for j in range(SW_KV_HEADS):
        kk2[j, 0:w, :] = kk2[j, w:2 * w, :]
        kk2[j, 2 * w:3 * w, :] = kk2[j, 3 * w:4 * w, :]
        vv2[j, 0:w, 0:LANE] = vv2[j, w:2 * w, 0:LANE]
        vv2[j, 2 * w:3 * w, 0:LANE] = vv2[j, 3 * w:4 * w, 0:LANE]
    cos, sa, sb = cos_ref[...], sa_ref[...], sb_ref[...]
    q = (_rope(qkv_ref[:, 0:D_MODEL], cos, sa, sb) * (SW_HEAD_DIM ** -0.5)).astype(BF16)
    k = _rope(qkv_ref[:, D_MODEL:D_MODEL + SW_KV], cos, sa, sb)
    v = qkv_ref[:, D_MODEL + SW_KV:D_MODEL + 2 * SW_KV]
    kc_ref[...] = k
    vc_ref[...] = v
    for j, (left, right) in enumerate(_head_pair_operands(k)):
        kk2[j, w:2 * w, :] = left.astype(BF16)
        kk2[j, 3 * w:4 * w, :] = right.astype(BF16)
    for j, (left, right) in enumerate(_head_pair_operands(v)):
        vv2[j, w:2 * w, 0:LANE] = left.astype(BF16)
        vv2[j, 3 * w:4 * w, 0:LANE] = right.astype(BF16)
    qi = lax.broadcasted_iota(jnp.int32, (w, 2 * w), 0) + w
    sj = lax.broadcasted_iota(jnp.int32, (w, 2 * w), 1)
    d = qi - sj
    mask = (d >= 0) & (d <= WINDOW) & ((sj >= w) | (n > 0))
    lane = lax.broadcasted_iota(jnp.int32, (w, LANE), 1)
    n_pb = D_MODEL // LANE
    scores = [_dot_nt(q[:, pb * LANE:(pb + 1) * LANE], kk2[pb // 2]) for pb in range(n_pb)]
    soft = [_pair_softmax(scores[pb], mask, (sinks_ref[2 * pb], sinks_ref[2 * pb + 1]), lane, w)
            for pb in range(n_pb)]
    outs = [_dot(soft[pb][0], vv2[pb // 2]) for pb in range(n_pb)]
    for pb in range(n_pb):
        o = outs[pb][:, 0:LANE] / (outs[pb][:, LANE:2 * LANE] + soft[pb][1])
        o_ref[:, pb * LANE:(pb + 1) * LANE] = o.astype(o_ref.dtype)


def _swa_prompt(sinks, qkv, cos, sa, sb, *, batch, seq):
    nb = seq // WINDOW
    row = lambda b, n: (b * nb + n, 0)
    tab = lambda b, n: (n, 0)
    per_b = lambda b, n: (b, 0, 0)
    return pl.pallas_call(
        _swa_prompt_kernel,
        out_shape=[jax.ShapeDtypeStruct((batch * seq, D_MODEL), BF16),
                   jax.ShapeDtypeStruct((batch, WINDOW, SW_KV), F32),
                   jax.ShapeDtypeStruct((batch, WINDOW, SW_KV), F32)],
        grid=(batch, nb),
        in_specs=[pl.BlockSpec(memory_space=pltpu.SMEM),
                  pl.BlockSpec((WINDOW, D_MODEL + 2 * SW_KV), row),
                  pl.BlockSpec((WINDOW, LANE), tab),
                  pl.BlockSpec((WINDOW, LANE), tab),
                  pl.BlockSpec((WINDOW, LANE), tab)],
        out_specs=[pl.BlockSpec((WINDOW, D_MODEL), row),
                   pl.BlockSpec((None, WINDOW, SW_KV), per_b),
                   pl.BlockSpec((None, WINDOW, SW_KV), per_b)],
        scratch_shapes=[pltpu.VMEM((SW_KV_HEADS, 4 * WINDOW, LANE), BF16),
                        pltpu.VMEM((SW_KV_HEADS, 4 * WINDOW, 2 * LANE), BF16)],
        compiler_params=_params(("arbitrary", "arbitrary")),
        name="swa_prompt",
    )(sinks, qkv, cos, sa, sb)


def _swa_sample_kernel(sinks_ref, qkv_ref, kbuf_ref, vbuf_ref, cos_ref, sa_ref, sb_ref,
                       o_ref, kc_ref, vc_ref, kk_s, vv_s, kk2, vv2, *, nseq, t_new):
    wb = WINDOW
    tp = SUBLANE

    @pl.when(pl.program_id(0) == 0)
    def _():
        kk2[...] = jnp.zeros_like(kk2)
        for g in range(nseq):
            for j in range(SW_KV_HEADS):
                vv2[g, j, :, 0:LANE] = jnp.zeros((4 * wb, LANE), BF16)
                vv2[g, j, :, LANE:2 * LANE] = _ones_columns(wb)

    cos, sa, sb = cos_ref[...], sa_ref[...], sb_ref[...]
    qs = []
    for g in range(nseq):
        q = (_rope(qkv_ref[g, :, 0:D_MODEL], cos, sa, sb) * (SW_HEAD_DIM ** -0.5)).astype(BF16)
        k_new = _rope(qkv_ref[g, :, D_MODEL:D_MODEL + SW_KV], cos, sa, sb)
        v_new = qkv_ref[g, :, D_MODEL + SW_KV:D_MODEL + 2 * SW_KV]
        k_old, v_old = kbuf_ref[g], vbuf_ref[g]
        kk_s[g, 0:wb, :] = k_old
        vv_s[g, 0:wb, :] = v_old
        kk_s[g, wb:wb + tp, :] = k_new
        vv_s[g, wb:wb + tp, :] = v_new
        kc_ref[g] = kk_s[g, pl.ds(t_new, wb), :]
        vc_ref[g] = vv_s[g, pl.ds(t_new, wb), :]
        for r0, n_r, kx, vx in ((0, wb, k_old, v_old), (wb, tp, k_new, v_new)):
            for j, (left, right) in enumerate(_head_pair_operands(kx)):
                kk2[g, j, r0:r0 + n_r, :] = left.astype(BF16)
                kk2[g, j, 2 * wb + r0:2 * wb + r0 + n_r, :] = right.astype(BF16)
            for j, (left, right) in enumerate(_head_pair_operands(vx)):
                vv2[g, j, r0:r0 + n_r, 0:LANE] = left.astype(BF16)
                vv2[g, j, 2 * wb + r0:2 * wb + r0 + n_r, 0:LANE] = right.astype(BF16)
        qs.append([jnp.concatenate([q[:, (2 * j) * LANE:(2 * j + 1) * LANE],
                                    q[:, (2 * j + 1) * LANE:(2 * j + 2) * LANE]], axis=0)
                   for j in range(SW_KV_HEADS)])
    rows = 2 * tp
    qt = lax.broadcasted_iota(jnp.int32, (rows, 2 * wb), 0) % tp
    sj = lax.broadcasted_iota(jnp.int32, (rows, 2 * wb), 1)
    d = qt + wb - sj
    mask = (d >= 0) & (d <= WINDOW)
    top = lax.broadcasted_iota(jnp.int32, (rows, 1), 0) < tp
    lane = lax.broadcasted_iota(jnp.int32, (rows, LANE), 1)
    items = [(g, j) for g in range(nseq) for j in range(SW_KV_HEADS)]
    scores = [_dot_nt(qs[g][j], kk2[g, j]) for g, j in items]
    soft = []
    for n, (g, j) in enumerate(items):
        sinks = [jnp.where(top, sinks_ref[4 * j + half], sinks_ref[4 * j + 2 + half]) for half in range(2)]
        soft.append(_pair_softmax(scores[n], mask, sinks, lane, wb))
    outs = [_dot(soft[n][0], vv2[g, j]) for n, (g, j) in enumerate(items)]
    for n, (g, j) in enumerate(items):
        o = outs[n][:, 0:LANE] / (outs[n][:, LANE:2 * LANE] + soft[n][1])
        o_ref[g, :, (2 * j) * LANE:(2 * j + 1) * LANE] = o[0:tp, :]
        o_ref[g, :, (2 * j + 1) * LANE:(2 * j + 2) * LANE] = o[tp:2 * tp, :]


def _swa_sample(sinks, qkv, kbuf, vbuf, cos, sa, sb, *, batch, nseq, t_new):
    per_b = lambda b: (b, 0, 0)
    tab = lambda b: (0, 0)
    kern = functools.partial(_swa_sample_kernel, nseq=nseq, t_new=t_new)
    return pl.pallas_call(
        kern,
        out_shape=[jax.ShapeDtypeStruct((batch, SUBLANE, D_MODEL), F32),
                   jax.ShapeDtypeStruct((batch, WINDOW, SW_KV), F32),
                   jax.ShapeDtypeStruct((batch, WINDOW, SW_KV), F32)],
        grid=(batch // nseq,),
        in_specs=[pl.BlockSpec(memory_space=pltpu.SMEM),
                  pl.BlockSpec((nseq, SUBLANE, D_MODEL + 2 * SW_KV), per_b),
                  pl.BlockSpec((nseq, WINDOW, SW_KV), per_b),
                  pl.BlockSpec((nseq, WINDOW, SW_KV), per_b),
                  pl.BlockSpec((SUBLANE, LANE), tab),
                  pl.BlockSpec((SUBLANE, LANE), tab),
                  pl.BlockSpec((SUBLANE, LANE), tab)],
        out_specs=[pl.BlockSpec((nseq, SUBLANE, D_MODEL), per_b),
                   pl.BlockSpec((nseq, WINDOW, SW_KV), per_b),
                   pl.BlockSpec((nseq, WINDOW, SW_KV), per_b)],
        scratch_shapes=[pltpu.VMEM((nseq, WINDOW + SUBLANE, SW_KV), F32),
                        pltpu.VMEM((nseq, WINDOW + SUBLANE, SW_KV), F32),
                        pltpu.VMEM((nseq, SW_KV_HEADS, 4 * WINDOW, LANE), BF16),
                        pltpu.VMEM((nseq, SW_KV_HEADS, 4 * WINDOW, 2 * LANE), BF16)],
        compiler_params=_params(("arbitrary",)),
        name="swa_sample",
    )(sinks, qkv, kbuf, vbuf, cos, sa, sb)


def _out_ffn_kernel(x_ref, g2_ref, post2_ref, ga_ref, gb_ref, odn_ref, osw_ref, wo_ref,
                    sh_ref, sc_ref, g_ref, pre_ref, post_ref, wg_ref, wu_ref, wd_ref, o_ref,
                    *, per_token, tiles_per_seq):
    rows = _row_groups(x_ref.shape[0])
    g2 = _mod_rows(g2_ref, per_token, tiles_per_seq)
    sh = _mod_rows(sh_ref, per_token, tiles_per_seq)
    sc = _mod_rows(sc_ref, per_token, tiles_per_seq)
    g = _mod_rows(g_ref, per_token, tiles_per_seq)
    pick = (lambda m, r: m[r]) if per_token else (lambda m, r: m)
    f32 = lambda ref, r: ref[r, :].astype(F32)
    ys = [(jax.nn.sigmoid(f32(ga_ref, r)) * f32(odn_ref, r)
           + jax.nn.sigmoid(f32(gb_ref, r)) * f32(osw_ref, r)).astype(BF16) for r in rows]
    ps = [_dot(y, wo_ref[...]) for y in ys]
    xs = [x_ref[r, :] + pick(g2, r) * _rms(p, post2_ref[...]) for p, r in zip(ps, rows)]
    outs = _ffn_stages(xs, rows, pick, sh, sc, g, pre_ref[...], post_ref[...], wg_ref, wu_ref, wd_ref)
    for o, r in zip(outs, rows):
        o_ref[r, :] = o


def _out_ffn(x, mod, mod_block, seq_len, post2, gates, o_dn, o_sw, w_out, pre, post, wg, wu, wd):
    n = x.shape[0]
    tm = min(512, n)
    dff = wg.shape[1]
    kern = functools.partial(_out_ffn_kernel, per_token=mod_block is None, tiles_per_seq=max(seq_len // tm, 1))
    row = lambda i: (i, 0)
    return pl.pallas_call(
        kern,
        out_shape=jax.ShapeDtypeStruct((n, D_MODEL), F32),
        grid=(n // tm,),
        in_specs=[pl.BlockSpec((tm, D_MODEL), row)]
        + _mod_specs(mod_block, tm, (5,))
        + [_const_spec((1, D_MODEL)),
           pl.BlockSpec((tm, D_MODEL), lambda i: (i, 0)),
           pl.BlockSpec((tm, D_MODEL), lambda i: (i, 1)),
           pl.BlockSpec((tm, D_MODEL), row),
           pl.BlockSpec((tm, D_MODEL), row),
           _const_spec((D_MODEL, D_MODEL))]
        + _mod_specs(mod_block, tm, (6, 7, 8))
        + [_const_spec((1, D_MODEL)), _const_spec((1, D_MODEL)),
           _const_spec((D_MODEL, dff)), _const_spec((D_MODEL, dff)), _const_spec((dff, D_MODEL))],
        out_specs=pl.BlockSpec((tm, D_MODEL), row),
        compiler_params=_params(("arbitrary",)),
        name="out_ffn",
    )(x, mod, post2, gates, gates, o_dn, o_sw, w_out, mod, mod, mod, pre, post, wg, wu, wd)


def _rope_tables(pos):
    half = ROT_DIM // 2
    inv_freq = ROPE_THETA ** (-jnp.arange(half, dtype=F32) * (2.0 / ROT_DIM))
    ang = pos.astype(F32)[:, None] * inv_freq[None, :]
    cos, sin = jnp.cos(ang), jnp.sin(ang)
    n = pos.shape[0]
    rest = SW_HEAD_DIM - ROT_DIM
    c64 = jnp.concatenate([cos, cos, jnp.ones((n, rest), F32)], axis=1)
    a64 = jnp.concatenate([-sin, jnp.zeros((n, half + rest), F32)], axis=1)
    b64 = jnp.concatenate([jnp.zeros((n, half), F32), sin, jnp.zeros((n, rest), F32)], axis=1)
    rep = LANE // SW_HEAD_DIM
    return tuple(jnp.tile(t, (1, rep)) for t in (c64, a64, b64))


def _pad_rows(a, rows):
    return jnp.pad(a, ((0, rows - a.shape[0]),) + ((0, 0),) * (a.ndim - 1))


def _layer(x, mod, mod_block, seq, rows, wts, conv0, s0, dn_fn, swa_fn):
    def per_seq(a):
        a = a.reshape(-1, seq, a.shape[-1])
        return a if rows == seq else jnp.pad(a, ((0, 0), (0, rows - seq), (0, 0)))

    def flat(a):
        return a[:, :seq].reshape(-1, a.shape[-1])

    x = _ffn(x, mod, 0, mod_block, seq, wts["pre1"], wts["post1"], wts["wg1"], wts["wu1"], wts["wd1"])
    u, z, qkv, gates, ba = _proj(x, mod, mod_block, seq, wts["pre2"], *wts["w_in"])
    o_dn, conv_new, s_new = dn_fn(per_seq(u), per_seq(z), per_seq(ba), wts["conv_w"], wts["gparam"],
                                  wts["dn_norm"], conv0, s0)
    o_sw, k_new, v_new = swa_fn(per_seq(qkv))
    x = _out_ffn(x, mod, mod_block, seq, wts["post2"], gates, flat(o_dn), flat(o_sw), wts["w_out"],
                 wts["pre3"], wts["post3"], wts["wg2"], wts["wu2"], wts["wd2"])
    return x, (k_new, v_new, conv_new, s_new)


def kernel(x_prompt, x_sample, cache_swa_k, cache_swa_v, state_conv, state_delta, c_prompt, c_sample,
           w_ada, b_ada, ffn1_norm_pre, ffn1_norm_post, ffn1_w_gate, ffn1_w_up, ffn1_w_down,
           mix_norm_pre, mix_norm_post, w_in, conv_w, a_log, dt_bias, dn_norm, sinks, w_out,
           ffn2_norm_pre, ffn2_norm_post, ffn2_w_gate, ffn2_w_up, ffn2_w_down):
    depth = w_ada.shape[0]
    bp, tp, _ = x_prompt.shape
    bs, ts, _ = x_sample.shape
    assert tp % 512 == 0 and 1 <= ts <= SUBLANE and bs % SAMPLE_SEQS == 0 and (bs * ts) % SUBLANE == 0

    cos_p, sa_p, sb_p = _rope_tables(jnp.arange(tp))
    cos_s, sa_s, sb_s = (_pad_rows(t, SUBLANE) for t in _rope_tables(PAST_LEN + jnp.arange(ts)))

    xp = x_prompt.reshape(bp * tp, D_MODEL)
    xs = x_sample.reshape(bs * ts, D_MODEL)
    c_all = jnp.concatenate([jnp.repeat(c_sample, ts, axis=0), _pad_rows(c_prompt, SUBLANE)], axis=0)
    prompt_mod_block = bs * ts // SUBLANE

    outs_p, outs_s = [], []
    for l in range(depth):
        gparam = jnp.zeros((SUBLANE, LANE), F32)
        gparam = gparam.at[0, DN_HEADS:2 * DN_HEADS].set(dt_bias[l]).at[1, DN_HEADS:2 * DN_HEADS].set(a_log[l])
        wts = dict(
            pre1=ffn1_norm_pre[l][None], post1=ffn1_norm_post[l][None],
            wg1=_to_bf16(ffn1_w_gate[l]), wu1=_to_bf16(ffn1_w_up[l]), wd1=_to_bf16(ffn1_w_down[l]),
            pre2=mix_norm_pre[l][None], post2=mix_norm_post[l][None], w_in=_split_w_in(w_in[l]),
            conv_w=_pad_rows(conv_w[l], SUBLANE), gparam=gparam, dn_norm=dn_norm[l][None],
            w_out=_to_bf16(w_out[l]),
            pre3=ffn2_norm_pre[l][None], post3=ffn2_norm_post[l][None],
            wg2=_to_bf16(ffn2_w_gate[l]), wu2=_to_bf16(ffn2_w_up[l]), wd2=_to_bf16(ffn2_w_down[l]),
        )
        b_ada_l = b_ada[l][None]
        mod = _ada(c_all, w_ada[l], b_ada_l)
        sink_l = sinks[l]

        conv0_p = jnp.zeros((bp, SUBLANE, CONV_CH), F32)
        s0_p = jnp.zeros((bp, DN_HEADS, DN_DK, DN_DV), F32)
        dn_p = functools.partial(_deltanet, nseq=bp, tm=DN_TILE, chunk=DN_CHUNK, valid=DN_TILE, o_dtype=BF16)

        def swa_p(qkv):
            o, k, v = _swa_prompt(sink_l, qkv.reshape(bp * tp, -1), cos_p, sa_p, sb_p, batch=bp, seq=tp)
            return o.reshape(bp, tp, D_MODEL), k, v

        xp, st_p = _layer(xp, mod, prompt_mod_block, tp, tp, wts, conv0_p, s0_p, dn_p, swa_p)

        conv0_s = jnp.pad(state_conv[l], ((0, 0), (SUBLANE - CONV_W + 1, 0), (0, 0)))
        kbuf = cache_swa_k[l].reshape(bs, WINDOW, SW_KV)
        vbuf = cache_swa_v[l].reshape(bs, WINDOW, SW_KV)
        dn_s = functools.partial(_deltanet, nseq=SAMPLE_SEQS, tm=SUBLANE, chunk=SUBLANE, valid=ts, o_dtype=F32)

        swa_s = lambda qkv: _swa_sample(sink_l, qkv, kbuf, vbuf, cos_s, sa_s, sb_s,
                                        batch=bs, nseq=SAMPLE_SEQS, t_new=ts)
        xs, st_s = _layer(xs, mod, None, ts, SUBLANE, wts, conv0_s, state_delta[l], dn_s, swa_s)
        outs_p.append(st_p)
        outs_s.append(st_s)

    def stack(outs, i):
        return jnp.stack([o[i] for o in outs])

    def kv5(a, batch):
        return a.reshape(depth, batch, WINDOW, SW_KV_HEADS, SW_HEAD_DIM)

    tail = slice(SUBLANE - CONV_W + 1, SUBLANE)
    y_p = xp.reshape(bp, tp, D_MODEL)
    y_s = xs.reshape(bs, ts, D_MODEL)
    return (y_p, y_s,
            kv5(stack(outs_p, 0), bp), kv5(stack(outs_p, 1), bp),
            stack(outs_p, 2)[:, :, tail], stack(outs_p, 3),
            kv5(stack(outs_s, 0), bs), kv5(stack(outs_s, 1), bs),
            stack(outs_s, 2)[:, :, tail], stack(outs_s, 3))
```

```python
import functools

import jax
import jax.numpy as jnp
from jax import lax
from jax.experimental import pallas as pl
from jax.experimental.pallas import tpu as pltpu

F32, BF16 = jnp.float32, jnp.bfloat16

D_MODEL = 1024
DN_HEADS, DN_DK, DN_DV = 8, 128, 128
CONV_W = 4
CONV_CH = 3 * D_MODEL
DN_CHUNK = 64
SW_HEAD_DIM, SW_HEADS, SW_KV_HEADS, SW_GROUP = 64, 16, 4, 4
SW_KV = SW_KV_HEADS * SW_HEAD_DIM
WINDOW = 128
ROT_DIM = 16
ROPE_THETA = 500000.0
PAST_LEN = 8192
EPS = 1e-6
NEG_INF = -1e30

LANE = 128
SUBLANE = 8
VMEM_LIMIT = 56 * 1024 * 1024

P_U, P_Z, P_QKV, P_G, P_BA = 0, 3072, 4096, 5632, 7680
P_TOTAL = 7808
HI = lax.Precision.HIGHEST

DN_TILE = 128
DN_GROUP = 4
SAMPLE_SEQS = 8
FFN_SUBTILES = 2


def _dot(a, b):
    return jnp.dot(a, b, preferred_element_type=F32)


def _dot_nt(a, b):
    return lax.dot_general(a, b, (((1,), (1,)), ((), ())), preferred_element_type=F32)


def _dot_tn(a, b):
    return lax.dot_general(a, b, (((0,), (0,)), ((), ())), preferred_element_type=F32)


def _dot_hi(a, b):
    return jnp.dot(a, b, preferred_element_type=F32, precision=HI)


def _mm_bf(a, b):
    return _dot(a.astype(BF16), b.astype(BF16))


def _sigmoid(x):
    return 0.5 * jnp.tanh(0.5 * x) + 0.5


def _silu(x):
    return x * _sigmoid(x)


def _rms(x, gain):
    ms = jnp.mean(x * x, axis=-1, keepdims=True)
    return x * lax.rsqrt(ms + EPS) * gain


def _mod_rows(ref, per_token, tiles_per_seq):
    if per_token:
        return ref[...]
    b = pl.program_id(0) // tiles_per_seq
    return ref[pl.ds(b, 1), :]


def _const_spec(shape):
    return pl.BlockSpec(shape, lambda *_: (0,) * len(shape), pipeline_mode=pl.Buffered(1))


def _params(sem):
    return pltpu.CompilerParams(dimension_semantics=sem, vmem_limit_bytes=VMEM_LIMIT)


def _ada_kernel(c_ref, w_ref, b_ref, o_ref):
    h = _silu(c_ref[...]).astype(BF16)
    o_ref[...] = _dot(h, w_ref[...].astype(BF16)) + b_ref[...]


def _ada(c, w, b):
    rows, n = c.shape[0], w.shape[1]
    return pl.pallas_call(
        _ada_kernel,
        out_shape=jax.ShapeDtypeStruct((rows, n), F32),
        grid=(n // D_MODEL,),
        in_specs=[
            pl.BlockSpec((rows, D_MODEL), lambda j: (0, 0)),
            pl.BlockSpec((D_MODEL, D_MODEL), lambda j: (0, j)),
            pl.BlockSpec((1, D_MODEL), lambda j: (0, j)),
        ],
        out_specs=pl.BlockSpec((rows, D_MODEL), lambda j: (0, j)),
        compiler_params=_params(("arbitrary",)),
        name="ada",
    )(c, w, b)


def _mod_specs(mod_block, tm, cols):
    if mod_block is None:
        return [pl.BlockSpec((tm, D_MODEL), functools.partial(lambda i, c: (i, c), c=c)) for c in cols]
    return [pl.BlockSpec((SUBLANE, D_MODEL), functools.partial(lambda i, c: (mod_block, c), c=c)) for c in cols]


def _ffn_stages(xs, rows, pick, sh, sc, g, pre, post, wg_ref, wu_ref, wd_ref):
    hs = [(_rms(x, pre) * (1.0 + pick(sc, r)) + pick(sh, r)).astype(BF16) for x, r in zip(xs, rows)]
    gates = [_dot(h, wg_ref[...]) for h in hs]
    ups = [_dot(h, wu_ref[...]) for h in hs]
    acts = [(_silu(a) * b).astype(BF16) for a, b in zip(gates, ups)]
    ys = [_dot(a, wd_ref[...]) for a in acts]
    return [x + 0.5 * pick(g, r) * _rms(y, post) for x, y, r in zip(xs, ys, rows)]


def _row_groups(tm):
    return [slice(i * tm // FFN_SUBTILES, (i + 1) * tm // FFN_SUBTILES) for i in range(FFN_SUBTILES)]


def _ffn_kernel(x_ref, sh_ref, sc_ref, g_ref, pre_ref, post_ref, wg_ref, wu_ref, wd_ref, o_ref,
                *, per_token, tiles_per_seq):
    rows = _row_groups(x_ref.shape[0])
    sh = _mod_rows(sh_ref, per_token, tiles_per_seq)
    sc = _mod_rows(sc_ref, per_token, tiles_per_seq)
    g = _mod_rows(g_ref, per_token, tiles_per_seq)
    pick = (lambda m, r: m[r]) if per_token else (lambda m, r: m)
    xs = [x_ref[r, :] for r in rows]
    outs = _ffn_stages(xs, rows, pick, sh, sc, g, pre_ref[...], post_ref[...], wg_ref, wu_ref, wd_ref)
    for o, r in zip(outs, rows):
        o_ref[r, :] = o


def _ffn(x, mod, col0, mod_block, seq_len, pre, post, wg, wu, wd):
    n = x.shape[0]
    tm = min(512, n)
    dff = wg.shape[1]
    kern = functools.partial(_ffn_kernel, per_token=mod_block is None, tiles_per_seq=max(seq_len // tm, 1))
    row = lambda i: (i, 0)
    return pl.pallas_call(
        kern,
        out_shape=jax.ShapeDtypeStruct((n, D_MODEL), F32),
        grid=(n // tm,),
        in_specs=[pl.BlockSpec((tm, D_MODEL), row)]
        + _mod_specs(mod_block, tm, (col0, col0 + 1, col0 + 2))
        + [_const_spec((1, D_MODEL)), _const_spec((1, D_MODEL)),
           _const_spec((D_MODEL, dff)), _const_spec((D_MODEL, dff)), _const_spec((dff, D_MODEL))],
        out_specs=pl.BlockSpec((tm, D_MODEL), row),
        compiler_params=_params(("arbitrary",)),
        name="ffn",
    )(x, mod, mod, mod, pre, post, wg, wu, wd)


def _proj_kernel(x_ref, sh_ref, sc_ref, pre_ref, wa_ref, wb_ref, wc_ref, u_ref, z_ref, qkv_ref, g_ref, ba_ref,
                 *, per_token, tiles_per_seq):
    sh = _mod_rows(sh_ref, per_token, tiles_per_seq)
    sc = _mod_rows(sc_ref, per_token, tiles_per_seq)
    h = (_rms(x_ref[...], pre_ref[...]) * (1.0 + sc) + sh).astype(BF16)
    u_ref[...] = _dot(h, wa_ref[:, 0:P_Z])
    z_ref[...] = _dot(h, wa_ref[:, P_Z:P_QKV])
    qkv_ref[...] = _dot(h, wb_ref[:, 0:P_G - P_QKV])
    g_ref[...] = _dot(h, wb_ref[:, P_G - P_QKV:P_BA - P_QKV]).astype(g_ref.dtype)
    ba_ref[...] = _dot(h, wc_ref[...])


def _proj(x, mod, mod_block, seq_len, pre, wa, wb, wc):
    n = x.shape[0]
    tm = min(512, n)
    kern = functools.partial(_proj_kernel, per_token=mod_block is None, tiles_per_seq=max(seq_len // tm, 1))
    row = lambda i: (i, 0)
    widths = (P_Z - P_U, P_QKV - P_Z, P_G - P_QKV, P_BA - P_G, P_TOTAL - P_BA)
    return pl.pallas_call(
        kern,
        out_shape=[jax.ShapeDtypeStruct((n, wd), BF16 if wd == P_BA - P_G else F32) for wd in widths],
        grid=(n // tm,),
        in_specs=[pl.BlockSpec((tm, D_MODEL), row)]
        + _mod_specs(mod_block, tm, (3, 4))
        + [_const_spec((1, D_MODEL)), _const_spec(wa.shape), _const_spec(wb.shape), _const_spec(wc.shape)],
        out_specs=[pl.BlockSpec((tm, wd), row) for wd in widths],
        compiler_params=_params(("arbitrary",)),
        name="proj",
    )(x, mod, mod, pre, wa, wb, wc)


def _tri_inv_all(lows, c):
    ii = lax.broadcasted_iota(jnp.int32, (c, c), 0)
    jj = lax.broadcasted_iota(jnp.int32, (c, c), 1)
    eye = jnp.where(ii == jj, 1.0, 0.0).astype(F32)
    xs = [eye - jnp.where(ii // 2 == jj // 2, low, 0.0) for low in lows]
    b = 2
    while b < c:
        join = (ii // (2 * b) == jj // (2 * b)) & (ii // b != jj // b)
        ys = [_mm_bf(x, jnp.where(join, low, 0.0)) for x, low in zip(xs, lows)]
        xs = [x - _mm_bf(y, x) for x, y in zip(xs, ys)]
        b *= 2
    return xs


def _dn_kernel(u_ref, z_ref, ba_ref, cw_ref, gp_ref, dnn_ref, conv0_ref, s0_ref,
               o_ref, convo_ref, so_ref, ubuf, q_s, k_s, v_s, bg_s, u_s, w_s, qe_s, kd_s, a_s, eg_s,
               *, nseq, tm, chunk, valid):
    t = pl.program_id(1)
    rows_all = nseq * tm

    n_cb = CONV_CH // LANE

    @pl.when(t == 0)
    def _():
        for g in range(nseq):
            for cb in range(n_cb):
                ubuf[g, cb, 0:SUBLANE, :] = conv0_ref[g, :, cb * LANE:(cb + 1) * LANE]
        so_ref[...] = s0_ref[...]

    for g in range(nseq):
        rws = slice(g * tm, (g + 1) * tm)
        for cb in range(n_cb):
            cols = slice(cb * LANE, (cb + 1) * LANE)
            ubuf[g, cb, SUBLANE:SUBLANE + tm, :] = u_ref[g, :, cols]
            acc = ubuf[g, cb, pl.ds(SUBLANE - CONV_W + 1, tm), :] * cw_ref[0:1, cols]
            for i in range(1, CONV_W):
                acc = acc + ubuf[g, cb, pl.ds(SUBLANE - CONV_W + 1 + i, tm), :] * cw_ref[i:i + 1, cols]
            a = _silu(acc)
            sec, off = divmod(cb * LANE, D_MODEL)
            dst = slice(off, off + LANE)
            if sec == 2:
                v_s[rws, dst] = a
            else:
                an = a * lax.rsqrt(jnp.sum(a * a, axis=-1, keepdims=True) + EPS)
                if sec == 0:
                    q_s[rws, dst] = an * (DN_DK ** -0.5)
                else:
                    k_s[rws, dst] = an
            tail = ubuf[g, cb, pl.ds(valid, SUBLANE), :]
            ubuf[g, cb, 0:SUBLANE, :] = tail
            convo_ref[g, :, cols] = tail

    lane = lax.broadcasted_iota(jnp.int32, (tm, LANE), 1)
    rowi = lax.broadcasted_iota(jnp.int32, (tm, LANE), 0)
    for g in range(nseq):
        ba = ba_ref[g]
        xg = ba + gp_ref[0:1, :]
        softplus = jnp.maximum(xg, 0.0) + jnp.log1p(jnp.exp(-jnp.abs(xg)))
        gdec = -jnp.exp(gp_ref[1:2, :]) * softplus
        bg = jnp.where(lane < DN_HEADS, _sigmoid(ba), gdec)
        bg_s[g * tm:(g + 1) * tm, :] = jnp.where(rowi < valid, bg, 0.0)

    c = chunk
    ii = lax.broadcasted_iota(jnp.int32, (c, c), 0)
    jj = lax.broadcasted_iota(jnp.int32, (c, c), 1)
    incl = ii >= jj
    strict = ii > jj
    tril = jnp.where(incl, 1.0, 0.0).astype(F32)
    sel = jnp.where(lax.broadcasted_iota(jnp.int32, (SUBLANE, LANE), 1)
                    == lax.broadcasted_iota(jnp.int32, (SUBLANE, LANE), 0) + DN_HEADS, 1.0, 0.0).astype(F32)
    dnn = dnn_ref[...]

    heads = range(DN_HEADS)
    hcols = [slice(h * DN_DK, (h + 1) * DN_DK) for h in heads]
    acols = [slice(h * c, (h + 1) * c) for h in heads]
    n_chunks = rows_all // c
    chunks_per_seq = tm // c
    gpc = DN_GROUP if n_chunks % DN_GROUP == 0 else 1

    def chunk_rows(ci):
        return pl.ds(ci * c, c) if isinstance(ci, int) else pl.ds(pl.multiple_of(ci * c, c), c)

    def prep(cis):
        n_c = len(cis)
        rows = [chunk_rows(ci) for ci in cis]
        pairs = [(j, h) for j in range(n_c) for h in heads]
        bgc = [bg_s[r, :] for r in rows]
        gcum = [_dot_hi(tril, b) for b in bgc]
        gcum_t = [lax.dot_general(sel, g, (((1,), (1,)), ((), ())), preferred_element_type=F32, precision=HI)
                  for g in gcum]
        for j in range(n_c):
            eg_s[cis[j]] = jnp.broadcast_to(jnp.exp(gcum_t[j][:, c - 1:c]), (DN_HEADS, DN_DV))
        spread = lambda col: jnp.broadcast_to(col, (c, DN_DK))
        beta = [spread(bgc[j][:, h:h + 1]) for j, h in pairs]
        gc = [spread(gcum[j][:, DN_HEADS + h:DN_HEADS + h + 1]) for j, h in pairs]
        decay = [jnp.where(incl, jnp.exp(jnp.where(incl, gc[p][:, :c] - gcum_t[j][h:h + 1, :], 0.0)), 0.0)
                 for p, (j, h) in enumerate(pairs)]
        q = [q_s[rows[j], hcols[h]] for j, h in pairs]
        k = [k_s[rows[j], hcols[h]] for j, h in pairs]
        npair = range(len(pairs))
        kb = [k[p] * beta[p] for p in npair]
        k_bf = [k[p].astype(BF16) for p in npair]
        akk = [_dot_nt(kb[p].astype(BF16), k_bf[p]) for p in npair]
        aqk = [_dot_nt(q[p].astype(BF16), k_bf[p]) for p in npair]
        tinv = _tri_inv_all([jnp.where(strict, akk[p] * decay[p], 0.0) for p in npair], c)
        egc = [jnp.exp(gc[p]) for p in npair]
        rhs = [jnp.concatenate([v_s[rows[j], hcols[h]] * beta[p], kb[p] * egc[p]], axis=1)
               for p, (j, h) in enumerate(pairs)]
        uw = [_mm_bf(tinv[p], rhs[p]) for p in npair]
        sdt = w_s.dtype
        for p, (j, h) in enumerate(pairs):
            u_s[rows[j], hcols[h]] = uw[p][:, :DN_DV]
            w_s[rows[j], hcols[h]] = uw[p][:, DN_DV:].astype(sdt)
            qe_s[rows[j], hcols[h]] = (q[p] * egc[p]).astype(sdt)
            kd_s[rows[j], hcols[h]] = (k[p] * jnp.exp(gc[p][c - 1:c, :] - gc[p])).astype(sdt)
            a_s[rows[j], acols[h]] = jnp.where(incl, aqk[p] * decay[p], 0.0).astype(sdt)

    def state(cpos):
        trip = [(g, h) for g in range(nseq) for h in heads]
        rows = [chunk_rows(g * chunks_per_seq + cpos) for g in range(nseq)]
        local = chunk_rows(cpos)
        egb = [eg_s[g * chunks_per_seq + cpos] for g in range(nseq)]
        s_old = [so_ref[g, h] for g, h in trip]
        s_bf = [s.astype(BF16) for s in s_old]
        ws = [_dot(w_s[rows[g], hcols[h]].astype(BF16), s_bf[n]) for n, (g, h) in enumerate(trip)]
        qs = [_dot(qe_s[rows[g], hcols[h]].astype(BF16), s_bf[n]) for n, (g, h) in enumerate(trip)]
        vn_bf = [(u_s[rows[g], hcols[h]] - ws[n]).astype(BF16) for n, (g, h) in enumerate(trip)]
        av = [_dot(a_s[rows[g], acols[h]].astype(BF16), vn_bf[n]) for n, (g, h) in enumerate(trip)]
        kv = [_dot_tn(kd_s[rows[g], hcols[h]].astype(BF16), vn_bf[n]) for n, (g, h) in enumerate(trip)]
        for n, (g, h) in enumerate(trip):
            so_ref[g, h] = s_old[n] * egb[g][h:h + 1, :] + kv[n]
            o = _rms(qs[n] + av[n], dnn) * _silu(z_ref[g, local, hcols[h]])
            o_ref[g, local, hcols[h]] = o.astype(o_ref.dtype)

    for gi in range(n_chunks // gpc):
        prep([gi * gpc + j for j in range(gpc)])
    if chunks_per_seq == 1:
        state(0)
    else:
        def state_body(cpos, carry):
            state(cpos)
            return carry
        lax.fori_loop(0, chunks_per_seq, state_body, 0)


def _deltanet(u, z, ba, cw, gp, dnn, conv0, s0, *, nseq, tm, chunk, valid, o_dtype):
    batch, seq, _ = u.shape
    nt = seq // tm
    rows = nseq * tm
    row = lambda b, t: (b, t, 0)
    per_b3 = lambda b, t: (b, 0, 0)
    per_b4 = lambda b, t: (b, 0, 0, 0)
    sdt = BF16 if chunk % 16 == 0 else F32
    kern = functools.partial(_dn_kernel, nseq=nseq, tm=tm, chunk=chunk, valid=valid)
    return pl.pallas_call(
        kern,
        out_shape=[jax.ShapeDtypeStruct((batch, seq, D_MODEL), o_dtype),
                   jax.ShapeDtypeStruct((batch, SUBLANE, CONV_CH), F32),
                   jax.ShapeDtypeStruct((batch, DN_HEADS, DN_DK, DN_DV), F32)],
        grid=(batch // nseq, nt),
        in_specs=[pl.BlockSpec((nseq, tm, CONV_CH), row),
                  pl.BlockSpec((nseq, tm, D_MODEL), row),
                  pl.BlockSpec((nseq, tm, LANE), row),
                  pl.BlockSpec((SUBLANE, CONV_CH), lambda b, t: (0, 0)),
                  pl.BlockSpec((SUBLANE, LANE), lambda b, t: (0, 0)),
                  pl.BlockSpec((1, DN_DV), lambda b, t: (0, 0)),
                  pl.BlockSpec((nseq, SUBLANE, CONV_CH), per_b3),
                  pl.BlockSpec((nseq, DN_HEADS, DN_DK, DN_DV), per_b4)],
        out_specs=[pl.BlockSpec((nseq, tm, D_MODEL), row),
                   pl.BlockSpec((nseq, SUBLANE, CONV_CH), per_b3),
                   pl.BlockSpec((nseq, DN_HEADS, DN_DK, DN_DV), per_b4)],
        scratch_shapes=[pltpu.VMEM((nseq, CONV_CH // LANE, SUBLANE + tm, LANE), F32),
                        pltpu.VMEM((rows, D_MODEL), F32),
                        pltpu.VMEM((rows, D_MODEL), F32),
                        pltpu.VMEM((rows, D_MODEL), F32),
                        pltpu.VMEM((rows, LANE), F32),
                        pltpu.VMEM((rows, D_MODEL), F32),
                        pltpu.VMEM((rows, D_MODEL), sdt),
                        pltpu.VMEM((rows, D_MODEL), sdt),
                        pltpu.VMEM((rows, D_MODEL), sdt),
                        pltpu.VMEM((rows, DN_HEADS * chunk), sdt),
                        pltpu.VMEM((rows // chunk, DN_HEADS, DN_DV), F32)],
        compiler_params=_params(("arbitrary", "arbitrary")),
        name="deltanet",
    )(u, z, ba, cw, gp, dnn, conv0, s0)


def _rope(x, cos, sa, sb):
    cols = []
    for cb in range(x.shape[1] // LANE):
        xc = x[:, cb * LANE:(cb + 1) * LANE]
        cols.append(xc * cos + pltpu.roll(xc, LANE - ROT_DIM // 2, 1) * sa + pltpu.roll(xc, ROT_DIM // 2, 1) * sb)
    return cols[0] if len(cols) == 1 else jnp.concatenate(cols, axis=1)


def _head_pair_operands(x):
    lane = lax.broadcasted_iota(jnp.int32, (x.shape[0], LANE), 1)
    out = []
    for cb in range(x.shape[1] // LANE):
        c = x[:, cb * LANE:(cb + 1) * LANE]
        lo = jnp.where(lane < SW_HEAD_DIM, c, 0.0)
        hi = jnp.where(lane >= SW_HEAD_DIM, c, 0.0)
        out.append((lo, pltpu.roll(lo, SW_HEAD_DIM, 1)))
        out.append((pltpu.roll(hi, SW_HEAD_DIM, 1), hi))
    return out


def _ones_columns(w):
    rowi = lax.broadcasted_iota(jnp.int32, (4 * w, LANE), 0)
    lanei = lax.broadcasted_iota(jnp.int32, (4 * w, LANE), 1)
    return jnp.where((rowi < 2 * w) == (lanei < SW_HEAD_DIM), 1.0, 0.0).astype(BF16)


def _pair_softmax(scores, mask, sinks, lane, w):
    parts, es = [], []
    for half in range(2):
        s = jnp.where(mask, scores[:, half * 2 * w:(half + 1) * 2 * w], NEG_INF)
        m = jnp.maximum(jnp.max(s, axis=-1, keepdims=True), sinks[half])
        parts.append(jnp.exp(s - m).astype(BF16))
        es.append(jnp.exp(sinks[half] - m))
    return jnp.concatenate(parts, axis=1), jnp.where(lane < SW_HEAD_DIM, es[0], es[1])


def _swa_prompt_kernel(sinks_ref, qkv_ref, cos_ref, sa_ref, sb_ref, o_ref, kc_ref, vc_ref, kk2, vv2):
    n = pl.program_id(1)
    w = WINDOW

    @pl.when(n == 0)
    def _():
        kk2[...] = jnp.zeros_like(kk2)
        for j in range(SW_KV_HEADS):
            vv2[j, :, 0:LANE] = jnp.zeros((4 * w, LANE), BF16)
            vv2[j, :, LANE:2 * LANE] = _ones_columns(w)

    for j in range(SW_KV_HEADS):
        kk2[j, 0:w, :] = kk2[j, w:2 * w, :]
        kk2[j, 2 * w:3 * w, :] = kk2[j, 3 * w:4 * w, :]
        vv2[j, 0:w, 0:LANE] = vv2[j, w:2 * w, 0:LANE]
        vv2[j, 2 * w:3 * w, 0:LANE] = vv2[j, 3 * w:4 * w, 0:LANE]
    cos, sa, sb = cos_ref[...], sa_ref[...], sb_ref[...]
    q = (_rope(qkv_ref[:, 0:D_MODEL], cos, sa, sb) * (SW_HEAD_DIM ** -0.5)).astype(BF16)
    k = _rope(qkv_ref[:, D_MODEL:D_MODEL + SW_KV], cos, sa, sb)
    v = qkv_ref[:, D_MODEL + SW_KV:D_MODEL + 2 * SW_KV]
    kc_ref[...] = k
    vc_ref[...] = v
    for j, (left, right) in enumerate(_head_pair_operands(k)):
        kk2[j, w:2 * w, :] = left.astype(BF16)
        kk2[j, 3 * w:4 * w, :] = right.astype(BF16)
    for j, (left, right) in enumerate(_head_pair_operands(v)):
        vv2[j, w:2 * w, 0:LANE] = left.astype(BF16)
        vv2[j, 3 * w:4 * w, 0:LANE] = right.astype(BF16)
    qi = lax.broadcasted_iota(jnp.int32, (w, 2 * w), 0) + w
    sj = lax.broadcasted_iota(jnp.int32, (w, 2 * w), 1)
    d = qi - sj
    mask = (d >= 0) & (d <= WINDOW) & ((sj >= w) | (n > 0))
    lane = lax.broadcasted_iota(jnp.int32, (w, LANE), 1)
    n_pb = D_MODEL // LANE
    scores = [_dot_nt(q[:, pb * LANE:(pb + 1) * LANE], kk2[pb // 2]) for pb in range(n_pb)]
    soft = [_pair_softmax(scores[pb], mask, (sinks_ref[2 * pb], sinks_ref[2 * pb + 1]), lane, w)
            for pb in range(n_pb)]
    outs = [_dot(soft[pb][0], vv2[pb // 2]) for pb in range(n_pb)]
    for pb in range(n_pb):
        o = outs[pb][:, 0:LANE] / (outs[pb][:, LANE:2 * LANE] + soft[pb][1])
        o_ref[:, pb * LANE:(pb + 1) * LANE] = o.astype(o_ref.dtype)


def _swa_prompt(sinks, qkv, cos, sa, sb, *, batch, seq):
    nb = seq // WINDOW
    row = lambda b, n: (b * nb + n, 0)
    tab = lambda b, n: (n, 0)
    per_b = lambda b, n: (b, 0, 0)
    return pl.pallas_call(
        _swa_prompt_kernel,
        out_shape=[jax.ShapeDtypeStruct((batch * seq, D_MODEL), BF16),
                   jax.ShapeDtypeStruct((batch, WINDOW, SW_KV), F32),
                   jax.ShapeDtypeStruct((batch, WINDOW, SW_KV), F32)],
        grid=(batch, nb),
        in_specs=[pl.BlockSpec(memory_space=pltpu.SMEM),
                  pl.BlockSpec((WINDOW, D_MODEL + 2 * SW_KV), row),
                  pl.BlockSpec((WINDOW, LANE), tab),
                  pl.BlockSpec((WINDOW, LANE), tab),
                  pl.BlockSpec((WINDOW, LANE), tab)],
        out_specs=[pl.BlockSpec((WINDOW, D_MODEL), row),
                   pl.BlockSpec((None, WINDOW, SW_KV), per_b),
                   pl.BlockSpec((None, WINDOW, SW_KV), per_b)],
        scratch_shapes=[pltpu.VMEM((SW_KV_HEADS, 4 * WINDOW, LANE), BF16),
                        pltpu.VMEM((SW_KV_HEADS, 4 * WINDOW, 2 * LANE), BF16)],
        compiler_params=_params(("arbitrary", "arbitrary")),
        name="swa_prompt",
    )(sinks, qkv, cos, sa, sb)


def _swa_sample_kernel(sinks_ref, qkv_ref, kbuf_ref, vbuf_ref, cos_ref, sa_ref, sb_ref,
                       o_ref, kc_ref, vc_ref, kk_s, vv_s, kk2, vv2, *, nseq, t_new):
    wb = WINDOW
    tp = SUBLANE

    @pl.when(pl.program_id(0) == 0)
    def _():
        kk2[...] = jnp.zeros_like(kk2)
        for g in range(nseq):
            for j in range(SW_KV_HEADS):
                vv2[g, j, :, 0:LANE] = jnp.zeros((4 * wb, LANE), BF16)
                vv2[g, j, :, LANE:2 * LANE] = _ones_columns(wb)

    cos, sa, sb = cos_ref[...], sa_ref[...], sb_ref[...]
    qs = []
    for g in range(nseq):
        q = (_rope(qkv_ref[g, :, 0:D_MODEL], cos, sa, sb) * (SW_HEAD_DIM ** -0.5)).astype(BF16)
        k_new = _rope(qkv_ref[g, :, D_MODEL:D_MODEL + SW_KV], cos, sa, sb)
        v_new = qkv_ref[g, :, D_MODEL + SW_KV:D_MODEL + 2 * SW_KV]
        k_old, v_old = kbuf_ref[g], vbuf_ref[g]
        kk_s[g, 0:wb, :] = k_old
        vv_s[g, 0:wb, :] = v_old
        kk_s[g, wb:wb + tp, :] = k_new
        vv_s[g, wb:wb + tp, :] = v_new
        kc_ref[g] = kk_s[g, pl.ds(t_new, wb), :]
        vc_ref[g] = vv_s[g, pl.ds(t_new, wb), :]
        for r0, n_r, kx, vx in ((0, wb, k_old, v_old), (wb, tp, k_new, v_new)):
            for j, (left, right) in enumerate(_head_pair_operands(kx)):
                kk2[g, j, r0:r0 + n_r, :] = left.astype(BF16)
                kk2[g, j, 2 * wb + r0:2 * wb + r0 + n_r, :] = right.astype(BF16)
            for j, (left, right) in enumerate(_head_pair_operands(vx)):
                vv2[g, j, r0:r0 + n_r, 0:LANE] = left.astype(BF16)
                vv2[g, j, 2 * wb + r0:2 * wb + r0 + n_r, 0:LANE] = right.astype(BF16)
        qs.append([jnp.concatenate([q[:, (2 * j) * LANE:(2 * j + 1) * LANE],
                                    q[:, (2 * j + 1) * LANE:(2 * j + 2) * LANE]], axis=0)
                   for j in range(SW_KV_HEADS)])
    rows = 2 * tp
    qt = lax.broadcasted_iota(jnp.int32, (rows, 2 * wb), 0) % tp
    sj = lax.broadcasted_iota(jnp.int32, (rows, 2 * wb), 1)
    d = qt + wb - sj
    mask = (d >= 0) & (d <= WINDOW)
    top = lax.broadcasted_iota(jnp.int32, (rows, 1), 0) < tp
    lane = lax.broadcasted_iota(jnp.int32, (rows, LANE), 1)
    items = [(g, j) for g in range(nseq) for j in range(SW_KV_HEADS)]
    scores = [_dot_nt(qs[g][j], kk2[g, j]) for g, j in items]
    soft = []
    for n, (g, j) in enumerate(items):
        sinks = [jnp.where(top, sinks_ref[4 * j + half], sinks_ref[4 * j + 2 + half]) for half in range(2)]
        soft.append(_pair_softmax(scores[n], mask, sinks, lane, wb))
    outs = [_dot(soft[n][0], vv2[g, j]) for n, (g, j) in enumerate(items)]
    for n, (g, j) in enumerate(items):
        o = outs[n][:, 0:LANE] / (outs[n][:, LANE:2 * LANE] + soft[n][1])
        o_ref[g, :, (2 * j) * LANE:(2 * j + 1) * LANE] = o[0:tp, :]
        o_ref[g, :, (2 * j + 1) * LANE:(2 * j + 2) * LANE] = o[tp:2 * tp, :]


def _swa_sample(sinks, qkv, kbuf, vbuf, cos, sa, sb, *, batch, nseq, t_new):
    per_b = lambda b: (b, 0, 0)
    tab = lambda b: (0, 0)
    kern = functools.partial(_swa_sample_kernel, nseq=nseq, t_new=t_new)
    return pl.pallas_call(
        kern,
        out_shape=[jax.ShapeDtypeStruct((batch, SUBLANE, D_MODEL), F32),
                   jax.ShapeDtypeStruct((batch, WINDOW, SW_KV), F32),
                   jax.ShapeDtypeStruct((batch, WINDOW, SW_KV), F32)],
        grid=(batch // nseq,),
        in_specs=[pl.BlockSpec(memory_space=pltpu.SMEM),
                  pl.BlockSpec((nseq, SUBLANE, D_MODEL + 2 * SW_KV), per_b),
                  pl.BlockSpec((nseq, WINDOW, SW_KV), per_b),
                  pl.BlockSpec((nseq, WINDOW, SW_KV), per_b),
                  pl.BlockSpec((SUBLANE, LANE), tab),
                  pl.BlockSpec((SUBLANE, LANE), tab),
                  pl.BlockSpec((SUBLANE, LANE), tab)],
        out_specs=[pl.BlockSpec((nseq, SUBLANE, D_MODEL), per_b),
                   pl.BlockSpec((nseq, WINDOW, SW_KV), per_b),
                   pl.BlockSpec((nseq, WINDOW, SW_KV), per_b)],
        scratch_shapes=[pltpu.VMEM((nseq, WINDOW + SUBLANE, SW_KV), F32),
                        pltpu.VMEM((nseq, WINDOW + SUBLANE, SW_KV), F32),
                        pltpu.VMEM((nseq, SW_KV_HEADS, 4 * WINDOW, LANE), BF16),
                        pltpu.VMEM((nseq, SW_KV_HEADS, 4 * WINDOW, 2 * LANE), BF16)],
        compiler_params=_params(("arbitrary",)),
        name="swa_sample",
    )(sinks, qkv, kbuf, vbuf, cos, sa, sb)


def _out_ffn_kernel(x_ref, g2_ref, post2_ref, ga_ref, gb_ref, odn_ref, osw_ref, wo_ref,
                    sh_ref, sc_ref, g_ref, pre_ref, post_ref, wg_ref, wu_ref, wd_ref, o_ref,
                    *, per_token, tiles_per_seq):
    rows = _row_groups(x_ref.shape[0])
    g2 = _mod_rows(g2_ref, per_token, tiles_per_seq)
    sh = _mod_rows(sh_ref, per_token, tiles_per_seq)
    sc = _mod_rows(sc_ref, per_token, tiles_per_seq)
    g = _mod_rows(g_ref, per_token, tiles_per_seq)
    pick = (lambda m, r: m[r]) if per_token else (lambda m, r: m)
    f32 = lambda ref, r: ref[r, :].astype(F32)
    ys = [(_sigmoid(f32(ga_ref, r)) * f32(odn_ref, r)
           + _sigmoid(f32(gb_ref, r)) * f32(osw_ref, r)).astype(BF16) for r in rows]
    ps = [_dot(y, wo_ref[...]) for y in ys]
    xs = [x_ref[r, :] + pick(g2, r) * _rms(p, post2_ref[...]) for p, r in zip(ps, rows)]
    outs = _ffn_stages(xs, rows, pick, sh, sc, g, pre_ref[...], post_ref[...], wg_ref, wu_ref, wd_ref)
    for o, r in zip(outs, rows):
        o_ref[r, :] = o


def _out_ffn(x, mod, mod_block, seq_len, post2, gates, o_dn, o_sw, w_out, pre, post, wg, wu, wd):
    n = x.shape[0]
    tm = min(512, n)
    dff = wg.shape[1]
    kern = functools.partial(_out_ffn_kernel, per_token=mod_block is None, tiles_per_seq=max(seq_len // tm, 1))
    row = lambda i: (i, 0)
    return pl.pallas_call(
        kern,
        out_shape=jax.ShapeDtypeStruct((n, D_MODEL), F32),
        grid=(n // tm,),
        in_specs=[pl.BlockSpec((tm, D_MODEL), row)]
        + _mod_specs(mod_block, tm, (5,))
        + [_const_spec((1, D_MODEL)),
           pl.BlockSpec((tm, D_MODEL), lambda i: (i, 0)),
           pl.BlockSpec((tm, D_MODEL), lambda i: (i, 1)),
           pl.BlockSpec((tm, D_MODEL), row),
           pl.BlockSpec((tm, D_MODEL), row),
           _const_spec((D_MODEL, D_MODEL))]
        + _mod_specs(mod_block, tm, (6, 7, 8))
        + [_const_spec((1, D_MODEL)), _const_spec((1, D_MODEL)),
           _const_spec((D_MODEL, dff)), _const_spec((D_MODEL, dff)), _const_spec((dff, D_MODEL))],
        out_specs=pl.BlockSpec((tm, D_MODEL), row),
        compiler_params=_params(("arbitrary",)),
        name="out_ffn",
    )(x, mod, post2, gates, gates, o_dn, o_sw, w_out, mod, mod, mod, pre, post, wg, wu, wd)


def _rope_tables(pos):
    half = ROT_DIM // 2
    inv_freq = ROPE_THETA ** (-jnp.arange(half, dtype=F32) * (2.0 / ROT_DIM))
    ang = pos.astype(F32)[:, None] * inv_freq[None, :]
    cos, sin = jnp.cos(ang), jnp.sin(ang)
    n = pos.shape[0]
    rest = SW_HEAD_DIM - ROT_DIM
    c64 = jnp.concatenate([cos, cos, jnp.ones((n, rest), F32)], axis=1)
    a64 = jnp.concatenate([-sin, jnp.zeros((n, half + rest), F32)], axis=1)
    b64 = jnp.concatenate([jnp.zeros((n, half), F32), sin, jnp.zeros((n, rest), F32)], axis=1)
    rep = LANE // SW_HEAD_DIM
    return tuple(jnp.tile(t, (1, rep)) for t in (c64, a64, b64))


def _pad_rows(a, rows):
    return jnp.pad(a, ((0, rows - a.shape[0]),) + ((0, 0),) * (a.ndim - 1))


def _layer(x, mod, mod_block, seq, rows, wts, conv0, s0, dn_fn, swa_fn):
    def per_seq(a):
        a = a.reshape(-1, seq, a.shape[-1])
        return a if rows == seq else jnp.pad(a, ((0, 0), (0, rows - seq), (0, 0)))

    def flat(a):
        return a[:, :seq].reshape(-1, a.shape[-1])

    x = _ffn(x, mod, 0, mod_block, seq, wts["pre1"], wts["post1"], wts["wg1"], wts["wu1"], wts["wd1"])
    u, z, qkv, gates, ba = _proj(x, mod, mod_block, seq, wts["pre2"], *wts["w_in"])
    o_dn, conv_new, s_new = dn_fn(per_seq(u), per_seq(z), per_seq(ba), wts["conv_w"], wts["gparam"],
                                  wts["dn_norm"], conv0, s0)
    o_sw, k_new, v_new = swa_fn(per_seq(qkv))
    x = _out_ffn(x, mod, mod_block, seq, wts["post2"], gates, flat(o_dn), flat(o_sw), wts["w_out"],
                 wts["pre3"], wts["post3"], wts["wg2"], wts["wu2"], wts["wd2"])
    return x, (k_new, v_new, conv_new, s_new)


def kernel(x_prompt, x_sample, cache_swa_k, cache_swa_v, state_conv, state_delta, c_prompt, c_sample,
           w_ada, b_ada, ffn1_norm_pre, ffn1_norm_post, ffn1_w_gate, ffn1_w_up, ffn1_w_down,
           mix_norm_pre, mix_norm_post, w_in, conv_w, a_log, dt_bias, dn_norm, sinks, w_out,
           ffn2_norm_pre, ffn2_norm_post, ffn2_w_gate, ffn2_w_up, ffn2_w_down):
    depth = w_ada.shape[0]
    bp, tp, _ = x_prompt.shape
    bs, ts, _ = x_sample.shape
    assert tp % 512 == 0 and 1 <= ts <= SUBLANE and bs % SAMPLE_SEQS == 0 and (bs * ts) % SUBLANE == 0

    cos_p, sa_p, sb_p = _rope_tables(jnp.arange(tp))
    cos_s, sa_s, sb_s = (_pad_rows(t, SUBLANE) for t in _rope_tables(PAST_LEN + jnp.arange(ts)))

    xp = x_prompt.reshape(bp * tp, D_MODEL)
    xs = x_sample.reshape(bs * ts, D_MODEL)
    c_all = jnp.concatenate([jnp.repeat(c_sample, ts, axis=0), _pad_rows(c_prompt, SUBLANE)], axis=0)
    prompt_mod_block = bs * ts // SUBLANE

    outs_p, outs_s = [], []
    for l in range(depth):
        wi = w_in[l]
        n_ba = 2 * DN_HEADS
        w_in_a = wi[:, :P_QKV].astype(BF16)
        w_in_b = wi[:, P_QKV + n_ba:].astype(BF16)
        w_in_c = jnp.pad(wi[:, P_QKV:P_QKV + n_ba], ((0, 0), (0, LANE - n_ba))).astype(BF16)
        gparam = jnp.zeros((SUBLANE, LANE), F32)
        gparam = gparam.at[0, DN_HEADS:2 * DN_HEADS].set(dt_bias[l]).at[1, DN_HEADS:2 * DN_HEADS].set(a_log[l])
        wts = dict(
            pre1=ffn1_norm_pre[l][None], post1=ffn1_norm_post[l][None],
            wg1=ffn1_w_gate[l].astype(BF16), wu1=ffn1_w_up[l].astype(BF16), wd1=ffn1_w_down[l].astype(BF16),
            pre2=mix_norm_pre[l][None], post2=mix_norm_post[l][None], w_in=(w_in_a, w_in_b, w_in_c),
            conv_w=_pad_rows(conv_w[l], SUBLANE), gparam=gparam, dn_norm=dn_norm[l][None],
            w_out=w_out[l].astype(BF16),
            pre3=ffn2_norm_pre[l][None], post3=ffn2_norm_post[l][None],
            wg2=ffn2_w_gate[l].astype(BF16), wu2=ffn2_w_up[l].astype(BF16), wd2=ffn2_w_down[l].astype(BF16),
        )
        b_ada_l = b_ada[l][None]
        mod = _ada(c_all, w_ada[l], b_ada_l)
        sink_l = sinks[l]

        conv0_p = jnp.zeros((bp, SUBLANE, CONV_CH), F32)
        s0_p = jnp.zeros((bp, DN_HEADS, DN_DK, DN_DV), F32)
        dn_p = functools.partial(_deltanet, nseq=bp, tm=DN_TILE, chunk=DN_CHUNK, valid=DN_TILE, o_dtype=BF16)

        def swa_p(qkv):
            o, k, v = _swa_prompt(sink_l, qkv.reshape(bp * tp, -1), cos_p, sa_p, sb_p, batch=bp, seq=tp)
            return o.reshape(bp, tp, D_MODEL), k, v

        xp, st_p = _layer(xp, mod, prompt_mod_block, tp, tp, wts, conv0_p, s0_p, dn_p, swa_p)

        conv0_s = jnp.pad(state_conv[l], ((0, 0), (SUBLANE - CONV_W + 1, 0), (0, 0)))
        kbuf = cache_swa_k[l].reshape(bs, WINDOW, SW_KV)
        vbuf = cache_swa_v[l].reshape(bs, WINDOW, SW_KV)
        dn_s = functools.partial(_deltanet, nseq=SAMPLE_SEQS, tm=SUBLANE, chunk=SUBLANE, valid=ts, o_dtype=F32)

        swa_s = lambda qkv: _swa_sample(sink_l, qkv, kbuf, vbuf, cos_s, sa_s, sb_s,
                                        batch=bs, nseq=SAMPLE_SEQS, t_new=ts)
        xs, st_s = _layer(xs, mod, None, ts, SUBLANE, wts, conv0_s, state_delta[l], dn_s, swa_s)
        outs_p.append(st_p)
        outs_s.append(st_s)

    def stack(outs, i):
        return jnp.stack([o[i] for o in outs])

    def kv5(a, batch):
        return a.reshape(depth, batch, WINDOW, SW_KV_HEADS, SW_HEAD_DIM)

    tail = slice(SUBLANE - CONV_W + 1, SUBLANE)
    y_p = xp.reshape(bp, tp, D_MODEL)
    y_s = xs.reshape(bs, ts, D_MODEL)
    return (y_p, y_s,
            kv5(stack(outs_p, 0), bp), kv5(stack(outs_p, 1), bp),
            stack(outs_p, 2)[:, :, tail], stack(outs_p, 3),
            kv5(stack(outs_s, 0), bs), kv5(stack(outs_s, 1), bs),
            stack(outs_s, 2)[:, :, tail], stack(outs_s, 3))
```

```python
import functools

import jax
import jax.numpy as jnp
from jax import lax
from jax.experimental import pallas as pl
from jax.experimental.pallas import tpu as pltpu

F32, BF16 = jnp.float32, jnp.bfloat16

D_MODEL = 1024
DN_HEADS, DN_DK, DN_DV = 8, 128, 128
CONV_W = 4
CONV_CH = 3 * D_MODEL
DN_CHUNK = 64
SW_HEAD_DIM, SW_HEADS, SW_KV_HEADS, SW_GROUP = 64, 16, 4, 4
SW_KV = SW_KV_HEADS * SW_HEAD_DIM
WINDOW = 128
ROT_DIM = 16
ROPE_THETA = 500000.0
PAST_LEN = 8192
EPS = 1e-6
NEG_INF = -1e30

LANE = 128
SUBLANE = 8
VMEM_LIMIT = 56 * 1024 * 1024

P_U, P_Z, P_QKV, P_G, P_BA = 0, 3072, 4096, 5632, 7680
P_TOTAL = 7808
HI = lax.Precision.HIGHEST

DN_TILE = 128
DN_GROUP = 4
SAMPLE_SEQS = 8
FFN_SUBTILES = 2
PROJ_FUSED_TILE = 512


def _dot(a, b):
    return jnp.dot(a, b, preferred_element_type=F32)


def _dot_nt(a, b):
    return lax.dot_general(a, b, (((1,), (1,)), ((), ())), preferred_element_type=F32)


def _dot_tn(a, b):
    return lax.dot_general(a, b, (((0,), (0,)), ((), ())), preferred_element_type=F32)


def _dot_hi(a, b):
    return jnp.dot(a, b, preferred_element_type=F32, precision=HI)


def _mm_bf(a, b):
    return _dot(a.astype(BF16), b.astype(BF16))


def _sigmoid(x):
    return 0.5 * jnp.tanh(0.5 * x) + 0.5


def _silu(x):
    return x * _sigmoid(x)


def _rms(x, gain):
    ms = jnp.mean(x * x, axis=-1, keepdims=True)
    return x * lax.rsqrt(ms + EPS) * gain


def _mod_rows(ref, per_token, tiles_per_seq):
    if per_token:
        return ref[...]
    b = pl.program_id(0) // tiles_per_seq
    return ref[pl.ds(b, 1), :]


def _const_spec(shape):
    return pl.BlockSpec(shape, lambda *_: (0,) * len(shape), pipeline_mode=pl.Buffered(1))


def _params(sem):
    return pltpu.CompilerParams(dimension_semantics=sem, vmem_limit_bytes=VMEM_LIMIT)


def _ada_kernel(c_ref, w_ref, b_ref, o_ref):
    h = _silu(c_ref[...]).astype(BF16)
    o_ref[...] = _dot(h, w_ref[...].astype(BF16)) + b_ref[...]


def _ada(c, w, b):
    rows, n = c.shape[0], w.shape[1]
    return pl.pallas_call(
        _ada_kernel,
        out_shape=jax.ShapeDtypeStruct((rows, n), F32),
        grid=(n // D_MODEL,),
        in_specs=[
            pl.BlockSpec((rows, D_MODEL), lambda j: (0, 0)),
            pl.BlockSpec((D_MODEL, D_MODEL), lambda j: (0, j)),
            pl.BlockSpec((1, D_MODEL), lambda j: (0, j)),
        ],
        out_specs=pl.BlockSpec((rows, D_MODEL), lambda j: (0, j)),
        compiler_params=_params(("arbitrary",)),
        name="ada",
    )(c, w, b)


def _mod_specs(mod_block, tm, cols):
    if mod_block is None:
        return [pl.BlockSpec((tm, D_MODEL), functools.partial(lambda i, c: (i, c), c=c)) for c in cols]
    return [pl.BlockSpec((SUBLANE, D_MODEL), functools.partial(lambda i, c: (mod_block, c), c=c)) for c in cols]


def _ffn_stages(xs, rows, pick, sh, sc, g, pre, post, wg_ref, wu_ref, wd_ref):
    hs = [(_rms(x, pre) * (1.0 + pick(sc, r)) + pick(sh, r)).astype(BF16) for x, r in zip(xs, rows)]
    gates = [_dot(h, wg_ref[...]) for h in hs]
    ups = [_dot(h, wu_ref[...]) for h in hs]
    acts = [(_silu(a) * b).astype(BF16) for a, b in zip(gates, ups)]
    ys = [_dot(a, wd_ref[...]) for a in acts]
    return [x + 0.5 * pick(g, r) * _rms(y, post) for x, y, r in zip(xs, ys, rows)]


def _row_groups(tm):
    return [slice(i * tm // FFN_SUBTILES, (i + 1) * tm // FFN_SUBTILES) for i in range(FFN_SUBTILES)]


def _ffn_kernel(x_ref, sh_ref, sc_ref, g_ref, pre_ref, post_ref, wg_ref, wu_ref, wd_ref, o_ref,
                *, per_token, tiles_per_seq):
    rows = _row_groups(x_ref.shape[0])
    sh = _mod_rows(sh_ref, per_token, tiles_per_seq)
    sc = _mod_rows(sc_ref, per_token, tiles_per_seq)
    g = _mod_rows(g_ref, per_token, tiles_per_seq)
    pick = (lambda m, r: m[r]) if per_token else (lambda m, r: m)
    xs = [x_ref[r, :] for r in rows]
    outs = _ffn_stages(xs, rows, pick, sh, sc, g, pre_ref[...], post_ref[...], wg_ref, wu_ref, wd_ref)
    for o, r in zip(outs, rows):
        o_ref[r, :] = o


def _ffn(x, mod, col0, mod_block, seq_len, pre, post, wg, wu, wd):
    n = x.shape[0]
    tm = min(512, n)
    dff = wg.shape[1]
    kern = functools.partial(_ffn_kernel, per_token=mod_block is None, tiles_per_seq=max(seq_len // tm, 1))
    row = lambda i: (i, 0)
    return pl.pallas_call(
        kern,
        out_shape=jax.ShapeDtypeStruct((n, D_MODEL), F32),
        grid=(n // tm,),
        in_specs=[pl.BlockSpec((tm, D_MODEL), row)]
        + _mod_specs(mod_block, tm, (col0, col0 + 1, col0 + 2))
        + [_const_spec((1, D_MODEL)), _const_spec((1, D_MODEL)),
           _const_spec((D_MODEL, dff)), _const_spec((D_MODEL, dff)), _const_spec((dff, D_MODEL))],
        out_specs=pl.BlockSpec((tm, D_MODEL), row),
        compiler_params=_params(("arbitrary",)),
        name="ffn",
    )(x, mod, mod, mod, pre, post, wg, wu, wd)


def _proj_kernel(x_ref, sh_ref, sc_ref, pre_ref, wa_ref, wb_ref, wc_ref, u_ref, z_ref, qkv_ref, g_ref, ba_ref,
                 *, per_token, tiles_per_seq):
    sh = _mod_rows(sh_ref, per_token, tiles_per_seq)
    sc = _mod_rows(sc_ref, per_token, tiles_per_seq)
    h = (_rms(x_ref[...], pre_ref[...]) * (1.0 + sc) + sh).astype(BF16)
    u_ref[...] = _dot(h, wa_ref[:, 0:P_Z])
    z_ref[...] = _dot(h, wa_ref[:, P_Z:P_QKV])
    qkv_ref[...] = _dot(h, wb_ref[:, 0:P_G - P_QKV])
    g_ref[...] = _dot(h, wb_ref[:, P_G - P_QKV:P_BA - P_QKV]).astype(g_ref.dtype)
    ba_ref[...] = _dot(h, wc_ref[...])


def _proj(x, mod, mod_block, seq_len, pre, wa, wb, wc):
    n = x.shape[0]
    tm = min(512, n)
    kern = functools.partial(_proj_kernel, per_token=mod_block is None, tiles_per_seq=max(seq_len // tm, 1))
    row = lambda i: (i, 0)
    widths = (P_Z - P_U, P_QKV - P_Z, P_G - P_QKV, P_BA - P_G, P_TOTAL - P_BA)
    return pl.pallas_call(
        kern,
        out_shape=[jax.ShapeDtypeStruct((n, wd), BF16 if wd == P_BA - P_G else F32) for wd in widths],
        grid=(n // tm,),
        in_specs=[pl.BlockSpec((tm, D_MODEL), row)]
        + _mod_specs(mod_block, tm, (3, 4))
        + [_const_spec((1, D_MODEL)), _const_spec(wa.shape), _const_spec(wb.shape), _const_spec(wc.shape)],
        out_specs=[pl.BlockSpec((tm, wd), row) for wd in widths],
        compiler_params=_params(("arbitrary",)),
        name="proj",
    )(x, mod, mod, pre, wa, wb, wc)


def _conv_norm(u_cols, ubuf_cb, cw_ref, cols):
    rows = u_cols.shape[0]
    ubuf_cb[SUBLANE:SUBLANE + rows, :] = u_cols
    acc = ubuf_cb[pl.ds(SUBLANE - CONV_W + 1, rows), :] * cw_ref[0:1, cols]
    for i in range(1, CONV_W):
        acc = acc + ubuf_cb[pl.ds(SUBLANE - CONV_W + 1 + i, rows), :] * cw_ref[i:i + 1, cols]
    return _silu(acc)


def _store_qkv(a, cb, q_dst, k_dst, v_dst, rows):
    sec, off = divmod(cb * LANE, D_MODEL)
    dst = slice(off, off + LANE)
    if sec == 2:
        v_dst[rows, dst] = a
    else:
        an = a * lax.rsqrt(jnp.sum(a * a, axis=-1, keepdims=True) + EPS)
        if sec == 0:
            q_dst[rows, dst] = an * (DN_DK ** -0.5)
        else:
            k_dst[rows, dst] = an


def _proj_prompt_kernel(x_ref, sh_ref, sc_ref, pre_ref, wa_ref, wb_ref, wc_ref, cw_ref, cos_ref, sa_ref, sb_ref,
                        q_ref, k_ref, v_ref, z_ref, qkv_ref, g_ref, ba_ref, tail_ref, ubuf, *, tiles_per_seq):
    tm = x_ref.shape[0]
    n_cb = CONV_CH // LANE
    all_rows = slice(0, tm)

    @pl.when(pl.program_id(0) % tiles_per_seq == 0)
    def _():
        ubuf[:, 0:SUBLANE, :] = jnp.zeros((n_cb, SUBLANE, LANE), F32)

    sh = _mod_rows(sh_ref, False, tiles_per_seq)
    sc = _mod_rows(sc_ref, False, tiles_per_seq)
    h = (_rms(x_ref[...], pre_ref[...]) * (1.0 + sc) + sh).astype(BF16)
    u = _dot(h, wa_ref[:, 0:P_Z])
    z_ref[...] = _dot(h, wa_ref[:, P_Z:P_QKV])
    s = _dot(h, wb_ref[:, 0:P_G - P_QKV])
    g_ref[...] = _dot(h, wb_ref[:, P_G - P_QKV:P_BA - P_QKV]).astype(g_ref.dtype)
    ba_ref[...] = _dot(h, wc_ref[...])
    for cb in range(n_cb):
        cols = slice(cb * LANE, (cb + 1) * LANE)
        a = _conv_norm(u[:, cols], ubuf.at[cb], cw_ref, cols)
        _store_qkv(a, cb, q_ref, k_ref, v_ref, all_rows)
        tail = ubuf[cb, pl.ds(tm, SUBLANE), :]
        ubuf[cb, 0:SUBLANE, :] = tail
        tail_ref[:, cols] = tail
    cos, sa, sb = cos_ref[...], sa_ref[...], sb_ref[...]
    qkv_ref[:, 0:D_MODEL] = _rope(s[:, 0:D_MODEL], cos, sa, sb) * (SW_HEAD_DIM ** -0.5)
    qkv_ref[:, D_MODEL:D_MODEL + SW_KV] = _rope(s[:, D_MODEL:D_MODEL + SW_KV], cos, sa, sb)
    qkv_ref[:, D_MODEL + SW_KV:D_MODEL + 2 * SW_KV] = s[:, D_MODEL + SW_KV:D_MODEL + 2 * SW_KV]


def _proj_prompt(x, mod, mod_block, seq_len, pre, wa, wb, wc, cw, cos, sa, sb):
    n = x.shape[0]
    tm = PROJ_FUSED_TILE
    tps = seq_len // tm
    kern = functools.partial(_proj_prompt_kernel, tiles_per_seq=tps)
    row = lambda i: (i, 0)
    tab = lambda i: (i % tps, 0)
    widths = (D_MODEL, D_MODEL, D_MODEL, P_QKV - P_Z, P_G - P_QKV, P_BA - P_G, P_TOTAL - P_BA)
    dtypes = (F32, F32, F32, F32, F32, BF16, F32)
    return pl.pallas_call(
        kern,
        out_shape=[jax.ShapeDtypeStruct((n, wd), dt) for wd, dt in zip(widths, dtypes)]
        + [jax.ShapeDtypeStruct((n // seq_len, SUBLANE, CONV_CH), F32)],
        grid=(n // tm,),
        in_specs=[pl.BlockSpec((tm, D_MODEL), row)]
        + _mod_specs(mod_block, tm, (3, 4))
        + [_const_spec((1, D_MODEL)), _const_spec(wa.shape), _const_spec(wb.shape), _const_spec(wc.shape),
           _const_spec(cw.shape),
           pl.BlockSpec((tm, LANE), tab), pl.BlockSpec((tm, LANE), tab), pl.BlockSpec((tm, LANE), tab)],
        out_specs=[pl.BlockSpec((tm, wd), row) for wd in widths]
        + [pl.BlockSpec((None, SUBLANE, CONV_CH), lambda i: (i // tps, 0, 0))],
        scratch_shapes=[pltpu.VMEM((CONV_CH // LANE, SUBLANE + tm, LANE), F32)],
        compiler_params=_params(("arbitrary",)),
        name="proj_prompt",
    )(x, mod, mod, pre, wa, wb, wc, cw, cos, sa, sb)


def _tri_inv_all(lows, c):
    ii = lax.broadcasted_iota(jnp.int32, (c, c), 0)
    jj = lax.broadcasted_iota(jnp.int32, (c, c), 1)
    eye = jnp.where(ii == jj, 1.0, 0.0).astype(F32)
    xs = [eye - jnp.where(ii // 2 == jj // 2, low, 0.0) for low in lows]
    b = 2
    while b < c:
        join = (ii // (2 * b) == jj // (2 * b)) & (ii // b != jj // b)
        ys = [_mm_bf(x, jnp.where(join, low, 0.0)) for x, low in zip(xs, lows)]
        xs = [x - _mm_bf(y, x) for x, y in zip(xs, ys)]
        b *= 2
    return xs


def _dn_kernel(*refs, nseq, tm, chunk, valid, fused):
    if fused:
        (q_ref, k_ref, v_ref, z_ref, ba_ref, gp_ref, dnn_ref, s0_ref, o_ref, so_ref,
         q_s, k_s, v_s, bg_s, u_s, w_s, qe_s, kd_s, a_s, eg_s) = refs
    else:
        (u_ref, z_ref, ba_ref, cw_ref, gp_ref, dnn_ref, conv0_ref, s0_ref, o_ref, convo_ref, so_ref,
         ubuf, q_s, k_s, v_s, bg_s, u_s, w_s, qe_s, kd_s, a_s, eg_s) = refs
    t = pl.program_id(1)
    rows_all = nseq * tm
    n_cb = CONV_CH // LANE

    @pl.when(t == 0)
    def _():
        so_ref[...] = s0_ref[...]
        if not fused:
            for g in range(nseq):
                for cb in range(n_cb):
                    ubuf[g, cb, 0:SUBLANE, :] = conv0_ref[g, :, cb * LANE:(cb + 1) * LANE]

    for g in range(nseq):
        rws = slice(g * tm, (g + 1) * tm)
        if fused:
            q_s[rws, :] = q_ref[g]
            k_s[rws, :] = k_ref[g]
            v_s[rws, :] = v_ref[g]
            continue
        for cb in range(n_cb):
            cols = slice(cb * LANE, (cb + 1) * LANE)
            a = _conv_norm(u_ref[g, :, cols], ubuf.at[g, cb], cw_ref, cols)
            _store_qkv(a, cb, q_s, k_s, v_s, rws)
            tail = ubuf[g, cb, pl.ds(valid, SUBLANE), :]
            ubuf[g, cb, 0:SUBLANE, :] = tail
            convo_ref[g, :, cols] = tail

    lane = lax.broadcasted_iota(jnp.int32, (tm, LANE), 1)
    rowi = lax.broadcasted_iota(jnp.int32, (tm, LANE), 0)
    for g in range(nseq):
        ba = ba_ref[g]
        xg = ba + gp_ref[0:1, :]
        softplus = jnp.maximum(xg, 0.0) + jnp.log1p(jnp.exp(-jnp.abs(xg)))
        gdec = -jnp.exp(gp_ref[1:2, :]) * softplus
        bg = jnp.where(lane < DN_HEADS, _sigmoid(ba), gdec)
        bg_s[g * tm:(g + 1) * tm, :] = jnp.where(rowi < valid, bg, 0.0)

    c = chunk
    ii = lax.broadcasted_iota(jnp.int32, (c, c), 0)
    jj = lax.broadcasted_iota(jnp.int32, (c, c), 1)
    incl = ii >= jj
    strict = ii > jj
    tril = jnp.where(incl, 1.0, 0.0).astype(F32)
    sel = jnp.where(lax.broadcasted_iota(jnp.int32, (SUBLANE, LANE), 1)
                    == lax.broadcasted_iota(jnp.int32, (SUBLANE, LANE), 0) + DN_HEADS, 1.0, 0.0).astype(F32)
    dnn = dnn_ref[...]

    heads = range(DN_HEADS)
    hcols = [slice(h * DN_DK, (h + 1) * DN_DK) for h in heads]
    acols = [slice(h * c, (h + 1) * c) for h in heads]
    n_chunks = rows_all // c
    chunks_per_seq = tm // c
    gpc = DN_GROUP if n_chunks % DN_GROUP == 0 else 1

    def chunk_rows(ci):
        return pl.ds(ci * c, c) if isinstance(ci, int) else pl.ds(pl.multiple_of(ci * c, c), c)

    def prep(cis):
        n_c = len(cis)
        rows = [chunk_rows(ci) for ci in cis]
        pairs = [(j, h) for j in range(n_c) for h in heads]
        bgc = [bg_s[r, :] for r in rows]
        gcum = [_dot_hi(tril, b) for b in bgc]
        gcum_t = [lax.dot_general(sel, g, (((1,), (1,)), ((), ())), preferred_element_type=F32, precision=HI)
                  for g in gcum]
        for j in range(n_c):
            eg_s[cis[j]] = jnp.broadcast_to(jnp.exp(gcum_t[j][:, c - 1:c]), (DN_HEADS, DN_DV))
        spread = lambda col: jnp.broadcast_to(col, (c, DN_DK))
        beta = [spread(bgc[j][:, h:h + 1]) for j, h in pairs]
        gc = [spread(gcum[j][:, DN_HEADS + h:DN_HEADS + h + 1]) for j, h in pairs]
        decay = [jnp.where(incl, jnp.exp(jnp.where(incl, gc[p][:, :c] - gcum_t[j][h:h + 1, :], 0.0)), 0.0)
                 for p, (j, h) in enumerate(pairs)]
        q = [q_s[rows[j], hcols[h]] for j, h in pairs]
        k = [k_s[rows[j], hcols[h]] for j, h in pairs]
        npair = range(len(pairs))
        kb = [k[p] * beta[p] for p in npair]
        k_bf = [k[p].astype(BF16) for p in npair]
        akk = [_dot_nt(kb[p].astype(BF16), k_bf[p]) for p in npair]
        aqk = [_dot_nt(q[p].astype(BF16), k_bf[p]) for p in npair]
        tinv = _tri_inv_all([jnp.where(strict, akk[p] * decay[p], 0.0) for p in npair], c)
        egc = [jnp.exp(gc[p]) for p in npair]
        rhs = [jnp.concatenate([v_s[rows[j], hcols[h]] * beta[p], kb[p] * egc[p]], axis=1)
               for p, (j, h) in enumerate(pairs)]
        uw = [_mm_bf(tinv[p], rhs[p]) for p in npair]
        sdt = w_s.dtype
        for p, (j, h) in enumerate(pairs):
            u_s[rows[j], hcols[h]] = uw[p][:, :DN_DV]
            w_s[rows[j], hcols[h]] = uw[p][:, DN_DV:].astype(sdt)
            qe_s[rows[j], hcols[h]] = (q[p] * egc[p]).astype(sdt)
            kd_s[rows[j], hcols[h]] = (k[p] * jnp.exp(gc[p][c - 1:c, :] - gc[p])).astype(sdt)
            a_s[rows[j], acols[h]] = jnp.where(incl, aqk[p] * decay[p], 0.0).astype(sdt)

    def state(cpos):
        trip = [(g, h) for g in range(nseq) for h in heads]
        rows = [chunk_rows(g * chunks_per_seq + cpos) for g in range(nseq)]
        local = chunk_rows(cpos)
        egb = [eg_s[g * chunks_per_seq + cpos] for g in range(nseq)]
        s_old = [so_ref[g, h] for g, h in trip]
        s_bf = [s.astype(BF16) for s in s_old]
        ws = [_dot(w_s[rows[g], hcols[h]].astype(BF16), s_bf[n]) for n, (g, h) in enumerate(trip)]
        qs = [_dot(qe_s[rows[g], hcols[h]].astype(BF16), s_bf[n]) for n, (g, h) in enumerate(trip)]
        vn_bf = [(u_s[rows[g], hcols[h]] - ws[n]).astype(BF16) for n, (g, h) in enumerate(trip)]
        av = [_dot(a_s[rows[g], acols[h]].astype(BF16), vn_bf[n]) for n, (g, h) in enumerate(trip)]
        kv = [_dot_tn(kd_s[rows[g], hcols[h]].astype(BF16), vn_bf[n]) for n, (g, h) in enumerate(trip)]
        for n, (g, h) in enumerate(trip):
            so_ref[g, h] = s_old[n] * egb[g][h:h + 1, :] + kv[n]
            o = _rms(qs[n] + av[n], dnn) * _silu(z_ref[g, local, hcols[h]])
            o_ref[g, local, hcols[h]] = o.astype(o_ref.dtype)

    for gi in range(n_chunks // gpc):
        prep([gi * gpc + j for j in range(gpc)])
    if chunks_per_seq == 1:
        state(0)
    else:
        def state_body(cpos, carry):
            state(cpos)
            return carry
        lax.fori_loop(0, chunks_per_seq, state_body, 0)


def _dn_scratch(nseq, tm, chunk):
    rows = nseq * tm
    sdt = BF16 if chunk % 16 == 0 else F32
    return [pltpu.VMEM((rows, D_MODEL), F32),
            pltpu.VMEM((rows, D_MODEL), F32),
            pltpu.VMEM((rows, D_MODEL), F32),
            pltpu.VMEM((rows, LANE), F32),
            pltpu.VMEM((rows, D_MODEL), F32),
            pltpu.VMEM((rows, D_MODEL), sdt),
            pltpu.VMEM((rows, D_MODEL), sdt),
            pltpu.VMEM((rows, D_MODEL), sdt),
            pltpu.VMEM((rows, DN_HEADS * chunk), sdt),
            pltpu.VMEM((rows // chunk, DN_HEADS, DN_DV), F32)]


def _deltanet(u, z, ba, cw, gp, dnn, conv0, s0, *, nseq, tm, chunk, valid, o_dtype):
    batch, seq, _ = u.shape
    nt = seq // tm
    row = lambda b, t: (b, t, 0)
    per_b3 = lambda b, t: (b, 0, 0)
    per_b4 = lambda b, t: (b, 0, 0, 0)
    kern = functools.partial(_dn_kernel, nseq=nseq, tm=tm, chunk=chunk, valid=valid, fused=False)
    return pl.pallas_call(
        kern,
        out_shape=[jax.ShapeDtypeStruct((batch, seq, D_MODEL), o_dtype),
                   jax.ShapeDtypeStruct((batch, SUBLANE, CONV_CH), F32),
                   jax.ShapeDtypeStruct((batch, DN_HEADS, DN_DK, DN_DV), F32)],
        grid=(batch // nseq, nt),
        in_specs=[pl.BlockSpec((nseq, tm, CONV_CH), row),
                  pl.BlockSpec((nseq, tm, D_MODEL), row),
                  pl.BlockSpec((nseq, tm, LANE), row),
                  pl.BlockSpec((SUBLANE, CONV_CH), lambda b, t: (0, 0)),
                  pl.BlockSpec((SUBLANE, LANE), lambda b, t: (0, 0)),
                  pl.BlockSpec((1, DN_DV), lambda b, t: (0, 0)),
                  pl.BlockSpec((nseq, SUBLANE, CONV_CH), per_b3),
                  pl.BlockSpec((nseq, DN_HEADS, DN_DK, DN_DV), per_b4)],
        out_specs=[pl.BlockSpec((nseq, tm, D_MODEL), row),
                   pl.BlockSpec((nseq, SUBLANE, CONV_CH), per_b3),
                   pl.BlockSpec((nseq, DN_HEADS, DN_DK, DN_DV), per_b4)],
        scratch_shapes=[pltpu.VMEM((nseq, CONV_CH // LANE, SUBLANE + tm, LANE), F32)] + _dn_scratch(nseq, tm, chunk),
        compiler_params=_params(("arbitrary", "arbitrary")),
        name="deltanet",
    )(u, z, ba, cw, gp, dnn, conv0, s0)


def _deltanet_fused(q, k, v, z, ba, gp, dnn, s0, *, nseq, tm, chunk, o_dtype):
    batch, seq, _ = q.shape
    nt = seq // tm
    row = lambda b, t: (b, t, 0)
    per_b4 = lambda b, t: (b, 0, 0, 0)
    kern = functools.partial(_dn_kernel, nseq=nseq, tm=tm, chunk=chunk, valid=tm, fused=True)
    wide = pl.BlockSpec((nseq, tm, D_MODEL), row)
    return pl.pallas_call(
        kern,
        out_shape=[jax.ShapeDtypeStruct((batch, seq, D_MODEL), o_dtype),
                   jax.ShapeDtypeStruct((batch, DN_HEADS, DN_DK, DN_DV), F32)],
        grid=(batch // nseq, nt),
        in_specs=[wide, wide, wide, wide,
                  pl.BlockSpec((nseq, tm, LANE), row),
                  pl.BlockSpec((SUBLANE, LANE), lambda b, t: (0, 0)),
                  pl.BlockSpec((1, DN_DV), lambda b, t: (0, 0)),
                  pl.BlockSpec((nseq, DN_HEADS, DN_DK, DN_DV), per_b4)],
        out_specs=[wide, pl.BlockSpec((nseq, DN_HEADS, DN_DK, DN_DV), per_b4)],
        scratch_shapes=_dn_scratch(nseq, tm, chunk),
        compiler_params=_params(("arbitrary", "arbitrary")),
        name="deltanet_fused",
    )(q, k, v, z, ba, gp, dnn, s0)


def _rope(x, cos, sa, sb):
    cols = []
    for cb in range(x.shape[1] // LANE):
        xc = x[:, cb * LANE:(cb + 1) * LANE]
        cols.append(xc * cos + pltpu.roll(xc, LANE - ROT_DIM // 2, 1) * sa + pltpu.roll(xc, ROT_DIM // 2, 1) * sb)
    return cols[0] if len(cols) == 1 else jnp.concatenate(cols, axis=1)


def _head_pair_operands(x):
    lane = lax.broadcasted_iota(jnp.int32, (x.shape[0], LANE), 1)
    out = []
    for cb in range(x.shape[1] // LANE):
        c = x[:, cb * LANE:(cb + 1) * LANE]
        lo = jnp.where(lane < SW_HEAD_DIM, c, 0.0)
        hi = jnp.where(lane >= SW_HEAD_DIM, c, 0.0)
        out.append((lo, pltpu.roll(lo, SW_HEAD_DIM, 1)))
        out.append((pltpu.roll(hi, SW_HEAD_DIM, 1), hi))
    return out


def _ones_columns(w):
    rowi = lax.broadcasted_iota(jnp.int32, (4 * w, LANE), 0)
    lanei = lax.broadcasted_iota(jnp.int32, (4 * w, LANE), 1)
    return jnp.where((rowi < 2 * w) == (lanei < SW_HEAD_DIM), 1.0, 0.0).astype(BF16)


def _pair_softmax(scores, mask, sinks, lane, w):
    parts, es = [], []
    for half in range(2):
        s = jnp.where(mask, scores[:, half * 2 * w:(half + 1) * 2 * w], NEG_INF)
        m = jnp.maximum(jnp.max(s, axis=-1, keepdims=True), sinks[half])
        parts.append(jnp.exp(s - m).astype(BF16))
        es.append(jnp.exp(sinks[half] - m))
    return jnp.concatenate(parts, axis=1), jnp.where(lane < SW_HEAD_DIM, es[0], es[1])


def _swa_prompt_kernel(sinks_ref, qkv_ref, o_ref, kc_ref, vc_ref, kk2, vv2):
    n = pl.program_id(1)
    w = WINDOW

    @pl.when(n == 0)
    def _():
        kk2[...] = jnp.zeros_like(kk2)
        for j in range(SW_KV_HEADS):
            vv2[j, :, 0:LANE] = jnp.zeros((4 * w, LANE), BF16)
            vv2[j, :, LANE:2 * LANE] = _ones_columns(w)

    for j in range(SW_KV_HEADS):
        kk2[j, 0:w, :] = kk2[j, w:2 * w, :]
        kk2[j, 2 * w:3 * w, :] = kk2[j, 3 * w:4 * w, :]
        vv2[j, 0:w, 0:LANE] = vv2[j, w:2 * w, 0:LANE]
        vv2[j, 2 * w:3 * w, 0:LANE] = vv2[j, 3 * w:4 * w, 0:LANE]
    q = qkv_ref[:, 0:D_MODEL].astype(BF16)
    k = qkv_ref[:, D_MODEL:D_MODEL + SW_KV]
    v = qkv_ref[:, D_MODEL + SW_KV:D_MODEL + 2 * SW_KV]
    kc_ref[...] = k
    vc_ref[...] = v
    for j, (left, right) in enumerate(_head_pair_operands(k)):
        kk2[j, w:2 * w, :] = left.astype(BF16)
        kk2[j, 3 * w:4 * w, :] = right.astype(BF16)
    for j, (left, right) in enumerate(_head_pair_operands(v)):
        vv2[j, w:2 * w, 0:LANE] = left.astype(BF16)
        vv2[j, 3 * w:4 * w, 0:LANE] = right.astype(BF16)
    qi = lax.broadcasted_iota(jnp.int32, (w, 2 * w), 0) + w
    sj = lax.broadcasted_iota(jnp.int32, (w, 2 * w), 1)
    d = qi - sj
    mask = (d >= 0) & (d <= WINDOW) & ((sj >= w) | (n > 0))
    lane = lax.broadcasted_iota(jnp.int32, (w, LANE), 1)
    n_pb = D_MODEL // LANE
    scores = [_dot_nt(q[:, pb * LANE:(pb + 1) * LANE], kk2[pb // 2]) for pb in range(n_pb)]
    soft = [_pair_softmax(scores[pb], mask, (sinks_ref[2 * pb], sinks_ref[2 * pb + 1]), lane, w)
            for pb in range(n_pb)]
    outs = [_dot(soft[pb][0], vv2[pb // 2]) for pb in range(n_pb)]
    for pb in range(n_pb):
        o = outs[pb][:, 0:LANE] / (outs[pb][:, LANE:2 * LANE] + soft[pb][1])
        o_ref[:, pb * LANE:(pb + 1) * LANE] = o.astype(o_ref.dtype)


def _swa_prompt(sinks, qkv, *, batch, seq):
    nb = seq // WINDOW
    row = lambda b, n: (b * nb + n, 0)
    per_b = lambda b, n: (b, 0, 0)
    return pl.pallas_call(
        _swa_prompt_kernel,
        out_shape=[jax.ShapeDtypeStruct((batch * seq, D_MODEL), BF16),
                   jax.ShapeDtypeStruct((batch, WINDOW, SW_KV), F32),
                   jax.ShapeDtypeStruct((batch, WINDOW, SW_KV), F32)],
        grid=(batch, nb),
        in_specs=[pl.BlockSpec(memory_space=pltpu.SMEM),
                  pl.BlockSpec((WINDOW, D_MODEL + 2 * SW_KV), row)],
        out_specs=[pl.BlockSpec((WINDOW, D_MODEL), row),
                   pl.BlockSpec((None, WINDOW, SW_KV), per_b),
                   pl.BlockSpec((None, WINDOW, SW_KV), per_b)],
        scratch_shapes=[pltpu.VMEM((SW_KV_HEADS, 4 * WINDOW, LANE), BF16),
                        pltpu.VMEM((SW_KV_HEADS, 4 * WINDOW, 2 * LANE), BF16)],
        compiler_params=_params(("arbitrary", "arbitrary")),
        name="swa_prompt",
    )(sinks, qkv)


def _swa_sample_kernel(sinks_ref, qkv_ref, kbuf_ref, vbuf_ref, cos_ref, sa_ref, sb_ref,
                       o_ref, kc_ref, vc_ref, kk_s, vv_s, kk2, vv2, *, nseq, t_new):
    wb = WINDOW
    tp = SUBLANE

    @pl.when(pl.program_id(0) == 0)
    def _():
        kk2[...] = jnp.zeros_like(kk2)
        for g in range(nseq):
            for j in range(SW_KV_HEADS):
                vv2[g, j, :, 0:LANE] = jnp.zeros((4 * wb, LANE), BF16)
                vv2[g, j, :, LANE:2 * LANE] = _ones_columns(wb)

    cos, sa, sb = cos_ref[...], sa_ref[...], sb_ref[...]
    qs = []
    for g in range(nseq):
        q = (_rope(qkv_ref[g, :, 0:D_MODEL], cos, sa, sb) * (SW_HEAD_DIM ** -0.5)).astype(BF16)
        k_new = _rope(qkv_ref[g, :, D_MODEL:D_MODEL + SW_KV], cos, sa, sb)
        v_new = qkv_ref[g, :, D_MODEL + SW_KV:D_MODEL + 2 * SW_KV]
        k_old, v_old = kbuf_ref[g], vbuf_ref[g]
        kk_s[g, 0:wb, :] = k_old
        vv_s[g, 0:wb, :] = v_old
        kk_s[g, wb:wb + tp, :] = k_new
        vv_s[g, wb:wb + tp, :] = v_new
        kc_ref[g] = kk_s[g, pl.ds(t_new, wb), :]
        vc_ref[g] = vv_s[g, pl.ds(t_new, wb), :]
        for r0, n_r, kx, vx in ((0, wb, k_old, v_old), (wb, tp, k_new, v_new)):
            for j, (left, right) in enumerate(_head_pair_operands(kx)):
                kk2[g, j, r0:r0 + n_r, :] = left.astype(BF16)
                kk2[g, j, 2 * wb + r0:2 * wb + r0 + n_r, :] = right.astype(BF16)
            for j, (left, right) in enumerate(_head_pair_operands(vx)):
                vv2[g, j, r0:r0 + n_r, 0:LANE] = left.astype(BF16)
                vv2[g, j, 2 * wb + r0:2 * wb + r0 + n_r, 0:LANE] = right.astype(BF16)
        qs.append([jnp.concatenate([q[:, (2 * j) * LANE:(2 * j + 1) * LANE],
                                    q[:, (2 * j + 1) * LANE:(2 * j + 2) * LANE]], axis=0)
                   for j in range(SW_KV_HEADS)])
    rows = 2 * tp
    qt = lax.broadcasted_iota(jnp.int32, (rows, 2 * wb), 0) % tp
    sj = lax.broadcasted_iota(jnp.int32, (rows, 2 * wb), 1)
    d = qt + wb - sj
    mask = (d >= 0) & (d <= WINDOW)
    top = lax.broadcasted_iota(jnp.int32, (rows, 1), 0) < tp
    lane = lax.broadcasted_iota(jnp.int32, (rows, LANE), 1)
    items = [(g, j) for g in range(nseq) for j in range(SW_KV_HEADS)]
    scores = [_dot_nt(qs[g][j], kk2[g, j]) for g, j in items]
    soft = []
    for n, (g, j) in enumerate(items):
        sinks = [jnp.where(top, sinks_ref[4 * j + half], sinks_ref[4 * j + 2 + half]) for half in range(2)]
        soft.append(_pair_softmax(scores[n], mask, sinks, lane, wb))
    outs = [_dot(soft[n][0], vv2[g, j]) for n, (g, j) in enumerate(items)]
    for n, (g, j) in enumerate(items):
        o = outs[n][:, 0:LANE] / (outs[n][:, LANE:2 * LANE] + soft[n][1])
        o_ref[g, :, (2 * j) * LANE:(2 * j + 1) * LANE] = o[0:tp, :]
        o_ref[g, :, (2 * j + 1) * LANE:(2 * j + 2) * LANE] = o[tp:2 * tp, :]


def _swa_sample(sinks, qkv, kbuf, vbuf, cos, sa, sb, *, batch, nseq, t_new):
    per_b = lambda b: (b, 0, 0)
    tab = lambda b: (0, 0)
    kern = functools.partial(_swa_sample_kernel, nseq=nseq, t_new=t_new)
    return pl.pallas_call(
        kern,
        out_shape=[jax.ShapeDtypeStruct((batch, SUBLANE, D_MODEL), F32),
                   jax.ShapeDtypeStruct((batch, WINDOW, SW_KV), F32),
                   jax.ShapeDtypeStruct((batch, WINDOW, SW_KV), F32)],
        grid=(batch // nseq,),
        in_specs=[pl.BlockSpec(memory_space=pltpu.SMEM),
                  pl.BlockSpec((nseq, SUBLANE, D_MODEL + 2 * SW_KV), per_b),
                  pl.BlockSpec((nseq, WINDOW, SW_KV), per_b),
                  pl.BlockSpec((nseq, WINDOW, SW_KV), per_b),
                  pl.BlockSpec((SUBLANE, LANE), tab),
                  pl.BlockSpec((SUBLANE, LANE), tab),
                  pl.BlockSpec((SUBLANE, LANE), tab)],
        out_specs=[pl.BlockSpec((nseq, SUBLANE, D_MODEL), per_b),
                   pl.BlockSpec((nseq, WINDOW, SW_KV), per_b),
                   pl.BlockSpec((nseq, WINDOW, SW_KV), per_b)],
        scratch_shapes=[pltpu.VMEM((nseq, WINDOW + SUBLANE, SW_KV), F32),
                        pltpu.VMEM((nseq, WINDOW + SUBLANE, SW_KV), F32),
                        pltpu.VMEM((nseq, SW_KV_HEADS, 4 * WINDOW, LANE), BF16),
                        pltpu.VMEM((nseq, SW_KV_HEADS, 4 * WINDOW, 2 * LANE), BF16)],
        compiler_params=_params(("arbitrary",)),
        name="swa_sample",
    )(sinks, qkv, kbuf, vbuf, cos, sa, sb)


def _out_ffn_kernel(x_ref, g2_ref, post2_ref, ga_ref, gb_ref, odn_ref, osw_ref, wo_ref,
                    sh_ref, sc_ref, g_ref, pre_ref, post_ref, wg_ref, wu_ref, wd_ref, o_ref,
                    *, per_token, tiles_per_seq):
    rows = _row_groups(x_ref.shape[0])
    g2 = _mod_rows(g2_ref, per_token, tiles_per_seq)
    sh = _mod_rows(sh_ref, per_token, tiles_per_seq)
    sc = _mod_rows(sc_ref, per_token, tiles_per_seq)
    g = _mod_rows(g_ref, per_token, tiles_per_seq)
    pick = (lambda m, r: m[r]) if per_token else (lambda m, r: m)
    f32 = lambda ref, r: ref[r, :].astype(F32)
    ys = [(_sigmoid(f32(ga_ref, r)) * f32(odn_ref, r)
           + _sigmoid(f32(gb_ref, r)) * f32(osw_ref, r)).astype(BF16) for r in rows]
    ps = [_dot(y, wo_ref[...]) for y in ys]
    xs = [x_ref[r, :] + pick(g2, r) * _rms(p, post2_ref[...]) for p, r in zip(ps, rows)]
    outs = _ffn_stages(xs, rows, pick, sh, sc, g, pre_ref[...], post_ref[...], wg_ref, wu_ref, wd_ref)
    for o, r in zip(outs, rows):
        o_ref[r, :] = o


def _out_ffn(x, mod, mod_block, seq_len, post2, gates, o_dn, o_sw, w_out, pre, post, wg, wu, wd):
    n = x.shape[0]
    tm = min(512, n)
    dff = wg.shape[1]
    kern = functools.partial(_out_ffn_kernel, per_token=mod_block is None, tiles_per_seq=max(seq_len // tm, 1))
    row = lambda i: (i, 0)
    return pl.pallas_call(
        kern,
        out_shape=jax.ShapeDtypeStruct((n, D_MODEL), F32),
        grid=(n // tm,),
        in_specs=[pl.BlockSpec((tm, D_MODEL), row)]
        + _mod_specs(mod_block, tm, (5,))
        + [_const_spec((1, D_MODEL)),
           pl.BlockSpec((tm, D_MODEL), lambda i: (i, 0)),
           pl.BlockSpec((tm, D_MODEL), lambda i: (i, 1)),
           pl.BlockSpec((tm, D_MODEL), row),
           pl.BlockSpec((tm, D_MODEL), row),
           _const_spec((D_MODEL, D_MODEL))]
        + _mod_specs(mod_block, tm, (6, 7, 8))
        + [_const_spec((1, D_MODEL)), _const_spec((1, D_MODEL)),
           _const_spec((D_MODEL, dff)), _const_spec((D_MODEL, dff)), _const_spec((dff, D_MODEL))],
        out_specs=pl.BlockSpec((tm, D_MODEL), row),
        compiler_params=_params(("arbitrary",)),
        name="out_ffn",
    )(x, mod, post2, gates, gates, o_dn, o_sw, w_out, mod, mod, mod, pre, post, wg, wu, wd)


def _rope_tables(pos):
    half = ROT_DIM // 2
    inv_freq = ROPE_THETA ** (-jnp.arange(half, dtype=F32) * (2.0 / ROT_DIM))
    ang = pos.astype(F32)[:, None] * inv_freq[None, :]
    cos, sin = jnp.cos(ang), jnp.sin(ang)
    n = pos.shape[0]
    rest = SW_HEAD_DIM - ROT_DIM
    c64 = jnp.concatenate([cos, cos, jnp.ones((n, rest), F32)], axis=1)
    a64 = jnp.concatenate([-sin, jnp.zeros((n, half + rest), F32)], axis=1)
    b64 = jnp.concatenate([jnp.zeros((n, half), F32), sin, jnp.zeros((n, rest), F32)], axis=1)
    rep = LANE // SW_HEAD_DIM
    return tuple(jnp.tile(t, (1, rep)) for t in (c64, a64, b64))


def _pad_rows(a, rows):
    return jnp.pad(a, ((0, rows - a.shape[0]),) + ((0, 0),) * (a.ndim - 1))


def _layer_sample(x, mod, seq, rows, wts, conv0, s0, dn_fn, swa_fn):
    def per_seq(a):
        a = a.reshape(-1, seq, a.shape[-1])
        return a if rows == seq else jnp.pad(a, ((0, 0), (0, rows - seq), (0, 0)))

    def flat(a):
        return a[:, :seq].reshape(-1, a.shape[-1])

    x = _ffn(x, mod, 0, None, seq, wts["pre1"], wts["post1"], wts["wg1"], wts["wu1"], wts["wd1"])
    u, z, qkv, gates, ba = _proj(x, mod, None, seq, wts["pre2"], *wts["w_in"])
    o_dn, conv_new, s_new = dn_fn(per_seq(u), per_seq(z), per_seq(ba), wts["conv_w"], wts["gparam"],
                                  wts["dn_norm"], conv0, s0)
    o_sw, k_new, v_new = swa_fn(per_seq(qkv))
    x = _out_ffn(x, mod, None, seq, wts["post2"], gates, flat(o_dn), flat(o_sw), wts["w_out"],
                 wts["pre3"], wts["post3"], wts["wg2"], wts["wu2"], wts["wd2"])
    return x, (k_new, v_new, conv_new, s_new)


def _layer_prompt(x, mod, mod_block, batch, seq, wts, s0, rope, sink):
    per_seq = lambda a: a.reshape(batch, seq, a.shape[-1])
    x = _ffn(x, mod, 0, mod_block, seq, wts["pre1"], wts["post1"], wts["wg1"], wts["wu1"], wts["wd1"])
    q, k, v, z, qkv, gates, ba, conv_new = _proj_prompt(x, mod, mod_block, seq, wts["pre2"], *wts["w_in"],
                                                        wts["conv_w"], *rope)
    o_dn, s_new = _deltanet_fused(per_seq(q), per_seq(k), per_seq(v), per_seq(z), per_seq(ba), wts["gparam"],
                                  wts["dn_norm"], s0, nseq=batch, tm=DN_TILE, chunk=DN_CHUNK, o_dtype=BF16)
    o_sw, k_new, v_new = _swa_prompt(sink, qkv, batch=batch, seq=seq)
    x = _out_ffn(x, mod, mod_block, seq, wts["post2"], gates, o_dn.reshape(batch * seq, D_MODEL), o_sw,
                 wts["w_out"], wts["pre3"], wts["post3"], wts["wg2"], wts["wu2"], wts["wd2"])
    return x, (k_new, v_new, conv_new, s_new)


def kernel(x_prompt, x_sample, cache_swa_k, cache_swa_v, state_conv, state_delta, c_prompt, c_sample,
           w_ada, b_ada, ffn1_norm_pre, ffn1_norm_post, ffn1_w_gate, ffn1_w_up, ffn1_w_down,
           mix_norm_pre, mix_norm_post, w_in, conv_w, a_log, dt_bias, dn_norm, sinks, w_out,
           ffn2_norm_pre, ffn2_norm_post, ffn2_w_gate, ffn2_w_up, ffn2_w_down):
    depth = w_ada.shape[0]
    bp, tp, _ = x_prompt.shape
    bs, ts, _ = x_sample.shape
    assert tp % 512 == 0 and 1 <= ts <= SUBLANE and bs % SAMPLE_SEQS == 0 and (bs * ts) % SUBLANE == 0

    cos_p, sa_p, sb_p = _rope_tables(jnp.arange(tp))
    cos_s, sa_s, sb_s = (_pad_rows(t, SUBLANE) for t in _rope_tables(PAST_LEN + jnp.arange(ts)))

    xp = x_prompt.reshape(bp * tp, D_MODEL)
    xs = x_sample.reshape(bs * ts, D_MODEL)
    c_all = jnp.concatenate([jnp.repeat(c_sample, ts, axis=0), _pad_rows(c_prompt, SUBLANE)], axis=0)
    prompt_mod_block = bs * ts // SUBLANE

    outs_p, outs_s = [], []
    for l in range(depth):
        wi = w_in[l]
        n_ba = 2 * DN_HEADS
        w_in_a = wi[:, :P_QKV].astype(BF16)
        w_in_b = wi[:, P_QKV + n_ba:].astype(BF16)
        w_in_c = jnp.pad(wi[:, P_QKV:P_QKV + n_ba], ((0, 0), (0, LANE - n_ba))).astype(BF16)
        gparam = jnp.zeros((SUBLANE, LANE), F32)
        gparam = gparam.at[0, DN_HEADS:2 * DN_HEADS].set(dt_bias[l]).at[1, DN_HEADS:2 * DN_HEADS].set(a_log[l])
        wts = dict(
            pre1=ffn1_norm_pre[l][None], post1=ffn1_norm_post[l][None],
            wg1=ffn1_w_gate[l].astype(BF16), wu1=ffn1_w_up[l].astype(BF16), wd1=ffn1_w_down[l].astype(BF16),
            pre2=mix_norm_pre[l][None], post2=mix_norm_post[l][None], w_in=(w_in_a, w_in_b, w_in_c),
            conv_w=_pad_rows(conv_w[l], SUBLANE), gparam=gparam, dn_norm=dn_norm[l][None],
            w_out=w_out[l].astype(BF16),
            pre3=ffn2_norm_pre[l][None], post3=ffn2_norm_post[l][None],
            wg2=ffn2_w_gate[l].astype(BF16), wu2=ffn2_w_up[l].astype(BF16), wd2=ffn2_w_down[l].astype(BF16),
        )
        b_ada_l = b_ada[l][None]
        mod = _ada(c_all, w_ada[l], b_ada_l)
        sink_l = sinks[l]

        s0_p = jnp.zeros((bp, DN_HEADS, DN_DK, DN_DV), F32)
        xp, st_p = _layer_prompt(xp, mod, prompt_mod_block, bp, tp, wts, s0_p, (cos_p, sa_p, sb_p), sink_l)

        conv0_s = jnp.pad(state_conv[l], ((0, 0), (SUBLANE - CONV_W + 1, 0), (0, 0)))
        kbuf = cache_swa_k[l].reshape(bs, WINDOW, SW_KV)
        vbuf = cache_swa_v[l].reshape(bs, WINDOW, SW_KV)
        dn_s = functools.partial(_deltanet, nseq=SAMPLE_SEQS, tm=SUBLANE, chunk=SUBLANE, valid=ts, o_dtype=F32)

        swa_s = lambda qkv: _swa_sample(sink_l, qkv, kbuf, vbuf, cos_s, sa_s, sb_s,
                                        batch=bs, nseq=SAMPLE_SEQS, t_new=ts)
        xs, st_s = _layer_sample(xs, mod, ts, SUBLANE, wts, conv0_s, state_delta[l], dn_s, swa_s)
        outs_p.append(st_p)
        outs_s.append(st_s)

    def stack(outs, i):
        return jnp.stack([o[i] for o in outs])

    def kv5(a, batch):
        return a.reshape(depth, batch, WINDOW, SW_KV_HEADS, SW_HEAD_DIM)

    tail = slice(SUBLANE - CONV_W + 1, SUBLANE)
    y_p = xp.reshape(bp, tp, D_MODEL)
    y_s = xs.reshape(bs, ts, D_MODEL)
    return (y_p, y_s,
            kv5(stack(outs_p, 0), bp), kv5(stack(outs_p, 1), bp),
            stack(outs_p, 2)[:, :, tail], stack(outs_p, 3),
            kv5(stack(outs_s, 0), bs), kv5(stack(outs_s, 1), bs),
            stack(outs_s, 2)[:, :, tail], stack(outs_s, 3))
```

```python
import functools

import jax
import jax.numpy as jnp
from jax import lax
from jax.experimental import pallas as pl
from jax.experimental.pallas import tpu as pltpu

F32, BF16 = jnp.float32, jnp.bfloat16

D_MODEL = 1024
DN_HEADS, DN_DK, DN_DV = 8, 128, 128
CONV_W = 4
CONV_CH = 3 * D_MODEL
DN_CHUNK = 64
SW_HEAD_DIM, SW_HEADS, SW_KV_HEADS, SW_GROUP = 64, 16, 4, 4
SW_KV = SW_KV_HEADS * SW_HEAD_DIM
WINDOW = 128
ROT_DIM = 16
ROPE_THETA = 500000.0
PAST_LEN = 8192
EPS = 1e-6
NEG_INF = -1e30

LANE = 128
SUBLANE = 8
VMEM_LIMIT = 56 * 1024 * 1024

P_U, P_Z, P_QKV, P_G, P_BA = 0, 3072, 4096, 5632, 7680
P_TOTAL = 7808

DN_TILE = 128
DN_GROUP = 4
SAMPLE_SEQS = 8
FFN_SUBTILES = 2
ADA_COLS = 3072
PROJ_FUSED_TILE = 512


def _dot(a, b):
    return jnp.dot(a, b, preferred_element_type=F32)


def _dot_nt(a, b):
    return lax.dot_general(a, b, (((1,), (1,)), ((), ())), preferred_element_type=F32)


def _dot_tn(a, b):
    return lax.dot_general(a, b, (((0,), (0,)), ((), ())), preferred_element_type=F32)


def _split3(x):
    hi = x.astype(BF16)
    r1 = x - hi.astype(F32)
    mid = r1.astype(BF16)
    lo = (r1 - mid.astype(F32)).astype(BF16)
    return hi, mid, lo


def _dot_exact_lhs(a_bf, b):
    return sum(_dot(a_bf, t) for t in _split3(b))


def _dot_nt_exact_lhs(a_bf, b):
    return sum(_dot_nt(a_bf, t) for t in _split3(b))


def _mm_bf(a, b):
    return _dot(a.astype(BF16), b.astype(BF16))


def _sigmoid(x):
    return 0.5 * jnp.tanh(0.5 * x) + 0.5


def _silu(x):
    return x * _sigmoid(x)


def _rms(x, gain):
    ms = jnp.mean(x * x, axis=-1, keepdims=True)
    return x * lax.rsqrt(ms + EPS) * gain


def _mod_rows(ref, per_token, tiles_per_seq):
    if per_token:
        return ref[...]
    b = pl.program_id(0) // tiles_per_seq
    return ref[pl.ds(b, 1), :]


def _const_spec(shape):
    return pl.BlockSpec(shape, lambda *_: (0,) * len(shape), pipeline_mode=pl.Buffered(1))


def _params(sem):
    return pltpu.CompilerParams(dimension_semantics=sem, vmem_limit_bytes=VMEM_LIMIT)


def _ada_kernel(c_ref, w_ref, b_ref, o_ref):
    h = _silu(c_ref[...]).astype(BF16)
    o_ref[...] = _dot(h, w_ref[...].astype(BF16)) + b_ref[...]


def _ada(c, w, b):
    rows, n = c.shape[0], w.shape[1]
    tn = ADA_COLS
    return pl.pallas_call(
        _ada_kernel,
        out_shape=jax.ShapeDtypeStruct((rows, n), F32),
        grid=(n // tn,),
        in_specs=[
            pl.BlockSpec((rows, D_MODEL), lambda j: (0, 0)),
            pl.BlockSpec((D_MODEL, tn), lambda j: (0, j)),
            pl.BlockSpec((1, tn), lambda j: (0, j)),
        ],
        out_specs=pl.BlockSpec((rows, tn), lambda j: (0, j)),
        compiler_params=_params(("arbitrary",)),
        name="ada",
    )(c, w, b)


def _mod_specs(mod_block, tm, cols):
    if mod_block is None:
        return [pl.BlockSpec((tm, D_MODEL), functools.partial(lambda i, c: (i, c), c=c)) for c in cols]
    return [pl.BlockSpec((SUBLANE, D_MODEL), functools.partial(lambda i, c: (mod_block, c), c=c)) for c in cols]


def _ffn_stages(xs, rows, pick, sh, sc, g, pre, post, wg_ref, wu_ref, wd_ref):
    hs = [(_rms(x, pre) * (1.0 + pick(sc, r)) + pick(sh, r)).astype(BF16) for x, r in zip(xs, rows)]
    gates = [_dot(h, wg_ref[...]) for h in hs]
    ups = [_dot(h, wu_ref[...]) for h in hs]
    acts = [(_silu(a) * b).astype(BF16) for a, b in zip(gates, ups)]
    ys = [_dot(a, wd_ref[...]) for a in acts]
    return [x + 0.5 * pick(g, r) * _rms(y, post) for x, y, r in zip(xs, ys, rows)]


def _row_groups(tm):
    return [slice(i * tm // FFN_SUBTILES, (i + 1) * tm // FFN_SUBTILES) for i in range(FFN_SUBTILES)]


def _ffn_kernel(x_ref, sh_ref, sc_ref, g_ref, pre_ref, post_ref, wg_ref, wu_ref, wd_ref, o_ref,
                *, per_token, tiles_per_seq):
    rows = _row_groups(x_ref.shape[0])
    sh = _mod_rows(sh_ref, per_token, tiles_per_seq)
    sc = _mod_rows(sc_ref, per_token, tiles_per_seq)
    g = _mod_rows(g_ref, per_token, tiles_per_seq)
    pick = (lambda m, r: m[r]) if per_token else (lambda m, r: m)
    xs = [x_ref[r, :] for r in rows]
    outs = _ffn_stages(xs, rows, pick, sh, sc, g, pre_ref[...], post_ref[...], wg_ref, wu_ref, wd_ref)
    for o, r in zip(outs, rows):
        o_ref[r, :] = o


def _ffn(x, mod, col0, mod_block, seq_len, pre, post, wg, wu, wd):
    n = x.shape[0]
    tm = min(512, n)
    dff = wg.shape[1]
    kern = functools.partial(_ffn_kernel, per_token=mod_block is None, tiles_per_seq=max(seq_len // tm, 1))
    row = lambda i: (i, 0)
    return pl.pallas_call(
        kern,
        out_shape=jax.ShapeDtypeStruct((n, D_MODEL), F32),
        grid=(n // tm,),
        in_specs=[pl.BlockSpec((tm, D_MODEL), row)]
        + _mod_specs(mod_block, tm, (col0, col0 + 1, col0 + 2))
        + [_const_spec((1, D_MODEL)), _const_spec((1, D_MODEL)),
           _const_spec((D_MODEL, dff)), _const_spec((D_MODEL, dff)), _const_spec((dff, D_MODEL))],
        out_specs=pl.BlockSpec((tm, D_MODEL), row),
        compiler_params=_params(("arbitrary",)),
        name="ffn",
    )(x, mod, mod, mod, pre, post, wg, wu, wd)


def _proj_kernel(x_ref, sh_ref, sc_ref, pre_ref, wa_ref, wb_ref, wc_ref, u_ref, z_ref, qkv_ref, g_ref, ba_ref,
                 *, per_token, tiles_per_seq):
    sh = _mod_rows(sh_ref, per_token, tiles_per_seq)
    sc = _mod_rows(sc_ref, per_token, tiles_per_seq)
    h = (_rms(x_ref[...], pre_ref[...]) * (1.0 + sc) + sh).astype(BF16)
    u_ref[...] = _dot(h, wa_ref[:, 0:P_Z])
    z_ref[...] = _dot(h, wa_ref[:, P_Z:P_QKV])
    qkv_ref[...] = _dot(h, wb_ref[:, 0:P_G - P_QKV])
    g_ref[...] = _dot(h, wb_ref[:, P_G - P_QKV:P_BA - P_QKV]).astype(g_ref.dtype)
    ba_ref[...] = _dot(h, wc_ref[...])


def _proj(x, mod, mod_block, seq_len, pre, wa, wb, wc):
    n = x.shape[0]
    tm = min(512, n)
    kern = functools.partial(_proj_kernel, per_token=mod_block is None, tiles_per_seq=max(seq_len // tm, 1))
    row = lambda i: (i, 0)
    widths = (P_Z - P_U, P_QKV - P_Z, P_G - P_QKV, P_BA - P_G, P_TOTAL - P_BA)
    return pl.pallas_call(
        kern,
        out_shape=[jax.ShapeDtypeStruct((n, wd), BF16 if wd == P_BA - P_G else F32) for wd in widths],
        grid=(n // tm,),
        in_specs=[pl.BlockSpec((tm, D_MODEL), row)]
        + _mod_specs(mod_block, tm, (3, 4))
        + [_const_spec((1, D_MODEL)), _const_spec(wa.shape), _const_spec(wb.shape), _const_spec(wc.shape)],
        out_specs=[pl.BlockSpec((tm, wd), row) for wd in widths],
        compiler_params=_params(("arbitrary",)),
        name="proj",
    )(x, mod, mod, pre, wa, wb, wc)


def _conv_norm(u_cols, ubuf_cb, cw_ref, cols):
    rows = u_cols.shape[0]
    ubuf_cb[SUBLANE:SUBLANE + rows, :] = u_cols
    acc = ubuf_cb[pl.ds(SUBLANE - CONV_W + 1, rows), :] * cw_ref[0:1, cols]
    for i in range(1, CONV_W):
        acc = acc + ubuf_cb[pl.ds(SUBLANE - CONV_W + 1 + i, rows), :] * cw_ref[i:i + 1, cols]
    return _silu(acc)


def _store_qkv(a, cb, q_dst, k_dst, v_dst, rows):
    sec, off = divmod(cb * LANE, D_MODEL)
    dst = slice(off, off + LANE)
    if sec == 2:
        v_dst[rows, dst] = a
    else:
        an = a * lax.rsqrt(jnp.sum(a * a, axis=-1, keepdims=True) + EPS)
        if sec == 0:
            q_dst[rows, dst] = an * (DN_DK ** -0.5)
        else:
            k_dst[rows, dst] = an


def _proj_prompt_kernel(x_ref, sh_ref, sc_ref, pre_ref, wa_ref, wb_ref, wc_ref, cw_ref, cos_ref, sa_ref, sb_ref,
                        q_ref, k_ref, v_ref, z_ref, qkv_ref, g_ref, ba_ref, tail_ref, ubuf, *, tiles_per_seq):
    tm = x_ref.shape[0]
    n_cb = CONV_CH // LANE
    all_rows = slice(0, tm)

    @pl.when(pl.program_id(0) % tiles_per_seq == 0)
    def _():
        ubuf[:, 0:SUBLANE, :] = jnp.zeros((n_cb, SUBLANE, LANE), F32)

    sh = _mod_rows(sh_ref, False, tiles_per_seq)
    sc = _mod_rows(sc_ref, False, tiles_per_seq)
    h = (_rms(x_ref[...], pre_ref[...]) * (1.0 + sc) + sh).astype(BF16)
    u = _dot(h, wa_ref[:, 0:P_Z])
    z_ref[...] = _dot(h, wa_ref[:, P_Z:P_QKV])
    s = _dot(h, wb_ref[:, 0:P_G - P_QKV])
    g_ref[...] = _dot(h, wb_ref[:, P_G - P_QKV:P_BA - P_QKV]).astype(g_ref.dtype)
    ba_ref[...] = _dot(h, wc_ref[...])
    for cb in range(n_cb):
        cols = slice(cb * LANE, (cb + 1) * LANE)
        a = _conv_norm(u[:, cols], ubuf.at[cb], cw_ref, cols)
        _store_qkv(a, cb, q_ref, k_ref, v_ref, all_rows)
        tail = ubuf[cb, pl.ds(tm, SUBLANE), :]
        ubuf[cb, 0:SUBLANE, :] = tail
        tail_ref[:, cols] = tail
    cos, sa, sb = cos_ref[...], sa_ref[...], sb_ref[...]
    qkv_ref[:, 0:D_MODEL] = _rope(s[:, 0:D_MODEL], cos, sa, sb) * (SW_HEAD_DIM ** -0.5)
    qkv_ref[:, D_MODEL:D_MODEL + SW_KV] = _rope(s[:, D_MODEL:D_MODEL + SW_KV], cos, sa, sb)
    qkv_ref[:, D_MODEL + SW_KV:D_MODEL + 2 * SW_KV] = s[:, D_MODEL + SW_KV:D_MODEL + 2 * SW_KV]


def _proj_prompt(x, mod, mod_block, seq_len, pre, wa, wb, wc, cw, cos, sa, sb):
    n = x.shape[0]
    tm = PROJ_FUSED_TILE
    tps = seq_len // tm
    kern = functools.partial(_proj_prompt_kernel, tiles_per_seq=tps)
    row = lambda i: (i, 0)
    tab = lambda i: (i % tps, 0)
    widths = (D_MODEL, D_MODEL, D_MODEL, P_QKV - P_Z, P_G - P_QKV, P_BA - P_G, P_TOTAL - P_BA)
    dtypes = (F32, F32, F32, F32, F32, BF16, F32)
    return pl.pallas_call(
        kern,
        out_shape=[jax.ShapeDtypeStruct((n, wd), dt) for wd, dt in zip(widths, dtypes)]
        + [jax.ShapeDtypeStruct((n // seq_len, SUBLANE, CONV_CH), F32)],
        grid=(n // tm,),
        in_specs=[pl.BlockSpec((tm, D_MODEL), row)]
        + _mod_specs(mod_block, tm, (3, 4))
        + [_const_spec((1, D_MODEL)), _const_spec(wa.shape), _const_spec(wb.shape), _const_spec(wc.shape),
           _const_spec(cw.shape),
           pl.BlockSpec((tm, LANE), tab), pl.BlockSpec((tm, LANE), tab), pl.BlockSpec((tm, LANE), tab)],
        out_specs=[pl.BlockSpec((tm, wd), row) for wd in widths]
        + [pl.BlockSpec((None, SUBLANE, CONV_CH), lambda i: (i // tps, 0, 0))],
        scratch_shapes=[pltpu.VMEM((CONV_CH // LANE, SUBLANE + tm, LANE), F32)],
        compiler_params=_params(("arbitrary",)),
        name="proj_prompt",
    )(x, mod, mod, pre, wa, wb, wc, cw, cos, sa, sb)


def _tri_inv_all(lows, c):
    ii = lax.broadcasted_iota(jnp.int32, (c, c), 0)
    jj = lax.broadcasted_iota(jnp.int32, (c, c), 1)
    eye = jnp.where(ii == jj, 1.0, 0.0).astype(F32)
    xs = [eye - jnp.where(ii // 2 == jj // 2, low, 0.0) for low in lows]
    b = 2
    while b < c:
        join = (ii // (2 * b) == jj // (2 * b)) & (ii // b != jj // b)
        ys = [_mm_bf(x, jnp.where(join, low, 0.0)) for x, low in zip(xs, lows)]
        xs = [x - _mm_bf(y, x) for x, y in zip(xs, ys)]
        b *= 2
    return xs


def _dn_kernel(*refs, nseq, tm, chunk, valid, fused):
    if fused:
        (q_ref, k_ref, v_ref, z_ref, ba_ref, gp_ref, dnn_ref, s0_ref, o_ref, so_ref,
         q_s, k_s, v_s, bg_s, u_s, w_s, qe_s, kd_s, a_s, eg_s) = refs
    else:
        (u_ref, z_ref, ba_ref, cw_ref, gp_ref, dnn_ref, conv0_ref, s0_ref, o_ref, convo_ref, so_ref,
         ubuf, q_s, k_s, v_s, bg_s, u_s, w_s, qe_s, kd_s, a_s, eg_s) = refs
    t = pl.program_id(1)
    rows_all = nseq * tm
    n_cb = CONV_CH // LANE

    @pl.when(t == 0)
    def _():
        so_ref[...] = s0_ref[...]
        if not fused:
            for g in range(nseq):
                for cb in range(n_cb):
                    ubuf[g, cb, 0:SUBLANE, :] = conv0_ref[g, :, cb * LANE:(cb + 1) * LANE]

    for g in range(nseq):
        rws = slice(g * tm, (g + 1) * tm)
        if fused:
            q_s[rws, :] = q_ref[g]
            k_s[rws, :] = k_ref[g]
            v_s[rws, :] = v_ref[g]
            continue
        for cb in range(n_cb):
            cols = slice(cb * LANE, (cb + 1) * LANE)
            a = _conv_norm(u_ref[g, :, cols], ubuf.at[g, cb], cw_ref, cols)
            _store_qkv(a, cb, q_s, k_s, v_s, rws)
            tail = ubuf[g, cb, pl.ds(valid, SUBLANE), :]
            ubuf[g, cb, 0:SUBLANE, :] = tail
            convo_ref[g, :, cols] = tail

    lane = lax.broadcasted_iota(jnp.int32, (tm, LANE), 1)
    rowi = lax.broadcasted_iota(jnp.int32, (tm, LANE), 0)
    for g in range(nseq):
        ba = ba_ref[g]
        xg = ba + gp_ref[0:1, :]
        softplus = jnp.maximum(xg, 0.0) + jnp.log1p(jnp.exp(-jnp.abs(xg)))
        gdec = -jnp.exp(gp_ref[1:2, :]) * softplus
        bg = jnp.where(lane < DN_HEADS, _sigmoid(ba), gdec)
        bg_s[g * tm:(g + 1) * tm, :] = jnp.where(rowi < valid, bg, 0.0)

    c = chunk
    ii = lax.broadcasted_iota(jnp.int32, (c, c), 0)
    jj = lax.broadcasted_iota(jnp.int32, (c, c), 1)
    incl = ii >= jj
    strict = ii > jj
    tril = jnp.where(incl, 1.0, 0.0).astype(BF16)
    sel = jnp.where(lax.broadcasted_iota(jnp.int32, (SUBLANE, LANE), 1)
                    == lax.broadcasted_iota(jnp.int32, (SUBLANE, LANE), 0) + DN_HEADS, 1.0, 0.0).astype(BF16)
    dnn = dnn_ref[...]

    heads = range(DN_HEADS)
    hcols = [slice(h * DN_DK, (h + 1) * DN_DK) for h in heads]
    acols = [slice(h * c, (h + 1) * c) for h in heads]
    n_chunks = rows_all // c
    chunks_per_seq = tm // c
    gpc = DN_GROUP if n_chunks % DN_GROUP == 0 else 1

    def chunk_rows(ci):
        return pl.ds(ci * c, c) if isinstance(ci, int) else pl.ds(pl.multiple_of(ci * c, c), c)

    def prep(cis):
        n_c = len(cis)
        rows = [chunk_rows(ci) for ci in cis]
        pairs = [(j, h) for j in range(n_c) for h in heads]
        bgc = [bg_s[r, :] for r in rows]
        gcum = [_dot_exact_lhs(tril, b) for b in bgc]
        gcum_t = [_dot_nt_exact_lhs(sel, g) for g in gcum]
        for j in range(n_c):
            eg_s[cis[j]] = jnp.broadcast_to(jnp.exp(gcum_t[j][:, c - 1:c]), (DN_HEADS, DN_DV))
        spread = lambda col: jnp.broadcast_to(col, (c, DN_DK))
        beta = [spread(bgc[j][:, h:h + 1]) for j, h in pairs]
        gc = [spread(gcum[j][:, DN_HEADS + h:DN_HEADS + h + 1]) for j, h in pairs]
        decay = [jnp.where(incl, jnp.exp(jnp.where(incl, gc[p][:, :c] - gcum_t[j][h:h + 1, :], 0.0)), 0.0)
                 for p, (j, h) in enumerate(pairs)]
        q = [q_s[rows[j], hcols[h]] for j, h in pairs]
        k = [k_s[rows[j], hcols[h]] for j, h in pairs]
        npair = range(len(pairs))
        kb = [k[p] * beta[p] for p in npair]
        k_bf = [k[p].astype(BF16) for p in npair]
        akk = [_dot_nt(kb[p].astype(BF16), k_bf[p]) for p in npair]
        aqk = [_dot_nt(q[p].astype(BF16), k_bf[p]) for p in npair]
        tinv = _tri_inv_all([jnp.where(strict, akk[p] * decay[p], 0.0) for p in npair], c)
        egc = [jnp.exp(gc[p]) for p in npair]
        rhs = [jnp.concatenate([v_s[rows[j], hcols[h]] * beta[p], kb[p] * egc[p]], axis=1)
               for p, (j, h) in enumerate(pairs)]
        uw = [_mm_bf(tinv[p], rhs[p]) for p in npair]
        sdt = w_s.dtype
        for p, (j, h) in enumerate(pairs):
            u_s[rows[j], hcols[h]] = uw[p][:, :DN_DV]
            w_s[rows[j], hcols[h]] = uw[p][:, DN_DV:].astype(sdt)
            qe_s[rows[j], hcols[h]] = (q[p] * egc[p]).astype(sdt)
            kd_s[rows[j], hcols[h]] = (k[p] * jnp.exp(gc[p][c - 1:c, :] - gc[p])).astype(sdt)
            a_s[rows[j], acols[h]] = jnp.where(incl, aqk[p] * decay[p], 0.0).astype(sdt)

    def state(cpos):
        trip = [(g, h) for g in range(nseq) for h in heads]
        rows = [chunk_rows(g * chunks_per_seq + cpos) for g in range(nseq)]
        local = chunk_rows(cpos)
        egb = [eg_s[g * chunks_per_seq + cpos] for g in range(nseq)]
        s_old = [so_ref[g, h] for g, h in trip]
        s_bf = [s.astype(BF16) for s in s_old]
        ws = [_dot(w_s[rows[g], hcols[h]].astype(BF16), s_bf[n]) for n, (g, h) in enumerate(trip)]
        qs = [_dot(qe_s[rows[g], hcols[h]].astype(BF16), s_bf[n]) for n, (g, h) in enumerate(trip)]
        vn_bf = [(u_s[rows[g], hcols[h]] - ws[n]).astype(BF16) for n, (g, h) in enumerate(trip)]
        av = [_dot(a_s[rows[g], acols[h]].astype(BF16), vn_bf[n]) for n, (g, h) in enumerate(trip)]
        kv = [_dot_tn(kd_s[rows[g], hcols[h]].astype(BF16), vn_bf[n]) for n, (g, h) in enumerate(trip)]
        for n, (g, h) in enumerate(trip):
            so_ref[g, h] = s_old[n] * egb[g][h:h + 1, :] + kv[n]
            o = _rms(qs[n] + av[n], dnn) * _silu(z_ref[g, local, hcols[h]])
            o_ref[g, local, hcols[h]] = o.astype(o_ref.dtype)

    for gi in range(n_chunks // gpc):
        prep([gi * gpc + j for j in range(gpc)])
    if chunks_per_seq == 1:
        state(0)
    else:
        def state_body(cpos, carry):
            state(cpos)
            return carry
        lax.fori_loop(0, chunks_per_seq, state_body, 0)


def _dn_scratch(nseq, tm, chunk):
    rows = nseq * tm
    sdt = BF16 if chunk % 16 == 0 else F32
    return [pltpu.VMEM((rows, D_MODEL), F32),
            pltpu.VMEM((rows, D_MODEL), F32),
            pltpu.VMEM((rows, D_MODEL), F32),
            pltpu.VMEM((rows, LANE), F32),
            pltpu.VMEM((rows, D_MODEL), F32),
            pltpu.VMEM((rows, D_MODEL), sdt),
            pltpu.VMEM((rows, D_MODEL), sdt),
            pltpu.VMEM((rows, D_MODEL), sdt),
            pltpu.VMEM((rows, DN_HEADS * chunk), sdt),
            pltpu.VMEM((rows // chunk, DN_HEADS, DN_DV), F32)]


def _deltanet(u, z, ba, cw, gp, dnn, conv0, s0, *, nseq, tm, chunk, valid, o_dtype):
    batch, seq, _ = u.shape
    nt = seq // tm
    row = lambda b, t: (b, t, 0)
    per_b3 = lambda b, t: (b, 0, 0)
    per_b4 = lambda b, t: (b, 0, 0, 0)
    kern = functools.partial(_dn_kernel, nseq=nseq, tm=tm, chunk=chunk, valid=valid, fused=False)
    return pl.pallas_call(
        kern,
        out_shape=[jax.ShapeDtypeStruct((batch, seq, D_MODEL), o_dtype),
                   jax.ShapeDtypeStruct((batch, SUBLANE, CONV_CH), F32),
                   jax.ShapeDtypeStruct((batch, DN_HEADS, DN_DK, DN_DV), F32)],
        grid=(batch // nseq, nt),
        in_specs=[pl.BlockSpec((nseq, tm, CONV_CH), row),
                  pl.BlockSpec((nseq, tm, D_MODEL), row),
                  pl.BlockSpec((nseq, tm, LANE), row),
                  pl.BlockSpec((SUBLANE, CONV_CH), lambda b, t: (0, 0)),
                  pl.BlockSpec((SUBLANE, LANE), lambda b, t: (0, 0)),
                  pl.BlockSpec((1, DN_DV), lambda b, t: (0, 0)),
                  pl.BlockSpec((nseq, SUBLANE, CONV_CH), per_b3),
                  pl.BlockSpec((nseq, DN_HEADS, DN_DK, DN_DV), per_b4)],
        out_specs=[pl.BlockSpec((nseq, tm, D_MODEL), row),
                   pl.BlockSpec((nseq, SUBLANE, CONV_CH), per_b3),
                   pl.BlockSpec((nseq, DN_HEADS, DN_DK, DN_DV), per_b4)],
        scratch_shapes=[pltpu.VMEM((nseq, CONV_CH // LANE, SUBLANE + tm, LANE), F32)] + _dn_scratch(nseq, tm, chunk),
        compiler_params=_params(("arbitrary", "arbitrary")),
        name="deltanet",
    )(u, z, ba, cw, gp, dnn, conv0, s0)


def _deltanet_fused(q, k, v, z, ba, gp, dnn, s0, *, nseq, tm, chunk, o_dtype):
    batch, seq, _ = q.shape
    nt = seq // tm
    row = lambda b, t: (b, t, 0)
    per_b4 = lambda b, t: (b, 0, 0, 0)
    kern = functools.partial(_dn_kernel, nseq=nseq, tm=tm, chunk=chunk, valid=tm, fused=True)
    wide = pl.BlockSpec((nseq, tm, D_MODEL), row)
    return pl.pallas_call(
        kern,
        out_shape=[jax.ShapeDtypeStruct((batch, seq, D_MODEL), o_dtype),
                   jax.ShapeDtypeStruct((batch, DN_HEADS, DN_DK, DN_DV), F32)],
        grid=(batch // nseq, nt),
        in_specs=[wide, wide, wide, wide,
                  pl.BlockSpec((nseq, tm, LANE), row),
                  pl.BlockSpec((SUBLANE, LANE), lambda b, t: (0, 0)),
                  pl.BlockSpec((1, DN_DV), lambda b, t: (0, 0)),
                  pl.BlockSpec((nseq, DN_HEADS, DN_DK, DN_DV), per_b4)],
        out_specs=[wide, pl.BlockSpec((nseq, DN_HEADS, DN_DK, DN_DV), per_b4)],
        scratch_shapes=_dn_scratch(nseq, tm, chunk),
        compiler_params=_params(("arbitrary", "arbitrary")),
        name="deltanet_fused",
    )(q, k, v, z, ba, gp, dnn, s0)


def _rope(x, cos, sa, sb):
    cols = []
    for cb in range(x.shape[1] // LANE):
        xc = x[:, cb * LANE:(cb + 1) * LANE]
        cols.append(xc * cos + pltpu.roll(xc, LANE - ROT_DIM // 2, 1) * sa + pltpu.roll(xc, ROT_DIM // 2, 1) * sb)
    return cols[0] if len(cols) == 1 else jnp.concatenate(cols, axis=1)


def _head_pair_operands(x):
    lane = lax.broadcasted_iota(jnp.int32, (x.shape[0], LANE), 1)
    out = []
    for cb in range(x.shape[1] // LANE):
        c = x[:, cb * LANE:(cb + 1) * LANE]
        lo = jnp.where(lane < SW_HEAD_DIM, c, 0.0)
        hi = jnp.where(lane >= SW_HEAD_DIM, c, 0.0)
        out.append((lo, pltpu.roll(lo, SW_HEAD_DIM, 1)))
        out.append((pltpu.roll(hi, SW_HEAD_DIM, 1), hi))
    return out


def _ones_columns(w):
    rowi = lax.broadcasted_iota(jnp.int32, (4 * w, LANE), 0)
    lanei = lax.broadcasted_iota(jnp.int32, (4 * w, LANE), 1)
    return jnp.where((rowi < 2 * w) == (lanei < SW_HEAD_DIM), 1.0, 0.0).astype(BF16)


def _pair_softmax(scores, mask, sinks, lane, w):
    parts, es = [], []
    for half in range(2):
        s = jnp.where(mask, scores[:, half * 2 * w:(half + 1) * 2 * w], NEG_INF)
        m = jnp.maximum(jnp.max(s, axis=-1, keepdims=True), sinks[half])
        parts.append(jnp.exp(s - m).astype(BF16))
        es.append(jnp.exp(sinks[half] - m))
    return jnp.concatenate(parts, axis=1), jnp.where(lane < SW_HEAD_DIM, es[0], es[1])


def _swa_prompt_kernel(sinks_ref, qkv_ref, o_ref, kc_ref, vc_ref, kk2, vv2):
    n = pl.program_id(1)
    w = WINDOW

    @pl.when(n == 0)
    def _():
        kk2[...] = jnp.zeros_like(kk2)
        for j in range(SW_KV_HEADS):
            vv2[j, :, 0:LANE] = jnp.zeros((4 * w, LANE), BF16)
            vv2[j, :, LANE:2 * LANE] = _ones_columns(w)

    for j in range(SW_KV_HEADS):
        kk2[j, 0:w, :] = kk2[j, w:2 * w, :]
        kk2[j, 2 * w:3 * w, :] = kk2[j, 3 * w:4 * w, :]
        vv2[j, 0:w, 0:LANE] = vv2[j, w:2 * w, 0:LANE]
        vv2[j, 2 * w:3 * w, 0:LANE] = vv2[j, 3 * w:4 * w, 0:LANE]
    q = qkv_ref[:, 0:D_MODEL].astype(BF16)
    k = qkv_ref[:, D_MODEL:D_MODEL + SW_KV]
    v = qkv_ref[:, D_MODEL + SW_KV:D_MODEL + 2 * SW_KV]
    kc_ref[...] = k
    vc_ref[...] = v
    for j, (left, right) in enumerate(_head_pair_operands(k)):
        kk2[j, w:2 * w, :] = left.astype(BF16)
        kk2[j, 3 * w:4 * w, :] = right.astype(BF16)
    for j, (left, right) in enumerate(_head_pair_operands(v)):
        vv2[j, w:2 * w, 0:LANE] = left.astype(BF16)
        vv2[j, 3 * w:4 * w, 0:LANE] = right.astype(BF16)
    qi = lax.broadcasted_iota(jnp.int32, (w, 2 * w), 0) + w
    sj = lax.broadcasted_iota(jnp.int32, (w, 2 * w), 1)
    d = qi - sj
    mask = (d >= 0) & (d <= WINDOW) & ((sj >= w) | (n > 0))
    lane = lax.broadcasted_iota(jnp.int32, (w, LANE), 1)
    n_pb = D_MODEL // LANE
    scores = [_dot_nt(q[:, pb * LANE:(pb + 1) * LANE], kk2[pb // 2]) for pb in range(n_pb)]
    soft = [_pair_softmax(scores[pb], mask, (sinks_ref[2 * pb], sinks_ref[2 * pb + 1]), lane, w)
            for pb in range(n_pb)]
    outs = [_dot(soft[pb][0], vv2[pb // 2]) for pb in range(n_pb)]
    for pb in range(n_pb):
        o = outs[pb][:, 0:LANE] / (outs[pb][:, LANE:2 * LANE] + soft[pb][1])
        o_ref[:, pb * LANE:(pb + 1) * LANE] = o.astype(o_ref.dtype)


def _swa_prompt(sinks, qkv, *, batch, seq):
    nb = seq // WINDOW
    row = lambda b, n: (b * nb + n, 0)
    per_b = lambda b, n: (b, 0, 0)
    return pl.pallas_call(
        _swa_prompt_kernel,
        out_shape=[jax.ShapeDtypeStruct((batch * seq, D_MODEL), BF16),
                   jax.ShapeDtypeStruct((batch, WINDOW, SW_KV), F32),
                   jax.ShapeDtypeStruct((batch, WINDOW, SW_KV), F32)],
        grid=(batch, nb),
        in_specs=[pl.BlockSpec(memory_space=pltpu.SMEM),
                  pl.BlockSpec((WINDOW, D_MODEL + 2 * SW_KV), row)],
        out_specs=[pl.BlockSpec((WINDOW, D_MODEL), row),
                   pl.BlockSpec((None, WINDOW, SW_KV), per_b),
                   pl.BlockSpec((None, WINDOW, SW_KV), per_b)],
        scratch_shapes=[pltpu.VMEM((SW_KV_HEADS, 4 * WINDOW, LANE), BF16),
                        pltpu.VMEM((SW_KV_HEADS, 4 * WINDOW, 2 * LANE), BF16)],
        compiler_params=_params(("arbitrary", "arbitrary")),
        name="swa_prompt",
    )(sinks, qkv)


def _swa_sample_kernel(sinks_ref, qkv_ref, kbuf_ref, vbuf_ref, cos_ref, sa_ref, sb_ref,
                       o_ref, kc_ref, vc_ref, kk_s, vv_s, kk2, vv2, *, nseq, t_new):
    wb = WINDOW
    tp = SUBLANE

    @pl.when(pl.program_id(0) == 0)
    def _():
        kk2[...] = jnp.zeros_like(kk2)
        for g in range(nseq):
            for j in range(SW_KV_HEADS):
                vv2[g, j, :, 0:LANE] = jnp.zeros((4 * wb, LANE), BF16)
                vv2[g, j, :, LANE:2 * LANE] = _ones_columns(wb)

    cos, sa, sb = cos_ref[...], sa_ref[...], sb_ref[...]
    qs = []
    for g in range(nseq):
        q = (_rope(qkv_ref[g, :, 0:D_MODEL], cos, sa, sb) * (SW_HEAD_DIM ** -0.5)).astype(BF16)
        k_new = _rope(qkv_ref[g, :, D_MODEL:D_MODEL + SW_KV], cos, sa, sb)
        v_new = qkv_ref[g, :, D_MODEL + SW_KV:D_MODEL + 2 * SW_KV]
        k_old, v_old = kbuf_ref[g], vbuf_ref[g]
        kk_s[g, 0:wb, :] = k_old
        vv_s[g, 0:wb, :] = v_old
        kk_s[g, wb:wb + tp, :] = k_new
        vv_s[g, wb:wb + tp, :] = v_new
        kc_ref[g] = kk_s[g, pl.ds(t_new, wb), :]
        vc_ref[g] = vv_s[g, pl.ds(t_new, wb), :]
        for r0, n_r, kx, vx in ((0, wb, k_old, v_old), (wb, tp, k_new, v_new)):
            for j, (left, right) in enumerate(_head_pair_operands(kx)):
                kk2[g, j, r0:r0 + n_r, :] = left.astype(BF16)
                kk2[g, j, 2 * wb + r0:2 * wb + r0 + n_r, :] = right.astype(BF16)
            for j, (left, right) in enumerate(_head_pair_operands(vx)):
                vv2[g, j, r0:r0 + n_r, 0:LANE] = left.astype(BF16)
                vv2[g, j, 2 * wb + r0:2 * wb + r0 + n_r, 0:LANE] = right.astype(BF16)
        qs.append([jnp.concatenate([q[:, (2 * j) * LANE:(2 * j + 1) * LANE],
                                    q[:, (2 * j + 1) * LANE:(2 * j + 2) * LANE]], axis=0)
                   for j in range(SW_KV_HEADS)])
    rows = 2 * tp
    qt = lax.broadcasted_iota(jnp.int32, (rows, 2 * wb), 0) % tp
    sj = lax.broadcasted_iota(jnp.int32, (rows, 2 * wb), 1)
    d = qt + wb - sj
    mask = (d >= 0) & (d <= WINDOW)
    top = lax.broadcasted_iota(jnp.int32, (rows, 1), 0) < tp
    lane = lax.broadcasted_iota(jnp.int32, (rows, LANE), 1)
    items = [(g, j) for g in range(nseq) for j in range(SW_KV_HEADS)]
    scores = [_dot_nt(qs[g][j], kk2[g, j]) for g, j in items]
    soft = []
    for n, (g, j) in enumerate(items):
        sinks = [jnp.where(top, sinks_ref[4 * j + half], sinks_ref[4 * j + 2 + half]) for half in range(2)]
        soft.append(_pair_softmax(scores[n], mask, sinks, lane, wb))
    outs = [_dot(soft[n][0], vv2[g, j]) for n, (g, j) in enumerate(items)]
    for n, (g, j) in enumerate(items):
        o = outs[n][:, 0:LANE] / (outs[n][:, LANE:2 * LANE] + soft[n][1])
        o_ref[g, :, (2 * j) * LANE:(2 * j + 1) * LANE] = o[0:tp, :]
        o_ref[g, :, (2 * j + 1) * LANE:(2 * j + 2) * LANE] = o[tp:2 * tp, :]


def _swa_sample(sinks, qkv, kbuf, vbuf, cos, sa, sb, *, batch, nseq, t_new):
    per_b = lambda b: (b, 0, 0)
    tab = lambda b: (0, 0)
    kern = functools.partial(_swa_sample_kernel, nseq=nseq, t_new=t_new)
    return pl.pallas_call(
        kern,
        out_shape=[jax.ShapeDtypeStruct((batch, SUBLANE, D_MODEL), F32),
                   jax.ShapeDtypeStruct((batch, WINDOW, SW_KV), F32),
                   jax.ShapeDtypeStruct((batch, WINDOW, SW_KV), F32)],
        grid=(batch // nseq,),
        in_specs=[pl.BlockSpec(memory_space=pltpu.SMEM),
                  pl.BlockSpec((nseq, SUBLANE, D_MODEL + 2 * SW_KV), per_b),
                  pl.BlockSpec((nseq, WINDOW, SW_KV), per_b),
                  pl.BlockSpec((nseq, WINDOW, SW_KV), per_b),
                  pl.BlockSpec((SUBLANE, LANE), tab),
                  pl.BlockSpec((SUBLANE, LANE), tab),
                  pl.BlockSpec((SUBLANE, LANE), tab)],
        out_specs=[pl.BlockSpec((nseq, SUBLANE, D_MODEL), per_b),
                   pl.BlockSpec((nseq, WINDOW, SW_KV), per_b),
                   pl.BlockSpec((nseq, WINDOW, SW_KV), per_b)],
        scratch_shapes=[pltpu.VMEM((nseq, WINDOW + SUBLANE, SW_KV), F32),
                        pltpu.VMEM((nseq, WINDOW + SUBLANE, SW_KV), F32),
                        pltpu.VMEM((nseq, SW_KV_HEADS, 4 * WINDOW, LANE), BF16),
                        pltpu.VMEM((nseq, SW_KV_HEADS, 4 * WINDOW, 2 * LANE), BF16)],
        compiler_params=_params(("arbitrary",)),
        name="swa_sample",
    )(sinks, qkv, kbuf, vbuf, cos, sa, sb)


def _out_ffn_kernel(x_ref, g2_ref, post2_ref, ga_ref, gb_ref, odn_ref, osw_ref, wo_ref,
                    sh_ref, sc_ref, g_ref, pre_ref, post_ref, wg_ref, wu_ref, wd_ref, o_ref,
                    *, per_token, tiles_per_seq):
    rows = _row_groups(x_ref.shape[0])
    g2 = _mod_rows(g2_ref, per_token, tiles_per_seq)
    sh = _mod_rows(sh_ref, per_token, tiles_per_seq)
    sc = _mod_rows(sc_ref, per_token, tiles_per_seq)
    g = _mod_rows(g_ref, per_token, tiles_per_seq)
    pick = (lambda m, r: m[r]) if per_token else (lambda m, r: m)
    f32 = lambda ref, r: ref[r, :].astype(F32)
    ys = [(_sigmoid(f32(ga_ref, r)) * f32(odn_ref, r)
           + _sigmoid(f32(gb_ref, r)) * f32(osw_ref, r)).astype(BF16) for r in rows]
    ps = [_dot(y, wo_ref[...]) for y in ys]
    xs = [x_ref[r, :] + pick(g2, r) * _rms(p, post2_ref[...]) for p, r in zip(ps, rows)]
    outs = _ffn_stages(xs, rows, pick, sh, sc, g, pre_ref[...], post_ref[...], wg_ref, wu_ref, wd_ref)
    for o, r in zip(outs, rows):
        o_ref[r, :] = o


def _out_ffn(x, mod, mod_block, seq_len, post2, gates, o_dn, o_sw, w_out, pre, post, wg, wu, wd):
    n = x.shape[0]
    tm = min(512, n)
    dff = wg.shape[1]
    kern = functools.partial(_out_ffn_kernel, per_token=mod_block is None, tiles_per_seq=max(seq_len // tm, 1))
    row = lambda i: (i, 0)
    return pl.pallas_call(
        kern,
        out_shape=jax.ShapeDtypeStruct((n, D_MODEL), F32),
        grid=(n // tm,),
        in_specs=[pl.BlockSpec((tm, D_MODEL), row)]
        + _mod_specs(mod_block, tm, (5,))
        + [_const_spec((1, D_MODEL)),
           pl.BlockSpec((tm, D_MODEL), lambda i: (i, 0)),
           pl.BlockSpec((tm, D_MODEL), lambda i: (i, 1)),
           pl.BlockSpec((tm, D_MODEL), row),
           pl.BlockSpec((tm, D_MODEL), row),
           _const_spec((D_MODEL, D_MODEL))]
        + _mod_specs(mod_block, tm, (6, 7, 8))
        + [_const_spec((1, D_MODEL)), _const_spec((1, D_MODEL)),
           _const_spec((D_MODEL, dff)), _const_spec((D_MODEL, dff)), _const_spec((dff, D_MODEL))],
        out_specs=pl.BlockSpec((tm, D_MODEL), row),
        compiler_params=_params(("arbitrary",)),
        name="out_ffn",
    )(x, mod, post2, gates, gates, o_dn, o_sw, w_out, mod, mod, mod, pre, post, wg, wu, wd)


def _rope_tables(pos):
    half = ROT_DIM // 2
    inv_freq = ROPE_THETA ** (-jnp.arange(half, dtype=F32) * (2.0 / ROT_DIM))
    ang = pos.astype(F32)[:, None] * inv_freq[None, :]
    cos, sin = jnp.cos(ang), jnp.sin(ang)
    n = pos.shape[0]
    rest = SW_HEAD_DIM - ROT_DIM
    c64 = jnp.concatenate([cos, cos, jnp.ones((n, rest), F32)], axis=1)
    a64 = jnp.concatenate([-sin, jnp.zeros((n, half + rest), F32)], axis=1)
    b64 = jnp.concatenate([jnp.zeros((n, half), F32), sin, jnp.zeros((n, rest), F32)], axis=1)
    rep = LANE // SW_HEAD_DIM
    return tuple(jnp.tile(t, (1, rep)) for t in (c64, a64, b64))


def _pad_rows(a, rows):
    return jnp.pad(a, ((0, rows - a.shape[0]),) + ((0, 0),) * (a.ndim - 1))


def _layer_sample(x, mod, seq, rows, wts, conv0, s0, dn_fn, swa_fn):
    def per_seq(a):
        a = a.reshape(-1, seq, a.shape[-1])
        return a if rows == seq else jnp.pad(a, ((0, 0), (0, rows - seq), (0, 0)))

    def flat(a):
        return a[:, :seq].reshape(-1, a.shape[-1])

    x = _ffn(x, mod, 0, None, seq, wts["pre1"], wts["post1"], wts["wg1"], wts["wu1"], wts["wd1"])
    u, z, qkv, gates, ba = _proj(x, mod, None, seq, wts["pre2"], *wts["w_in"])
    o_dn, conv_new, s_new = dn_fn(per_seq(u), per_seq(z), per_seq(ba), wts["conv_w"], wts["gparam"],
                                  wts["dn_norm"], conv0, s0)
    o_sw, k_new, v_new = swa_fn(per_seq(qkv))
    x = _out_ffn(x, mod, None, seq, wts["post2"], gates, flat(o_dn), flat(o_sw), wts["w_out"],
                 wts["pre3"], wts["post3"], wts["wg2"], wts["wu2"], wts["wd2"])
    return x, (k_new, v_new, conv_new, s_new)


def _layer_prompt(x, mod, mod_block, batch, seq, wts, s0, rope, sink):
    per_seq = lambda a: a.reshape(batch, seq, a.shape[-1])
    x = _ffn(x, mod, 0, mod_block, seq, wts["pre1"], wts["post1"], wts["wg1"], wts["wu1"], wts["wd1"])
    q, k, v, z, qkv, gates, ba, conv_new = _proj_prompt(x, mod, mod_block, seq, wts["pre2"], *wts["w_in"],
                                                        wts["conv_w"], *rope)
    o_dn, s_new = _deltanet_fused(per_seq(q), per_seq(k), per_seq(v), per_seq(z), per_seq(ba), wts["gparam"],
                                  wts["dn_norm"], s0, nseq=batch, tm=DN_TILE, chunk=DN_CHUNK, o_dtype=BF16)
    o_sw, k_new, v_new = _swa_prompt(sink, qkv, batch=batch, seq=seq)
    x = _out_ffn(x, mod, mod_block, seq, wts["post2"], gates, o_dn.reshape(batch * seq, D_MODEL), o_sw,
                 wts["w_out"], wts["pre3"], wts["post3"], wts["wg2"], wts["wu2"], wts["wd2"])
    return x, (k_new, v_new, conv_new, s_new)


def kernel(x_prompt, x_sample, cache_swa_k, cache_swa_v, state_conv, state_delta, c_prompt, c_sample,
           w_ada, b_ada, ffn1_norm_pre, ffn1_norm_post, ffn1_w_gate, ffn1_w_up, ffn1_w_down,
           mix_norm_pre, mix_norm_post, w_in, conv_w, a_log, dt_bias, dn_norm, sinks, w_out,
           ffn2_norm_pre, ffn2_norm_post, ffn2_w_gate, ffn2_w_up, ffn2_w_down):
    depth = w_ada.shape[0]
    bp, tp, _ = x_prompt.shape
    bs, ts, _ = x_sample.shape
    assert tp % 512 == 0 and 1 <= ts <= SUBLANE and bs % SAMPLE_SEQS == 0 and (bs * ts) % SUBLANE == 0

    cos_p, sa_p, sb_p = _rope_tables(jnp.arange(tp))
    cos_s, sa_s, sb_s = (_pad_rows(t, SUBLANE) for t in _rope_tables(PAST_LEN + jnp.arange(ts)))

    xp = x_prompt.reshape(bp * tp, D_MODEL)
    xs = x_sample.reshape(bs * ts, D_MODEL)
    c_all = jnp.concatenate([jnp.repeat(c_sample, ts, axis=0), _pad_rows(c_prompt, SUBLANE)], axis=0)
    prompt_mod_block = bs * ts // SUBLANE

    outs_p, outs_s = [], []
    for l in range(depth):
        wi = w_in[l]
        n_ba = 2 * DN_HEADS
        w_in_a = wi[:, :P_QKV].astype(BF16)
        w_in_b = wi[:, P_QKV + n_ba:].astype(BF16)
        w_in_c = jnp.pad(wi[:, P_QKV:P_QKV + n_ba], ((0, 0), (0, LANE - n_ba))).astype(BF16)
        gparam = jnp.zeros((SUBLANE, LANE), F32)
        gparam = gparam.at[0, DN_HEADS:2 * DN_HEADS].set(dt_bias[l]).at[1, DN_HEADS:2 * DN_HEADS].set(a_log[l])
        wts = dict(
            pre1=ffn1_norm_pre[l][None], post1=ffn1_norm_post[l][None],
            wg1=ffn1_w_gate[l].astype(BF16), wu1=ffn1_w_up[l].astype(BF16), wd1=ffn1_w_down[l].astype(BF16),
            pre2=mix_norm_pre[l][None], post2=mix_norm_post[l][None], w_in=(w_in_a, w_in_b, w_in_c),
            conv_w=_pad_rows(conv_w[l], SUBLANE), gparam=gparam, dn_norm=dn_norm[l][None],
            w_out=w_out[l].astype(BF16),
            pre3=ffn2_norm_pre[l][None], post3=ffn2_norm_post[l][None],
            wg2=ffn2_w_gate[l].astype(BF16), wu2=ffn2_w_up[l].astype(BF16), wd2=ffn2_w_down[l].astype(BF16),
        )
        b_ada_l = b_ada[l][None]
        mod = _ada(c_all, w_ada[l], b_ada_l)
        sink_l = sinks[l]

        s0_p = jnp.zeros((bp, DN_HEADS, DN_DK, DN_DV), F32)
        xp, st_p = _layer_prompt(xp, mod, prompt_mod_block, bp, tp, wts, s0_p, (cos_p, sa_p, sb_p), sink_l)

        conv0_s = jnp.pad(state_conv[l], ((0, 0), (SUBLANE - CONV_W + 1, 0), (0, 0)))
        kbuf = cache_swa_k[l].reshape(bs, WINDOW, SW_KV)
        vbuf = cache_swa_v[l].reshape(bs, WINDOW, SW_KV)
        dn_s = functools.partial(_deltanet, nseq=SAMPLE_SEQS, tm=SUBLANE, chunk=SUBLANE, valid=ts, o_dtype=F32)

        swa_s = lambda qkv: _swa_sample(sink_l, qkv, kbuf, vbuf, cos_s, sa_s, sb_s,
                                        batch=bs, nseq=SAMPLE_SEQS, t_new=ts)
        xs, st_s = _layer_sample(xs, mod, ts, SUBLANE, wts, conv0_s, state_delta[l], dn_s, swa_s)
        outs_p.append(st_p)
        outs_s.append(st_s)

    def stack(outs, i):
        return jnp.stack([o[i] for o in outs])

    def kv5(a, batch):
        return a.reshape(depth, batch, WINDOW, SW_KV_HEADS, SW_HEAD_DIM)

    tail = slice(SUBLANE - CONV_W + 1, SUBLANE)
    y_p = xp.reshape(bp, tp, D_MODEL)
    y_s = xs.reshape(bs, ts, D_MODEL)
    return (y_p, y_s,
            kv5(stack(outs_p, 0), bp), kv5(stack(outs_p, 1), bp),
            stack(outs_p, 2)[:, :, tail], stack(outs_p, 3),
            kv5(stack(outs_s, 0), bs), kv5(stack(outs_s, 1), bs),
            stack(outs_s, 2)[:, :, tail], stack(outs_s, 3))
```

```python
import functools

import jax
import jax.numpy as jnp
from jax import lax
from jax.experimental import pallas as pl
from jax.experimental.pallas import tpu as pltpu

F32, BF16 = jnp.float32, jnp.bfloat16

D_MODEL = 1024
DN_HEADS, DN_DK, DN_DV = 8, 128, 128
CONV_W = 4
CONV_CH = 3 * D_MODEL
DN_CHUNK = 64
SW_HEAD_DIM, SW_HEADS, SW_KV_HEADS, SW_GROUP = 64, 16, 4, 4
SW_KV = SW_KV_HEADS * SW_HEAD_DIM
WINDOW = 128
ROT_DIM = 16
ROPE_THETA = 500000.0
PAST_LEN = 8192
EPS = 1e-6
NEG_INF = -1e30

LANE = 128
SUBLANE = 8
VMEM_LIMIT = 56 * 1024 * 1024

P_U, P_Z, P_QKV, P_G, P_BA = 0, 3072, 4096, 5632, 7680
P_TOTAL = 7808

DN_TILE = 128
DN_GROUP = 4
SAMPLE_SEQS = 8
FFN_SUBTILES = 2
ADA_COLS = 1024
PROJ_FUSED_TILE = 512


def _dot(a, b):
    return jnp.dot(a, b, preferred_element_type=F32)


def _dot_nt(a, b):
    return lax.dot_general(a, b, (((1,), (1,)), ((), ())), preferred_element_type=F32)


def _dot_tn(a, b):
    return lax.dot_general(a, b, (((0,), (0,)), ((), ())), preferred_element_type=F32)


def _split3(x):
    hi = x.astype(BF16)
    r1 = x - hi.astype(F32)
    mid = r1.astype(BF16)
    lo = (r1 - mid.astype(F32)).astype(BF16)
    return hi, mid, lo


def _dot_exact_lhs(a_bf, b):
    return sum(_dot(a_bf, t) for t in _split3(b))


def _dot_nt_exact_lhs(a_bf, b):
    return sum(_dot_nt(a_bf, t) for t in _split3(b))


def _mm_bf(a, b):
    return _dot(a.astype(BF16), b.astype(BF16))


def _sigmoid(x):
    return 0.5 * jnp.tanh(0.5 * x) + 0.5


def _silu(x):
    return x * _sigmoid(x)


def _rms(x, gain):
    ms = jnp.mean(x * x, axis=-1, keepdims=True)
    return x * lax.rsqrt(ms + EPS) * gain


def _mod_rows(ref, per_token, tiles_per_seq):
    if per_token:
        return ref[...]
    b = pl.program_id(0) // tiles_per_seq
    return ref[pl.ds(b, 1), :]


def _const_spec(shape):
    return pl.BlockSpec(shape, lambda *_: (0,) * len(shape), pipeline_mode=pl.Buffered(1))


def _params(sem):
    return pltpu.CompilerParams(dimension_semantics=sem, vmem_limit_bytes=VMEM_LIMIT)


def _ada_kernel(c_ref, w_ref, b_ref, o_ref):
    h = _silu(c_ref[...]).astype(BF16)
    o_ref[...] = _dot(h, w_ref[...].astype(BF16)) + b_ref[...]


def _ada(c, w, b):
    rows, n = c.shape[0], w.shape[1]
    tn = ADA_COLS
    return pl.pallas_call(
        _ada_kernel,
        out_shape=jax.ShapeDtypeStruct((rows, n), F32),
        grid=(n // tn,),
        in_specs=[
            pl.BlockSpec((rows, D_MODEL), lambda j: (0, 0)),
            pl.BlockSpec((D_MODEL, tn), lambda j: (0, j)),
            pl.BlockSpec((1, tn), lambda j: (0, j)),
        ],
        out_specs=pl.BlockSpec((rows, tn), lambda j: (0, j)),
        compiler_params=_params(("arbitrary",)),
        name="ada",
    )(c, w, b)


def _mod_specs(mod_block, tm, cols):
    if mod_block is None:
        return [pl.BlockSpec((tm, D_MODEL), functools.partial(lambda i, c: (i, c), c=c)) for c in cols]
    return [pl.BlockSpec((SUBLANE, D_MODEL), functools.partial(lambda i, c: (mod_block, c), c=c)) for c in cols]


def _ffn_stages(xs, rows, pick, sh, sc, g, pre, post, wg_ref, wu_ref, wd_ref):
    hs = [(_rms(x, pre) * (1.0 + pick(sc, r)) + pick(sh, r)).astype(BF16) for x, r in zip(xs, rows)]
    gates = [_dot(h, wg_ref[...]) for h in hs]
    ups = [_dot(h, wu_ref[...]) for h in hs]
    acts = [(_silu(a) * b).astype(BF16) for a, b in zip(gates, ups)]
    ys = [_dot(a, wd_ref[...]) for a in acts]
    return [x + 0.5 * pick(g, r) * _rms(y, post) for x, y, r in zip(xs, ys, rows)]


def _row_groups(tm):
    return [slice(i * tm // FFN_SUBTILES, (i + 1) * tm // FFN_SUBTILES) for i in range(FFN_SUBTILES)]


def _ffn_kernel(x_ref, sh_ref, sc_ref, g_ref, pre_ref, post_ref, wg_ref, wu_ref, wd_ref, o_ref,
                *, per_token, tiles_per_seq):
    rows = _row_groups(x_ref.shape[0])
    sh = _mod_rows(sh_ref, per_token, tiles_per_seq)
    sc = _mod_rows(sc_ref, per_token, tiles_per_seq)
    g = _mod_rows(g_ref, per_token, tiles_per_seq)
    pick = (lambda m, r: m[r]) if per_token else (lambda m, r: m)
    xs = [x_ref[r, :] for r in rows]
    outs = _ffn_stages(xs, rows, pick, sh, sc, g, pre_ref[...], post_ref[...], wg_ref, wu_ref, wd_ref)
    for o, r in zip(outs, rows):
        o_ref[r, :] = o


def _ffn(x, mod, col0, mod_block, seq_len, pre, post, wg, wu, wd):
    n = x.shape[0]
    tm = min(512, n)
    dff = wg.shape[1]
    kern = functools.partial(_ffn_kernel, per_token=mod_block is None, tiles_per_seq=max(seq_len // tm, 1))
    row = lambda i: (i, 0)
    return pl.pallas_call(
        kern,
        out_shape=jax.ShapeDtypeStruct((n, D_MODEL), F32),
        grid=(n // tm,),
        in_specs=[pl.BlockSpec((tm, D_MODEL), row)]
        + _mod_specs(mod_block, tm, (col0, col0 + 1, col0 + 2))
        + [_const_spec((1, D_MODEL)), _const_spec((1, D_MODEL)),
           _const_spec((D_MODEL, dff)), _const_spec((D_MODEL, dff)), _const_spec((dff, D_MODEL))],
        out_specs=pl.BlockSpec((tm, D_MODEL), row),
        compiler_params=_params(("arbitrary",)),
        name="ffn",
    )(x, mod, mod, mod, pre, post, wg, wu, wd)


def _proj_kernel(x_ref, sh_ref, sc_ref, pre_ref, wa_ref, wb_ref, wc_ref, u_ref, z_ref, qkv_ref, g_ref, ba_ref,
                 *, per_token, tiles_per_seq):
    sh = _mod_rows(sh_ref, per_token, tiles_per_seq)
    sc = _mod_rows(sc_ref, per_token, tiles_per_seq)
    h = (_rms(x_ref[...], pre_ref[...]) * (1.0 + sc) + sh).astype(BF16)
    u_ref[...] = _dot(h, wa_ref[:, 0:P_Z])
    z_ref[...] = _dot(h, wa_ref[:, P_Z:P_QKV])
    qkv_ref[...] = _dot(h, wb_ref[:, 0:P_G - P_QKV])
    g_ref[...] = _dot(h, wb_ref[:, P_G - P_QKV:P_BA - P_QKV]).astype(g_ref.dtype)
    ba_ref[...] = _dot(h, wc_ref[...])


def _proj(x, mod, mod_block, seq_len, pre, wa, wb, wc):
    n = x.shape[0]
    tm = min(512, n)
    kern = functools.partial(_proj_kernel, per_token=mod_block is None, tiles_per_seq=max(seq_len // tm, 1))
    row = lambda i: (i, 0)
    widths = (P_Z - P_U, P_QKV - P_Z, P_G - P_QKV, P_BA - P_G, P_TOTAL - P_BA)
    return pl.pallas_call(
        kern,
        out_shape=[jax.ShapeDtypeStruct((n, wd), BF16 if wd == P_BA - P_G else F32) for wd in widths],
        grid=(n // tm,),
        in_specs=[pl.BlockSpec((tm, D_MODEL), row)]
        + _mod_specs(mod_block, tm, (3, 4))
        + [_const_spec((1, D_MODEL)), _const_spec(wa.shape), _const_spec(wb.shape), _const_spec(wc.shape)],
        out_specs=[pl.BlockSpec((tm, wd), row) for wd in widths],
        compiler_params=_params(("arbitrary",)),
        name="proj",
    )(x, mod, mod, pre, wa, wb, wc)


def _conv_norm(u_cols, ubuf_cb, cw_ref, cols, rows=None):
    rows = u_cols.shape[0] if rows is None else rows
    ubuf_cb[SUBLANE:SUBLANE + u_cols.shape[0], :] = u_cols
    acc = ubuf_cb[pl.ds(SUBLANE - CONV_W + 1, rows), :] * cw_ref[0:1, cols]
    for i in range(1, CONV_W):
        acc = acc + ubuf_cb[pl.ds(SUBLANE - CONV_W + 1 + i, rows), :] * cw_ref[i:i + 1, cols]
    return _silu(acc)


def _store_qkv(a, cb, q_dst, k_dst, v_dst, rows):
    sec, off = divmod(cb * LANE, D_MODEL)
    dst = slice(off, off + LANE)
    if sec == 2:
        v_dst[rows, dst] = a
    else:
        an = a * lax.rsqrt(jnp.sum(a * a, axis=-1, keepdims=True) + EPS)
        if sec == 0:
            q_dst[rows, dst] = an * (DN_DK ** -0.5)
        else:
            k_dst[rows, dst] = an


def _proj_prompt_kernel(x_ref, sh_ref, sc_ref, pre_ref, wa_ref, wb_ref, wc_ref, cw_ref, cos_ref, sa_ref, sb_ref,
                        q_ref, k_ref, v_ref, z_ref, qkv_ref, g_ref, ba_ref, tail_ref, ubuf, *, tiles_per_seq):
    tm = x_ref.shape[0]
    n_cb = CONV_CH // LANE
    all_rows = slice(0, tm)

    @pl.when(pl.program_id(0) % tiles_per_seq == 0)
    def _():
        ubuf[:, 0:SUBLANE, :] = jnp.zeros((n_cb, SUBLANE, LANE), F32)

    sh = _mod_rows(sh_ref, False, tiles_per_seq)
    sc = _mod_rows(sc_ref, False, tiles_per_seq)
    h = (_rms(x_ref[...], pre_ref[...]) * (1.0 + sc) + sh).astype(BF16)
    u = _dot(h, wa_ref[:, 0:P_Z])
    z_ref[...] = _dot(h, wa_ref[:, P_Z:P_QKV])
    s = _dot(h, wb_ref[:, 0:P_G - P_QKV])
    g_ref[...] = _dot(h, wb_ref[:, P_G - P_QKV:P_BA - P_QKV]).astype(g_ref.dtype)
    ba_ref[...] = _dot(h, wc_ref[...])
    for cb in range(n_cb):
        cols = slice(cb * LANE, (cb + 1) * LANE)
        a = _conv_norm(u[:, cols], ubuf.at[cb], cw_ref, cols)
        _store_qkv(a, cb, q_ref, k_ref, v_ref, all_rows)
        tail = ubuf[cb, pl.ds(tm, SUBLANE), :]
        ubuf[cb, 0:SUBLANE, :] = tail
        tail_ref[:, cols] = tail
    cos, sa, sb = cos_ref[...], sa_ref[...], sb_ref[...]
    qkv_ref[:, 0:D_MODEL] = _rope(s[:, 0:D_MODEL], cos, sa, sb) * (SW_HEAD_DIM ** -0.5)
    qkv_ref[:, D_MODEL:D_MODEL + SW_KV] = _rope(s[:, D_MODEL:D_MODEL + SW_KV], cos, sa, sb)
    qkv_ref[:, D_MODEL + SW_KV:D_MODEL + 2 * SW_KV] = s[:, D_MODEL + SW_KV:D_MODEL + 2 * SW_KV]


def _proj_prompt(x, mod, mod_block, seq_len, pre, wa, wb, wc, cw, cos, sa, sb):
    n = x.shape[0]
    tm = PROJ_FUSED_TILE
    tps = seq_len // tm
    kern = functools.partial(_proj_prompt_kernel, tiles_per_seq=tps)
    row = lambda i: (i, 0)
    tab = lambda i: (i % tps, 0)
    widths = (D_MODEL, D_MODEL, D_MODEL, P_QKV - P_Z, P_G - P_QKV, P_BA - P_G, P_TOTAL - P_BA)
    dtypes = (F32, F32, F32, F32, F32, BF16, F32)
    return pl.pallas_call(
        kern,
        out_shape=[jax.ShapeDtypeStruct((n, wd), dt) for wd, dt in zip(widths, dtypes)]
        + [jax.ShapeDtypeStruct((n // seq_len, SUBLANE, CONV_CH), F32)],
        grid=(n // tm,),
        in_specs=[pl.BlockSpec((tm, D_MODEL), row)]
        + _mod_specs(mod_block, tm, (3, 4))
        + [_const_spec((1, D_MODEL)), _const_spec(wa.shape), _const_spec(wb.shape), _const_spec(wc.shape),
           _const_spec(cw.shape),
           pl.BlockSpec((tm, LANE), tab), pl.BlockSpec((tm, LANE), tab), pl.BlockSpec((tm, LANE), tab)],
        out_specs=[pl.BlockSpec((tm, wd), row) for wd in widths]
        + [pl.BlockSpec((None, SUBLANE, CONV_CH), lambda i: (i // tps, 0, 0))],
        scratch_shapes=[pltpu.VMEM((CONV_CH // LANE, SUBLANE + tm, LANE), F32)],
        compiler_params=_params(("arbitrary",)),
        name="proj_prompt",
    )(x, mod, mod, pre, wa, wb, wc, cw, cos, sa, sb)


def _tri_inv_all(lows, c):
    ii = lax.broadcasted_iota(jnp.int32, (c, c), 0)
    jj = lax.broadcasted_iota(jnp.int32, (c, c), 1)
    eye = jnp.where(ii == jj, 1.0, 0.0).astype(F32)
    xs = [eye - jnp.where(ii // 2 == jj // 2, low, 0.0) for low in lows]
    b = 2
    while b < c:
        join = (ii // (2 * b) == jj // (2 * b)) & (ii // b != jj // b)
        ys = [_mm_bf(x, jnp.where(join, low, 0.0)) for x, low in zip(xs, lows)]
        xs = [x - _mm_bf(y, x) for x, y in zip(xs, ys)]
        b *= 2
    return xs


def _dn_kernel(*refs, nseq, tm, chunk, valid, fused):
    if fused:
        (q_ref, k_ref, v_ref, z_ref, ba_ref, gp_ref, dnn_ref, s0_ref, o_ref, so_ref,
         q_s, k_s, v_s, bg_s, u_s, w_s, qe_s, kd_s, a_s, eg_s) = refs
    else:
        (u_ref, z_ref, ba_ref, cw_ref, gp_ref, dnn_ref, conv0_ref, s0_ref, o_ref, convo_ref, so_ref,
         ubuf, q_s, k_s, v_s, bg_s, u_s, w_s, qe_s, kd_s, a_s, eg_s) = refs
    t = pl.program_id(1)
    rows_all = nseq * tm
    n_cb = CONV_CH // LANE

    @pl.when(t == 0)
    def _():
        so_ref[...] = s0_ref[...]
        if not fused:
            for g in range(nseq):
                for cb in range(n_cb):
                    ubuf[g, cb, 0:SUBLANE, :] = conv0_ref[g, :, cb * LANE:(cb + 1) * LANE]
                    if valid < tm:
                        ubuf[g, cb, SUBLANE + valid:SUBLANE + tm, :] = jnp.zeros((tm - valid, LANE), F32)

    for g in range(nseq):
        rws = slice(g * tm, (g + 1) * tm)
        if fused:
            q_s[rws, :] = q_ref[g]
            k_s[rws, :] = k_ref[g]
            v_s[rws, :] = v_ref[g]
            continue
        for cb in range(n_cb):
            cols = slice(cb * LANE, (cb + 1) * LANE)
            a = _conv_norm(u_ref[g, :, cols], ubuf.at[g, cb], cw_ref, cols, rows=tm)
            _store_qkv(a, cb, q_s, k_s, v_s, rws)
            tail = ubuf[g, cb, pl.ds(valid, SUBLANE), :]
            ubuf[g, cb, 0:SUBLANE, :] = tail
            convo_ref[g, :, cols] = tail

    lane = lax.broadcasted_iota(jnp.int32, (valid, LANE), 1)
    for g in range(nseq):
        ba = ba_ref[g]
        xg = ba + gp_ref[0:1, :]
        softplus = jnp.maximum(xg, 0.0) + jnp.log1p(jnp.exp(-jnp.abs(xg)))
        gdec = -jnp.exp(gp_ref[1:2, :]) * softplus
        bg_s[g * tm:g * tm + valid, :] = jnp.where(lane < DN_HEADS, _sigmoid(ba), gdec)
        if valid < tm:
            bg_s[g * tm + valid:(g + 1) * tm, :] = jnp.zeros((tm - valid, LANE), F32)

    c = chunk
    ii = lax.broadcasted_iota(jnp.int32, (c, c), 0)
    jj = lax.broadcasted_iota(jnp.int32, (c, c), 1)
    incl = ii >= jj
    strict = ii > jj
    tril = jnp.where(incl, 1.0, 0.0).astype(BF16)
    sel = jnp.where(lax.broadcasted_iota(jnp.int32, (SUBLANE, LANE), 1)
                    == lax.broadcasted_iota(jnp.int32, (SUBLANE, LANE), 0) + DN_HEADS, 1.0, 0.0).astype(BF16)
    dnn = dnn_ref[...]

    heads = range(DN_HEADS)
    hcols = [slice(h * DN_DK, (h + 1) * DN_DK) for h in heads]
    acols = [slice(h * c, (h + 1) * c) for h in heads]
    n_chunks = rows_all // c
    chunks_per_seq = tm // c
    gpc = DN_GROUP if n_chunks % DN_GROUP == 0 else 1

    def chunk_rows(ci):
        return pl.ds(ci * c, c) if isinstance(ci, int) else pl.ds(pl.multiple_of(ci * c, c), c)

    def prep(cis):
        n_c = len(cis)
        rows = [chunk_rows(ci) for ci in cis]
        pairs = [(j, h) for j in range(n_c) for h in heads]
        bgc = [bg_s[r, :] for r in rows]
        gcum = [_dot_exact_lhs(tril, b) for b in bgc]
        gcum_t = [_dot_nt_exact_lhs(sel, g) for g in gcum]
        for j in range(n_c):
            eg_s[cis[j]] = jnp.broadcast_to(jnp.exp(gcum_t[j][:, c - 1:c]), (DN_HEADS, DN_DV))
        spread = lambda col: jnp.broadcast_to(col, (c, DN_DK))
        beta = [spread(bgc[j][:, h:h + 1]) for j, h in pairs]
        gc = [spread(gcum[j][:, DN_HEADS + h:DN_HEADS + h + 1]) for j, h in pairs]
        decay = [jnp.where(incl, jnp.exp(jnp.where(incl, gc[p][:, :c] - gcum_t[j][h:h + 1, :], 0.0)), 0.0)
                 for p, (j, h) in enumerate(pairs)]
        q = [q_s[rows[j], hcols[h]] for j, h in pairs]
        k = [k_s[rows[j], hcols[h]] for j, h in pairs]
        npair = range(len(pairs))
        kb = [k[p] * beta[p] for p in npair]
        k_bf = [k[p].astype(BF16) for p in npair]
        akk = [_dot_nt(kb[p].astype(BF16), k_bf[p]) for p in npair]
        aqk = [_dot_nt(q[p].astype(BF16), k_bf[p]) for p in npair]
        tinv = _tri_inv_all([jnp.where(strict, akk[p] * decay[p], 0.0) for p in npair], c)
        egc = [jnp.exp(gc[p]) for p in npair]
        rhs = [jnp.concatenate([v_s[rows[j], hcols[h]] * beta[p], kb[p] * egc[p]], axis=1)
               for p, (j, h) in enumerate(pairs)]
        uw = [_mm_bf(tinv[p], rhs[p]) for p in npair]
        sdt = w_s.dtype
        for p, (j, h) in enumerate(pairs):
            u_s[rows[j], hcols[h]] = uw[p][:, :DN_DV]
            w_s[rows[j], hcols[h]] = uw[p][:, DN_DV:].astype(sdt)
            qe_s[rows[j], hcols[h]] = (q[p] * egc[p]).astype(sdt)
            kd_s[rows[j], hcols[h]] = (k[p] * jnp.exp(gc[p][c - 1:c, :] - gc[p])).astype(sdt)
            a_s[rows[j], acols[h]] = jnp.where(incl, aqk[p] * decay[p], 0.0).astype(sdt)

    def state(cpos):
        trip = [(g, h) for g in range(nseq) for h in heads]
        rows = [chunk_rows(g * chunks_per_seq + cpos) for g in range(nseq)]
        local = chunk_rows(cpos)
        egb = [eg_s[g * chunks_per_seq + cpos] for g in range(nseq)]
        s_old = [so_ref[g, h] for g, h in trip]
        s_bf = [s.astype(BF16) for s in s_old]
        ws = [_dot(w_s[rows[g], hcols[h]].astype(BF16), s_bf[n]) for n, (g, h) in enumerate(trip)]
        qs = [_dot(qe_s[rows[g], hcols[h]].astype(BF16), s_bf[n]) for n, (g, h) in enumerate(trip)]
        vn_bf = [(u_s[rows[g], hcols[h]] - ws[n]).astype(BF16) for n, (g, h) in enumerate(trip)]
        av = [_dot(a_s[rows[g], acols[h]].astype(BF16), vn_bf[n]) for n, (g, h) in enumerate(trip)]
        kv = [_dot_tn(kd_s[rows[g], hcols[h]].astype(BF16), vn_bf[n]) for n, (g, h) in enumerate(trip)]
        for n, (g, h) in enumerate(trip):
            so_ref[g, h] = s_old[n] * egb[g][h:h + 1, :] + kv[n]
            o = _rms(qs[n] + av[n], dnn)
            if valid < tm:
                o_ref[g, :, hcols[h]] = (o[0:valid] * _silu(z_ref[g, :, hcols[h]])).astype(o_ref.dtype)
            else:
                o_ref[g, local, hcols[h]] = (o * _silu(z_ref[g, local, hcols[h]])).astype(o_ref.dtype)

    for gi in range(n_chunks // gpc):
        prep([gi * gpc + j for j in range(gpc)])
    if chunks_per_seq == 1:
        state(0)
    else:
        def state_body(cpos, carry):
            state(cpos)
            return carry
        lax.fori_loop(0, chunks_per_seq, state_body, 0)


def _dn_scratch(nseq, tm, chunk):
    rows = nseq * tm
    sdt = BF16 if chunk % 16 == 0 else F32
    return [pltpu.VMEM((rows, D_MODEL), F32),
            pltpu.VMEM((rows, D_MODEL), F32),
            pltpu.VMEM((rows, D_MODEL), F32),
            pltpu.VMEM((rows, LANE), F32),
            pltpu.VMEM((rows, D_MODEL), F32),
            pltpu.VMEM((rows, D_MODEL), sdt),
            pltpu.VMEM((rows, D_MODEL), sdt),
            pltpu.VMEM((rows, D_MODEL), sdt),
            pltpu.VMEM((rows, DN_HEADS * chunk), sdt),
            pltpu.VMEM((rows // chunk, DN_HEADS, DN_DV), F32)]


def _deltanet(u, z, ba, cw, gp, dnn, conv0, s0, *, nseq, tm, chunk, valid, o_dtype):
    batch, seq, _ = u.shape
    nt = max(seq // tm, 1)
    rows_in = tm if seq >= tm else seq
    assert rows_in == valid or valid == tm
    row = lambda b, t: (b, t, 0)
    per_b3 = lambda b, t: (b, 0, 0)
    per_b4 = lambda b, t: (b, 0, 0, 0)
    kern = functools.partial(_dn_kernel, nseq=nseq, tm=tm, chunk=chunk, valid=valid, fused=False)
    return pl.pallas_call(
        kern,
        out_shape=[jax.ShapeDtypeStruct((batch, seq, D_MODEL), o_dtype),
                   jax.ShapeDtypeStruct((batch, SUBLANE, CONV_CH), F32),
                   jax.ShapeDtypeStruct((batch, DN_HEADS, DN_DK, DN_DV), F32)],
        grid=(batch // nseq, nt),
        in_specs=[pl.BlockSpec((nseq, rows_in, CONV_CH), row),
                  pl.BlockSpec((nseq, rows_in, D_MODEL), row),
                  pl.BlockSpec((nseq, rows_in, LANE), row),
                  pl.BlockSpec((SUBLANE, CONV_CH), lambda b, t: (0, 0)),
                  pl.BlockSpec((SUBLANE, LANE), lambda b, t: (0, 0)),
                  pl.BlockSpec((1, DN_DV), lambda b, t: (0, 0)),
                  pl.BlockSpec((nseq, SUBLANE, CONV_CH), per_b3),
                  pl.BlockSpec((nseq, DN_HEADS, DN_DK, DN_DV), per_b4)],
        out_specs=[pl.BlockSpec((nseq, rows_in, D_MODEL), row),
                   pl.BlockSpec((nseq, SUBLANE, CONV_CH), per_b3),
                   pl.BlockSpec((nseq, DN_HEADS, DN_DK, DN_DV), per_b4)],
        scratch_shapes=[pltpu.VMEM((nseq, CONV_CH // LANE, SUBLANE + tm, LANE), F32)] + _dn_scratch(nseq, tm, chunk),
        compiler_params=_params(("arbitrary", "arbitrary")),
        name="deltanet",
    )(u, z, ba, cw, gp, dnn, conv0, s0)


def _deltanet_fused(q, k, v, z, ba, gp, dnn, s0, *, nseq, tm, chunk, o_dtype):
    batch, seq, _ = q.shape
    nt = seq // tm
    row = lambda b, t: (b, t, 0)
    per_b4 = lambda b, t: (b, 0, 0, 0)
    kern = functools.partial(_dn_kernel, nseq=nseq, tm=tm, chunk=chunk, valid=tm, fused=True)
    wide = pl.BlockSpec((nseq, tm, D_MODEL), row)
    return pl.pallas_call(
        kern,
        out_shape=[jax.ShapeDtypeStruct((batch, seq, D_MODEL), o_dtype),
                   jax.ShapeDtypeStruct((batch, DN_HEADS, DN_DK, DN_DV), F32)],
        grid=(batch // nseq, nt),
        in_specs=[wide, wide, wide, wide,
                  pl.BlockSpec((nseq, tm, LANE), row),
                  pl.BlockSpec((SUBLANE, LANE), lambda b, t: (0, 0)),
                  pl.BlockSpec((1, DN_DV), lambda b, t: (0, 0)),
                  pl.BlockSpec((nseq, DN_HEADS, DN_DK, DN_DV), per_b4)],
        out_specs=[wide, pl.BlockSpec((nseq, DN_HEADS, DN_DK, DN_DV), per_b4)],
        scratch_shapes=_dn_scratch(nseq, tm, chunk),
        compiler_params=_params(("arbitrary", "arbitrary")),
        name="deltanet_fused",
    )(q, k, v, z, ba, gp, dnn, s0)


def _rope(x, cos, sa, sb):
    cols = []
    for cb in range(x.shape[1] // LANE):
        xc = x[:, cb * LANE:(cb + 1) * LANE]
        cols.append(xc * cos + pltpu.roll(xc, LANE - ROT_DIM // 2, 1) * sa + pltpu.roll(xc, ROT_DIM // 2, 1) * sb)
    return cols[0] if len(cols) == 1 else jnp.concatenate(cols, axis=1)


def _head_pair_operands(x):
    lane = lax.broadcasted_iota(jnp.int32, (x.shape[0], LANE), 1)
    out = []
    for cb in range(x.shape[1] // LANE):
        c = x[:, cb * LANE:(cb + 1) * LANE]
        lo = jnp.where(lane < SW_HEAD_DIM, c, 0.0)
        hi = jnp.where(lane >= SW_HEAD_DIM, c, 0.0)
        out.append((lo, pltpu.roll(lo, SW_HEAD_DIM, 1)))
        out.append((pltpu.roll(hi, SW_HEAD_DIM, 1), hi))
    return out


def _ones_columns(w):
    rowi = lax.broadcasted_iota(jnp.int32, (4 * w, LANE), 0)
    lanei = lax.broadcasted_iota(jnp.int32, (4 * w, LANE), 1)
    return jnp.where((rowi < 2 * w) == (lanei < SW_HEAD_DIM), 1.0, 0.0).astype(BF16)


def _pair_softmax(scores, mask, sinks, lane, w):
    parts, es = [], []
    for half in range(2):
        s = jnp.where(mask, scores[:, half * 2 * w:(half + 1) * 2 * w], NEG_INF)
        m = jnp.maximum(jnp.max(s, axis=-1, keepdims=True), sinks[half])
        parts.append(jnp.exp(s - m).astype(BF16))
        es.append(jnp.exp(sinks[half] - m))
    return jnp.concatenate(parts, axis=1), jnp.where(lane < SW_HEAD_DIM, es[0], es[1])


def _swa_prompt_kernel(sinks_ref, qkv_ref, o_ref, kc_ref, vc_ref, kk2, vv2):
    n = pl.program_id(1)
    w = WINDOW

    @pl.when(n == 0)
    def _():
        kk2[...] = jnp.zeros_like(kk2)
        for j in range(SW_KV_HEADS):
            vv2[j, :, 0:LANE] = jnp.zeros((4 * w, LANE), BF16)
            vv2[j, :, LANE:2 * LANE] = _ones_columns(w)

    for j in range(SW_KV_HEADS):
        kk2[j, 0:w, :] = kk2[j, w:2 * w, :]
        kk2[j, 2 * w:3 * w, :] = kk2[j, 3 * w:4 * w, :]
        vv2[j, 0:w, 0:LANE] = vv2[j, w:2 * w, 0:LANE]
        vv2[j, 2 * w:3 * w, 0:LANE] = vv2[j, 3 * w:4 * w, 0:LANE]
    q = qkv_ref[:, 0:D_MODEL].astype(BF16)
    k = qkv_ref[:, D_MODEL:D_MODEL + SW_KV]
    v = qkv_ref[:, D_MODEL + SW_KV:D_MODEL + 2 * SW_KV]
    kc_ref[...] = k
    vc_ref[...] = v
    for j, (left, right) in enumerate(_head_pair_operands(k)):
        kk2[j, w:2 * w, :] = left.astype(BF16)
        kk2[j, 3 * w:4 * w, :] = right.astype(BF16)
    for j, (left, right) in enumerate(_head_pair_operands(v)):
        vv2[j, w:2 * w, 0:LANE] = left.astype(BF16)
        vv2[j, 3 * w:4 * w, 0:LANE] = right.astype(BF16)
    qi = lax.broadcasted_iota(jnp.int32, (w, 2 * w), 0) + w
    sj = lax.broadcasted_iota(jnp.int32, (w, 2 * w), 1)
    d = qi - sj
    mask = (d >= 0) & (d <= WINDOW) & ((sj >= w) | (n > 0))
    lane = lax.broadcasted_iota(jnp.int32, (w, LANE), 1)
    n_pb = D_MODEL // LANE
    scores = [_dot_nt(q[:, pb * LANE:(pb + 1) * LANE], kk2[pb // 2]) for pb in range(n_pb)]
    soft = [_pair_softmax(scores[pb], mask, (sinks_ref[2 * pb], sinks_ref[2 * pb + 1]), lane, w)
            for pb in range(n_pb)]
    outs = [_dot(soft[pb][0], vv2[pb // 2]) for pb in range(n_pb)]
    for pb in range(n_pb):
        o = outs[pb][:, 0:LANE] / (outs[pb][:, LANE:2 * LANE] + soft[pb][1])
        o_ref[:, pb * LANE:(pb + 1) * LANE] = o.astype(o_ref.dtype)


def _swa_prompt(sinks, qkv, *, batch, seq):
    nb = seq // WINDOW
    row = lambda b, n: (b * nb + n, 0)
    per_b = lambda b, n: (b, 0, 0)
    return pl.pallas_call(
        _swa_prompt_kernel,
        out_shape=[jax.ShapeDtypeStruct((batch * seq, D_MODEL), BF16),
                   jax.ShapeDtypeStruct((batch, WINDOW, SW_KV), F32),
                   jax.ShapeDtypeStruct((batch, WINDOW, SW_KV), F32)],
        grid=(batch, nb),
        in_specs=[pl.BlockSpec(memory_space=pltpu.SMEM),
                  pl.BlockSpec((WINDOW, D_MODEL + 2 * SW_KV), row)],
        out_specs=[pl.BlockSpec((WINDOW, D_MODEL), row),
                   pl.BlockSpec((None, WINDOW, SW_KV), per_b),
                   pl.BlockSpec((None, WINDOW, SW_KV), per_b)],
        scratch_shapes=[pltpu.VMEM((SW_KV_HEADS, 4 * WINDOW, LANE), BF16),
                        pltpu.VMEM((SW_KV_HEADS, 4 * WINDOW, 2 * LANE), BF16)],
        compiler_params=_params(("arbitrary", "arbitrary")),
        name="swa_prompt",
    )(sinks, qkv)


def _swa_sample_kernel(sinks_ref, qkv_ref, kbuf_ref, vbuf_ref, cos_ref, sa_ref, sb_ref,
                       o_ref, kc_ref, vc_ref, kk_s, vv_s, kk2, vv2, *, nseq, t_new):
    wb = WINDOW
    tp = t_new

    @pl.when(pl.program_id(0) == 0)
    def _():
        kk_s[...] = jnp.zeros_like(kk_s)
        vv_s[...] = jnp.zeros_like(vv_s)
        kk2[...] = jnp.zeros_like(kk2)
        for g in range(nseq):
            for j in range(SW_KV_HEADS):
                vv2[g, j, :, 0:LANE] = jnp.zeros((4 * wb, LANE), BF16)
                vv2[g, j, :, LANE:2 * LANE] = _ones_columns(wb)

    cos, sa, sb = cos_ref[0:tp, :], sa_ref[0:tp, :], sb_ref[0:tp, :]
    qs = []
    for g in range(nseq):
        q = (_rope(qkv_ref[g, :, 0:D_MODEL], cos, sa, sb) * (SW_HEAD_DIM ** -0.5)).astype(BF16)
        k_new = _rope(qkv_ref[g, :, D_MODEL:D_MODEL + SW_KV], cos, sa, sb)
        v_new = qkv_ref[g, :, D_MODEL + SW_KV:D_MODEL + 2 * SW_KV]
        k_old, v_old = kbuf_ref[g], vbuf_ref[g]
        kk_s[g, 0:wb, :] = k_old
        vv_s[g, 0:wb, :] = v_old
        kk_s[g, wb:wb + tp, :] = k_new
        vv_s[g, wb:wb + tp, :] = v_new
        kc_ref[g] = kk_s[g, pl.ds(t_new, wb), :]
        vc_ref[g] = vv_s[g, pl.ds(t_new, wb), :]
        for r0, n_r, kx, vx in ((0, wb, k_old, v_old), (wb, tp, k_new, v_new)):
            for j, (left, right) in enumerate(_head_pair_operands(kx)):
                kk2[g, j, r0:r0 + n_r, :] = left.astype(BF16)
                kk2[g, j, 2 * wb + r0:2 * wb + r0 + n_r, :] = right.astype(BF16)
            for j, (left, right) in enumerate(_head_pair_operands(vx)):
                vv2[g, j, r0:r0 + n_r, 0:LANE] = left.astype(BF16)
                vv2[g, j, 2 * wb + r0:2 * wb + r0 + n_r, 0:LANE] = right.astype(BF16)
        qs.append([jnp.concatenate([q[:, (2 * j) * LANE:(2 * j + 1) * LANE],
                                    q[:, (2 * j + 1) * LANE:(2 * j + 2) * LANE]], axis=0)
                   for j in range(SW_KV_HEADS)])
    rows = 2 * tp
    qt = lax.broadcasted_iota(jnp.int32, (rows, 2 * wb), 0) % tp
    sj = lax.broadcasted_iota(jnp.int32, (rows, 2 * wb), 1)
    d = qt + wb - sj
    mask = (d >= 0) & (d <= WINDOW)
    top = lax.broadcasted_iota(jnp.int32, (rows, 1), 0) < tp
    lane = lax.broadcasted_iota(jnp.int32, (rows, LANE), 1)
    items = [(g, j) for g in range(nseq) for j in range(SW_KV_HEADS)]
    scores = [_dot_nt(qs[g][j], kk2[g, j]) for g, j in items]
    soft = []
    for n, (g, j) in enumerate(items):
        sinks = [jnp.where(top, sinks_ref[4 * j + half], sinks_ref[4 * j + 2 + half]) for half in range(2)]
        soft.append(_pair_softmax(scores[n], mask, sinks, lane, wb))
    outs = [_dot(soft[n][0], vv2[g, j]) for n, (g, j) in enumerate(items)]
    for n, (g, j) in enumerate(items):
        o = outs[n][:, 0:LANE] / (outs[n][:, LANE:2 * LANE] + soft[n][1])
        o_ref[g, :, (2 * j) * LANE:(2 * j + 1) * LANE] = o[0:tp, :]
        o_ref[g, :, (2 * j + 1) * LANE:(2 * j + 2) * LANE] = o[tp:2 * tp, :]


def _swa_sample(sinks, qkv, kbuf, vbuf, cos, sa, sb, *, batch, nseq, t_new):
    per_b = lambda b: (b, 0, 0)
    tab = lambda b: (0, 0)
    kern = functools.partial(_swa_sample_kernel, nseq=nseq, t_new=t_new)
    return pl.pallas_call(
        kern,
        out_shape=[jax.ShapeDtypeStruct((batch, t_new, D_MODEL), F32),
                   jax.ShapeDtypeStruct((batch, WINDOW, SW_KV), F32),
                   jax.ShapeDtypeStruct((batch, WINDOW, SW_KV), F32)],
        grid=(batch // nseq,),
        in_specs=[pl.BlockSpec(memory_space=pltpu.SMEM),
                  pl.BlockSpec((nseq, t_new, D_MODEL + 2 * SW_KV), per_b),
                  pl.BlockSpec((nseq, WINDOW, SW_KV), per_b),
                  pl.BlockSpec((nseq, WINDOW, SW_KV), per_b),
                  pl.BlockSpec((SUBLANE, LANE), tab),
                  pl.BlockSpec((SUBLANE, LANE), tab),
                  pl.BlockSpec((SUBLANE, LANE), tab)],
        out_specs=[pl.BlockSpec((nseq, t_new, D_MODEL), per_b),
                   pl.BlockSpec((nseq, WINDOW, SW_KV), per_b),
                   pl.BlockSpec((nseq, WINDOW, SW_KV), per_b)],
        scratch_shapes=[pltpu.VMEM((nseq, WINDOW + SUBLANE, SW_KV), F32),
                        pltpu.VMEM((nseq, WINDOW + SUBLANE, SW_KV), F32),
                        pltpu.VMEM((nseq, SW_KV_HEADS, 4 * WINDOW, LANE), BF16),
                        pltpu.VMEM((nseq, SW_KV_HEADS, 4 * WINDOW, 2 * LANE), BF16)],
        compiler_params=_params(("arbitrary",)),
        name="swa_sample",
    )(sinks, qkv, kbuf, vbuf, cos, sa, sb)


def _out_ffn_kernel(x_ref, g2_ref, post2_ref, ga_ref, gb_ref, odn_ref, osw_ref, wo_ref,
                    sh_ref, sc_ref, g_ref, pre_ref, post_ref, wg_ref, wu_ref, wd_ref, o_ref,
                    *, per_token, tiles_per_seq):
    rows = _row_groups(x_ref.shape[0])
    g2 = _mod_rows(g2_ref, per_token, tiles_per_seq)
    sh = _mod_rows(sh_ref, per_token, tiles_per_seq)
    sc = _mod_rows(sc_ref, per_token, tiles_per_seq)
    g = _mod_rows(g_ref, per_token, tiles_per_seq)
    pick = (lambda m, r: m[r]) if per_token else (lambda m, r: m)
    f32 = lambda ref, r: ref[r, :].astype(F32)
    ys = [(_sigmoid(f32(ga_ref, r)) * f32(odn_ref, r)
           + _sigmoid(f32(gb_ref, r)) * f32(osw_ref, r)).astype(BF16) for r in rows]
    ps = [_dot(y, wo_ref[...]) for y in ys]
    xs = [x_ref[r, :] + pick(g2, r) * _rms(p, post2_ref[...]) for p, r in zip(ps, rows)]
    outs = _ffn_stages(xs, rows, pick, sh, sc, g, pre_ref[...], post_ref[...], wg_ref, wu_ref, wd_ref)
    for o, r in zip(outs, rows):
        o_ref[r, :] = o


def _out_ffn(x, mod, mod_block, seq_len, post2, gates, o_dn, o_sw, w_out, pre, post, wg, wu, wd):
    n = x.shape[0]
    tm = min(512, n)
    dff = wg.shape[1]
    kern = functools.partial(_out_ffn_kernel, per_token=mod_block is None, tiles_per_seq=max(seq_len // tm, 1))
    row = lambda i: (i, 0)
    return pl.pallas_call(
        kern,
        out_shape=jax.ShapeDtypeStruct((n, D_MODEL), F32),
        grid=(n // tm,),
        in_specs=[pl.BlockSpec((tm, D_MODEL), row)]
        + _mod_specs(mod_block, tm, (5,))
        + [_const_spec((1, D_MODEL)),
           pl.BlockSpec((tm, D_MODEL), lambda i: (i, 0)),
           pl.BlockSpec((tm, D_MODEL), lambda i: (i, 1)),
           pl.BlockSpec((tm, D_MODEL), row),
           pl.BlockSpec((tm, D_MODEL), row),
           _const_spec((D_MODEL, D_MODEL))]
        + _mod_specs(mod_block, tm, (6, 7, 8))
        + [_const_spec((1, D_MODEL)), _const_spec((1, D_MODEL)),
           _const_spec((D_MODEL, dff)), _const_spec((D_MODEL, dff)), _const_spec((dff, D_MODEL))],
        out_specs=pl.BlockSpec((tm, D_MODEL), row),
        compiler_params=_params(("arbitrary",)),
        name="out_ffn",
    )(x, mod, post2, gates, gates, o_dn, o_sw, w_out, mod, mod, mod, pre, post, wg, wu, wd)


def _rope_tables(pos):
    half = ROT_DIM // 2
    inv_freq = ROPE_THETA ** (-jnp.arange(half, dtype=F32) * (2.0 / ROT_DIM))
    ang = pos.astype(F32)[:, None] * inv_freq[None, :]
    cos, sin = jnp.cos(ang), jnp.sin(ang)
    n = pos.shape[0]
    rest = SW_HEAD_DIM - ROT_DIM
    c64 = jnp.concatenate([cos, cos, jnp.ones((n, rest), F32)], axis=1)
    a64 = jnp.concatenate([-sin, jnp.zeros((n, half + rest), F32)], axis=1)
    b64 = jnp.concatenate([jnp.zeros((n, half), F32), sin, jnp.zeros((n, rest), F32)], axis=1)
    rep = LANE // SW_HEAD_DIM
    return tuple(jnp.tile(t, (1, rep)) for t in (c64, a64, b64))


def _pad_rows(a, rows):
    return jnp.pad(a, ((0, rows - a.shape[0]),) + ((0, 0),) * (a.ndim - 1))


def _layer_sample(x, mod, seq, wts, conv0, s0, dn_fn, swa_fn):
    per_seq = lambda a: a.reshape(-1, seq, a.shape[-1])
    flat = lambda a: a.reshape(-1, a.shape[-1])
    x = _ffn(x, mod, 0, None, seq, wts["pre1"], wts["post1"], wts["wg1"], wts["wu1"], wts["wd1"])
    u, z, qkv, gates, ba = _proj(x, mod, None, seq, wts["pre2"], *wts["w_in"])
    o_dn, conv_new, s_new = dn_fn(per_seq(u), per_seq(z), per_seq(ba), wts["conv_w"], wts["gparam"],
                                  wts["dn_norm"], conv0, s0)
    o_sw, k_new, v_new = swa_fn(per_seq(qkv))
    x = _out_ffn(x, mod, None, seq, wts["post2"], gates, flat(o_dn), flat(o_sw), wts["w_out"],
                 wts["pre3"], wts["post3"], wts["wg2"], wts["wu2"], wts["wd2"])
    return x, (k_new, v_new, conv_new, s_new)


def _layer_prompt(x, mod, mod_block, batch, seq, wts, s0, rope, sink):
    per_seq = lambda a: a.reshape(batch, seq, a.shape[-1])
    x = _ffn(x, mod, 0, mod_block, seq, wts["pre1"], wts["post1"], wts["wg1"], wts["wu1"], wts["wd1"])
    q, k, v, z, qkv, gates, ba, conv_new = _proj_prompt(x, mod, mod_block, seq, wts["pre2"], *wts["w_in"],
                                                        wts["conv_w"], *rope)
    o_dn, s_new = _deltanet_fused(per_seq(q), per_seq(k), per_seq(v), per_seq(z), per_seq(ba), wts["gparam"],
                                  wts["dn_norm"], s0, nseq=batch, tm=DN_TILE, chunk=DN_CHUNK, o_dtype=BF16)
    o_sw, k_new, v_new = _swa_prompt(sink, qkv, batch=batch, seq=seq)
    x = _out_ffn(x, mod, mod_block, seq, wts["post2"], gates, o_dn.reshape(batch * seq, D_MODEL), o_sw,
                 wts["w_out"], wts["pre3"], wts["post3"], wts["wg2"], wts["wu2"], wts["wd2"])
    return x, (k_new, v_new, conv_new, s_new)


def kernel(x_prompt, x_sample, cache_swa_k, cache_swa_v, state_conv, state_delta, c_prompt, c_sample,
           w_ada, b_ada, ffn1_norm_pre, ffn1_norm_post, ffn1_w_gate, ffn1_w_up, ffn1_w_down,
           mix_norm_pre, mix_norm_post, w_in, conv_w, a_log, dt_bias, dn_norm, sinks, w_out,
           ffn2_norm_pre, ffn2_norm_post, ffn2_w_gate, ffn2_w_up, ffn2_w_down):
    depth = w_ada.shape[0]
    bp, tp, _ = x_prompt.shape
    bs, ts, _ = x_sample.shape
    assert tp % 512 == 0 and 1 <= ts <= SUBLANE and bs % SAMPLE_SEQS == 0 and (bs * ts) % SUBLANE == 0

    cos_p, sa_p, sb_p = _rope_tables(jnp.arange(tp))
    cos_s, sa_s, sb_s = (_pad_rows(t, SUBLANE) for t in _rope_tables(PAST_LEN + jnp.arange(ts)))

    xp = x_prompt.reshape(bp * tp, D_MODEL)
    xs = x_sample.reshape(bs * ts, D_MODEL)
    c_all = jnp.concatenate([jnp.repeat(c_sample, ts, axis=0), _pad_rows(c_prompt, SUBLANE)], axis=0)
    prompt_mod_block = bs * ts // SUBLANE

    outs_p, outs_s = [], []
    for l in range(depth):
        wi = w_in[l]
        n_ba = 2 * DN_HEADS
        w_in_a = wi[:, :P_QKV].astype(BF16)
        w_in_b = wi[:, P_QKV + n_ba:].astype(BF16)
        w_in_c = jnp.pad(wi[:, P_QKV:P_QKV + n_ba], ((0, 0), (0, LANE - n_ba))).astype(BF16)
        gparam = jnp.zeros((SUBLANE, LANE), F32)
        gparam = gparam.at[0, DN_HEADS:2 * DN_HEADS].set(dt_bias[l]).at[1, DN_HEADS:2 * DN_HEADS].set(a_log[l])
        wts = dict(
            pre1=ffn1_norm_pre[l][None], post1=ffn1_norm_post[l][None],
            wg1=ffn1_w_gate[l].astype(BF16), wu1=ffn1_w_up[l].astype(BF16), wd1=ffn1_w_down[l].astype(BF16),
            pre2=mix_norm_pre[l][None], post2=mix_norm_post[l][None], w_in=(w_in_a, w_in_b, w_in_c),
            conv_w=_pad_rows(conv_w[l], SUBLANE), gparam=gparam, dn_norm=dn_norm[l][None],
            w_out=w_out[l].astype(BF16),
            pre3=ffn2_norm_pre[l][None], post3=ffn2_norm_post[l][None],
            wg2=ffn2_w_gate[l].astype(BF16), wu2=ffn2_w_up[l].astype(BF16), wd2=ffn2_w_down[l].astype(BF16),
        )
        b_ada_l = b_ada[l][None]
        mod = _ada(c_all, w_ada[l], b_ada_l)
        sink_l = sinks[l]

        s0_p = jnp.zeros((bp, DN_HEADS, DN_DK, DN_DV), F32)
        xp, st_p = _layer_prompt(xp, mod, prompt_mod_block, bp, tp, wts, s0_p, (cos_p, sa_p, sb_p), sink_l)

        conv0_s = jnp.pad(state_conv[l], ((0, 0), (SUBLANE - CONV_W + 1, 0), (0, 0)))
        kbuf = cache_swa_k[l].reshape(bs, WINDOW, SW_KV)
        vbuf = cache_swa_v[l].reshape(bs, WINDOW, SW_KV)
        dn_s = functools.partial(_deltanet, nseq=SAMPLE_SEQS, tm=SUBLANE, chunk=SUBLANE, valid=ts, o_dtype=F32)

        swa_s = lambda qkv: _swa_sample(sink_l, qkv, kbuf, vbuf, cos_s, sa_s, sb_s,
                                        batch=bs, nseq=SAMPLE_SEQS, t_new=ts)
        xs, st_s = _layer_sample(xs, mod, ts, wts, conv0_s, state_delta[l], dn_s, swa_s)
        outs_p.append(st_p)
        outs_s.append(st_s)

    def stack(outs, i):
        return jnp.stack([o[i] for o in outs])

    def kv5(a, batch):
        return a.reshape(depth, batch, WINDOW, SW_KV_HEADS, SW_HEAD_DIM)

    tail = slice(SUBLANE - CONV_W + 1, SUBLANE)
    y_p = xp.reshape(bp, tp, D_MODEL)
    y_s = xs.reshape(bs, ts, D_MODEL)
    return (y_p, y_s,
            kv5(stack(outs_p, 0), bp), kv5(stack(outs_p, 1), bp),
            stack(outs_p, 2)[:, :, tail], stack(outs_p, 3),
            kv5(stack(outs_s, 0), bs), kv5(stack(outs_s, 1), bs),
            stack(outs_s, 2)[:, :, tail], stack(outs_s, 3))
```

```python
import functools

import jax
import jax.numpy as jnp
from jax import lax
from jax.experimental import pallas as pl
from jax.experimental.pallas import tpu as pltpu

F32, BF16 = jnp.float32, jnp.bfloat16

D_MODEL = 1024
DN_HEADS, DN_DK, DN_DV = 8, 128, 128
CONV_W = 4
CONV_CH = 3 * D_MODEL
DN_CHUNK = 64
SW_HEAD_DIM, SW_HEADS, SW_KV_HEADS, SW_GROUP = 64, 16, 4, 4
SW_KV = SW_KV_HEADS * SW_HEAD_DIM
WINDOW = 128
ROT_DIM = 16
ROPE_THETA = 500000.0
PAST_LEN = 8192
EPS = 1e-6
NEG_INF = -1e30

LANE = 128
SUBLANE = 8
VMEM_LIMIT = 56 * 1024 * 1024

P_U, P_Z, P_QKV, P_G, P_BA = 0, 3072, 4096, 5632, 7680
P_TOTAL = 7808

DN_TILE = 128
DN_GROUP = 4
SAMPLE_SEQS = 8
FFN_TILE = 1024
FFN_SUBTILES = 4
ADA_COLS = 1024
PROJ_FUSED_TILE = 512


def _dot(a, b):
    return jnp.dot(a, b, preferred_element_type=F32)


def _dot_nt(a, b):
    return lax.dot_general(a, b, (((1,), (1,)), ((), ())), preferred_element_type=F32)


def _dot_tn(a, b):
    return lax.dot_general(a, b, (((0,), (0,)), ((), ())), preferred_element_type=F32)


def _split3(x):
    hi = x.astype(BF16)
    r1 = x - hi.astype(F32)
    mid = r1.astype(BF16)
    lo = (r1 - mid.astype(F32)).astype(BF16)
    return hi, mid, lo


def _dot_exact_lhs(a_bf, b):
    return sum(_dot(a_bf, t) for t in _split3(b))


def _dot_nt_exact_lhs(a_bf, b):
    return sum(_dot_nt(a_bf, t) for t in _split3(b))


def _mm_bf(a, b):
    return _dot(a.astype(BF16), b.astype(BF16))


def _sigmoid(x):
    return 0.5 * jnp.tanh(0.5 * x) + 0.5


def _silu(x):
    return x * _sigmoid(x)


def _rms(x, gain):
    ms = jnp.mean(x * x, axis=-1, keepdims=True)
    return x * lax.rsqrt(ms + EPS) * gain


def _mod_rows(ref, per_token, tiles_per_seq):
    if per_token:
        return ref[...]
    b = pl.program_id(0) // tiles_per_seq
    return ref[pl.ds(b, 1), :]


def _const_spec(shape):
    return pl.BlockSpec(shape, lambda *_: (0,) * len(shape), pipeline_mode=pl.Buffered(1))


def _params(sem):
    return pltpu.CompilerParams(dimension_semantics=sem, vmem_limit_bytes=VMEM_LIMIT)


def _ada_kernel(c_ref, w_ref, b_ref, o_ref):
    h = _silu(c_ref[...]).astype(BF16)
    o_ref[...] = _dot(h, w_ref[...].astype(BF16)) + b_ref[...]


def _ada(c, w, b):
    rows, n = c.shape[0], w.shape[1]
    tn = ADA_COLS
    return pl.pallas_call(
        _ada_kernel,
        out_shape=jax.ShapeDtypeStruct((rows, n), F32),
        grid=(n // tn,),
        in_specs=[
            pl.BlockSpec((rows, D_MODEL), lambda j: (0, 0)),
            pl.BlockSpec((D_MODEL, tn), lambda j: (0, j)),
            pl.BlockSpec((1, tn), lambda j: (0, j)),
        ],
        out_specs=pl.BlockSpec((rows, tn), lambda j: (0, j)),
        compiler_params=_params(("arbitrary",)),
        name="ada",
    )(c, w, b)


def _mod_specs(mod_block, tm, cols):
    if mod_block is None:
        return [pl.BlockSpec((tm, D_MODEL), functools.partial(lambda i, c: (i, c), c=c)) for c in cols]
    return [pl.BlockSpec((SUBLANE, D_MODEL), functools.partial(lambda i, c: (mod_block, c), c=c)) for c in cols]


def _ffn_stages(xs, rows, pick, sh, sc, g, pre, post, wg_ref, wu_ref, wd_ref):
    hs = [(_rms(x, pre) * (1.0 + pick(sc, r)) + pick(sh, r)).astype(BF16) for x, r in zip(xs, rows)]
    gates = [_dot(h, wg_ref[...]) for h in hs]
    ups = [_dot(h, wu_ref[...]) for h in hs]
    acts = [(_silu(a) * b).astype(BF16) for a, b in zip(gates, ups)]
    ys = [_dot(a, wd_ref[...]) for a in acts]
    return [x + 0.5 * pick(g, r) * _rms(y, post) for x, y, r in zip(xs, ys, rows)]


def _row_groups(tm):
    n = max(min(FFN_SUBTILES, tm // 256), 1)
    return [slice(i * tm // n, (i + 1) * tm // n) for i in range(n)]


def _ffn_kernel(x_ref, sh_ref, sc_ref, g_ref, pre_ref, post_ref, wg_ref, wu_ref, wd_ref, o_ref,
                *, per_token, tiles_per_seq):
    rows = _row_groups(x_ref.shape[0])
    sh = _mod_rows(sh_ref, per_token, tiles_per_seq)
    sc = _mod_rows(sc_ref, per_token, tiles_per_seq)
    g = _mod_rows(g_ref, per_token, tiles_per_seq)
    pick = (lambda m, r: m[r]) if per_token else (lambda m, r: m)
    xs = [x_ref[r, :] for r in rows]
    outs = _ffn_stages(xs, rows, pick, sh, sc, g, pre_ref[...], post_ref[...], wg_ref, wu_ref, wd_ref)
    for o, r in zip(outs, rows):
        o_ref[r, :] = o


def _ffn(x, mod, col0, mod_block, seq_len, pre, post, wg, wu, wd):
    n = x.shape[0]
    tm = min(FFN_TILE, n if mod_block is None else seq_len)
    dff = wg.shape[1]
    kern = functools.partial(_ffn_kernel, per_token=mod_block is None, tiles_per_seq=max(seq_len // tm, 1))
    row = lambda i: (i, 0)
    return pl.pallas_call(
        kern,
        out_shape=jax.ShapeDtypeStruct((n, D_MODEL), F32),
        grid=(n // tm,),
        in_specs=[pl.BlockSpec((tm, D_MODEL), row)]
        + _mod_specs(mod_block, tm, (col0, col0 + 1, col0 + 2))
        + [_const_spec((1, D_MODEL)), _const_spec((1, D_MODEL)),
           _const_spec((D_MODEL, dff)), _const_spec((D_MODEL, dff)), _const_spec((dff, D_MODEL))],
        out_specs=pl.BlockSpec((tm, D_MODEL), row),
        compiler_params=_params(("arbitrary",)),
        name="ffn",
    )(x, mod, mod, mod, pre, post, wg, wu, wd)


def _proj_kernel(x_ref, sh_ref, sc_ref, pre_ref, wa_ref, wb_ref, wc_ref, u_ref, z_ref, qkv_ref, g_ref, ba_ref,
                 *, per_token, tiles_per_seq):
    sh = _mod_rows(sh_ref, per_token, tiles_per_seq)
    sc = _mod_rows(sc_ref, per_token, tiles_per_seq)
    h = (_rms(x_ref[...], pre_ref[...]) * (1.0 + sc) + sh).astype(BF16)
    u_ref[...] = _dot(h, wa_ref[:, 0:P_Z])
    z_ref[...] = _dot(h, wa_ref[:, P_Z:P_QKV])
    qkv_ref[...] = _dot(h, wb_ref[:, 0:P_G - P_QKV])
    g_ref[...] = _dot(h, wb_ref[:, P_G - P_QKV:P_BA - P_QKV]).astype(g_ref.dtype)
    ba_ref[...] = _dot(h, wc_ref[...])


def _proj(x, mod, mod_block, seq_len, pre, wa, wb, wc):
    n = x.shape[0]
    tm = min(512, n)
    kern = functools.partial(_proj_kernel, per_token=mod_block is None, tiles_per_seq=max(seq_len // tm, 1))
    row = lambda i: (i, 0)
    widths = (P_Z - P_U, P_QKV - P_Z, P_G - P_QKV, P_BA - P_G, P_TOTAL - P_BA)
    return pl.pallas_call(
        kern,
        out_shape=[jax.ShapeDtypeStruct((n, wd), BF16 if wd == P_BA - P_G else F32) for wd in widths],
        grid=(n // tm,),
        in_specs=[pl.BlockSpec((tm, D_MODEL), row)]
        + _mod_specs(mod_block, tm, (3, 4))
        + [_const_spec((1, D_MODEL)), _const_spec(wa.shape), _const_spec(wb.shape), _const_spec(wc.shape)],
        out_specs=[pl.BlockSpec((tm, wd), row) for wd in widths],
        compiler_params=_params(("arbitrary",)),
        name="proj",
    )(x, mod, mod, pre, wa, wb, wc)


def _conv_norm(u_cols, ubuf_cb, cw_ref, cols, rows=None):
    rows = u_cols.shape[0] if rows is None else rows
    ubuf_cb[SUBLANE:SUBLANE + u_cols.shape[0], :] = u_cols
    acc = ubuf_cb[pl.ds(SUBLANE - CONV_W + 1, rows), :] * cw_ref[0:1, cols]
    for i in range(1, CONV_W):
        acc = acc + ubuf_cb[pl.ds(SUBLANE - CONV_W + 1 + i, rows), :] * cw_ref[i:i + 1, cols]
    return _silu(acc)


def _store_qkv(a, cb, q_dst, k_dst, v_dst, rows):
    sec, off = divmod(cb * LANE, D_MODEL)
    dst = slice(off, off + LANE)
    if sec == 2:
        v_dst[rows, dst] = a
    else:
        an = a * lax.rsqrt(jnp.sum(a * a, axis=-1, keepdims=True) + EPS)
        if sec == 0:
            q_dst[rows, dst] = an * (DN_DK ** -0.5)
        else:
            k_dst[rows, dst] = an


def _proj_prompt_kernel(x_ref, sh_ref, sc_ref, pre_ref, wa_ref, wb_ref, wc_ref, cw_ref, cos_ref, sa_ref, sb_ref,
                        q_ref, k_ref, v_ref, z_ref, qkv_ref, g_ref, ba_ref, tail_ref, ubuf, *, tiles_per_seq):
    tm = x_ref.shape[0]
    n_cb = CONV_CH // LANE
    all_rows = slice(0, tm)

    @pl.when(pl.program_id(0) % tiles_per_seq == 0)
    def _():
        ubuf[:, 0:SUBLANE, :] = jnp.zeros((n_cb, SUBLANE, LANE), F32)

    sh = _mod_rows(sh_ref, False, tiles_per_seq)
    sc = _mod_rows(sc_ref, False, tiles_per_seq)
    h = (_rms(x_ref[...], pre_ref[...]) * (1.0 + sc) + sh).astype(BF16)
    u = _dot(h, wa_ref[:, 0:P_Z])
    z_ref[...] = _dot(h, wa_ref[:, P_Z:P_QKV])
    s = _dot(h, wb_ref[:, 0:P_G - P_QKV])
    g_ref[...] = _dot(h, wb_ref[:, P_G - P_QKV:P_BA - P_QKV]).astype(g_ref.dtype)
    ba_ref[...] = _dot(h, wc_ref[...])
    for cb in range(n_cb):
        cols = slice(cb * LANE, (cb + 1) * LANE)
        a = _conv_norm(u[:, cols], ubuf.at[cb], cw_ref, cols)
        _store_qkv(a, cb, q_ref, k_ref, v_ref, all_rows)
        tail = ubuf[cb, pl.ds(tm, SUBLANE), :]
        ubuf[cb, 0:SUBLANE, :] = tail
        tail_ref[:, cols] = tail
    cos, sa, sb = cos_ref[...], sa_ref[...], sb_ref[...]
    qkv_ref[:, 0:D_MODEL] = _rope(s[:, 0:D_MODEL], cos, sa, sb) * (SW_HEAD_DIM ** -0.5)
    qkv_ref[:, D_MODEL:D_MODEL + SW_KV] = _rope(s[:, D_MODEL:D_MODEL + SW_KV], cos, sa, sb)
    qkv_ref[:, D_MODEL + SW_KV:D_MODEL + 2 * SW_KV] = s[:, D_MODEL + SW_KV:D_MODEL + 2 * SW_KV]


def _proj_prompt(x, mod, mod_block, seq_len, pre, wa, wb, wc, cw, cos, sa, sb):
    n = x.shape[0]
    tm = PROJ_FUSED_TILE
    tps = seq_len // tm
    kern = functools.partial(_proj_prompt_kernel, tiles_per_seq=tps)
    row = lambda i: (i, 0)
    tab = lambda i: (i % tps, 0)
    widths = (D_MODEL, D_MODEL, D_MODEL, P_QKV - P_Z, P_G - P_QKV, P_BA - P_G, P_TOTAL - P_BA)
    dtypes = (F32, F32, F32, F32, F32, BF16, F32)
    return pl.pallas_call(
        kern,
        out_shape=[jax.ShapeDtypeStruct((n, wd), dt) for wd, dt in zip(widths, dtypes)]
        + [jax.ShapeDtypeStruct((n // seq_len, SUBLANE, CONV_CH), F32)],
        grid=(n // tm,),
        in_specs=[pl.BlockSpec((tm, D_MODEL), row)]
        + _mod_specs(mod_block, tm, (3, 4))
        + [_const_spec((1, D_MODEL)), _const_spec(wa.shape), _const_spec(wb.shape), _const_spec(wc.shape),
           _const_spec(cw.shape),
           pl.BlockSpec((tm, LANE), tab), pl.BlockSpec((tm, LANE), tab), pl.BlockSpec((tm, LANE), tab)],
        out_specs=[pl.BlockSpec((tm, wd), row) for wd in widths]
        + [pl.BlockSpec((None, SUBLANE, CONV_CH), lambda i: (i // tps, 0, 0))],
        scratch_shapes=[pltpu.VMEM((CONV_CH // LANE, SUBLANE + tm, LANE), F32)],
        compiler_params=_params(("arbitrary",)),
        name="proj_prompt",
    )(x, mod, mod, pre, wa, wb, wc, cw, cos, sa, sb)


def _tri_inv_all(lows, c):
    ii = lax.broadcasted_iota(jnp.int32, (c, c), 0)
    jj = lax.broadcasted_iota(jnp.int32, (c, c), 1)
    eye = jnp.where(ii == jj, 1.0, 0.0).astype(F32)
    xs = [eye - jnp.where(ii // 2 == jj // 2, low, 0.0) for low in lows]
    b = 2
    while b < c:
        join = (ii // (2 * b) == jj // (2 * b)) & (ii // b != jj // b)
        ys = [_mm_bf(x, jnp.where(join, low, 0.0)) for x, low in zip(xs, lows)]
        xs = [x - _mm_bf(y, x) for x, y in zip(xs, ys)]
        b *= 2
    return xs


def _dn_kernel(*refs, nseq, tm, chunk, valid, fused):
    if fused:
        (q_ref, k_ref, v_ref, z_ref, ba_ref, gp_ref, dnn_ref, s0_ref, o_ref, so_ref,
         q_s, k_s, v_s, bg_s, u_s, w_s, qe_s, kd_s, a_s, eg_s) = refs
    else:
        (u_ref, z_ref, ba_ref, cw_ref, gp_ref, dnn_ref, conv0_ref, s0_ref, o_ref, convo_ref, so_ref,
         ubuf, q_s, k_s, v_s, bg_s, u_s, w_s, qe_s, kd_s, a_s, eg_s) = refs
    t = pl.program_id(1)
    rows_all = nseq * tm
    n_cb = CONV_CH // LANE

    @pl.when(t == 0)
    def _():
        so_ref[...] = s0_ref[...]
        if not fused:
            for g in range(nseq):
                for cb in range(n_cb):
                    ubuf[g, cb, 0:SUBLANE, :] = jnp.zeros((SUBLANE, LANE), F32)
                    ubuf[g, cb, SUBLANE - CONV_W + 1:SUBLANE, :] = conv0_ref[g, :, cb * LANE:(cb + 1) * LANE]
                    if valid < tm:
                        ubuf[g, cb, SUBLANE + valid:SUBLANE + tm, :] = jnp.zeros((tm - valid, LANE), F32)

    for g in range(nseq):
        rws = slice(g * tm, (g + 1) * tm)
        if fused:
            q_s[rws, :] = q_ref[g]
            k_s[rws, :] = k_ref[g]
            v_s[rws, :] = v_ref[g]
            continue
        for cb in range(n_cb):
            cols = slice(cb * LANE, (cb + 1) * LANE)
            a = _conv_norm(u_ref[g, :, cols], ubuf.at[g, cb], cw_ref, cols, rows=tm)
            _store_qkv(a, cb, q_s, k_s, v_s, rws)
            tail = ubuf[g, cb, pl.ds(valid, SUBLANE), :]
            ubuf[g, cb, 0:SUBLANE, :] = tail
            convo_ref[g, :, cols] = tail[SUBLANE - CONV_W + 1:SUBLANE, :]

    lane = lax.broadcasted_iota(jnp.int32, (valid, LANE), 1)
    for g in range(nseq):
        ba = ba_ref[g]
        xg = ba + gp_ref[0:1, :]
        softplus = jnp.maximum(xg, 0.0) + jnp.log1p(jnp.exp(-jnp.abs(xg)))
        gdec = -jnp.exp(gp_ref[1:2, :]) * softplus
        bg_s[g * tm:g * tm + valid, :] = jnp.where(lane < DN_HEADS, _sigmoid(ba), gdec)
        if valid < tm:
            bg_s[g * tm + valid:(g + 1) * tm, :] = jnp.zeros((tm - valid, LANE), F32)

    c = chunk
    ii = lax.broadcasted_iota(jnp.int32, (c, c), 0)
    jj = lax.broadcasted_iota(jnp.int32, (c, c), 1)
    incl = ii >= jj
    strict = ii > jj
    tril = jnp.where(incl, 1.0, 0.0).astype(BF16)
    sel = jnp.where(lax.broadcasted_iota(jnp.int32, (SUBLANE, LANE), 1)
                    == lax.broadcasted_iota(jnp.int32, (SUBLANE, LANE), 0) + DN_HEADS, 1.0, 0.0).astype(BF16)
    dnn = dnn_ref[...]

    heads = range(DN_HEADS)
    hcols = [slice(h * DN_DK, (h + 1) * DN_DK) for h in heads]
    acols = [slice(h * c, (h + 1) * c) for h in heads]
    n_chunks = rows_all // c
    chunks_per_seq = tm // c
    gpc = DN_GROUP if n_chunks % DN_GROUP == 0 else 1

    def chunk_rows(ci):
        return pl.ds(ci * c, c) if isinstance(ci, int) else pl.ds(pl.multiple_of(ci * c, c), c)

    def prep(cis):
        n_c = len(cis)
        rows = [chunk_rows(ci) for ci in cis]
        pairs = [(j, h) for j in range(n_c) for h in heads]
        bgc = [bg_s[r, :] for r in rows]
        gcum = [_dot_exact_lhs(tril, b) for b in bgc]
        gcum_t = [_dot_nt_exact_lhs(sel, g) for g in gcum]
        for j in range(n_c):
            eg_s[cis[j]] = jnp.broadcast_to(jnp.exp(gcum_t[j][:, c - 1:c]), (DN_HEADS, DN_DV))
        spread = lambda col: jnp.broadcast_to(col, (c, DN_DK))
        beta = [spread(bgc[j][:, h:h + 1]) for j, h in pairs]
        gc = [spread(gcum[j][:, DN_HEADS + h:DN_HEADS + h + 1]) for j, h in pairs]
        decay = [jnp.where(incl, jnp.exp(jnp.where(incl, gc[p][:, :c] - gcum_t[j][h:h + 1, :], 0.0)), 0.0)
                 for p, (j, h) in enumerate(pairs)]
        q = [q_s[rows[j], hcols[h]] for j, h in pairs]
        k = [k_s[rows[j], hcols[h]] for j, h in pairs]
        npair = range(len(pairs))
        kb = [k[p] * beta[p] for p in npair]
        k_bf = [k[p].astype(BF16) for p in npair]
        akk = [_dot_nt(kb[p].astype(BF16), k_bf[p]) for p in npair]
        aqk = [_dot_nt(q[p].astype(BF16), k_bf[p]) for p in npair]
        tinv = _tri_inv_all([jnp.where(strict, akk[p] * decay[p], 0.0) for p in npair], c)
        egc = [jnp.exp(gc[p]) for p in npair]
        rhs = [jnp.concatenate([v_s[rows[j], hcols[h]] * beta[p], kb[p] * egc[p]], axis=1)
               for p, (j, h) in enumerate(pairs)]
        uw = [_mm_bf(tinv[p], rhs[p]) for p in npair]
        sdt = w_s.dtype
        for p, (j, h) in enumerate(pairs):
            u_s[rows[j], hcols[h]] = uw[p][:, :DN_DV]
            w_s[rows[j], hcols[h]] = uw[p][:, DN_DV:].astype(sdt)
            qe_s[rows[j], hcols[h]] = (q[p] * egc[p]).astype(sdt)
            kd_s[rows[j], hcols[h]] = (k[p] * jnp.exp(gc[p][c - 1:c, :] - gc[p])).astype(sdt)
            a_s[rows[j], acols[h]] = jnp.where(incl, aqk[p] * decay[p], 0.0).astype(sdt)

    def state(cpos):
        trip = [(g, h) for g in range(nseq) for h in heads]
        rows = [chunk_rows(g * chunks_per_seq + cpos) for g in range(nseq)]
        local = chunk_rows(cpos)
        egb = [eg_s[g * chunks_per_seq + cpos] for g in range(nseq)]
        s_old = [so_ref[g, h] for g, h in trip]
        s_bf = [s.astype(BF16) for s in s_old]
        ws = [_dot(w_s[rows[g], hcols[h]].astype(BF16), s_bf[n]) for n, (g, h) in enumerate(trip)]
        qs = [_dot(qe_s[rows[g], hcols[h]].astype(BF16), s_bf[n]) for n, (g, h) in enumerate(trip)]
        vn_bf = [(u_s[rows[g], hcols[h]] - ws[n]).astype(BF16) for n, (g, h) in enumerate(trip)]
        av = [_dot(a_s[rows[g], acols[h]].astype(BF16), vn_bf[n]) for n, (g, h) in enumerate(trip)]
        kv = [_dot_tn(kd_s[rows[g], hcols[h]].astype(BF16), vn_bf[n]) for n, (g, h) in enumerate(trip)]
        for n, (g, h) in enumerate(trip):
            so_ref[g, h] = s_old[n] * egb[g][h:h + 1, :] + kv[n]
            o = _rms(qs[n] + av[n], dnn)
            if valid < tm:
                o_ref[g, :, hcols[h]] = (o[0:valid] * _silu(z_ref[g, :, hcols[h]])).astype(o_ref.dtype)
            else:
                o_ref[g, local, hcols[h]] = (o * _silu(z_ref[g, local, hcols[h]])).astype(o_ref.dtype)

    for gi in range(n_chunks // gpc):
        prep([gi * gpc + j for j in range(gpc)])
    if chunks_per_seq == 1:
        state(0)
    else:
        def state_body(cpos, carry):
            state(cpos)
            return carry
        lax.fori_loop(0, chunks_per_seq, state_body, 0)


def _dn_scratch(nseq, tm, chunk):
    rows = nseq * tm
    sdt = BF16 if chunk % 16 == 0 else F32
    return [pltpu.VMEM((rows, D_MODEL), F32),
            pltpu.VMEM((rows, D_MODEL), F32),
            pltpu.VMEM((rows, D_MODEL), F32),
            pltpu.VMEM((rows, LANE), F32),
            pltpu.VMEM((rows, D_MODEL), F32),
            pltpu.VMEM((rows, D_MODEL), sdt),
            pltpu.VMEM((rows, D_MODEL), sdt),
            pltpu.VMEM((rows, D_MODEL), sdt),
            pltpu.VMEM((rows, DN_HEADS * chunk), sdt),
            pltpu.VMEM((rows // chunk, DN_HEADS, DN_DV), F32)]


def _deltanet(u, z, ba, cw, gp, dnn, conv0, s0, *, nseq, tm, chunk, valid, o_dtype):
    batch, seq, _ = u.shape
    nt = max(seq // tm, 1)
    rows_in = tm if seq >= tm else seq
    assert rows_in == valid or valid == tm
    row = lambda b, t: (b, t, 0)
    per_b3 = lambda b, t: (b, 0, 0)
    per_b4 = lambda b, t: (b, 0, 0, 0)
    kern = functools.partial(_dn_kernel, nseq=nseq, tm=tm, chunk=chunk, valid=valid, fused=False)
    return pl.pallas_call(
        kern,
        out_shape=[jax.ShapeDtypeStruct((batch, seq, D_MODEL), o_dtype),
                   jax.ShapeDtypeStruct((batch, CONV_W - 1, CONV_CH), F32),
                   jax.ShapeDtypeStruct((batch, DN_HEADS, DN_DK, DN_DV), F32)],
        grid=(batch // nseq, nt),
        in_specs=[pl.BlockSpec((nseq, rows_in, CONV_CH), row),
                  pl.BlockSpec((nseq, rows_in, D_MODEL), row),
                  pl.BlockSpec((nseq, rows_in, LANE), row),
                  pl.BlockSpec((SUBLANE, CONV_CH), lambda b, t: (0, 0)),
                  pl.BlockSpec((SUBLANE, LANE), lambda b, t: (0, 0)),
                  pl.BlockSpec((1, DN_DV), lambda b, t: (0, 0)),
                  pl.BlockSpec((nseq, CONV_W - 1, CONV_CH), per_b3),
                  pl.BlockSpec((nseq, DN_HEADS, DN_DK, DN_DV), per_b4)],
        out_specs=[pl.BlockSpec((nseq, rows_in, D_MODEL), row),
                   pl.BlockSpec((nseq, CONV_W - 1, CONV_CH), per_b3),
                   pl.BlockSpec((nseq, DN_HEADS, DN_DK, DN_DV), per_b4)],
        scratch_shapes=[pltpu.VMEM((nseq, CONV_CH // LANE, SUBLANE + tm, LANE), F32)] + _dn_scratch(nseq, tm, chunk),
        compiler_params=_params(("arbitrary", "arbitrary")),
        name="deltanet",
    )(u, z, ba, cw, gp, dnn, conv0, s0)


def _deltanet_fused(q, k, v, z, ba, gp, dnn, s0, *, nseq, tm, chunk, o_dtype):
    batch, seq, _ = q.shape
    nt = seq // tm
    row = lambda b, t: (b, t, 0)
    per_b4 = lambda b, t: (b, 0, 0, 0)
    kern = functools.partial(_dn_kernel, nseq=nseq, tm=tm, chunk=chunk, valid=tm, fused=True)
    wide = pl.BlockSpec((nseq, tm, D_MODEL), row)
    return pl.pallas_call(
        kern,
        out_shape=[jax.ShapeDtypeStruct((batch, seq, D_MODEL), o_dtype),
                   jax.ShapeDtypeStruct((batch, DN_HEADS, DN_DK, DN_DV), F32)],
        grid=(batch // nseq, nt),
        in_specs=[wide, wide, wide, wide,
                  pl.BlockSpec((nseq, tm, LANE), row),
                  pl.BlockSpec((SUBLANE, LANE), lambda b, t: (0, 0)),
                  pl.BlockSpec((1, DN_DV), lambda b, t: (0, 0)),
                  pl.BlockSpec((nseq, DN_HEADS, DN_DK, DN_DV), per_b4)],
        out_specs=[wide, pl.BlockSpec((nseq, DN_HEADS, DN_DK, DN_DV), per_b4)],
        scratch_shapes=_dn_scratch(nseq, tm, chunk),
        compiler_params=_params(("arbitrary", "arbitrary")),
        name="deltanet_fused",
    )(q, k, v, z, ba, gp, dnn, s0)


def _rope(x, cos, sa, sb):
    cols = []
    for cb in range(x.shape[1] // LANE):
        xc = x[:, cb * LANE:(cb + 1) * LANE]
        cols.append(xc * cos + pltpu.roll(xc, LANE - ROT_DIM // 2, 1) * sa + pltpu.roll(xc, ROT_DIM // 2, 1) * sb)
    return cols[0] if len(cols) == 1 else jnp.concatenate(cols, axis=1)


def _head_pair_operands(x):
    lane = lax.broadcasted_iota(jnp.int32, (x.shape[0], LANE), 1)
    out = []
    for cb in range(x.shape[1] // LANE):
        c = x[:, cb * LANE:(cb + 1) * LANE]
        lo = jnp.where(lane < SW_HEAD_DIM, c, 0.0)
        hi = jnp.where(lane >= SW_HEAD_DIM, c, 0.0)
        out.append((lo, pltpu.roll(lo, SW_HEAD_DIM, 1)))
        out.append((pltpu.roll(hi, SW_HEAD_DIM, 1), hi))
    return out


def _ones_columns(w):
    rowi = lax.broadcasted_iota(jnp.int32, (4 * w, LANE), 0)
    lanei = lax.broadcasted_iota(jnp.int32, (4 * w, LANE), 1)
    return jnp.where((rowi < 2 * w) == (lanei < SW_HEAD_DIM), 1.0, 0.0).astype(BF16)


def _pair_softmax(scores, mask, sinks, lane, w):
    parts, es = [], []
    for half in range(2):
        s = jnp.where(mask, scores[:, half * 2 * w:(half + 1) * 2 * w], NEG_INF)
        m = jnp.maximum(jnp.max(s, axis=-1, keepdims=True), sinks[half])
        parts.append(jnp.exp(s - m).astype(BF16))
        es.append(jnp.exp(sinks[half] - m))
    return jnp.concatenate(parts, axis=1), jnp.where(lane < SW_HEAD_DIM, es[0], es[1])


def _swa_prompt_kernel(sinks_ref, qkv_ref, o_ref, kc_ref, vc_ref, kk2, vv2):
    n = pl.program_id(1)
    w = WINDOW

    @pl.when(n == 0)
    def _():
        kk2[...] = jnp.zeros_like(kk2)
        for j in range(SW_KV_HEADS):
            vv2[j, :, 0:LANE] = jnp.zeros((4 * w, LANE), BF16)
            vv2[j, :, LANE:2 * LANE] = _ones_columns(w)

    for j in range(SW_KV_HEADS):
        kk2[j, 0:w, :] = kk2[j, w:2 * w, :]
        kk2[j, 2 * w:3 * w, :] = kk2[j, 3 * w:4 * w, :]
        vv2[j, 0:w, 0:LANE] = vv2[j, w:2 * w, 0:LANE]
        vv2[j, 2 * w:3 * w, 0:LANE] = vv2[j, 3 * w:4 * w, 0:LANE]
    q = qkv_ref[:, 0:D_MODEL].astype(BF16)
    k = qkv_ref[:, D_MODEL:D_MODEL + SW_KV]
    v = qkv_ref[:, D_MODEL + SW_KV:D_MODEL + 2 * SW_KV]
    kc_ref[...] = k
    vc_ref[...] = v
    for j, (left, right) in enumerate(_head_pair_operands(k)):
        kk2[j, w:2 * w, :] = left.astype(BF16)
        kk2[j, 3 * w:4 * w, :] = right.astype(BF16)
    for j, (left, right) in enumerate(_head_pair_operands(v)):
        vv2[j, w:2 * w, 0:LANE] = left.astype(BF16)
        vv2[j, 3 * w:4 * w, 0:LANE] = right.astype(BF16)
    qi = lax.broadcasted_iota(jnp.int32, (w, 2 * w), 0) + w
    sj = lax.broadcasted_iota(jnp.int32, (w, 2 * w), 1)
    d = qi - sj
    mask = (d >= 0) & (d <= WINDOW) & ((sj >= w) | (n > 0))
    lane = lax.broadcasted_iota(jnp.int32, (w, LANE), 1)
    n_pb = D_MODEL // LANE
    scores = [_dot_nt(q[:, pb * LANE:(pb + 1) * LANE], kk2[pb // 2]) for pb in range(n_pb)]
    soft = [_pair_softmax(scores[pb], mask, (sinks_ref[2 * pb], sinks_ref[2 * pb + 1]), lane, w)
            for pb in range(n_pb)]
    outs = [_dot(soft[pb][0], vv2[pb // 2]) for pb in range(n_pb)]
    for pb in range(n_pb):
        o = outs[pb][:, 0:LANE] / (outs[pb][:, LANE:2 * LANE] + soft[pb][1])
        o_ref[:, pb * LANE:(pb + 1) * LANE] = o.astype(o_ref.dtype)


def _swa_prompt(sinks, qkv, *, batch, seq):
    nb = seq // WINDOW
    row = lambda b, n: (b * nb + n, 0)
    per_b = lambda b, n: (b, 0, 0)
    return pl.pallas_call(
        _swa_prompt_kernel,
        out_shape=[jax.ShapeDtypeStruct((batch * seq, D_MODEL), BF16),
                   jax.ShapeDtypeStruct((batch, WINDOW, SW_KV), F32),
                   jax.ShapeDtypeStruct((batch, WINDOW, SW_KV), F32)],
        grid=(batch, nb),
        in_specs=[pl.BlockSpec(memory_space=pltpu.SMEM),
                  pl.BlockSpec((WINDOW, D_MODEL + 2 * SW_KV), row)],
        out_specs=[pl.BlockSpec((WINDOW, D_MODEL), row),
                   pl.BlockSpec((None, WINDOW, SW_KV), per_b),
                   pl.BlockSpec((None, WINDOW, SW_KV), per_b)],
        scratch_shapes=[pltpu.VMEM((SW_KV_HEADS, 4 * WINDOW, LANE), BF16),
                        pltpu.VMEM((SW_KV_HEADS, 4 * WINDOW, 2 * LANE), BF16)],
        compiler_params=_params(("arbitrary", "arbitrary")),
        name="swa_prompt",
    )(sinks, qkv)


def _swa_sample_kernel(sinks_ref, qkv_ref, kbuf_ref, vbuf_ref, cos_ref, sa_ref, sb_ref,
                       o_ref, kc_ref, vc_ref, kk_s, vv_s, kk2, vv2, *, nseq, t_new):
    wb = WINDOW
    tp = t_new

    @pl.when(pl.program_id(0) == 0)
    def _():
        kk_s[...] = jnp.zeros_like(kk_s)
        vv_s[...] = jnp.zeros_like(vv_s)
        kk2[...] = jnp.zeros_like(kk2)
        for g in range(nseq):
            for j in range(SW_KV_HEADS):
                vv2[g, j, :, 0:LANE] = jnp.zeros((4 * wb, LANE), BF16)
                vv2[g, j, :, LANE:2 * LANE] = _ones_columns(wb)

    cos, sa, sb = cos_ref[0:tp, :], sa_ref[0:tp, :], sb_ref[0:tp, :]
    qs = []
    for g in range(nseq):
        q = (_rope(qkv_ref[g, :, 0:D_MODEL], cos, sa, sb) * (SW_HEAD_DIM ** -0.5)).astype(BF16)
        k_new = _rope(qkv_ref[g, :, D_MODEL:D_MODEL + SW_KV], cos, sa, sb)
        v_new = qkv_ref[g, :, D_MODEL + SW_KV:D_MODEL + 2 * SW_KV]
        k_old, v_old = kbuf_ref[g], vbuf_ref[g]
        kk_s[g, 0:wb, :] = k_old
        vv_s[g, 0:wb, :] = v_old
        kk_s[g, wb:wb + tp, :] = k_new
        vv_s[g, wb:wb + tp, :] = v_new
        kc_ref[g] = kk_s[g, pl.ds(t_new, wb), :]
        vc_ref[g] = vv_s[g, pl.ds(t_new, wb), :]
        for r0, n_r, kx, vx in ((0, wb, k_old, v_old), (wb, tp, k_new, v_new)):
            for j, (left, right) in enumerate(_head_pair_operands(kx)):
                kk2[g, j, r0:r0 + n_r, :] = left.astype(BF16)
                kk2[g, j, 2 * wb + r0:2 * wb + r0 + n_r, :] = right.astype(BF16)
            for j, (left, right) in enumerate(_head_pair_operands(vx)):
                vv2[g, j, r0:r0 + n_r, 0:LANE] = left.astype(BF16)
                vv2[g, j, 2 * wb + r0:2 * wb + r0 + n_r, 0:LANE] = right.astype(BF16)
        qs.append([jnp.concatenate([q[:, (2 * j) * LANE:(2 * j + 1) * LANE],
                                    q[:, (2 * j + 1) * LANE:(2 * j + 2) * LANE]], axis=0)
                   for j in range(SW_KV_HEADS)])
    rows = 2 * tp
    qt = lax.broadcasted_iota(jnp.int32, (rows, 2 * wb), 0) % tp
    sj = lax.broadcasted_iota(jnp.int32, (rows, 2 * wb), 1)
    d = qt + wb - sj
    mask = (d >= 0) & (d <= WINDOW)
    top = lax.broadcasted_iota(jnp.int32, (rows, 1), 0) < tp
    lane = lax.broadcasted_iota(jnp.int32, (rows, LANE), 1)
    items = [(g, j) for g in range(nseq) for j in range(SW_KV_HEADS)]
    scores = [_dot_nt(qs[g][j], kk2[g, j]) for g, j in items]
    soft = []
    for n, (g, j) in enumerate(items):
        sinks = [jnp.where(top, sinks_ref[4 * j + half], sinks_ref[4 * j + 2 + half]) for half in range(2)]
        soft.append(_pair_softmax(scores[n], mask, sinks, lane, wb))
    outs = [_dot(soft[n][0], vv2[g, j]) for n, (g, j) in enumerate(items)]
    for n, (g, j) in enumerate(items):
        o = outs[n][:, 0:LANE] / (outs[n][:, LANE:2 * LANE] + soft[n][1])
        o_ref[g, :, (2 * j) * LANE:(2 * j + 1) * LANE] = o[0:tp, :]
        o_ref[g, :, (2 * j + 1) * LANE:(2 * j + 2) * LANE] = o[tp:2 * tp, :]


def _swa_sample(sinks, qkv, kbuf, vbuf, cos, sa, sb, *, batch, nseq, t_new):
    per_b = lambda b: (b, 0, 0)
    tab = lambda b: (0, 0)
    kern = functools.partial(_swa_sample_kernel, nseq=nseq, t_new=t_new)
    return pl.pallas_call(
        kern,
        out_shape=[jax.ShapeDtypeStruct((batch, t_new, D_MODEL), F32),
                   jax.ShapeDtypeStruct((batch, WINDOW, SW_KV), F32),
                   jax.ShapeDtypeStruct((batch, WINDOW, SW_KV), F32)],
        grid=(batch // nseq,),
        in_specs=[pl.BlockSpec(memory_space=pltpu.SMEM),
                  pl.BlockSpec((nseq, t_new, D_MODEL + 2 * SW_KV), per_b),
                  pl.BlockSpec((nseq, WINDOW, SW_KV), per_b),
                  pl.BlockSpec((nseq, WINDOW, SW_KV), per_b),
                  pl.BlockSpec((SUBLANE, LANE), tab),
                  pl.BlockSpec((SUBLANE, LANE), tab),
                  pl.BlockSpec((SUBLANE, LANE), tab)],
        out_specs=[pl.BlockSpec((nseq, t_new, D_MODEL), per_b),
                   pl.BlockSpec((nseq, WINDOW, SW_KV), per_b),
                   pl.BlockSpec((nseq, WINDOW, SW_KV), per_b)],
        scratch_shapes=[pltpu.VMEM((nseq, WINDOW + SUBLANE, SW_KV), F32),
                        pltpu.VMEM((nseq, WINDOW + SUBLANE, SW_KV), F32),
                        pltpu.VMEM((nseq, SW_KV_HEADS, 4 * WINDOW, LANE), BF16),
                        pltpu.VMEM((nseq, SW_KV_HEADS, 4 * WINDOW, 2 * LANE), BF16)],
        compiler_params=_params(("arbitrary",)),
        name="swa_sample",
    )(sinks, qkv, kbuf, vbuf, cos, sa, sb)


def _out_ffn_kernel(x_ref, g2_ref, post2_ref, ga_ref, gb_ref, odn_ref, osw_ref, wo_ref,
                    sh_ref, sc_ref, g_ref, pre_ref, post_ref, wg_ref, wu_ref, wd_ref, o_ref,
                    *, per_token, tiles_per_seq):
    rows = _row_groups(x_ref.shape[0])
    g2 = _mod_rows(g2_ref, per_token, tiles_per_seq)
    sh = _mod_rows(sh_ref, per_token, tiles_per_seq)
    sc = _mod_rows(sc_ref, per_token, tiles_per_seq)
    g = _mod_rows(g_ref, per_token, tiles_per_seq)
    pick = (lambda m, r: m[r]) if per_token else (lambda m, r: m)
    f32 = lambda ref, r: ref[r, :].astype(F32)
    ys = [(_sigmoid(f32(ga_ref, r)) * f32(odn_ref, r)
           + _sigmoid(f32(gb_ref, r)) * f32(osw_ref, r)).astype(BF16) for r in rows]
    ps = [_dot(y, wo_ref[...]) for y in ys]
    xs = [x_ref[r, :] + pick(g2, r) * _rms(p, post2_ref[...]) for p, r in zip(ps, rows)]
    outs = _ffn_stages(xs, rows, pick, sh, sc, g, pre_ref[...], post_ref[...], wg_ref, wu_ref, wd_ref)
    for o, r in zip(outs, rows):
        o_ref[r, :] = o


def _out_ffn(x, mod, mod_block, seq_len, post2, gates, o_dn, o_sw, w_out, pre, post, wg, wu, wd):
    n = x.shape[0]
    tm = min(512, n)
    dff = wg.shape[1]
    kern = functools.partial(_out_ffn_kernel, per_token=mod_block is None, tiles_per_seq=max(seq_len // tm, 1))
    row = lambda i: (i, 0)
    return pl.pallas_call(
        kern,
        out_shape=jax.ShapeDtypeStruct((n, D_MODEL), F32),
        grid=(n // tm,),
        in_specs=[pl.BlockSpec((tm, D_MODEL), row)]
        + _mod_specs(mod_block, tm, (5,))
        + [_const_spec((1, D_MODEL)),
           pl.BlockSpec((tm, D_MODEL), lambda i: (i, 0)),
           pl.BlockSpec((tm, D_MODEL), lambda i: (i, 1)),
           pl.BlockSpec((tm, D_MODEL), row),
           pl.BlockSpec((tm, D_MODEL), row),
           _const_spec((D_MODEL, D_MODEL))]
        + _mod_specs(mod_block, tm, (6, 7, 8))
        + [_const_spec((1, D_MODEL)), _const_spec((1, D_MODEL)),
           _const_spec((D_MODEL, dff)), _const_spec((D_MODEL, dff)), _const_spec((dff, D_MODEL))],
        out_specs=pl.BlockSpec((tm, D_MODEL), row),
        compiler_params=_params(("arbitrary",)),
        name="out_ffn",
    )(x, mod, post2, gates, gates, o_dn, o_sw, w_out, mod, mod, mod, pre, post, wg, wu, wd)


def _rope_tables(pos):
    half = ROT_DIM // 2
    inv_freq = ROPE_THETA ** (-jnp.arange(half, dtype=F32) * (2.0 / ROT_DIM))
    ang = pos.astype(F32)[:, None] * inv_freq[None, :]
    cos, sin = jnp.cos(ang), jnp.sin(ang)
    n = pos.shape[0]
    rest = SW_HEAD_DIM - ROT_DIM
    c64 = jnp.concatenate([cos, cos, jnp.ones((n, rest), F32)], axis=1)
    a64 = jnp.concatenate([-sin, jnp.zeros((n, half + rest), F32)], axis=1)
    b64 = jnp.concatenate([jnp.zeros((n, half), F32), sin, jnp.zeros((n, rest), F32)], axis=1)
    rep = LANE // SW_HEAD_DIM
    return tuple(jnp.tile(t, (1, rep)) for t in (c64, a64, b64))


def _pad_rows(a, rows):
    return jnp.pad(a, ((0, rows - a.shape[0]),) + ((0, 0),) * (a.ndim - 1))


def _layer_sample(x, mod, seq, wts, conv0, s0, dn_fn, swa_fn):
    per_seq = lambda a: a.reshape(-1, seq, a.shape[-1])
    flat = lambda a: a.reshape(-1, a.shape[-1])
    x = _ffn(x, mod, 0, None, seq, wts["pre1"], wts["post1"], wts["wg1"], wts["wu1"], wts["wd1"])
    u, z, qkv, gates, ba = _proj(x, mod, None, seq, wts["pre2"], *wts["w_in"])
    o_dn, conv_new, s_new = dn_fn(per_seq(u), per_seq(z), per_seq(ba), wts["conv_w"], wts["gparam"],
                                  wts["dn_norm"], conv0, s0)
    o_sw, k_new, v_new = swa_fn(per_seq(qkv))
    x = _out_ffn(x, mod, None, seq, wts["post2"], gates, flat(o_dn), flat(o_sw), wts["w_out"],
                 wts["pre3"], wts["post3"], wts["wg2"], wts["wu2"], wts["wd2"])
    return x, (k_new, v_new, conv_new, s_new)


def _layer_prompt(x, mod, mod_block, batch, seq, wts, s0, rope, sink):
    per_seq = lambda a: a.reshape(batch, seq, a.shape[-1])
    x = _ffn(x, mod, 0, mod_block, seq, wts["pre1"], wts["post1"], wts["wg1"], wts["wu1"], wts["wd1"])
    q, k, v, z, qkv, gates, ba, conv_new = _proj_prompt(x, mod, mod_block, seq, wts["pre2"], *wts["w_in"],
                                                        wts["conv_w"], *rope)
    o_dn, s_new = _deltanet_fused(per_seq(q), per_seq(k), per_seq(v), per_seq(z), per_seq(ba), wts["gparam"],
                                  wts["dn_norm"], s0, nseq=batch, tm=DN_TILE, chunk=DN_CHUNK, o_dtype=BF16)
    o_sw, k_new, v_new = _swa_prompt(sink, qkv, batch=batch, seq=seq)
    x = _out_ffn(x, mod, mod_block, seq, wts["post2"], gates, o_dn.reshape(batch * seq, D_MODEL), o_sw,
                 wts["w_out"], wts["pre3"], wts["post3"], wts["wg2"], wts["wu2"], wts["wd2"])
    return x, (k_new, v_new, conv_new, s_new)


def kernel(x_prompt, x_sample, cache_swa_k, cache_swa_v, state_conv, state_delta, c_prompt, c_sample,
           w_ada, b_ada, ffn1_norm_pre, ffn1_norm_post, ffn1_w_gate, ffn1_w_up, ffn1_w_down,
           mix_norm_pre, mix_norm_post, w_in, conv_w, a_log, dt_bias, dn_norm, sinks, w_out,
           ffn2_norm_pre, ffn2_norm_post, ffn2_w_gate, ffn2_w_up, ffn2_w_down):
    depth = w_ada.shape[0]
    bp, tp, _ = x_prompt.shape
    bs, ts, _ = x_sample.shape
    assert tp % 512 == 0 and 1 <= ts <= SUBLANE and bs % SAMPLE_SEQS == 0 and (bs * ts) % SUBLANE == 0

    cos_p, sa_p, sb_p = _rope_tables(jnp.arange(tp))
    cos_s, sa_s, sb_s = (_pad_rows(t, SUBLANE) for t in _rope_tables(PAST_LEN + jnp.arange(ts)))

    xp = x_prompt.reshape(bp * tp, D_MODEL)
    xs = x_sample.reshape(bs * ts, D_MODEL)
    c_all = jnp.concatenate([jnp.repeat(c_sample, ts, axis=0), _pad_rows(c_prompt, SUBLANE)], axis=0)
    prompt_mod_block = bs * ts // SUBLANE

    outs_p, outs_s = [], []
    for l in range(depth):
        wi = w_in[l]
        n_ba = 2 * DN_HEADS
        w_in_a = wi[:, :P_QKV].astype(BF16)
        w_in_b = wi[:, P_QKV + n_ba:].astype(BF16)
        w_in_c = jnp.pad(wi[:, P_QKV:P_QKV + n_ba], ((0, 0), (0, LANE - n_ba))).astype(BF16)
        gparam = jnp.zeros((SUBLANE, LANE), F32)
        gparam = gparam.at[0, DN_HEADS:2 * DN_HEADS].set(dt_bias[l]).at[1, DN_HEADS:2 * DN_HEADS].set(a_log[l])
        wts = dict(
            pre1=ffn1_norm_pre[l][None], post1=ffn1_norm_post[l][None],
            wg1=ffn1_w_gate[l].astype(BF16), wu1=ffn1_w_up[l].astype(BF16), wd1=ffn1_w_down[l].astype(BF16),
            pre2=mix_norm_pre[l][None], post2=mix_norm_post[l][None], w_in=(w_in_a, w_in_b, w_in_c),
            conv_w=_pad_rows(conv_w[l], SUBLANE), gparam=gparam, dn_norm=dn_norm[l][None],
            w_out=w_out[l].astype(BF16),
            pre3=ffn2_norm_pre[l][None], post3=ffn2_norm_post[l][None],
            wg2=ffn2_w_gate[l].astype(BF16), wu2=ffn2_w_up[l].astype(BF16), wd2=ffn2_w_down[l].astype(BF16),
        )
        b_ada_l = b_ada[l][None]
        mod = _ada(c_all, w_ada[l], b_ada_l)
        sink_l = sinks[l]

        s0_p = jnp.zeros((bp, DN_HEADS, DN_DK, DN_DV), F32)
        xp, st_p = _layer_prompt(xp, mod, prompt_mod_block, bp, tp, wts, s0_p, (cos_p, sa_p, sb_p), sink_l)

        kbuf = cache_swa_k[l].reshape(bs, WINDOW, SW_KV)
        vbuf = cache_swa_v[l].reshape(bs, WINDOW, SW_KV)
        dn_s = functools.partial(_deltanet, nseq=SAMPLE_SEQS, tm=SUBLANE, chunk=SUBLANE, valid=ts, o_dtype=F32)

        swa_s = lambda qkv: _swa_sample(sink_l, qkv, kbuf, vbuf, cos_s, sa_s, sb_s,
                                        batch=bs, nseq=SAMPLE_SEQS, t_new=ts)
        xs, st_s = _layer_sample(xs, mod, ts, wts, state_conv[l], state_delta[l], dn_s, swa_s)
        outs_p.append(st_p)
        outs_s.append(st_s)

    def stack(outs, i):
        return jnp.stack([o[i] for o in outs])

    def kv5(a, batch):
        return a.reshape(depth, batch, WINDOW, SW_KV_HEADS, SW_HEAD_DIM)

    tail = slice(SUBLANE - CONV_W + 1, SUBLANE)
    y_p = xp.reshape(bp, tp, D_MODEL)
    y_s = xs.reshape(bs, ts, D_MODEL)
    return (y_p, y_s,
            kv5(stack(outs_p, 0), bp), kv5(stack(outs_p, 1), bp),
            stack(outs_p, 2)[:, :, tail], stack(outs_p, 3),
            kv5(stack(outs_s, 0), bs), kv5(stack(outs_s, 1), bs),
            stack(outs_s, 2), stack(outs_s, 3))
```

```python
import functools

import jax
import jax.numpy as jnp
from jax import lax
from jax.experimental import pallas as pl
from jax.experimental.pallas import tpu as pltpu

F32, BF16 = jnp.float32, jnp.bfloat16

D_MODEL = 1024
DN_HEADS, DN_DK, DN_DV = 8, 128, 128
CONV_W = 4
CONV_CH = 3 * D_MODEL
DN_CHUNK = 64
SW_HEAD_DIM, SW_HEADS, SW_KV_HEADS, SW_GROUP = 64, 16, 4, 4
SW_KV = SW_KV_HEADS * SW_HEAD_DIM
WINDOW = 128
ROT_DIM = 16
ROPE_THETA = 500000.0
PAST_LEN = 8192
EPS = 1e-6
NEG_INF = -1e30

LANE = 128
SUBLANE = 8
VMEM_LIMIT = 56 * 1024 * 1024

P_U, P_Z, P_QKV, P_G, P_BA = 0, 3072, 4096, 5632, 7680
P_TOTAL = 7808

DN_TILE = 128
DN_GROUP = 4
SAMPLE_SEQS = 8
FFN_TILE = 1024
FFN_SUBTILES = 4
ADA_COLS = 1024
PROJ_FUSED_TILE = 512


def _dot(a, b):
    return jnp.dot(a, b, preferred_element_type=F32)


def _dot_nt(a, b):
    return lax.dot_general(a, b, (((1,), (1,)), ((), ())), preferred_element_type=F32)


def _dot_tn(a, b):
    return lax.dot_general(a, b, (((0,), (0,)), ((), ())), preferred_element_type=F32)


def _split3(x):
    hi = x.astype(BF16)
    r1 = x - hi.astype(F32)
    mid = r1.astype(BF16)
    lo = (r1 - mid.astype(F32)).astype(BF16)
    return hi, mid, lo


def _dot_exact_lhs(a_bf, b):
    return sum(_dot(a_bf, t) for t in _split3(b))


def _dot_nt_exact_lhs(a_bf, b):
    return sum(_dot_nt(a_bf, t) for t in _split3(b))


def _mm_bf(a, b):
    return _dot(a.astype(BF16), b.astype(BF16))


def _sigmoid(x):
    return 0.5 * jnp.tanh(0.5 * x) + 0.5


def _silu(x):
    return x * _sigmoid(x)


def _rms(x, gain):
    ms = jnp.mean(x * x, axis=-1, keepdims=True)
    return x * lax.rsqrt(ms + EPS) * gain


def _mod_rows(ref, per_token, tiles_per_seq):
    if per_token:
        return ref[...]
    b = pl.program_id(0) // tiles_per_seq
    return ref[pl.ds(b, 1), :]


def _const_spec(shape):
    return pl.BlockSpec(shape, lambda *_: (0,) * len(shape), pipeline_mode=pl.Buffered(1))


def _params(sem):
    return pltpu.CompilerParams(dimension_semantics=sem, vmem_limit_bytes=VMEM_LIMIT)


def _ada_kernel(c_ref, w_ref, b_ref, o_ref):
    h = _silu(c_ref[...]).astype(BF16)
    o_ref[...] = _dot(h, w_ref[...].astype(BF16)) + b_ref[...]


def _ada(c, w, b):
    rows, n = c.shape[0], w.shape[1]
    tn = ADA_COLS
    return pl.pallas_call(
        _ada_kernel,
        out_shape=jax.ShapeDtypeStruct((rows, n), F32),
        grid=(n // tn,),
        in_specs=[
            pl.BlockSpec((rows, D_MODEL), lambda j: (0, 0)),
            pl.BlockSpec((D_MODEL, tn), lambda j: (0, j)),
            pl.BlockSpec((1, tn), lambda j: (0, j)),
        ],
        out_specs=pl.BlockSpec((rows, tn), lambda j: (0, j)),
        compiler_params=_params(("arbitrary",)),
        name="ada",
    )(c, w, b)


def _mod_specs(mod_block, tm, cols):
    if mod_block is None:
        return [pl.BlockSpec((tm, D_MODEL), functools.partial(lambda i, c: (i, c), c=c)) for c in cols]
    return [pl.BlockSpec((SUBLANE, D_MODEL), functools.partial(lambda i, c: (mod_block, c), c=c)) for c in cols]


def _ffn_stages(xs, rows, pick, sh, sc, g, pre, post, wg_ref, wu_ref, wd_ref):
    hs = [(_rms(x, pre) * (1.0 + pick(sc, r)) + pick(sh, r)).astype(BF16) for x, r in zip(xs, rows)]
    gates = [_dot(h, wg_ref[...]) for h in hs]
    ups = [_dot(h, wu_ref[...]) for h in hs]
    acts = [(_silu(a) * b).astype(BF16) for a, b in zip(gates, ups)]
    ys = [_dot(a, wd_ref[...]) for a in acts]
    return [x + 0.5 * pick(g, r) * _rms(y, post) for x, y, r in zip(xs, ys, rows)]


def _row_groups(tm):
    return [slice(i * tm // FFN_SUBTILES, (i + 1) * tm // FFN_SUBTILES) for i in range(FFN_SUBTILES)]


def _ffn_kernel(x_ref, sh_ref, sc_ref, g_ref, pre_ref, post_ref, wg_ref, wu_ref, wd_ref, o_ref,
                *, per_token, tiles_per_seq):
    rows = _row_groups(x_ref.shape[0])
    sh = _mod_rows(sh_ref, per_token, tiles_per_seq)
    sc = _mod_rows(sc_ref, per_token, tiles_per_seq)
    g = _mod_rows(g_ref, per_token, tiles_per_seq)
    pick = (lambda m, r: m[r]) if per_token else (lambda m, r: m)
    xs = [x_ref[r, :] for r in rows]
    outs = _ffn_stages(xs, rows, pick, sh, sc, g, pre_ref[...], post_ref[...], wg_ref, wu_ref, wd_ref)
    for o, r in zip(outs, rows):
        o_ref[r, :] = o


def _ffn(x, mod, col0, mod_block, seq_len, pre, post, wg, wu, wd):
    n = x.shape[0]
    tm = min(FFN_TILE, n if mod_block is None else seq_len)
    dff = wg.shape[1]
    kern = functools.partial(_ffn_kernel, per_token=mod_block is None, tiles_per_seq=max(seq_len // tm, 1))
    row = lambda i: (i, 0)
    return pl.pallas_call(
        kern,
        out_shape=jax.ShapeDtypeStruct((n, D_MODEL), F32),
        grid=(n // tm,),
        in_specs=[pl.BlockSpec((tm, D_MODEL), row)]
        + _mod_specs(mod_block, tm, (col0, col0 + 1, col0 + 2))
        + [_const_spec((1, D_MODEL)), _const_spec((1, D_MODEL)),
           _const_spec((D_MODEL, dff)), _const_spec((D_MODEL, dff)), _const_spec((dff, D_MODEL))],
        out_specs=pl.BlockSpec((tm, D_MODEL), row),
        compiler_params=_params(("arbitrary",)),
        name="ffn",
    )(x, mod, mod, mod, pre, post, wg, wu, wd)


def _proj_kernel(x_ref, sh_ref, sc_ref, pre_ref, wa_ref, wb_ref, wc_ref, u_ref, z_ref, qkv_ref, g_ref, ba_ref,
                 *, per_token, tiles_per_seq):
    sh = _mod_rows(sh_ref, per_token, tiles_per_seq)
    sc = _mod_rows(sc_ref, per_token, tiles_per_seq)
    h = (_rms(x_ref[...], pre_ref[...]) * (1.0 + sc) + sh).astype(BF16)
    u_ref[...] = _dot(h, wa_ref[:, 0:P_Z])
    z_ref[...] = _dot(h, wa_ref[:, P_Z:P_QKV])
    qkv_ref[...] = _dot(h, wb_ref[:, 0:P_G - P_QKV])
    g_ref[...] = _dot(h, wb_ref[:, P_G - P_QKV:P_BA - P_QKV]).astype(g_ref.dtype)
    ba_ref[...] = _dot(h, wc_ref[...])


def _proj(x, mod, mod_block, seq_len, pre, wa, wb, wc):
    n = x.shape[0]
    tm = min(512, n)
    kern = functools.partial(_proj_kernel, per_token=mod_block is None, tiles_per_seq=max(seq_len // tm, 1))
    row = lambda i: (i, 0)
    widths = (P_Z - P_U, P_QKV - P_Z, P_G - P_QKV, P_BA - P_G, P_TOTAL - P_BA)
    return pl.pallas_call(
        kern,
        out_shape=[jax.ShapeDtypeStruct((n, wd), BF16 if wd == P_BA - P_G else F32) for wd in widths],
        grid=(n // tm,),
        in_specs=[pl.BlockSpec((tm, D_MODEL), row)]
        + _mod_specs(mod_block, tm, (3, 4))
        + [_const_spec((1, D_MODEL)), _const_spec(wa.shape), _const_spec(wb.shape), _const_spec(wc.shape)],
        out_specs=[pl.BlockSpec((tm, wd), row) for wd in widths],
        compiler_params=_params(("arbitrary",)),
        name="proj",
    )(x, mod, mod, pre, wa, wb, wc)


def _conv_norm(u_cols, ubuf_cb, cw_ref, cols, rows=None):
    rows = u_cols.shape[0] if rows is None else rows
    ubuf_cb[SUBLANE:SUBLANE + u_cols.shape[0], :] = u_cols
    acc = ubuf_cb[pl.ds(SUBLANE - CONV_W + 1, rows), :] * cw_ref[0:1, cols]
    for i in range(1, CONV_W):
        acc = acc + ubuf_cb[pl.ds(SUBLANE - CONV_W + 1 + i, rows), :] * cw_ref[i:i + 1, cols]
    return _silu(acc)


def _store_qkv(a, cb, q_dst, k_dst, v_dst, rows):
    sec, off = divmod(cb * LANE, D_MODEL)
    dst = slice(off, off + LANE)
    if sec == 2:
        v_dst[rows, dst] = a
    else:
        an = a * lax.rsqrt(jnp.sum(a * a, axis=-1, keepdims=True) + EPS)
        if sec == 0:
            q_dst[rows, dst] = an * (DN_DK ** -0.5)
        else:
            k_dst[rows, dst] = an


def _proj_prompt_kernel(x_ref, sh_ref, sc_ref, pre_ref, wa_ref, wb_ref, wc_ref, cw_ref, cos_ref, sa_ref, sb_ref,
                        q_ref, k_ref, v_ref, z_ref, qkv_ref, g_ref, ba_ref, tail_ref, ubuf, *, tiles_per_seq):
    tm = x_ref.shape[0]
    n_cb = CONV_CH // LANE
    all_rows = slice(0, tm)

    @pl.when(pl.program_id(0) % tiles_per_seq == 0)
    def _():
        ubuf[:, 0:SUBLANE, :] = jnp.zeros((n_cb, SUBLANE, LANE), F32)

    sh = _mod_rows(sh_ref, False, tiles_per_seq)
    sc = _mod_rows(sc_ref, False, tiles_per_seq)
    h = (_rms(x_ref[...], pre_ref[...]) * (1.0 + sc) + sh).astype(BF16)
    u = _dot(h, wa_ref[:, 0:P_Z])
    z_ref[...] = _dot(h, wa_ref[:, P_Z:P_QKV])
    s = _dot(h, wb_ref[:, 0:P_G - P_QKV])
    g_ref[...] = _dot(h, wb_ref[:, P_G - P_QKV:P_BA - P_QKV]).astype(g_ref.dtype)
    ba_ref[...] = _dot(h, wc_ref[...])
    for cb in range(n_cb):
        cols = slice(cb * LANE, (cb + 1) * LANE)
        a = _conv_norm(u[:, cols], ubuf.at[cb], cw_ref, cols)
        _store_qkv(a, cb, q_ref, k_ref, v_ref, all_rows)
        tail = ubuf[cb, pl.ds(tm, SUBLANE), :]
        ubuf[cb, 0:SUBLANE, :] = tail
        tail_ref[:, cols] = tail
    cos, sa, sb = cos_ref[...], sa_ref[...], sb_ref[...]
    qkv_ref[:, 0:D_MODEL] = _rope(s[:, 0:D_MODEL], cos, sa, sb) * (SW_HEAD_DIM ** -0.5)
    qkv_ref[:, D_MODEL:D_MODEL + SW_KV] = _rope(s[:, D_MODEL:D_MODEL + SW_KV], cos, sa, sb)
    qkv_ref[:, D_MODEL + SW_KV:D_MODEL + 2 * SW_KV] = s[:, D_MODEL + SW_KV:D_MODEL + 2 * SW_KV]


def _proj_prompt(x, mod, mod_block, seq_len, pre, wa, wb, wc, cw, cos, sa, sb):
    n = x.shape[0]
    tm = PROJ_FUSED_TILE
    tps = seq_len // tm
    kern = functools.partial(_proj_prompt_kernel, tiles_per_seq=tps)
    row = lambda i: (i, 0)
    tab = lambda i: (i % tps, 0)
    widths = (D_MODEL, D_MODEL, D_MODEL, P_QKV - P_Z, P_G - P_QKV, P_BA - P_G, P_TOTAL - P_BA)
    dtypes = (F32, F32, F32, F32, F32, BF16, F32)
    return pl.pallas_call(
        kern,
        out_shape=[jax.ShapeDtypeStruct((n, wd), dt) for wd, dt in zip(widths, dtypes)]
        + [jax.ShapeDtypeStruct((n // seq_len, SUBLANE, CONV_CH), F32)],
        grid=(n // tm,),
        in_specs=[pl.BlockSpec((tm, D_MODEL), row)]
        + _mod_specs(mod_block, tm, (3, 4))
        + [_const_spec((1, D_MODEL)), _const_spec(wa.shape), _const_spec(wb.shape), _const_spec(wc.shape),
           _const_spec(cw.shape),
           pl.BlockSpec((tm, LANE), tab), pl.BlockSpec((tm, LANE), tab), pl.BlockSpec((tm, LANE), tab)],
        out_specs=[pl.BlockSpec((tm, wd), row) for wd in widths]
        + [pl.BlockSpec((None, SUBLANE, CONV_CH), lambda i: (i // tps, 0, 0))],
        scratch_shapes=[pltpu.VMEM((CONV_CH // LANE, SUBLANE + tm, LANE), F32)],
        compiler_params=_params(("arbitrary",)),
        name="proj_prompt",
    )(x, mod, mod, pre, wa, wb, wc, cw, cos, sa, sb)


def _tri_inv_all(lows, c):
    ii = lax.broadcasted_iota(jnp.int32, (c, c), 0)
    jj = lax.broadcasted_iota(jnp.int32, (c, c), 1)
    eye = jnp.where(ii == jj, 1.0, 0.0).astype(F32)
    xs = [eye - jnp.where(ii // 2 == jj // 2, low, 0.0) for low in lows]
    b = 2
    while b < c:
        join = (ii // (2 * b) == jj // (2 * b)) & (ii // b != jj // b)
        ys = [_mm_bf(x, jnp.where(join, low, 0.0)) for x, low in zip(xs, lows)]
        xs = [x - _mm_bf(y, x) for x, y in zip(xs, ys)]
        b *= 2
    return xs


def _dn_kernel(*refs, nseq, tm, chunk, valid, fused):
    if fused:
        (q_ref, k_ref, v_ref, z_ref, ba_ref, gp_ref, dnn_ref, s0_ref, o_ref, so_ref,
         q_s, k_s, v_s, bg_s, u_s, wq_s, kd_s, a_s, eg_s) = refs
    else:
        (u_ref, z_ref, ba_ref, cw_ref, gp_ref, dnn_ref, conv0_ref, s0_ref, o_ref, convo_ref, so_ref,
         ubuf, q_s, k_s, v_s, bg_s, u_s, wq_s, kd_s, a_s, eg_s) = refs
    t = pl.program_id(1)
    rows_all = nseq * tm
    n_cb = CONV_CH // LANE

    @pl.when(t == 0)
    def _():
        so_ref[...] = s0_ref[...]
        if not fused:
            for g in range(nseq):
                for cb in range(n_cb):
                    ubuf[g, cb, 0:SUBLANE, :] = jnp.zeros((SUBLANE, LANE), F32)
                    ubuf[g, cb, SUBLANE - CONV_W + 1:SUBLANE, :] = conv0_ref[g, :, cb * LANE:(cb + 1) * LANE]
                    if valid < tm:
                        ubuf[g, cb, SUBLANE + valid:SUBLANE + tm, :] = jnp.zeros((tm - valid, LANE), F32)

    for g in range(nseq):
        rws = slice(g * tm, (g + 1) * tm)
        if fused:
            q_s[rws, :] = q_ref[g]
            k_s[rws, :] = k_ref[g]
            v_s[rws, :] = v_ref[g]
            continue
        for cb in range(n_cb):
            cols = slice(cb * LANE, (cb + 1) * LANE)
            a = _conv_norm(u_ref[g, :, cols], ubuf.at[g, cb], cw_ref, cols, rows=tm)
            _store_qkv(a, cb, q_s, k_s, v_s, rws)
            tail = ubuf[g, cb, pl.ds(valid, SUBLANE), :]
            ubuf[g, cb, 0:SUBLANE, :] = tail
            convo_ref[g, :, cols] = tail[SUBLANE - CONV_W + 1:SUBLANE, :]

    lane = lax.broadcasted_iota(jnp.int32, (valid, LANE), 1)
    for g in range(nseq):
        ba = ba_ref[g]
        xg = ba + gp_ref[0:1, :]
        softplus = jnp.maximum(xg, 0.0) + jnp.log1p(jnp.exp(-jnp.abs(xg)))
        gdec = -jnp.exp(gp_ref[1:2, :]) * softplus
        bg_s[g * tm:g * tm + valid, :] = jnp.where(lane < DN_HEADS, _sigmoid(ba), gdec)
        if valid < tm:
            bg_s[g * tm + valid:(g + 1) * tm, :] = jnp.zeros((tm - valid, LANE), F32)

    c = chunk
    ii = lax.broadcasted_iota(jnp.int32, (c, c), 0)
    jj = lax.broadcasted_iota(jnp.int32, (c, c), 1)
    incl = ii >= jj
    strict = ii > jj
    tril = jnp.where(incl, 1.0, 0.0).astype(BF16)
    sel = jnp.where(lax.broadcasted_iota(jnp.int32, (SUBLANE, LANE), 1)
                    == lax.broadcasted_iota(jnp.int32, (SUBLANE, LANE), 0) + DN_HEADS, 1.0, 0.0).astype(BF16)
    dnn = dnn_ref[...]

    heads = range(DN_HEADS)
    hcols = [slice(h * DN_DK, (h + 1) * DN_DK) for h in heads]
    acols = [slice(h * c, (h + 1) * c) for h in heads]
    n_chunks = rows_all // c
    chunks_per_seq = tm // c
    gpc = DN_GROUP if n_chunks % DN_GROUP == 0 else 1

    def chunk_rows(ci):
        return pl.ds(ci * c, c) if isinstance(ci, int) else pl.ds(pl.multiple_of(ci * c, c), c)

    def prep(cis):
        n_c = len(cis)
        rows = [chunk_rows(ci) for ci in cis]
        pairs = [(j, h) for j in range(n_c) for h in heads]
        bgc = [bg_s[r, :] for r in rows]
        gcum = [_dot_exact_lhs(tril, b) for b in bgc]
        gcum_t = [_dot_nt_exact_lhs(sel, g) for g in gcum]
        for j in range(n_c):
            eg_s[cis[j]] = jnp.broadcast_to(jnp.exp(gcum_t[j][:, c - 1:c]), (DN_HEADS, DN_DV))
        spread = lambda col: jnp.broadcast_to(col, (c, DN_DK))
        beta = [spread(bgc[j][:, h:h + 1]) for j, h in pairs]
        gc = [spread(gcum[j][:, DN_HEADS + h:DN_HEADS + h + 1]) for j, h in pairs]
        decay = [jnp.where(incl, jnp.exp(jnp.where(incl, gc[p][:, :c] - gcum_t[j][h:h + 1, :], 0.0)), 0.0)
                 for p, (j, h) in enumerate(pairs)]
        q = [q_s[rows[j], hcols[h]] for j, h in pairs]
        k = [k_s[rows[j], hcols[h]] for j, h in pairs]
        npair = range(len(pairs))
        kb = [k[p] * beta[p] for p in npair]
        k_bf = [k[p].astype(BF16) for p in npair]
        akk = [_dot_nt(kb[p].astype(BF16), k_bf[p]) for p in npair]
        aqk = [_dot_nt(q[p].astype(BF16), k_bf[p]) for p in npair]
        tinv = _tri_inv_all([jnp.where(strict, akk[p] * decay[p], 0.0) for p in npair], c)
        egc = [jnp.exp(gc[p]) for p in npair]
        rhs = [jnp.concatenate([v_s[rows[j], hcols[h]] * beta[p], kb[p] * egc[p]], axis=1)
               for p, (j, h) in enumerate(pairs)]
        uw = [_mm_bf(tinv[p], rhs[p]) for p in npair]
        sdt = wq_s.dtype
        for p, (j, h) in enumerate(pairs):
            u_s[rows[j], hcols[h]] = uw[p][:, :DN_DV]
            wq_s[cis[j], 0:c, hcols[h]] = uw[p][:, DN_DV:].astype(sdt)
            wq_s[cis[j], c:2 * c, hcols[h]] = (q[p] * egc[p]).astype(sdt)
            kd_s[rows[j], hcols[h]] = (k[p] * jnp.exp(gc[p][c - 1:c, :] - gc[p])).astype(sdt)
            a_s[rows[j], acols[h]] = jnp.where(incl, aqk[p] * decay[p], 0.0).astype(sdt)

    def state(cpos):
        trip = [(g, h) for g in range(nseq) for h in heads]
        rows = [chunk_rows(g * chunks_per_seq + cpos) for g in range(nseq)]
        local = chunk_rows(cpos)
        egb = [eg_s[g * chunks_per_seq + cpos] for g in range(nseq)]
        s_old = [so_ref[g, h] for g, h in trip]
        s_bf = [s.astype(BF16) for s in s_old]
        wqs = [_dot(wq_s[g * chunks_per_seq + cpos, :, hcols[h]].astype(BF16), s_bf[n])
               for n, (g, h) in enumerate(trip)]
        ws = [r[0:c] for r in wqs]
        qs = [r[c:2 * c] for r in wqs]
        vn_bf = [(u_s[rows[g], hcols[h]] - ws[n]).astype(BF16) for n, (g, h) in enumerate(trip)]
        av = [_dot(a_s[rows[g], acols[h]].astype(BF16), vn_bf[n]) for n, (g, h) in enumerate(trip)]
        kv = [_dot_tn(kd_s[rows[g], hcols[h]].astype(BF16), vn_bf[n]) for n, (g, h) in enumerate(trip)]
        for n, (g, h) in enumerate(trip):
            so_ref[g, h] = s_old[n] * egb[g][h:h + 1, :] + kv[n]
            o = _rms(qs[n] + av[n], dnn)
            if valid < tm:
                o_ref[g, :, hcols[h]] = (o[0:valid] * _silu(z_ref[g, :, hcols[h]])).astype(o_ref.dtype)
            else:
                o_ref[g, local, hcols[h]] = (o * _silu(z_ref[g, local, hcols[h]])).astype(o_ref.dtype)

    for gi in range(n_chunks // gpc):
        prep([gi * gpc + j for j in range(gpc)])
    if chunks_per_seq == 1:
        state(0)
    else:
        def state_body(cpos, carry):
            state(cpos)
            return carry
        lax.fori_loop(0, chunks_per_seq, state_body, 0)


def _dn_scratch(nseq, tm, chunk):
    rows = nseq * tm
    sdt = BF16 if chunk % 16 == 0 else F32
    return [pltpu.VMEM((rows, D_MODEL), F32),
            pltpu.VMEM((rows, D_MODEL), F32),
            pltpu.VMEM((rows, D_MODEL), F32),
            pltpu.VMEM((rows, LANE), F32),
            pltpu.VMEM((rows, D_MODEL), F32),
            pltpu.VMEM((rows // chunk, 2 * chunk, D_MODEL), sdt),
            pltpu.VMEM((rows, D_MODEL), sdt),
            pltpu.VMEM((rows, DN_HEADS * chunk), sdt),
            pltpu.VMEM((rows // chunk, DN_HEADS, DN_DV), F32)]


def _deltanet(u, z, ba, cw, gp, dnn, conv0, s0, *, nseq, tm, chunk, valid, o_dtype):
    batch, seq, _ = u.shape
    nt = max(seq // tm, 1)
    rows_in = tm if seq >= tm else seq
    assert rows_in == valid or valid == tm
    row = lambda b, t: (b, t, 0)
    per_b3 = lambda b, t: (b, 0, 0)
    per_b4 = lambda b, t: (b, 0, 0, 0)
    kern = functools.partial(_dn_kernel, nseq=nseq, tm=tm, chunk=chunk, valid=valid, fused=False)
    return pl.pallas_call(
        kern,
        out_shape=[jax.ShapeDtypeStruct((batch, seq, D_MODEL), o_dtype),
                   jax.ShapeDtypeStruct((batch, CONV_W - 1, CONV_CH), F32),
                   jax.ShapeDtypeStruct((batch, DN_HEADS, DN_DK, DN_DV), F32)],
        grid=(batch // nseq, nt),
        in_specs=[pl.BlockSpec((nseq, rows_in, CONV_CH), row),
                  pl.BlockSpec((nseq, rows_in, D_MODEL), row),
                  pl.BlockSpec((nseq, rows_in, LANE), row),
                  pl.BlockSpec((SUBLANE, CONV_CH), lambda b, t: (0, 0)),
                  pl.BlockSpec((SUBLANE, LANE), lambda b, t: (0, 0)),
                  pl.BlockSpec((1, DN_DV), lambda b, t: (0, 0)),
                  pl.BlockSpec((nseq, CONV_W - 1, CONV_CH), per_b3),
                  pl.BlockSpec((nseq, DN_HEADS, DN_DK, DN_DV), per_b4)],
        out_specs=[pl.BlockSpec((nseq, rows_in, D_MODEL), row),
                   pl.BlockSpec((nseq, CONV_W - 1, CONV_CH), per_b3),
                   pl.BlockSpec((nseq, DN_HEADS, DN_DK, DN_DV), per_b4)],
        scratch_shapes=[pltpu.VMEM((nseq, CONV_CH // LANE, SUBLANE + tm, LANE), F32)] + _dn_scratch(nseq, tm, chunk),
        compiler_params=_params(("arbitrary", "arbitrary")),
        name="deltanet",
    )(u, z, ba, cw, gp, dnn, conv0, s0)


def _deltanet_fused(q, k, v, z, ba, gp, dnn, s0, *, nseq, tm, chunk, o_dtype):
    batch, seq, _ = q.shape
    nt = seq // tm
    row = lambda b, t: (b, t, 0)
    per_b4 = lambda b, t: (b, 0, 0, 0)
    kern = functools.partial(_dn_kernel, nseq=nseq, tm=tm, chunk=chunk, valid=tm, fused=True)
    wide = pl.BlockSpec((nseq, tm, D_MODEL), row)
    return pl.pallas_call(
        kern,
        out_shape=[jax.ShapeDtypeStruct((batch, seq, D_MODEL), o_dtype),
                   jax.ShapeDtypeStruct((batch, DN_HEADS, DN_DK, DN_DV), F32)],
        grid=(batch // nseq, nt),
        in_specs=[wide, wide, wide, wide,
                  pl.BlockSpec((nseq, tm, LANE), row),
                  pl.BlockSpec((SUBLANE, LANE), lambda b, t: (0, 0)),
                  pl.BlockSpec((1, DN_DV), lambda b, t: (0, 0)),
                  pl.BlockSpec((nseq, DN_HEADS, DN_DK, DN_DV), per_b4)],
        out_specs=[wide, pl.BlockSpec((nseq, DN_HEADS, DN_DK, DN_DV), per_b4)],
        scratch_shapes=_dn_scratch(nseq, tm, chunk),
        compiler_params=_params(("arbitrary", "arbitrary")),
        name="deltanet_fused",
    )(q, k, v, z, ba, gp, dnn, s0)


def _rope(x, cos, sa, sb):
    cols = []
    for cb in range(x.shape[1] // LANE):
        xc = x[:, cb * LANE:(cb + 1) * LANE]
        cols.append(xc * cos + pltpu.roll(xc, LANE - ROT_DIM // 2, 1) * sa + pltpu.roll(xc, ROT_DIM // 2, 1) * sb)
    return cols[0] if len(cols) == 1 else jnp.concatenate(cols, axis=1)


def _head_pair_operands(x):
    lane = lax.broadcasted_iota(jnp.int32, (x.shape[0], LANE), 1)
    out = []
    for cb in range(x.shape[1] // LANE):
        c = x[:, cb * LANE:(cb + 1) * LANE]
        lo = jnp.where(lane < SW_HEAD_DIM, c, 0.0)
        hi = jnp.where(lane >= SW_HEAD_DIM, c, 0.0)
        out.append((lo, pltpu.roll(lo, SW_HEAD_DIM, 1)))
        out.append((pltpu.roll(hi, SW_HEAD_DIM, 1), hi))
    return out


def _ones_columns(w):
    rowi = lax.broadcasted_iota(jnp.int32, (4 * w, LANE), 0)
    lanei = lax.broadcasted_iota(jnp.int32, (4 * w, LANE), 1)
    return jnp.where((rowi < 2 * w) == (lanei < SW_HEAD_DIM), 1.0, 0.0).astype(BF16)


def _pair_softmax(scores, mask, sinks, lane, w):
    parts, es = [], []
    for half in range(2):
        s = jnp.where(mask, scores[:, half * 2 * w:(half + 1) * 2 * w], NEG_INF)
        m = jnp.maximum(jnp.max(s, axis=-1, keepdims=True), sinks[half])
        parts.append(jnp.exp(s - m).astype(BF16))
        es.append(jnp.exp(sinks[half] - m))
    return jnp.concatenate(parts, axis=1), jnp.where(lane < SW_HEAD_DIM, es[0], es[1])


def _swa_prompt_kernel(sinks_ref, qkv_ref, o_ref, kc_ref, vc_ref, kk2, vv2):
    n = pl.program_id(1)
    w = WINDOW

    @pl.when(n == 0)
    def _():
        kk2[...] = jnp.zeros_like(kk2)
        for j in range(SW_KV_HEADS):
            vv2[j, :, 0:LANE] = jnp.zeros((4 * w, LANE), BF16)
            vv2[j, :, LANE:2 * LANE] = _ones_columns(w)

    for j in range(SW_KV_HEADS):
        kk2[j, 0:w, :] = kk2[j, w:2 * w, :]
        kk2[j, 2 * w:3 * w, :] = kk2[j, 3 * w:4 * w, :]
        vv2[j, 0:w, 0:LANE] = vv2[j, w:2 * w, 0:LANE]
        vv2[j, 2 * w:3 * w, 0:LANE] = vv2[j, 3 * w:4 * w, 0:LANE]
    q = qkv_ref[:, 0:D_MODEL].astype(BF16)
    k = qkv_ref[:, D_MODEL:D_MODEL + SW_KV]
    v = qkv_ref[:, D_MODEL + SW_KV:D_MODEL + 2 * SW_KV]
    kc_ref[...] = k
    vc_ref[...] = v
    for j, (left, right) in enumerate(_head_pair_operands(k)):
        kk2[j, w:2 * w, :] = left.astype(BF16)
        kk2[j, 3 * w:4 * w, :] = right.astype(BF16)
    for j, (left, right) in enumerate(_head_pair_operands(v)):
        vv2[j, w:2 * w, 0:LANE] = left.astype(BF16)
        vv2[j, 3 * w:4 * w, 0:LANE] = right.astype(BF16)
    qi = lax.broadcasted_iota(jnp.int32, (w, 2 * w), 0) + w
    sj = lax.broadcasted_iota(jnp.int32, (w, 2 * w), 1)
    d = qi - sj
    mask = (d >= 0) & (d <= WINDOW) & ((sj >= w) | (n > 0))
    lane = lax.broadcasted_iota(jnp.int32, (w, LANE), 1)
    n_pb = D_MODEL // LANE
    scores = [_dot_nt(q[:, pb * LANE:(pb + 1) * LANE], kk2[pb // 2]) for pb in range(n_pb)]
    soft = [_pair_softmax(scores[pb], mask, (sinks_ref[2 * pb], sinks_ref[2 * pb + 1]), lane, w)
            for pb in range(n_pb)]
    outs = [_dot(soft[pb][0], vv2[pb // 2]) for pb in range(n_pb)]
    for pb in range(n_pb):
        o = outs[pb][:, 0:LANE] / (outs[pb][:, LANE:2 * LANE] + soft[pb][1])
        o_ref[:, pb * LANE:(pb + 1) * LANE] = o.astype(o_ref.dtype)


def _swa_prompt(sinks, qkv, *, batch, seq):
    nb = seq // WINDOW
    row = lambda b, n: (b * nb + n, 0)
    per_b = lambda b, n: (b, 0, 0)
    return pl.pallas_call(
        _swa_prompt_kernel,
        out_shape=[jax.ShapeDtypeStruct((batch * seq, D_MODEL), BF16),
                   jax.ShapeDtypeStruct((batch, WINDOW, SW_KV), F32),
                   jax.ShapeDtypeStruct((batch, WINDOW, SW_KV), F32)],
        grid=(batch, nb),
        in_specs=[pl.BlockSpec(memory_space=pltpu.SMEM),
                  pl.BlockSpec((WINDOW, D_MODEL + 2 * SW_KV), row)],
        out_specs=[pl.BlockSpec((WINDOW, D_MODEL), row),
                   pl.BlockSpec((None, WINDOW, SW_KV), per_b),
                   pl.BlockSpec((None, WINDOW, SW_KV), per_b)],
        scratch_shapes=[pltpu.VMEM((SW_KV_HEADS, 4 * WINDOW, LANE), BF16),
                        pltpu.VMEM((SW_KV_HEADS, 4 * WINDOW, 2 * LANE), BF16)],
        compiler_params=_params(("arbitrary", "arbitrary")),
        name="swa_prompt",
    )(sinks, qkv)


def _swa_sample_kernel(sinks_ref, qkv_ref, kbuf_ref, vbuf_ref, cos_ref, sa_ref, sb_ref,
                       o_ref, kc_ref, vc_ref, kk_s, vv_s, kk2, vv2, *, nseq, t_new):
    wb = WINDOW
    tp = t_new

    @pl.when(pl.program_id(0) == 0)
    def _():
        kk_s[...] = jnp.zeros_like(kk_s)
        vv_s[...] = jnp.zeros_like(vv_s)
        kk2[...] = jnp.zeros_like(kk2)
        for g in range(nseq):
            for j in range(SW_KV_HEADS):
                vv2[g, j, :, 0:LANE] = jnp.zeros((4 * wb, LANE), BF16)
                vv2[g, j, :, LANE:2 * LANE] = _ones_columns(wb)

    cos, sa, sb = cos_ref[0:tp, :], sa_ref[0:tp, :], sb_ref[0:tp, :]
    qs = []
    for g in range(nseq):
        q = (_rope(qkv_ref[g, :, 0:D_MODEL], cos, sa, sb) * (SW_HEAD_DIM ** -0.5)).astype(BF16)
        k_new = _rope(qkv_ref[g, :, D_MODEL:D_MODEL + SW_KV], cos, sa, sb)
        v_new = qkv_ref[g, :, D_MODEL + SW_KV:D_MODEL + 2 * SW_KV]
        k_old, v_old = kbuf_ref[g], vbuf_ref[g]
        kk_s[g, 0:wb, :] = k_old
        vv_s[g, 0:wb, :] = v_old
        kk_s[g, wb:wb + tp, :] = k_new
        vv_s[g, wb:wb + tp, :] = v_new
        kc_ref[g] = kk_s[g, pl.ds(t_new, wb), :]
        vc_ref[g] = vv_s[g, pl.ds(t_new, wb), :]
        for r0, n_r, kx, vx in ((0, wb, k_old, v_old), (wb, tp, k_new, v_new)):
            for j, (left, right) in enumerate(_head_pair_operands(kx)):
                kk2[g, j, r0:r0 + n_r, :] = left.astype(BF16)
                kk2[g, j, 2 * wb + r0:2 * wb + r0 + n_r, :] = right.astype(BF16)
            for j, (left, right) in enumerate(_head_pair_operands(vx)):
                vv2[g, j, r0:r0 + n_r, 0:LANE] = left.astype(BF16)
                vv2[g, j, 2 * wb + r0:2 * wb + r0 + n_r, 0:LANE] = right.astype(BF16)
        qs.append([jnp.concatenate([q[:, (2 * j) * LANE:(2 * j + 1) * LANE],
                                    q[:, (2 * j + 1) * LANE:(2 * j + 2) * LANE]], axis=0)
                   for j in range(SW_KV_HEADS)])
    rows = 2 * tp
    qt = lax.broadcasted_iota(jnp.int32, (rows, 2 * wb), 0) % tp
    sj = lax.broadcasted_iota(jnp.int32, (rows, 2 * wb), 1)
    d = qt + wb - sj
    mask = (d >= 0) & (d <= WINDOW)
    top = lax.broadcasted_iota(jnp.int32, (rows, 1), 0) < tp
    lane = lax.broadcasted_iota(jnp.int32, (rows, LANE), 1)
    items = [(g, j) for g in range(nseq) for j in range(SW_KV_HEADS)]
    scores = [_dot_nt(qs[g][j], kk2[g, j]) for g, j in items]
    soft = []
    for n, (g, j) in enumerate(items):
        sinks = [jnp.where(top, sinks_ref[4 * j + half], sinks_ref[4 * j + 2 + half]) for half in range(2)]
        soft.append(_pair_softmax(scores[n], mask, sinks, lane, wb))
    outs = [_dot(soft[n][0], vv2[g, j]) for n, (g, j) in enumerate(items)]
    for n, (g, j) in enumerate(items):
        o = outs[n][:, 0:LANE] / (outs[n][:, LANE:2 * LANE] + soft[n][1])
        o_ref[g, :, (2 * j) * LANE:(2 * j + 1) * LANE] = o[0:tp, :]
        o_ref[g, :, (2 * j + 1) * LANE:(2 * j + 2) * LANE] = o[tp:2 * tp, :]


def _swa_sample(sinks, qkv, kbuf, vbuf, cos, sa, sb, *, batch, nseq, t_new):
    per_b = lambda b: (b, 0, 0)
    tab = lambda b: (0, 0)
    kern = functools.partial(_swa_sample_kernel, nseq=nseq, t_new=t_new)
    return pl.pallas_call(
        kern,
        out_shape=[jax.ShapeDtypeStruct((batch, t_new, D_MODEL), F32),
                   jax.ShapeDtypeStruct((batch, WINDOW, SW_KV), F32),
                   jax.ShapeDtypeStruct((batch, WINDOW, SW_KV), F32)],
        grid=(batch // nseq,),
        in_specs=[pl.BlockSpec(memory_space=pltpu.SMEM),
                  pl.BlockSpec((nseq, t_new, D_MODEL + 2 * SW_KV), per_b),
                  pl.BlockSpec((nseq, WINDOW, SW_KV), per_b),
                  pl.BlockSpec((nseq, WINDOW, SW_KV), per_b),
                  pl.BlockSpec((SUBLANE, LANE), tab),
                  pl.BlockSpec((SUBLANE, LANE), tab),
                  pl.BlockSpec((SUBLANE, LANE), tab)],
        out_specs=[pl.BlockSpec((nseq, t_new, D_MODEL), per_b),
                   pl.BlockSpec((nseq, WINDOW, SW_KV), per_b),
                   pl.BlockSpec((nseq, WINDOW, SW_KV), per_b)],
        scratch_shapes=[pltpu.VMEM((nseq, WINDOW + SUBLANE, SW_KV), F32),
                        pltpu.VMEM((nseq, WINDOW + SUBLANE, SW_KV), F32),
                        pltpu.VMEM((nseq, SW_KV_HEADS, 4 * WINDOW, LANE), BF16),
                        pltpu.VMEM((nseq, SW_KV_HEADS, 4 * WINDOW, 2 * LANE), BF16)],
        compiler_params=_params(("arbitrary",)),
        name="swa_sample",
    )(sinks, qkv, kbuf, vbuf, cos, sa, sb)


def _out_ffn_kernel(x_ref, g2_ref, post2_ref, ga_ref, gb_ref, odn_ref, osw_ref, wo_ref,
                    sh_ref, sc_ref, g_ref, pre_ref, post_ref, wg_ref, wu_ref, wd_ref, o_ref,
                    *, per_token, tiles_per_seq):
    rows = _row_groups(x_ref.shape[0])
    g2 = _mod_rows(g2_ref, per_token, tiles_per_seq)
    sh = _mod_rows(sh_ref, per_token, tiles_per_seq)
    sc = _mod_rows(sc_ref, per_token, tiles_per_seq)
    g = _mod_rows(g_ref, per_token, tiles_per_seq)
    pick = (lambda m, r: m[r]) if per_token else (lambda m, r: m)
    f32 = lambda ref, r: ref[r, :].astype(F32)
    ys = [(_sigmoid(f32(ga_ref, r)) * f32(odn_ref, r)
           + _sigmoid(f32(gb_ref, r)) * f32(osw_ref, r)).astype(BF16) for r in rows]
    ps = [_dot(y, wo_ref[...]) for y in ys]
    xs = [x_ref[r, :] + pick(g2, r) * _rms(p, post2_ref[...]) for p, r in zip(ps, rows)]
    outs = _ffn_stages(xs, rows, pick, sh, sc, g, pre_ref[...], post_ref[...], wg_ref, wu_ref, wd_ref)
    for o, r in zip(outs, rows):
        o_ref[r, :] = o


def _out_ffn(x, mod, mod_block, seq_len, post2, gates, o_dn, o_sw, w_out, pre, post, wg, wu, wd):
    n = x.shape[0]
    tm = min(512, n)
    dff = wg.shape[1]
    kern = functools.partial(_out_ffn_kernel, per_token=mod_block is None, tiles_per_seq=max(seq_len // tm, 1))
    row = lambda i: (i, 0)
    return pl.pallas_call(
        kern,
        out_shape=jax.ShapeDtypeStruct((n, D_MODEL), F32),
        grid=(n // tm,),
        in_specs=[pl.BlockSpec((tm, D_MODEL), row)]
        + _mod_specs(mod_block, tm, (5,))
        + [_const_spec((1, D_MODEL)),
           pl.BlockSpec((tm, D_MODEL), lambda i: (i, 0)),
           pl.BlockSpec((tm, D_MODEL), lambda i: (i, 1)),
           pl.BlockSpec((tm, D_MODEL), row),
           pl.BlockSpec((tm, D_MODEL), row),
           _const_spec((D_MODEL, D_MODEL))]
        + _mod_specs(mod_block, tm, (6, 7, 8))
        + [_const_spec((1, D_MODEL)), _const_spec((1, D_MODEL)),
           _const_spec((D_MODEL, dff)), _const_spec((D_MODEL, dff)), _const_spec((dff, D_MODEL))],
        out_specs=pl.BlockSpec((tm, D_MODEL), row),
        compiler_params=_params(("arbitrary",)),
        name="out_ffn",
    )(x, mod, post2, gates, gates, o_dn, o_sw, w_out, mod, mod, mod, pre, post, wg, wu, wd)


def _rope_tables(pos):
    half = ROT_DIM // 2
    inv_freq = ROPE_THETA ** (-jnp.arange(half, dtype=F32) * (2.0 / ROT_DIM))
    ang = pos.astype(F32)[:, None] * inv_freq[None, :]
    cos, sin = jnp.cos(ang), jnp.sin(ang)
    n = pos.shape[0]
    rest = SW_HEAD_DIM - ROT_DIM
    c64 = jnp.concatenate([cos, cos, jnp.ones((n, rest), F32)], axis=1)
    a64 = jnp.concatenate([-sin, jnp.zeros((n, half + rest), F32)], axis=1)
    b64 = jnp.concatenate([jnp.zeros((n, half), F32), sin, jnp.zeros((n, rest), F32)], axis=1)
    rep = LANE // SW_HEAD_DIM
    return tuple(jnp.tile(t, (1, rep)) for t in (c64, a64, b64))


def _pad_rows(a, rows):
    return jnp.pad(a, ((0, rows - a.shape[0]),) + ((0, 0),) * (a.ndim - 1))


def _layer_sample(x, mod, seq, wts, conv0, s0, dn_fn, swa_fn):
    per_seq = lambda a: a.reshape(-1, seq, a.shape[-1])
    flat = lambda a: a.reshape(-1, a.shape[-1])
    x = _ffn(x, mod, 0, None, seq, wts["pre1"], wts["post1"], wts["wg1"], wts["wu1"], wts["wd1"])
    u, z, qkv, gates, ba = _proj(x, mod, None, seq, wts["pre2"], *wts["w_in"])
    o_dn, conv_new, s_new = dn_fn(per_seq(u), per_seq(z), per_seq(ba), wts["conv_w"], wts["gparam"],
                                  wts["dn_norm"], conv0, s0)
    o_sw, k_new, v_new = swa_fn(per_seq(qkv))
    x = _out_ffn(x, mod, None, seq, wts["post2"], gates, flat(o_dn), flat(o_sw), wts["w_out"],
                 wts["pre3"], wts["post3"], wts["wg2"], wts["wu2"], wts["wd2"])
    return x, (k_new, v_new, conv_new, s_new)


def _layer_prompt(x, mod, mod_block, batch, seq, wts, s0, rope, sink):
    per_seq = lambda a: a.reshape(batch, seq, a.shape[-1])
    x = _ffn(x, mod, 0, mod_block, seq, wts["pre1"], wts["post1"], wts["wg1"], wts["wu1"], wts["wd1"])
    q, k, v, z, qkv, gates, ba, conv_new = _proj_prompt(x, mod, mod_block, seq, wts["pre2"], *wts["w_in"],
                                                        wts["conv_w"], *rope)
    o_dn, s_new = _deltanet_fused(per_seq(q), per_seq(k), per_seq(v), per_seq(z), per_seq(ba), wts["gparam"],
                                  wts["dn_norm"], s0, nseq=batch, tm=DN_TILE, chunk=DN_CHUNK, o_dtype=BF16)
    o_sw, k_new, v_new = _swa_prompt(sink, qkv, batch=batch, seq=seq)
    x = _out_ffn(x, mod, mod_block, seq, wts["post2"], gates, o_dn.reshape(batch * seq, D_MODEL), o_sw,
                 wts["w_out"], wts["pre3"], wts["post3"], wts["wg2"], wts["wu2"], wts["wd2"])
    return x, (k_new, v_new, conv_new, s_new)


def kernel(x_prompt, x_sample, cache_swa_k, cache_swa_v, state_conv, state_delta, c_prompt, c_sample,
           w_ada, b_ada, ffn1_norm_pre, ffn1_norm_post, ffn1_w_gate, ffn1_w_up, ffn1_w_down,
           mix_norm_pre, mix_norm_post, w_in, conv_w, a_log, dt_bias, dn_norm, sinks, w_out,
           ffn2_norm_pre, ffn2_norm_post, ffn2_w_gate, ffn2_w_up, ffn2_w_down):
    depth = w_ada.shape[0]
    bp, tp, _ = x_prompt.shape
    bs, ts, _ = x_sample.shape
    assert tp % 512 == 0 and 1 <= ts <= SUBLANE and bs % SAMPLE_SEQS == 0 and (bs * ts) % SUBLANE == 0

    cos_p, sa_p, sb_p = _rope_tables(jnp.arange(tp))
    cos_s, sa_s, sb_s = (_pad_rows(t, SUBLANE) for t in _rope_tables(PAST_LEN + jnp.arange(ts)))

    xp = x_prompt.reshape(bp * tp, D_MODEL)
    xs = x_sample.reshape(bs * ts, D_MODEL)
    c_all = jnp.concatenate([jnp.repeat(c_sample, ts, axis=0), _pad_rows(c_prompt, SUBLANE)], axis=0)
    prompt_mod_block = bs * ts // SUBLANE

    outs_p, outs_s = [], []
    for l in range(depth):
        wi = w_in[l]
        n_ba = 2 * DN_HEADS
        w_in_a = wi[:, :P_QKV].astype(BF16)
        w_in_b = wi[:, P_QKV + n_ba:].astype(BF16)
        w_in_c = jnp.pad(wi[:, P_QKV:P_QKV + n_ba], ((0, 0), (0, LANE - n_ba))).astype(BF16)
        gparam = jnp.zeros((SUBLANE, LANE), F32)
        gparam = gparam.at[0, DN_HEADS:2 * DN_HEADS].set(dt_bias[l]).at[1, DN_HEADS:2 * DN_HEADS].set(a_log[l])
        wts = dict(
            pre1=ffn1_norm_pre[l][None], post1=ffn1_norm_post[l][None],
            wg1=ffn1_w_gate[l].astype(BF16), wu1=ffn1_w_up[l].astype(BF16), wd1=ffn1_w_down[l].astype(BF16),
            pre2=mix_norm_pre[l][None], post2=mix_norm_post[l][None], w_in=(w_in_a, w_in_b, w_in_c),
            conv_w=_pad_rows(conv_w[l], SUBLANE), gparam=gparam, dn_norm=dn_norm[l][None],
            w_out=w_out[l].astype(BF16),
            pre3=ffn2_norm_pre[l][None], post3=ffn2_norm_post[l][None],
            wg2=ffn2_w_gate[l].astype(BF16), wu2=ffn2_w_up[l].astype(BF16), wd2=ffn2_w_down[l].astype(BF16),
        )
        b_ada_l = b_ada[l][None]
        mod = _ada(c_all, w_ada[l], b_ada_l)
        sink_l = sinks[l]

        s0_p = jnp.zeros((bp, DN_HEADS, DN_DK, DN_DV), F32)
        xp, st_p = _layer_prompt(xp, mod, prompt_mod_block, bp, tp, wts, s0_p, (cos_p, sa_p, sb_p), sink_l)

        kbuf = cache_swa_k[l].reshape(bs, WINDOW, SW_KV)
        vbuf = cache_swa_v[l].reshape(bs, WINDOW, SW_KV)
        dn_s = functools.partial(_deltanet, nseq=SAMPLE_SEQS, tm=SUBLANE, chunk=SUBLANE, valid=ts, o_dtype=F32)

        swa_s = lambda qkv: _swa_sample(sink_l, qkv, kbuf, vbuf, cos_s, sa_s, sb_s,
                                        batch=bs, nseq=SAMPLE_SEQS, t_new=ts)
        xs, st_s = _layer_sample(xs, mod, ts, wts, state_conv[l], state_delta[l], dn_s, swa_s)
        outs_p.append(st_p)
        outs_s.append(st_s)

    def stack(outs, i):
        return jnp.stack([o[i] for o in outs])

    def kv5(a, batch):
        return a.reshape(depth, batch, WINDOW, SW_KV_HEADS, SW_HEAD_DIM)

    tail = slice(SUBLANE - CONV_W + 1, SUBLANE)
    y_p = xp.reshape(bp, tp, D_MODEL)
    y_s = xs.reshape(bs, ts, D_MODEL)
    return (y_p, y_s,
            kv5(stack(outs_p, 0), bp), kv5(stack(outs_p, 1), bp),
            stack(outs_p, 2)[:, :, tail], stack(outs_p, 3),
            kv5(stack(outs_s, 0), bs), kv5(stack(outs_s, 1), bs),
            stack(outs_s, 2), stack(outs_s, 3))
```
